```python
import math, functools
import jax, jax.numpy as jnp
from jax import lax
import numpy as np

D_MODEL = 2048
BATCH = 4
SEQ = 2048
DEPTH = 1
DEC_BATCH = 8
DEC_SEQ = 8
PAST_LEN = 16384
PAGE_SIZE = 128

H_ATT = 8
HD_ATT = 128
C_ATT = H_ATT * HD_ATT
MOBA_BLOCK = 256
MOBA_TOPK = 3
MOBA_Q_CHUNK = 32
NUM_BUCKETS = 32
MAX_DISTANCE = 128
HS_RWKV = 64
C_RWKV = D_MODEL // 2
H_RWKV = C_RWKV // HS_RWKV
D_DECAY_LORA = 96
D_AAA_LORA = 96
D_GATE_LORA = 256
RWKV_PROJ = 3 * C_RWKV + D_DECAY_LORA + D_AAA_LORA + D_GATE_LORA
RWKV_SPLITS = (C_RWKV, 2 * C_RWKV, 3 * C_RWKV, 3 * C_RWKV + D_DECAY_LORA, 3 * C_RWKV + D_DECAY_LORA + D_AAA_LORA)
GN_EPS = 64e-5
N_IN = 3 * C_ATT + RWKV_PROJ + 2 * D_MODEL
D_FF = 5632
N_ADA = 9
DEEPNORM_ALPHA = (2 * DEPTH) ** 0.25
DEEPNORM_BETA = (8 * DEPTH) ** -0.25
LN_EPS = 1e-5

kernel_name = "hybrid_moba_rwkv7_macaron_decode_step"


def _layer_norm(x, g=None, b=None):
    xf = x.astype(jnp.float32)
    mu = jnp.mean(xf, -1, keepdims=True)
    xc = xf - mu
    var = jnp.mean(xc * xc, -1, keepdims=True)
    y = xc * lax.rsqrt(var + LN_EPS)
    if g is not None:
        y = y * g.astype(jnp.float32) + b.astype(jnp.float32)
    return y.astype(x.dtype)


def _modulate(x, shift, scale):
    return _layer_norm(x) * (1 + scale) + shift


def _swiglu(h, wi, wo):
    gu = h @ wi
    g, u = jnp.split(gu, 2, axis=-1)
    return (jax.nn.silu(g) * u) @ wo


def _t5_bucket(rel):
    n = jnp.maximum(rel, 0)
    max_exact = NUM_BUCKETS // 2
    nf = jnp.maximum(n, 1).astype(jnp.float32)
    large = max_exact + (jnp.log(nf / max_exact) / math.log(MAX_DISTANCE / max_exact)
                         * (NUM_BUCKETS - max_exact)).astype(jnp.int32)
    large = jnp.minimum(large, NUM_BUCKETS - 1)
    return jnp.where(n < max_exact, n, large)


def _block_means(kb, n_blocks):
    b = kb.shape[0]
    m = kb[:, :n_blocks * MOBA_BLOCK].astype(jnp.float32).reshape(
        b, n_blocks, MOBA_BLOCK, H_ATT, HD_ATT).mean(2)
    if n_blocks < MOBA_TOPK:
        m = jnp.concatenate([m, jnp.zeros((b, MOBA_TOPK - n_blocks, H_ATT, HD_ATT), jnp.float32)], 1)
    return m


def _moba_core(q, q_pos, means, gather, rel_table):
    bsz, nq, nh, _ = q.shape
    qf = q.astype(jnp.float32)
    nc = means.shape[1]
    own = q_pos // MOBA_BLOCK
    score = jnp.einsum('bqhd,bnhd->bhqn', qf, means)
    past = jnp.arange(nc, dtype=jnp.int32)[None, :] < own[:, None]
    score = jnp.where(past[None, None], score, -jnp.inf)
    top_s, top_i = lax.top_k(score, MOBA_TOPK)
    top_ok = top_s > -jnp.inf
    own_b = jnp.broadcast_to(own[None, None, :, None], (bsz, nh, nq, 1))
    ids = jnp.concatenate([jnp.where(top_ok, top_i, 0), own_b], -1)
    ok = jnp.concatenate([top_ok, jnp.ones((bsz, nh, nq, 1), bool)], -1)
    kpos = ids[..., None] * MOBA_BLOCK + jnp.arange(MOBA_BLOCK, dtype=jnp.int32)
    rel = q_pos[None, None, :, None, None] - kpos
    mask = ok[..., None] & (rel >= 0)
    kg, vg = gather(kpos)
    h_idx = jnp.arange(nh)[None, :, None, None, None]
    bias = rel_table.astype(jnp.float32)[_t5_bucket(rel), h_idx]
    logits = jnp.einsum('bqhd,bhqnkd->bhqnk', qf, kg.astype(jnp.float32)) * (HD_ATT ** -0.5) + bias
    logits = jnp.where(mask, logits, -jnp.inf)
    p = jax.nn.softmax(logits.reshape(bsz, nh, nq, -1), axis=-1).reshape(logits.shape)
    out = jnp.einsum('bhqnk,bhqnkd->bqhd', p, vg.astype(jnp.float32))
    return out.astype(q.dtype)


def _attend_prompt(q, k, v, rel_table):
    bsz, s = q.shape[0], q.shape[1]
    means = _block_means(k, s // MOBA_BLOCK)
    n_chunks = s // MOBA_Q_CHUNK
    qc = jnp.moveaxis(q.reshape(bsz, n_chunks, MOBA_Q_CHUNK, H_ATT, HD_ATT), 1, 0)
    pos = jnp.arange(s, dtype=jnp.int32).reshape(n_chunks, MOBA_Q_CHUNK)
    b_idx = jnp.arange(bsz)[:, None, None, None, None]
    h_idx = jnp.arange(H_ATT)[None, :, None, None, None]

    def gather(kpos):
        kp = jnp.minimum(kpos, s - 1)
        return k[b_idx, kp, h_idx], v[b_idx, kp, h_idx]

    out = lax.map(lambda a: _moba_core(a[0], a[1], means, gather, rel_table), (qc, pos))
    return jnp.moveaxis(out, 0, 1).reshape(bsz, s, H_ATT, HD_ATT)


def _attend_sample(q, k, v, cache_k_l, cache_v_l, page_table, rel_table):
    dbs, ds = q.shape[0], q.shape[1]
    n_full = PAST_LEN // MOBA_BLOCK
    n_need = -(-(n_full * MOBA_BLOCK) // PAGE_SIZE)
    kpages = cache_k_l[page_table[:, :n_need]].reshape(dbs, n_need * PAGE_SIZE, H_ATT, HD_ATT)
    means = _block_means(kpages, n_full)
    q_pos = PAST_LEN + jnp.arange(ds, dtype=jnp.int32)
    b_idx = jnp.arange(dbs)[:, None, None, None, None]
    h_idx = jnp.arange(H_ATT)[None, :, None, None, None]

    def gather(kpos):
        in_past = (kpos < PAST_LEN)[..., None]
        pp = jnp.minimum(kpos, PAST_LEN - 1)
        phys = page_table[b_idx, pp // PAGE_SIZE]
        off = pp % PAGE_SIZE
        pn = jnp.clip(kpos - PAST_LEN, 0, ds - 1)
        kg = jnp.where(in_past, cache_k_l[phys, off, h_idx], k[b_idx, pn, h_idx])
        vg = jnp.where(in_past, cache_v_l[phys, off, h_idx], v[b_idx, pn, h_idx])
        return kg, vg

    return _moba_core(q, q_pos, means, gather, rel_table)


def _rwkv7(z, shift0, wkv0, mu_shift, w0, w2, a0, a2, g2, k_k, k_a, r_k, lnx_g, lnx_b):
    bsz, t = z.shape[0], z.shape[1]
    f32 = jnp.float32
    z_prev = jnp.concatenate([shift0[:, None].astype(z.dtype), z[:, :-1]], axis=1)
    zs = z + (z_prev - z) * mu_shift
    r, k, v, zw, za, zg = jnp.split(zs, RWKV_SPLITS, axis=-1)
    w = -jax.nn.softplus(-(w0 + jnp.tanh(zw) @ w2)) - 0.5
    decay = jnp.exp(-jnp.exp(w.astype(f32)))
    a = jax.nn.sigmoid(a0 + za @ a2)
    g = jax.nn.sigmoid(zg) @ g2

    def heads(u):
        return u.astype(f32).reshape(bsz, t, H_RWKV, HS_RWKV)

    kk = heads(k * k_k)
    kk = kk / jnp.maximum(jnp.sqrt(jnp.sum(kk * kk, -1, keepdims=True)), 1e-12)
    k = k * (1 + (a - 1) * k_a)
    r4, k4, v4, a4, d4 = heads(r), heads(k), heads(v), heads(a), heads(decay)

    def step(s, inp):
        r_t, k_t, v_t, kk_t, a_t, d_t = inp
        sa = jnp.einsum('bhvk,bhk->bhv', s, -kk_t)
        s = (s * d_t[:, :, None, :] + sa[..., None] * (kk_t * a_t)[:, :, None, :]
             + v_t[..., None] * k_t[:, :, None, :])
        return s, jnp.einsum('bhvk,bhk->bhv', s, r_t)

    seq = tuple(jnp.swapaxes(u, 0, 1) for u in (r4, k4, v4, kk, a4, d4))
    s_last, ys = lax.scan(step, wkv0.astype(f32), seq)
    y = jnp.swapaxes(ys, 0, 1)
    mu = jnp.mean(y, -1, keepdims=True)
    yc = y - mu
    var = jnp.mean(yc * yc, -1, keepdims=True)
    yn = (yc * lax.rsqrt(var + GN_EPS)).reshape(bsz, t, C_RWKV) * lnx_g + lnx_b
    bonus = (jnp.sum(r4 * k4 * r_k, -1, keepdims=True) * v4).reshape(bsz, t, C_RWKV)
    out = (yn + bonus) * g
    return out.astype(z.dtype), s_last.astype(wkv0.dtype)


def _trunk_layer(x, c, attend, shift0, wkv0, w_ada, b_ada, ln_g, ln_b, ffn_wi, ffn_wo, w_in,
                 mu_shift, w0, w2, a0, a2, g2, k_k, k_a, r_k, lnx_g, lnx_b,
                 w_up_attn, w_up_rwkv, w_out):
    bsz, t = x.shape[0], x.shape[1]
    ada = (jax.nn.silu(c) @ w_ada + b_ada).reshape(bsz, N_ADA, D_MODEL)[:, :, None, :]
    sh1, sc1, gt1, sh2, sc2, gt2, sh3, sc3, gt3 = [ada[:, i] for i in range(N_ADA)]
    h = _modulate(x, sh1, sc1)
    x = _layer_norm(DEEPNORM_ALPHA * x + 0.5 * gt1 * _swiglu(h, ffn_wi[0], ffn_wo[0]), ln_g[0], ln_b[0])
    h = _modulate(x, sh2, sc2)
    p = h @ w_in
    q = p[..., :C_ATT].reshape(bsz, t, H_ATT, HD_ATT)
    k = p[..., C_ATT:2 * C_ATT].reshape(bsz, t, H_ATT, HD_ATT)
    v = p[..., 2 * C_ATT:3 * C_ATT].reshape(bsz, t, H_ATT, HD_ATT)
    z = p[..., 3 * C_ATT:3 * C_ATT + RWKV_PROJ]
    gate_att = p[..., 3 * C_ATT + RWKV_PROJ:3 * C_ATT + RWKV_PROJ + D_MODEL]
    gate_rwkv = p[..., 3 * C_ATT + RWKV_PROJ + D_MODEL:]
    o_att = attend(q, k, v).reshape(bsz, t, C_ATT)
    o_rwkv, wkv_new = _rwkv7(z, shift0, wkv0, mu_shift, w0, w2, a0, a2, g2, k_k, k_a, r_k, lnx_g, lnx_b)
    shift_new = z[:, -1].astype(shift0.dtype)
    m = jax.nn.sigmoid(gate_att) * (o_att @ w_up_attn) + jax.nn.sigmoid(gate_rwkv) * (o_rwkv @ w_up_rwkv)
    x = _layer_norm(DEEPNORM_ALPHA * x + gt2 * (m @ w_out), ln_g[1], ln_b[1])
    h = _modulate(x, sh3, sc3)
    x = _layer_norm(DEEPNORM_ALPHA * x + 0.5 * gt3 * _swiglu(h, ffn_wi[1], ffn_wo[1]), ln_g[2], ln_b[2])
    return x, k, v, wkv_new, shift_new


def setup_inputs(seed: int = 0) -> dict:
    key = jax.random.key(seed)
    ks = jax.random.split(key, 40)
    f32 = jnp.float32

    def nrm(kk, shape, scale=1.0):
        return jax.random.normal(kk, shape, f32) * scale

    n_pages = PAST_LEN // PAGE_SIZE
    n_used = DEC_BATCH * n_pages
    n_phys = n_used + max(1, n_used // 4)
    page_table = jax.random.permutation(ks[0], n_phys)[:n_used].reshape(DEC_BATCH, n_pages).astype(jnp.int32)
    L = DEPTH
    return {
        'x_prompt': nrm(ks[1], (BATCH, SEQ, D_MODEL)),
        'x_sample': nrm(ks[2], (DEC_BATCH, DEC_SEQ, D_MODEL)),
        'cache_k': nrm(ks[3], (L, n_phys, PAGE_SIZE, H_ATT, HD_ATT)),
        'cache_v': nrm(ks[4], (L, n_phys, PAGE_SIZE, H_ATT, HD_ATT)),
        'state_wkv': nrm(ks[5], (L, DEC_BATCH, H_RWKV, HS_RWKV, HS_RWKV), 0.3),
        'state_shift': nrm(ks[6], (L, DEC_BATCH, RWKV_PROJ)),
        'page_table': page_table,
        'c_prompt': nrm(ks[7], (BATCH, D_MODEL)),
        'c_sample': nrm(ks[8], (DEC_BATCH, D_MODEL)),
        'rel_bias': nrm(ks[9], (NUM_BUCKETS, H_ATT), 0.5),
        'w_ada': nrm(ks[10], (L, D_MODEL, N_ADA * D_MODEL), D_MODEL ** -0.5),
        'b_ada': nrm(ks[11], (L, N_ADA * D_MODEL), 0.01),
        'ln_g': 1.0 + nrm(ks[12], (L, 3, D_MODEL), 0.01),
        'ln_b': nrm(ks[13], (L, 3, D_MODEL), 0.01),
        'ffn_wi': nrm(ks[14], (L, 2, D_MODEL, 2 * D_FF), D_MODEL ** -0.5),
        'ffn_wo': nrm(ks[15], (L, 2, D_FF, D_MODEL), DEEPNORM_BETA * D_FF ** -0.5),
        'w_in': nrm(ks[16], (L, D_MODEL, N_IN), D_MODEL ** -0.5),
        'mu_shift': jax.random.uniform(ks[17], (L, RWKV_PROJ), f32),
        'w0': -1.0 + nrm(ks[18], (L, C_RWKV), 0.5),
        'w2': nrm(ks[19], (L, D_DECAY_LORA, C_RWKV), 0.3 * D_DECAY_LORA ** -0.5),
        'a0': nrm(ks[20], (L, C_RWKV), 0.1),
        'a2': nrm(ks[21], (L, D_AAA_LORA, C_RWKV), 0.3 * D_AAA_LORA ** -0.5),
        'g2': nrm(ks[22], (L, D_GATE_LORA, C_RWKV), D_GATE_LORA ** -0.5),
        'k_k': 0.85 + nrm(ks[23], (L, C_RWKV), 0.05),
        'k_a': 1.0 + nrm(ks[24], (L, C_RWKV), 0.05),
        'r_k': nrm(ks[25], (L, H_RWKV, HS_RWKV), 0.1),
        'lnx_g': 1.0 + nrm(ks[26], (L, C_RWKV), 0.01),
        'lnx_b': nrm(ks[27], (L, C_RWKV), 0.01),
        'w_up_attn': nrm(ks[28], (L, C_ATT, D_MODEL), DEEPNORM_BETA * C_ATT ** -0.5),
        'w_up_rwkv': nrm(ks[29], (L, C_RWKV, D_MODEL), DEEPNORM_BETA * C_RWKV ** -0.5),
        'w_out': nrm(ks[30], (L, D_MODEL, D_MODEL), DEEPNORM_BETA * D_MODEL ** -0.5),
    }


def reference(x_prompt, x_sample, cache_k, cache_v, state_wkv, state_shift, page_table, c_prompt, c_sample,
              rel_bias, w_ada, b_ada, ln_g, ln_b, ffn_wi, ffn_wo, w_in, mu_shift, w0, w2, a0, a2, g2,
              k_k, k_a, r_k, lnx_g, lnx_b, w_up_attn, w_up_rwkv, w_out):
    yp, ys = x_prompt, x_sample
    bsz = x_prompt.shape[0]
    shift_zero = jnp.zeros((bsz, RWKV_PROJ), state_shift.dtype)
    wkv_zero = jnp.zeros((bsz, H_RWKV, HS_RWKV, HS_RWKV), state_wkv.dtype)
    kp_l, vp_l, ks_l, vs_l, wp_l, ws_l, sp_l, ss_l = [], [], [], [], [], [], [], []
    for l in range(DEPTH):
        lw = (w_ada[l], b_ada[l], ln_g[l], ln_b[l], ffn_wi[l], ffn_wo[l], w_in[l], mu_shift[l], w0[l], w2[l],
              a0[l], a2[l], g2[l], k_k[l], k_a[l], r_k[l], lnx_g[l], lnx_b[l], w_up_attn[l], w_up_rwkv[l], w_out[l])
        attend_p = functools.partial(_attend_prompt, rel_table=rel_bias)
        yp, kp, vp, wp, sp = _trunk_layer(yp, c_prompt, attend_p, shift_zero, wkv_zero, *lw)
        attend_s = functools.partial(_attend_sample, cache_k_l=cache_k[l], cache_v_l=cache_v[l],
                                     page_table=page_table, rel_table=rel_bias)
        ys, kn, vn, wn, sn = _trunk_layer(ys, c_sample, attend_s, state_shift[l], state_wkv[l], *lw)
        kp_l.append(kp); vp_l.append(vp); ks_l.append(kn); vs_l.append(vn)
        wp_l.append(wp); ws_l.append(wn); sp_l.append(sp); ss_l.append(sn)
    return (yp, ys, jnp.stack(kp_l), jnp.stack(vp_l), jnp.stack(ks_l), jnp.stack(vs_l),
            jnp.stack(wp_l), jnp.stack(ws_l), jnp.stack(sp_l), jnp.stack(ss_l))
```

```python
import functools
import math

import jax
import jax.numpy as jnp
from jax import lax
from jax.experimental import pallas as pl
from jax.experimental.pallas import tpu as pltpu

F32 = jnp.float32
BF16 = jnp.bfloat16

D_MODEL = 2048
D_FF = 5632
N_ADA = 9
H_ATT = 8
HD_ATT = 128
C_ATT = H_ATT * HD_ATT
MOBA_BLOCK = 256
MOBA_TOPK = 3
NUM_BUCKETS = 32
MAX_DISTANCE = 128
PAGE_SIZE = 128
HS_RWKV = 64
C_RWKV = 1024
H_RWKV = C_RWKV // HS_RWKV
D_DECAY_LORA = 96
D_AAA_LORA = 96
D_GATE_LORA = 256
RWKV_PROJ = 3 * C_RWKV + D_DECAY_LORA + D_AAA_LORA + D_GATE_LORA
LORA_PAD = 128
Z_PAD = 3 * C_RWKV + 2 * LORA_PAD + D_GATE_LORA
GN_EPS = 64e-5
LN_EPS = 1e-5
DEPTH = 1
DEEPNORM_ALPHA = (2 * DEPTH) ** 0.25
NEG_BIG = -1e30

LANES = 128
CHUNK = 64
PAIRS_PER_STEP = 2
VMEM_LIMIT = 56 * 1024 * 1024


def _t5_thresholds():
    max_exact = NUM_BUCKETS // 2
    thr = list(range(1, max_exact + 1))
    for b in range(max_exact + 1, NUM_BUCKETS):
        x = max_exact * (MAX_DISTANCE / max_exact) ** ((b - max_exact) / (NUM_BUCKETS - max_exact))
        thr.append(int(math.ceil(x)))
    return tuple(thr)


T5_THR = _t5_thresholds()


def _cparams(sem, vmem=VMEM_LIMIT):
    return pltpu.CompilerParams(dimension_semantics=sem, vmem_limit_bytes=vmem)


def _ln(x):
    mu = jnp.mean(x, axis=-1, keepdims=True)
    xc = x - mu
    var = jnp.mean(xc * xc, axis=-1, keepdims=True)
    return xc * lax.rsqrt(var + LN_EPS)


def _split(x, n):
    parts = []
    for i in range(n):
        p = x.astype(BF16)
        parts.append(p)
        if i + 1 < n:
            x = x - p.astype(F32)
    return parts


_NN = (((1,), (0,)), ((), ()))
_NT = (((1,), (1,)), ((), ()))
_TN = (((0,), (0,)), ((), ()))


def _dotp(ap, bp, dims=_NN, order=None):
    if order is None:
        order = max(len(ap), len(bp))
    acc = None
    for i, a in enumerate(ap):
        for j, b in enumerate(bp):
            if i + j < order:
                t = lax.dot_general(a, b, dims, preferred_element_type=F32)
                acc = t if acc is None else acc + t
    return acc


def _t5_bias(rel, tbl_ref, h):
    bias = jnp.full(rel.shape, tbl_ref[0, h], F32)
    for b in range(1, NUM_BUCKETS):
        bias = jnp.where(rel >= T5_THR[b - 1], tbl_ref[b, h], bias)
    return bias


def _seg_ones():
    r = lax.broadcasted_iota(jnp.int32, (LANES, LANES), 0) // HS_RWKV
    c = lax.broadcasted_iota(jnp.int32, (LANES, LANES), 1) // HS_RWKV
    return jnp.where(r == c, 1.0, 0.0).astype(BF16)


def _segsum64(x, p128):
    outs = []
    for j in range(x.shape[-1] // LANES):
        xs = x[:, j * LANES:(j + 1) * LANES]
        outs.append(_dotp(_split(xs, 3), [p128]))
    return jnp.concatenate(outs, axis=-1)


ADA_TN = 1024


def _ada_kernel(c_ref, w_ref, b_ref, o_ref):
    c = c_ref[...]
    sc = c * jax.nn.sigmoid(c)
    rows = sc.shape[0]
    s0, s1, s2 = _split(sc, 3)
    w_hi, w_lo = _split(w_ref[...], 2)
    a = jnp.dot(jnp.concatenate([s0, s1, s2], axis=0), w_hi, preferred_element_type=F32)
    b = jnp.dot(jnp.concatenate([s0, s1], axis=0), w_lo, preferred_element_type=F32)
    o_ref[...] = (a[:rows] + a[rows:2 * rows] + a[2 * rows:] + b[:rows] + b[rows:]) + b_ref[...]


def _ada_call(c_all, w_ada, b_ada):
    rows, d = c_all.shape
    n = w_ada.shape[1]
    return pl.pallas_call(
        _ada_kernel,
        grid=(n // ADA_TN,),
        in_specs=[pl.BlockSpec((rows, d), lambda j: (0, 0)),
                  pl.BlockSpec((d, ADA_TN), lambda j: (0, j)),
                  pl.BlockSpec((1, ADA_TN), lambda j: (0, j))],
        out_specs=pl.BlockSpec((rows, ADA_TN), lambda j: (0, j)),
        out_shape=jax.ShapeDtypeStruct((rows, n), F32),
        compiler_params=_cparams(("parallel",)),
        name="ada",
    )(c_all, w_ada, b_ada.reshape(1, n))


FFN_TF = 512


def _ffn_kernel(x_ref, ada_ref, lng_ref, lnb_ref, wig_ref, wiu_ref, wo_ref, *rest, sub, emit_next):
    if emit_next:
        o_ref, h_next_ref, h_scr, acc = rest
    else:
        o_ref, h_scr, acc = rest
    bb, t, d = x_ref.shape
    j = pl.program_id(1)

    @pl.when(j == 0)
    def _():
        h = _ln(x_ref[...]) * (1.0 + ada_ref[:, 3 * sub + 1:3 * sub + 2, :]) + ada_ref[:, 3 * sub:3 * sub + 1, :]
        h_scr[...] = h.reshape(bb * t, d).astype(BF16)
        acc[...] = jnp.zeros_like(acc)

    h = h_scr[...]
    g = jnp.dot(h, wig_ref[...], preferred_element_type=F32)
    u = jnp.dot(h, wiu_ref[...], preferred_element_type=F32)
    act = (g * jax.nn.sigmoid(g)) * u
    acc[...] += jnp.dot(act.astype(BF16), wo_ref[...], preferred_element_type=F32)

    @pl.when(j == pl.num_programs(1) - 1)
    def _():
        gate = ada_ref[:, 3 * sub + 2:3 * sub + 3, :]
        y = DEEPNORM_ALPHA * x_ref[...] + 0.5 * gate * acc[...].reshape(bb, t, d)
        y = _ln(y) * lng_ref[sub:sub + 1, :] + lnb_ref[sub:sub + 1, :]
        o_ref[...] = y
        if emit_next:
            nxt = sub + 1
            hn = _ln(y) * (1.0 + ada_ref[:, 3 * nxt + 1:3 * nxt + 2, :]) + ada_ref[:, 3 * nxt:3 * nxt + 1, :]
            h_next_ref[...] = hn.astype(BF16)


def _ffn_call(x, ada, ln_g, ln_b, wi, wo, *, sub, emit_next, blk):
    bsz, t, d = x.shape
    bb, tt = blk
    nj = D_FF // FFN_TF
    grid = (bsz // bb, t // tt, nj)
    nt = t // tt
    grid = ((bsz // bb) * nt, nj)
    xmap = lambda i, j: (i // nt, i % nt, 0)
    amap = lambda i, j: (i // nt, 0, 0)
    out_shape = [jax.ShapeDtypeStruct(x.shape, F32)]
    out_specs = [pl.BlockSpec((bb, tt, d), xmap)]
    if emit_next:
        out_shape.append(jax.ShapeDtypeStruct(x.shape, BF16))
        out_specs.append(pl.BlockSpec((bb, tt, d), xmap))
    res = pl.pallas_call(
        functools.partial(_ffn_kernel, sub=sub, emit_next=emit_next),
        grid=grid,
        in_specs=[pl.BlockSpec((bb, tt, d), xmap),
                  pl.BlockSpec((bb, N_ADA, d), amap),
                  pl.BlockSpec((3, d), lambda i, j: (0, 0)),
                  pl.BlockSpec((3, d), lambda i, j: (0, 0)),
                  pl.BlockSpec((d, FFN_TF), lambda i, j: (0, j)),
                  pl.BlockSpec((d, FFN_TF), lambda i, j: (0, j + nj)),
                  pl.BlockSpec((FFN_TF, d), lambda i, j: (j, 0))],
        out_specs=out_specs,
        out_shape=out_shape,
        scratch_shapes=[pltpu.VMEM((bb * tt, d), BF16), pltpu.VMEM((bb * tt, d), F32)],
        compiler_params=_cparams(("parallel", "arbitrary")),
        name=f"ffn{sub}",
    )(x, ada, ln_g, ln_b, wi, wi, wo)
    return res if emit_next else res[0]


def _mm_kernel(h_ref, w_ref, o_ref):
    o_ref[...] = jnp.dot(h_ref[...], w_ref[...], preferred_element_type=F32)


def _mm_call(h, w, *, tm, tn=512, name="proj"):
    m, k = h.shape
    n = w.shape[1]
    return pl.pallas_call(
        _mm_kernel,
        grid=(m // tm, n // tn),
        in_specs=[pl.BlockSpec((tm, k), lambda i, j: (i, 0)),
                  pl.BlockSpec((k, tn), lambda i, j: (0, j))],
        out_specs=pl.BlockSpec((tm, tn), lambda i, j: (i, j)),
        out_shape=jax.ShapeDtypeStruct((m, n), F32),
        compiler_params=_cparams(("parallel", "arbitrary")),
        name=name,
    )(h, w)


def _bias_tiles_kernel(tbl_ref, o_ref):
    h = pl.program_id(0)
    r = lax.broadcasted_iota(jnp.int32, (MOBA_BLOCK, MOBA_BLOCK), 0)
    c = lax.broadcasted_iota(jnp.int32, (MOBA_BLOCK, MOBA_BLOCK), 1)
    rel0 = r - c
    o_ref[0, 0] = jnp.where(rel0 >= 0, _t5_bias(rel0, tbl_ref, h), NEG_BIG)
    o_ref[0, 1] = _t5_bias(rel0 + MOBA_BLOCK, tbl_ref, h)


def _bias_tiles_call(rel_bias):
    return pl.pallas_call(
        _bias_tiles_kernel,
        grid=(H_ATT,),
        in_specs=[pl.BlockSpec(memory_space=pltpu.SMEM)],
        out_specs=pl.BlockSpec((1, 2, MOBA_BLOCK, MOBA_BLOCK), lambda h: (h, 0, 0, 0)),
        out_shape=jax.ShapeDtypeStruct((H_ATT, 2, MOBA_BLOCK, MOBA_BLOCK), F32),
        compiler_params=_cparams(("arbitrary",)),
        name="bias_tiles",
    )(rel_bias)


def _rank_select(s, n_valid):
    col = lax.broadcasted_iota(jnp.int32, s.shape, 1)
    cnt = jnp.zeros(s.shape, jnp.int32)
    for m in range(n_valid):
        sm = s[:, m:m + 1]
        beats = (sm > s) | ((sm == s) & (m < col))
        cnt = cnt + jnp.where(beats, 1, 0)
    return (col < n_valid) & (cnt < MOBA_TOPK)


def _attn_prompt_kernel(far_ref, q_ref, k_ref, v_ref, bt_ref, o_ref):
    h = pl.program_id(1)
    s_len = q_ref.shape[1]
    nb = s_len // MOBA_BLOCK
    q = q_ref[0]
    k = k_ref[0]
    kb = k.astype(BF16)
    vb = v_ref[0].astype(BF16)
    qb = (q * (HD_ATT ** -0.5)).astype(BF16)
    means = jnp.mean(k.reshape(nb, MOBA_BLOCK, HD_ATT), axis=1)
    means = jnp.concatenate([means, jnp.zeros((LANES - nb, HD_ATT), F32)], axis=0)
    mparts = _split(means, 3)
    far = far_ref[h]
    for i in range(nb):
        rows = slice(i * MOBA_BLOCK, (i + 1) * MOBA_BLOCK)
        nk = (i + 1) * MOBA_BLOCK
        logits = lax.dot_general(qb[rows], kb[:nk], _NT, preferred_element_type=F32)
        tiles = []
        if i > 0:
            score = _dotp(_split(q[rows], 3), mparts, _NT, order=3)
            sel = _rank_select(score, i)
            negs = jnp.where(sel, 0.0, NEG_BIG)
        for j in range(i + 1):
            tile = logits[:, j * MOBA_BLOCK:(j + 1) * MOBA_BLOCK]
            if j == i:
                tile = tile + bt_ref[0, 0]
            elif j == i - 1:
                tile = tile + bt_ref[0, 1] + negs[:, j:j + 1]
            else:
                tile = tile + (far + negs[:, j:j + 1])
            tiles.append(tile)
        lg = jnp.concatenate(tiles, axis=-1) if len(tiles) > 1 else tiles[0]
        mx = jnp.max(lg, axis=-1, keepdims=True)
        p = jnp.exp(lg - mx)
        den = jnp.sum(p, axis=-1, keepdims=True)
        out = jnp.dot(p.astype(BF16), vb[:nk], preferred_element_type=F32) / den
        o_ref[0, rows, :] = out.astype(o_ref.dtype)


def _attn_prompt_call(q, k, v, bias_tiles, far):
    bsz, s_len, _ = q.shape
    qspec = pl.BlockSpec((1, s_len, HD_ATT), lambda b, h: (b, 0, h))
    return pl.pallas_call(
        _attn_prompt_kernel,
        grid=(bsz, H_ATT),
        in_specs=[pl.BlockSpec(memory_space=pltpu.SMEM), qspec, qspec, qspec,
                  pl.BlockSpec((1, 2, MOBA_BLOCK, MOBA_BLOCK), lambda b, h: (h, 0, 0, 0))],
        out_specs=pl.BlockSpec((1, s_len, HD_ATT), lambda b, h: (b, 0, h)),
        out_shape=jax.ShapeDtypeStruct((bsz, s_len, C_ATT), BF16),
        compiler_params=_cparams(("parallel", "arbitrary")),
        name="attn_prompt",
    )(far, q, k, v, bias_tiles)


MEAN_BLOCKS = 8
PAGES_PER_BLOCK = MOBA_BLOCK // PAGE_SIZE


def _cache_means_kernel(pt_ref, *refs):
    page_refs, o_ref = refs[:-1], refs[-1]
    rows = []
    for m in range(MEAN_BLOCKS):
        tot = None
        for u in range(PAGES_PER_BLOCK):
            s = jnp.sum(page_refs[m * PAGES_PER_BLOCK + u][0], axis=0, keepdims=True)
            tot = s if tot is None else tot + s
        rows.append(tot * (1.0 / MOBA_BLOCK))
    o_ref[0] = jnp.concatenate(rows, axis=0)


def _cache_means_call(page_table, cache_k2, n_blocks):
    dbs = page_table.shape[0]
    npg = MEAN_BLOCKS * PAGES_PER_BLOCK
    in_specs = [pl.BlockSpec((1, PAGE_SIZE, C_ATT), functools.partial(
        lambda b, g, pt, u: (pt[b, g * npg + u], 0, 0), u=u)) for u in range(npg)]
    return pl.pallas_call(
        _cache_means_kernel,
        grid_spec=pltpu.PrefetchScalarGridSpec(
            num_scalar_prefetch=1,
            grid=(dbs, n_blocks // MEAN_BLOCKS),
            in_specs=in_specs,
            out_specs=pl.BlockSpec((1, MEAN_BLOCKS, C_ATT), lambda b, g, pt: (b, g, 0)),
        ),
        out_shape=jax.ShapeDtypeStruct((dbs, n_blocks, C_ATT), F32),
        compiler_params=_cparams(("parallel", "arbitrary")),
        name="cache_means",
    )(page_table, *([cache_k2] * npg))


def _topk_ids_kernel(q_ref, m_ref, o_ref):
    q = q_ref[0]
    means = m_ref[0]
    n_blocks = means.shape[0]
    ds = q.shape[0]
    pad = jnp.zeros((LANES - n_blocks, HD_ATT), F32)
    col = lax.broadcasted_iota(jnp.int32, (ds, LANES), 1)
    for h in range(H_ATT):
        sl = slice(h * HD_ATT, (h + 1) * HD_ATT)
        mh = jnp.concatenate([means[:, sl], pad], axis=0) if n_blocks < LANES else means[:, sl]
        s = _dotp(_split(q[:, sl], 3), _split(mh, 3), _NT, order=3)
        s = jnp.where(col < n_blocks, s, NEG_BIG)
        ids = jnp.zeros((ds, LANES), jnp.int32)
        for t in range(MOBA_TOPK):
            mx = jnp.max(s, axis=-1, keepdims=True)
            idx = jnp.min(jnp.where(s == mx, col, LANES), axis=-1, keepdims=True)
            ids = jnp.where(col == t, idx, ids)
            s = jnp.where(col == idx, NEG_BIG * 2, s)
        o_ref[0, h] = ids


def _topk_ids_call(q, means):
    dbs, ds, _ = q.shape
    n_blocks = means.shape[1]
    return pl.pallas_call(
        _topk_ids_kernel,
        grid=(dbs,),
        in_specs=[pl.BlockSpec((1, ds, C_ATT), lambda b: (b, 0, 0)),
                  pl.BlockSpec((1, n_blocks, C_ATT), lambda b: (b, 0, 0))],
        out_specs=pl.BlockSpec((1, H_ATT, ds, LANES), lambda b: (b, 0, 0, 0)),
        out_shape=jax.ShapeDtypeStruct((dbs, H_ATT, ds, LANES), jnp.int32),
        compiler_params=_cparams(("parallel",)),
        name="topk_ids",
    )(q, means)


N_SEL_PAGES = MOBA_TOPK * PAGES_PER_BLOCK


def _attn_sample_kernel(phys_ref, ids_ref, tbl_ref, q_ref, kn_ref, vn_ref, *refs, past_len):
    kp_refs = refs[:N_SEL_PAGES]
    vp_refs = refs[N_SEL_PAGES:2 * N_SEL_PAGES]
    o_ref = refs[2 * N_SEL_PAGES]
    b, h, qi = pl.program_id(0), pl.program_id(1), pl.program_id(2)
    ds = q_ref.shape[1]
    flat = (b * H_ATT + h) * ds + qi
    q_pos = past_len + qi
    qrow = lax.broadcasted_iota(jnp.int32, (ds, 1), 0)
    q1 = jnp.sum(jnp.where(qrow == qi, q_ref[0], 0.0), axis=0, keepdims=True) * (HD_ATT ** -0.5)
    qb = jnp.broadcast_to(q1, (8, HD_ATT)).astype(BF16)
    lane = lax.broadcasted_iota(jnp.int32, (8, PAGE_SIZE), 1)
    logit_tiles = []
    for s in range(MOBA_TOPK):
        blk = ids_ref[flat * MOBA_TOPK + s]
        for u in range(PAGES_PER_BLOCK):
            kpg = kp_refs[s * PAGES_PER_BLOCK + u][0].astype(BF16)
            lg = lax.dot_general(qb, kpg, _NT, preferred_element_type=F32)
            rel = q_pos - (blk * MOBA_BLOCK + u * PAGE_SIZE + lane)
            logit_tiles.append(lg + _t5_bias(rel, tbl_ref, h))
    kn = kn_ref[0].astype(BF16)
    lo = lax.dot_general(qb, kn, _NT, preferred_element_type=F32)
    jn = lax.broadcasted_iota(jnp.int32, (8, ds), 1)
    rel_o = qi - jn
    lo = jnp.where(rel_o >= 0, lo + _t5_bias(rel_o, tbl_ref, h), NEG_BIG)
    mx = jnp.max(lo, axis=-1, keepdims=True)
    for t in logit_tiles:
        mx = jnp.maximum(mx, jnp.max(t, axis=-1, keepdims=True))
    po = jnp.exp(lo - mx)
    den = jnp.sum(po, axis=-1, keepdims=True)
    acc = jnp.dot(po.astype(BF16), vn_ref[0].astype(BF16), preferred_element_type=F32)
    for i, t in enumerate(logit_tiles):
        p = jnp.exp(t - mx)
        den = den + jnp.sum(p, axis=-1, keepdims=True)
        acc = acc + jnp.dot(p.astype(BF16), vp_refs[i][0].astype(BF16), preferred_element_type=F32)
    out = acc / den

    @pl.when(qi == 0)
    def _():
        o_ref[0] = jnp.zeros(o_ref.shape[1:], o_ref.dtype)

    o_ref[0] = jnp.where(qrow == qi, out, o_ref[0])


def _attn_sample_call(phys, ids, rel_bias, q, k_new, v_new, cache_k2, cache_v2, past_len):
    dbs, ds, _ = q.shape

    def pmap(b, h, qi, phys_ref, ids_ref, u):
        return (phys_ref[((b * H_ATT + h) * ds + qi) * N_SEL_PAGES + u], 0, h)

    page_specs = [pl.BlockSpec((1, PAGE_SIZE, HD_ATT), functools.partial(pmap, u=u)) for u in range(N_SEL_PAGES)]
    nspec = pl.BlockSpec((1, ds, HD_ATT), lambda b, h, qi, p, i: (b, 0, h))
    return pl.pallas_call(
        functools.partial(_attn_sample_kernel, past_len=past_len),
        grid_spec=pltpu.PrefetchScalarGridSpec(
            num_scalar_prefetch=2,
            grid=(dbs, H_ATT, ds),
            in_specs=[pl.BlockSpec(memory_space=pltpu.SMEM), nspec, nspec, nspec] + page_specs + page_specs,
            out_specs=pl.BlockSpec((1, ds, HD_ATT), lambda b, h, qi, p, i: (b, 0, h)),
        ),
        out_shape=jax.ShapeDtypeStruct((dbs, ds, C_ATT), F32),
        compiler_params=_cparams(("parallel", "arbitrary", "arbitrary")),
        name="attn_sample",
    )(phys, ids, rel_bias, q, k_new, v_new, *([cache_k2] * N_SEL_PAGES), *([cache_v2] * N_SEL_PAGES))


def _rwkv_pre_kernel(z_ref, zp_ref, sh0_ref, mu_ref, w0_ref, w2_ref, a0_ref, a2_ref, g2_ref, kk_ref, ka_ref,
                     r_o, k_o, v_o, kk_o, b_o, ld_o, g_o):
    i = pl.program_id(1)
    z = z_ref[0]
    tm = z.shape[0]
    first = jnp.where(i == 0, sh0_ref[0], zp_ref[0, 7:8, :])
    row = lax.broadcasted_iota(jnp.int32, (tm, 1), 0)
    zprev = jnp.where(row == 0, first, pltpu.roll(z, 1, 0))
    zs = z + (zprev - z) * mu_ref[...]
    c = C_RWKV
    r, k, v = zs[:, :c], zs[:, c:2 * c], zs[:, 2 * c:3 * c]
    zw = zs[:, 3 * c:3 * c + LORA_PAD]
    za = zs[:, 3 * c + LORA_PAD:3 * c + 2 * LORA_PAD]
    zg = zs[:, 3 * c + 2 * LORA_PAD:]
    wl = w0_ref[...] + jnp.dot(jnp.tanh(zw).astype(BF16), w2_ref[...], preferred_element_type=F32)
    w = -(jnp.maximum(-wl, 0.0) + jnp.log(1.0 + jnp.exp(-jnp.abs(wl)))) - 0.5
    ld_o[0] = -jnp.exp(w)
    a = jax.nn.sigmoid(a0_ref[...] + jnp.dot(za.astype(BF16), a2_ref[...], preferred_element_type=F32))
    g_o[0] = jnp.dot(jax.nn.sigmoid(zg).astype(BF16), g2_ref[...], preferred_element_type=F32)
    kkr = k * kk_ref[...]
    n2 = _segsum64(kkr * kkr, _seg_ones())
    kkn = kkr / jnp.maximum(jnp.sqrt(n2), 1e-12)
    r_o[0] = r
    k_o[0] = k * (1.0 + (a - 1.0) * ka_ref[...])
    v_o[0] = v
    kk_o[0] = kkn
    b_o[0] = kkn * a


def _rwkv_pre_call(z, shift0, p, *, tm):
    bsz, t, zp = z.shape
    nt = t // tm
    c = C_RWKV
    row = lambda b, i: (0, 0)
    vec = lambda n: pl.BlockSpec((1, n), row)
    tile = pl.BlockSpec((1, tm, c), lambda b, i: (b, i, 0))
    return pl.pallas_call(
        _rwkv_pre_kernel,
        grid=(bsz, nt),
        in_specs=[pl.BlockSpec((1, tm, zp), lambda b, i: (b, i, 0)),
                  pl.BlockSpec((1, 8, zp), lambda b, i: (b, jnp.maximum(i * (tm // 8) - 1, 0), 0)),
                  pl.BlockSpec((1, 1, zp), lambda b, i: (b, 0, 0)),
                  vec(zp), vec(c),
                  pl.BlockSpec((LORA_PAD, c), row), vec(c),
                  pl.BlockSpec((LORA_PAD, c), row),
                  pl.BlockSpec((D_GATE_LORA, c), row), vec(c), vec(c)],
        out_specs=[tile] * 7,
        out_shape=[jax.ShapeDtypeStruct((bsz, t, c), F32)] * 7,
        compiler_params=_cparams(("parallel", "arbitrary")),
        name="rwkv_pre",
    )(z, z, shift0.reshape(bsz, 1, zp), p["mu"], p["w0"], p["w2"], p["a0"], p["a2"], p["g2"], p["k_k"], p["k_a"])


def _chunk_pair(r, k, v, kk, b, logd, s_blk):
    c = CHUNK
    lane = lax.broadcasted_iota(jnp.int32, (1, LANES), 1)
    m_a = jnp.where(lane < HS_RWKV, 1.0, 0.0)
    m_b = 1.0 - m_a

    def stack2(x):
        return jnp.concatenate([x * m_a, x * m_b], axis=0)

    def pm(mcat, x):
        return _dotp(_split(mcat, 2), _split(stack2(x), 2), _NN, order=2)

    row = lax.broadcasted_iota(jnp.int32, (c, 2 * c), 0)
    coli = lax.broadcasted_iota(jnp.int32, (c, 2 * c), 1) % c
    strict = jnp.where(coli < row, 1.0, 0.0)
    incl = jnp.where(coli <= row, 1.0, 0.0)
    lr = lax.broadcasted_iota(jnp.int32, (c, c), 0)
    lc = lax.broadcasted_iota(jnp.int32, (c, c), 1)
    ltri = jnp.where(lc <= lr, 1.0, 0.0).astype(BF16)
    ld_parts = _split(logd, 3)
    cum = _dotp([ltri], ld_parts, _NN)
    gcol = jnp.exp(_dotp(ld_parts, [jnp.ones((c, LANES), BF16)], _TN))
    g_inv = jnp.exp(-cum)
    g_end = jnp.exp(cum[c - 1:c, :] - cum)
    kt = kk * jnp.exp(cum - logd)
    rt = r * jnp.exp(cum)
    p_all = jnp.concatenate([kt, rt], axis=0)
    z2 = jnp.concatenate([stack2(k * g_inv), stack2(b * g_inv)], axis=0)
    g4 = _dotp(_split(p_all, 2), _split(z2, 2), _NT, order=2)
    mk = g4[:c, :2 * c] * strict
    mb = g4[:c, 2 * c:] * strict
    ak = g4[c:, :2 * c] * incl
    ab = g4[c:, 2 * c:] * incl
    ps = _dotp(_split(p_all, 2), _split(s_blk, 2), _NN, order=2)
    rhs = ps[:c] + pm(mk, v)
    pj = -mb
    u = rhs + pm(pj, rhs)
    n = 2
    while n < c:
        pj = pm(pj, pj)
        u = u + pm(pj, u)
        n *= 2
    y = ps[c:] + _dotp(_split(jnp.concatenate([ak, -ab], axis=1), 2),
                       _split(jnp.concatenate([stack2(v), stack2(u)], axis=0), 2), _NN, order=2)
    rr = lax.broadcasted_iota(jnp.int32, (LANES, LANES), 0) // HS_RWKV
    cc = lax.broadcasted_iota(jnp.int32, (LANES, LANES), 1) // HS_RWKV
    upd = _dotp(_split(jnp.concatenate([k * g_end, -(b * g_end)], axis=0), 2),
                _split(jnp.concatenate([v, u], axis=0), 2), _TN, order=2)
    s_new = gcol * s_blk + jnp.where(rr == cc, upd, 0.0)
    return y, s_new


def _rwkv_chunk_kernel(r_ref, k_ref, v_ref, kk_ref, b_ref, ld_ref, s0_ref, y_ref, so_ref, s_scr):
    ci = pl.program_id(2)

    @pl.when(ci == 0)
    def _():
        s_scr[...] = s0_ref[0]

    for p in range(PAIRS_PER_STEP):
        sl = slice(p * LANES, (p + 1) * LANES)
        y, s_new = _chunk_pair(r_ref[0, :, sl], k_ref[0, :, sl], v_ref[0, :, sl], kk_ref[0, :, sl],
                               b_ref[0, :, sl], ld_ref[0, :, sl], s_scr[p])
        y_ref[0, :, sl] = y
        s_scr[p] = s_new

    @pl.when(ci == pl.num_programs(2) - 1)
    def _():
        so_ref[0] = s_scr[...]


def _rwkv_chunk_call(r, k, v, kk, b, logd, s0_blk):
    bsz, t, c = r.shape
    pp = PAIRS_PER_STEP
    w = pp * LANES
    tile = pl.BlockSpec((1, CHUNK, w), lambda bi, p, ci: (bi, ci, p))
    sspec = pl.BlockSpec((1, pp, LANES, LANES), lambda bi, p, ci: (bi, p, 0, 0))
    return pl.pallas_call(
        _rwkv_chunk_kernel,
        grid=(bsz, c // w, t // CHUNK),
        in_specs=[tile] * 6 + [sspec],
        out_specs=[tile, sspec],
        out_shape=[jax.ShapeDtypeStruct((bsz, t, c), F32),
                   jax.ShapeDtypeStruct(s0_blk.shape, F32)],
        scratch_shapes=[pltpu.VMEM((pp, LANES, LANES), F32)],
        compiler_params=_cparams(("parallel", "parallel", "arbitrary")),
        name="rwkv_chunk",
    )(r, k, v, kk, b, logd, s0_blk)


def _rwkv_post_kernel(y_ref, r_ref, k_ref, v_ref, g_ref, rk_ref, lg_ref, lb_ref, o_ref):
    p128 = _seg_ones()
    y = y_ref[...]
    inv = 1.0 / HS_RWKV
    mu = _segsum64(y, p128) * inv
    yc = y - mu
    var = _segsum64(yc * yc, p128) * inv
    yn = yc * lax.rsqrt(var + GN_EPS) * lg_ref[...] + lb_ref[...]
    bonus = _segsum64(r_ref[...] * k_ref[...] * rk_ref[...], p128) * v_ref[...]
    o_ref[...] = ((yn + bonus) * g_ref[...]).astype(o_ref.dtype)


def _rwkv_post_call(y, r, k, v, g, p, *, tm):
    m, c = y.shape
    tile = pl.BlockSpec((tm, c), lambda i: (i, 0))
    vec = pl.BlockSpec((1, c), lambda i: (0, 0))
    return pl.pallas_call(
        _rwkv_post_kernel,
        grid=(m // tm,),
        in_specs=[tile] * 5 + [vec] * 3,
        out_specs=tile,
        out_shape=jax.ShapeDtypeStruct((m, c), BF16),
        compiler_params=_cparams(("parallel",)),
        name="rwkv_post",
    )(y, r, k, v, g, p["r_k"], p["lnx_g"], p["lnx_b"])


def _combine_kernel(x_ref, ada_ref, lng_ref, lnb_ref, oa_ref, or_ref, ga_ref, gr_ref, wa_ref, wr_ref, wo_ref, o_ref):
    bb, t, d = x_ref.shape
    ua = jnp.dot(oa_ref[...], wa_ref[...], preferred_element_type=F32)
    ur = jnp.dot(or_ref[...], wr_ref[...], preferred_element_type=F32)
    m = jax.nn.sigmoid(ga_ref[...]) * ua + jax.nn.sigmoid(gr_ref[...]) * ur
    mo = jnp.dot(m.astype(BF16), wo_ref[...], preferred_element_type=F32)
    y = DEEPNORM_ALPHA * x_ref[...] + ada_ref[:, 5:6, :] * mo.reshape(bb, t, d)
    o_ref[...] = _ln(y) * lng_ref[1:2, :] + lnb_ref[1:2, :]


def _combine_call(x, ada, ln_g, ln_b, oa, orw, ga, gr, wa, wr, wo, *, blk):
    bsz, t, d = x.shape
    bb, tt = blk
    nt = t // tt
    rows = bb * tt
    xmap = lambda i: (i // nt, i % nt, 0)
    const = lambda i: (0, 0)
    rowt = lambda n: pl.BlockSpec((rows, n), lambda i: (i, 0))
    return pl.pallas_call(
        _combine_kernel,
        grid=((bsz // bb) * nt,),
        in_specs=[pl.BlockSpec((bb, tt, d), xmap),
                  pl.BlockSpec((bb, N_ADA, d), lambda i: (i // nt, 0, 0)),
                  pl.BlockSpec((3, d), const), pl.BlockSpec((3, d), const),
                  rowt(C_ATT), rowt(C_RWKV), rowt(d), rowt(d),
                  pl.BlockSpec((C_ATT, d), const), pl.BlockSpec((C_RWKV, d), const), pl.BlockSpec((d, d), const)],
        out_specs=pl.BlockSpec((bb, tt, d), xmap),
        out_shape=jax.ShapeDtypeStruct(x.shape, F32),
        compiler_params=_cparams(("parallel",)),
        name="combine",
    )(x, ada, ln_g, ln_b, oa, orw, ga, gr, wa, wr, wo)


def _rearrange_z(a):
    c3 = 3 * C_RWKV
    pad = [(0, 0)] * (a.ndim - 1) + [(0, LORA_PAD - D_DECAY_LORA)]
    return jnp.concatenate([a[..., :c3],
                            jnp.pad(a[..., c3:c3 + D_DECAY_LORA], pad),
                            jnp.pad(a[..., c3 + D_DECAY_LORA:c3 + D_DECAY_LORA + D_AAA_LORA], pad),
                            a[..., c3 + D_DECAY_LORA + D_AAA_LORA:]], axis=-1)


def _unarrange_z(a):
    c3 = 3 * C_RWKV
    return jnp.concatenate([a[..., :c3], a[..., c3:c3 + D_DECAY_LORA],
                            a[..., c3 + LORA_PAD:c3 + LORA_PAD + D_AAA_LORA], a[..., c3 + 2 * LORA_PAD:]], axis=-1)


def _state_to_blocks(state):
    bsz = state.shape[0]
    st = jnp.swapaxes(state, -1, -2).reshape(bsz, H_RWKV // 2, 2, HS_RWKV, HS_RWKV)
    blk = jnp.einsum('bphkv,hg->bphkgv', st, jnp.eye(2, dtype=state.dtype))
    return blk.reshape(bsz, H_RWKV // 2, LANES, LANES)


def _blocks_to_state(blk):
    bsz = blk.shape[0]
    b6 = blk.reshape(bsz, H_RWKV // 2, 2, HS_RWKV, 2, HS_RWKV)
    st = jnp.stack([b6[:, :, 0, :, 0, :], b6[:, :, 1, :, 1, :]], axis=2)
    return jnp.swapaxes(st, -1, -2).reshape(bsz, H_RWKV, HS_RWKV, HS_RWKV)


def _trunk(x, ada, attend, shift0, wkv0, w, *, ffn_blk, mm_tm, pre_tm, post_tm, comb_blk):
    bsz, t, d = x.shape
    m = bsz * t
    x1, h2 = _ffn_call(x, ada, w["ln_g"], w["ln_b"], w["wi0"], w["wo0"], sub=0, emit_next=True, blk=ffn_blk)
    h2f = h2.reshape(m, d)
    q = _mm_call(h2f, w["wq"], tm=mm_tm, name="proj_q").reshape(bsz, t, C_ATT)
    k = _mm_call(h2f, w["wk"], tm=mm_tm, name="proj_k").reshape(bsz, t, C_ATT)
    v = _mm_call(h2f, w["wv"], tm=mm_tm, name="proj_v").reshape(bsz, t, C_ATT)
    z = _mm_call(h2f, w["wz"], tm=mm_tm, name="proj_z").reshape(bsz, t, Z_PAD)
    ga = _mm_call(h2f, w["wga"], tm=mm_tm, name="proj_ga")
    gr = _mm_call(h2f, w["wgr"], tm=mm_tm, name="proj_gr")
    o_att = attend(q, k, v)
    r, k2, vv, kk, b, logd, g = _rwkv_pre_call(z, _rearrange_z(shift0), w, tm=pre_tm)
    tp = -(-t // CHUNK) * CHUNK
    seq = [r, k2, vv, kk, b, logd]
    if tp != t:
        seq = [jnp.pad(a, ((0, 0), (0, tp - t), (0, 0))) for a in seq]
    y, s_blk = _rwkv_chunk_call(*seq, _state_to_blocks(wkv0))
    y = y[:, :t]
    fl = lambda a: a.reshape(m, C_RWKV)
    o_rwkv = _rwkv_post_call(fl(y), fl(r), fl(k2), fl(vv), fl(g), w, tm=post_tm)
    x2 = _combine_call(x1, ada, w["ln_g"], w["ln_b"], o_att.reshape(m, C_ATT), o_rwkv, ga, gr,
                       w["wua"], w["wur"], w["wout"], blk=comb_blk)
    x3 = _ffn_call(x2, ada, w["ln_g"], w["ln_b"], w["wi1"], w["wo1"], sub=2, emit_next=False, blk=ffn_blk)
    shift_new = _unarrange_z(z[:, -1])
    return (x3, k.reshape(bsz, t, H_ATT, HD_ATT), v.reshape(bsz, t, H_ATT, HD_ATT), _blocks_to_state(s_blk), shift_new)


def kernel(x_prompt, x_sample, cache_k, cache_v, state_wkv, state_shift, page_table, c_prompt, c_sample, rel_bias, w_ada, b_ada, ln_g, ln_b, ffn_wi, ffn_wo, w_in, mu_shift, w0, w2, a0, a2, g2, k_k, k_a, r_k, lnx_g, lnx_b, w_up_attn, w_up_rwkv, w_out):
    assert w_ada.shape[0] == DEPTH == 1
    bsz, s_len, d = x_prompt.shape
    dbs, ds, _ = x_sample.shape
    past_len = page_table.shape[1] * PAGE_SIZE
    n_phys = cache_k.shape[1]
    l = 0
    win = w_in[l]
    c3 = 3 * C_ATT
    lora_rows = ((0, LORA_PAD - D_DECAY_LORA), (0, 0))
    w = {
        "ln_g": ln_g[l], "ln_b": ln_b[l],
        "wi0": ffn_wi[l, 0].astype(BF16), "wo0": ffn_wo[l, 0].astype(BF16),
        "wi1": ffn_wi[l, 1].astype(BF16), "wo1": ffn_wo[l, 1].astype(BF16),
        "wq": win[:, :C_ATT].astype(BF16), "wk": win[:, C_ATT:2 * C_ATT].astype(BF16),
        "wv": win[:, 2 * C_ATT:c3].astype(BF16),
        "wz": _rearrange_z(win[:, c3:c3 + RWKV_PROJ]).astype(BF16),
        "wga": win[:, c3 + RWKV_PROJ:c3 + RWKV_PROJ + d].astype(BF16),
        "wgr": win[:, c3 + RWKV_PROJ + d:].astype(BF16),
        "mu": _rearrange_z(mu_shift[l])[None, :],
        "w0": w0[l][None, :], "w2": jnp.pad(w2[l], lora_rows).astype(BF16),
        "a0": a0[l][None, :], "a2": jnp.pad(a2[l], lora_rows).astype(BF16),
        "g2": g2[l].astype(BF16), "k_k": k_k[l][None, :], "k_a": k_a[l][None, :],
        "r_k": r_k[l].reshape(1, C_RWKV), "lnx_g": lnx_g[l][None, :], "lnx_b": lnx_b[l][None, :],
        "wua": w_up_attn[l].astype(BF16), "wur": w_up_rwkv[l].astype(BF16), "wout": w_out[l].astype(BF16),
    }
    n_c = bsz + dbs
    c_rows = -(-n_c // 8) * 8
    c_all = jnp.concatenate([c_prompt, c_sample, jnp.zeros((c_rows - n_c, d), F32)], axis=0)
    ada = _ada_call(c_all, w_ada[l], b_ada[l]).reshape(c_rows, N_ADA, d)

    bias_tiles = _bias_tiles_call(rel_bias)
    far = rel_bias[NUM_BUCKETS - 1]
    attend_p = lambda q, k, v: _attn_prompt_call(q, k, v, bias_tiles, far)
    yp, kp, vp, wp, sp = _trunk(
        x_prompt, ada[:bsz], attend_p, jnp.zeros((bsz, RWKV_PROJ), F32),
        jnp.zeros((bsz, H_RWKV, HS_RWKV, HS_RWKV), F32), w,
        ffn_blk=(1, 512), mm_tm=1024, pre_tm=256, post_tm=256, comb_blk=(1, 256))

    cache_k2 = cache_k[l].reshape(n_phys, PAGE_SIZE, C_ATT)
    cache_v2 = cache_v[l].reshape(n_phys, PAGE_SIZE, C_ATT)
    n_full = past_len // MOBA_BLOCK

    def attend_s(q, k, v):
        means = _cache_means_call(page_table, cache_k2, n_full)
        ids = _topk_ids_call(q, means)[..., :MOBA_TOPK]
        b_idx = jnp.arange(dbs)[:, None, None, None, None]
        pg = ids[..., None] * PAGES_PER_BLOCK + jnp.arange(PAGES_PER_BLOCK)
        phys = page_table[b_idx, pg]
        o = _attn_sample_call(phys.reshape(-1), ids.reshape(-1), rel_bias, q, k, v, cache_k2, cache_v2, past_len)
        return o.astype(BF16)

    ys, kn, vn, wn, sn = _trunk(
        x_sample, ada[bsz:n_c], attend_s, state_shift[l], state_wkv[l], w,
        ffn_blk=(dbs, ds), mm_tm=dbs * ds, pre_tm=ds, post_tm=dbs * ds, comb_blk=(dbs, ds))

    return (yp, ys, kp[None], vp[None], kn[None], vn[None], wp[None], wn[None], sp[None], sn[None])
```

```python
import functools
import math

import jax
import jax.numpy as jnp
from jax import lax
from jax.experimental import pallas as pl
from jax.experimental.pallas import tpu as pltpu

F32 = jnp.float32
BF16 = jnp.bfloat16

D_MODEL = 2048
D_FF = 5632
N_ADA = 9
H_ATT = 8
HD_ATT = 128
C_ATT = H_ATT * HD_ATT
MOBA_BLOCK = 256
MOBA_TOPK = 3
NUM_BUCKETS = 32
MAX_DISTANCE = 128
PAGE_SIZE = 128
HS_RWKV = 64
C_RWKV = 1024
H_RWKV = C_RWKV // HS_RWKV
D_DECAY_LORA = 96
D_AAA_LORA = 96
D_GATE_LORA = 256
RWKV_PROJ = 3 * C_RWKV + D_DECAY_LORA + D_AAA_LORA + D_GATE_LORA
LORA_PAD = 128
Z_PAD = 3 * C_RWKV + 2 * LORA_PAD + D_GATE_LORA
GN_EPS = 64e-5
LN_EPS = 1e-5
DEPTH = 1
DEEPNORM_ALPHA = (2 * DEPTH) ** 0.25
NEG_BIG = -1e30

LANES = 128
CHUNK = 64
PAIRS_PER_STEP = 8
VMEM_LIMIT = 56 * 1024 * 1024


def _t5_thresholds():
    max_exact = NUM_BUCKETS // 2
    thr = list(range(1, max_exact + 1))
    for b in range(max_exact + 1, NUM_BUCKETS):
        x = max_exact * (MAX_DISTANCE / max_exact) ** ((b - max_exact) / (NUM_BUCKETS - max_exact))
        thr.append(int(math.ceil(x)))
    return tuple(thr)


T5_THR = _t5_thresholds()


def _cparams(sem, vmem=VMEM_LIMIT):
    return pltpu.CompilerParams(dimension_semantics=sem, vmem_limit_bytes=vmem)


def _ln(x):
    mu = jnp.mean(x, axis=-1, keepdims=True)
    xc = x - mu
    var = jnp.mean(xc * xc, axis=-1, keepdims=True)
    return xc * lax.rsqrt(var + LN_EPS)


def _split(x, n):
    parts = []
    for i in range(n):
        p = x.astype(BF16)
        parts.append(p)
        if i + 1 < n:
            x = x - p.astype(F32)
    return parts


_NN = (((1,), (0,)), ((), ()))
_NT = (((1,), (1,)), ((), ()))
_TN = (((0,), (0,)), ((), ()))


def _dotp(ap, bp, dims=_NN, order=None):
    if order is None:
        order = max(len(ap), len(bp))
    acc = None
    for i, a in enumerate(ap):
        for j, b in enumerate(bp):
            if i + j < order:
                t = lax.dot_general(a, b, dims, preferred_element_type=F32)
                acc = t if acc is None else acc + t
    return acc


def _t5_bias(rel, tbl_ref, h):
    bias = jnp.full(rel.shape, tbl_ref[0, h], F32)
    for b in range(1, NUM_BUCKETS):
        bias = jnp.where(rel >= T5_THR[b - 1], tbl_ref[b, h], bias)
    return bias


def _seg_ones():
    r = lax.broadcasted_iota(jnp.int32, (LANES, LANES), 0) // HS_RWKV
    c = lax.broadcasted_iota(jnp.int32, (LANES, LANES), 1) // HS_RWKV
    return jnp.where(r == c, 1.0, 0.0).astype(BF16)


def _segsum64(x, p128):
    outs = []
    for j in range(x.shape[-1] // LANES):
        xs = x[:, j * LANES:(j + 1) * LANES]
        outs.append(_dotp(_split(xs, 3), [p128]))
    return jnp.concatenate(outs, axis=-1)


ADA_TN = 1024


def _ada_kernel(c_ref, w_ref, b_ref, o_ref):
    c = c_ref[...]
    sc = c * jax.nn.sigmoid(c)
    rows = sc.shape[0]
    s0, s1, s2 = _split(sc, 3)
    w_hi, w_lo = _split(w_ref[...], 2)
    a = jnp.dot(jnp.concatenate([s0, s1, s2], axis=0), w_hi, preferred_element_type=F32)
    b = jnp.dot(jnp.concatenate([s0, s1], axis=0), w_lo, preferred_element_type=F32)
    o_ref[...] = (a[:rows] + a[rows:2 * rows] + a[2 * rows:] + b[:rows] + b[rows:]) + b_ref[...]


def _ada_call(c_all, w_ada, b_ada):
    rows, d = c_all.shape
    n = w_ada.shape[1]
    return pl.pallas_call(
        _ada_kernel,
        grid=(n // ADA_TN,),
        in_specs=[pl.BlockSpec((rows, d), lambda j: (0, 0)),
                  pl.BlockSpec((d, ADA_TN), lambda j: (0, j)),
                  pl.BlockSpec((1, ADA_TN), lambda j: (0, j))],
        out_specs=pl.BlockSpec((rows, ADA_TN), lambda j: (0, j)),
        out_shape=jax.ShapeDtypeStruct((rows, n), F32),
        compiler_params=_cparams(("parallel",)),
        name="ada",
    )(c_all, w_ada, b_ada.reshape(1, n))


FFN_TF = 512


def _ffn_kernel(x_ref, ada_ref, lng_ref, lnb_ref, wig_ref, wiu_ref, wo_ref, *rest, sub, emit_next):
    if emit_next:
        o_ref, h_next_ref, h_scr, acc = rest
    else:
        o_ref, h_scr, acc = rest
    bb, t, d = x_ref.shape
    j = pl.program_id(1)

    @pl.when(j == 0)
    def _():
        h = _ln(x_ref[...]) * (1.0 + ada_ref[:, 3 * sub + 1:3 * sub + 2, :]) + ada_ref[:, 3 * sub:3 * sub + 1, :]
        h_scr[...] = h.reshape(bb * t, d).astype(BF16)
        acc[...] = jnp.zeros_like(acc)

    h = h_scr[...]
    g = jnp.dot(h, wig_ref[...], preferred_element_type=F32)
    u = jnp.dot(h, wiu_ref[...], preferred_element_type=F32)
    act = (g * jax.nn.sigmoid(g)) * u
    acc[...] += jnp.dot(act.astype(BF16), wo_ref[...], preferred_element_type=F32)

    @pl.when(j == pl.num_programs(1) - 1)
    def _():
        gate = ada_ref[:, 3 * sub + 2:3 * sub + 3, :]
        y = DEEPNORM_ALPHA * x_ref[...] + 0.5 * gate * acc[...].reshape(bb, t, d)
        y = _ln(y) * lng_ref[sub:sub + 1, :] + lnb_ref[sub:sub + 1, :]
        o_ref[...] = y
        if emit_next:
            nxt = sub + 1
            hn = _ln(y) * (1.0 + ada_ref[:, 3 * nxt + 1:3 * nxt + 2, :]) + ada_ref[:, 3 * nxt:3 * nxt + 1, :]
            h_next_ref[...] = hn.astype(BF16)


def _ffn_call(x, ada, ln_g, ln_b, wi, wo, *, sub, emit_next, blk):
    bsz, t, d = x.shape
    bb, tt = blk
    nj = D_FF // FFN_TF
    grid = (bsz // bb, t // tt, nj)
    nt = t // tt
    grid = ((bsz // bb) * nt, nj)
    xmap = lambda i, j: (i // nt, i % nt, 0)
    amap = lambda i, j: (i // nt, 0, 0)
    out_shape = [jax.ShapeDtypeStruct(x.shape, F32)]
    out_specs = [pl.BlockSpec((bb, tt, d), xmap)]
    if emit_next:
        out_shape.append(jax.ShapeDtypeStruct(x.shape, BF16))
        out_specs.append(pl.BlockSpec((bb, tt, d), xmap))
    res = pl.pallas_call(
        functools.partial(_ffn_kernel, sub=sub, emit_next=emit_next),
        grid=grid,
        in_specs=[pl.BlockSpec((bb, tt, d), xmap),
                  pl.BlockSpec((bb, N_ADA, d), amap),
                  pl.BlockSpec((3, d), lambda i, j: (0, 0)),
                  pl.BlockSpec((3, d), lambda i, j: (0, 0)),
                  pl.BlockSpec((d, FFN_TF), lambda i, j: (0, j)),
                  pl.BlockSpec((d, FFN_TF), lambda i, j: (0, j + nj)),
                  pl.BlockSpec((FFN_TF, d), lambda i, j: (j, 0))],
        out_specs=out_specs,
        out_shape=out_shape,
        scratch_shapes=[pltpu.VMEM((bb * tt, d), BF16), pltpu.VMEM((bb * tt, d), F32)],
        compiler_params=_cparams(("parallel", "arbitrary")),
        name=f"ffn{sub}",
    )(x, ada, ln_g, ln_b, wi, wi, wo)
    return res if emit_next else res[0]


def _mm_kernel(h_ref, w_ref, o_ref):
    o_ref[...] = jnp.dot(h_ref[...], w_ref[...], preferred_element_type=F32)


def _mm_call(h, w, *, tm, tn=512, name="proj"):
    m, k = h.shape
    n = w.shape[1]
    return pl.pallas_call(
        _mm_kernel,
        grid=(m // tm, n // tn),
        in_specs=[pl.BlockSpec((tm, k), lambda i, j: (i, 0)),
                  pl.BlockSpec((k, tn), lambda i, j: (0, j))],
        out_specs=pl.BlockSpec((tm, tn), lambda i, j: (i, j)),
        out_shape=jax.ShapeDtypeStruct((m, n), F32),
        compiler_params=_cparams(("parallel", "arbitrary")),
        name=name,
    )(h, w)


def _bias_tiles_kernel(tbl_ref, o_ref):
    h = pl.program_id(0)
    r = lax.broadcasted_iota(jnp.int32, (MOBA_BLOCK, MOBA_BLOCK), 0)
    c = lax.broadcasted_iota(jnp.int32, (MOBA_BLOCK, MOBA_BLOCK), 1)
    rel0 = r - c
    o_ref[0, 0] = jnp.where(rel0 >= 0, _t5_bias(rel0, tbl_ref, h), NEG_BIG)
    o_ref[0, 1] = _t5_bias(rel0 + MOBA_BLOCK, tbl_ref, h)


def _bias_tiles_call(rel_bias):
    return pl.pallas_call(
        _bias_tiles_kernel,
        grid=(H_ATT,),
        in_specs=[pl.BlockSpec(memory_space=pltpu.SMEM)],
        out_specs=pl.BlockSpec((1, 2, MOBA_BLOCK, MOBA_BLOCK), lambda h: (h, 0, 0, 0)),
        out_shape=jax.ShapeDtypeStruct((H_ATT, 2, MOBA_BLOCK, MOBA_BLOCK), F32),
        compiler_params=_cparams(("arbitrary",)),
        name="bias_tiles",
    )(rel_bias)


def _rank_select(s, n_valid):
    col = lax.broadcasted_iota(jnp.int32, s.shape, 1)
    cnt = jnp.zeros(s.shape, jnp.int32)
    for m in range(n_valid):
        sm = s[:, m:m + 1]
        beats = (sm > s) | ((sm == s) & (m < col))
        cnt = cnt + jnp.where(beats, 1, 0)
    return (col < n_valid) & (cnt < MOBA_TOPK)


def _attn_prompt_kernel(far_ref, q_ref, k_ref, v_ref, bt_ref, o_ref):
    h = pl.program_id(1)
    s_len = q_ref.shape[1]
    nb = s_len // MOBA_BLOCK
    q = q_ref[0]
    k = k_ref[0]
    kb = k.astype(BF16)
    vb = v_ref[0].astype(BF16)
    qb = (q * (HD_ATT ** -0.5)).astype(BF16)
    means = jnp.mean(k.reshape(nb, MOBA_BLOCK, HD_ATT), axis=1)
    means = jnp.concatenate([means, jnp.zeros((LANES - nb, HD_ATT), F32)], axis=0)
    mparts = _split(means, 3)
    far = far_ref[h]
    for i in range(nb):
        rows = slice(i * MOBA_BLOCK, (i + 1) * MOBA_BLOCK)
        nk = (i + 1) * MOBA_BLOCK
        logits = lax.dot_general(qb[rows], kb[:nk], _NT, preferred_element_type=F32)
        tiles = []
        if i > 0:
            score = _dotp(_split(q[rows], 3), mparts, _NT, order=3)
            sel = _rank_select(score, i)
            negs = jnp.where(sel, 0.0, NEG_BIG)
        for j in range(i + 1):
            tile = logits[:, j * MOBA_BLOCK:(j + 1) * MOBA_BLOCK]
            if j == i:
                tile = tile + bt_ref[0, 0]
            elif j == i - 1:
                tile = tile + bt_ref[0, 1] + negs[:, j:j + 1]
            else:
                tile = tile + (far + negs[:, j:j + 1])
            tiles.append(tile)
        lg = jnp.concatenate(tiles, axis=-1) if len(tiles) > 1 else tiles[0]
        mx = jnp.max(lg, axis=-1, keepdims=True)
        p = jnp.exp(lg - mx)
        den = jnp.sum(p, axis=-1, keepdims=True)
        out = jnp.dot(p.astype(BF16), vb[:nk], preferred_element_type=F32) / den
        o_ref[0, rows, :] = out.astype(o_ref.dtype)


def _attn_prompt_call(q, k, v, bias_tiles, far):
    bsz, s_len, _ = q.shape
    qspec = pl.BlockSpec((1, s_len, HD_ATT), lambda b, h: (b, 0, h))
    return pl.pallas_call(
        _attn_prompt_kernel,
        grid=(bsz, H_ATT),
        in_specs=[pl.BlockSpec(memory_space=pltpu.SMEM), qspec, qspec, qspec,
                  pl.BlockSpec((1, 2, MOBA_BLOCK, MOBA_BLOCK), lambda b, h: (h, 0, 0, 0))],
        out_specs=pl.BlockSpec((1, s_len, HD_ATT), lambda b, h: (b, 0, h)),
        out_shape=jax.ShapeDtypeStruct((bsz, s_len, C_ATT), BF16),
        compiler_params=_cparams(("parallel", "arbitrary")),
        name="attn_prompt",
    )(far, q, k, v, bias_tiles)


MEAN_BLOCKS = 8
PAGES_PER_BLOCK = MOBA_BLOCK // PAGE_SIZE


def _cache_means_kernel(pt_ref, *refs):
    page_refs, o_ref = refs[:-1], refs[-1]
    rows = []
    for m in range(MEAN_BLOCKS):
        tot = None
        for u in range(PAGES_PER_BLOCK):
            s = jnp.sum(page_refs[m * PAGES_PER_BLOCK + u][0], axis=0, keepdims=True)
            tot = s if tot is None else tot + s
        rows.append(tot * (1.0 / MOBA_BLOCK))
    o_ref[0] = jnp.concatenate(rows, axis=0)


def _cache_means_call(page_table, cache_k2, n_blocks):
    dbs = page_table.shape[0]
    npg = MEAN_BLOCKS * PAGES_PER_BLOCK
    in_specs = [pl.BlockSpec((1, PAGE_SIZE, C_ATT), functools.partial(
        lambda b, g, pt, u: (pt[b, g * npg + u], 0, 0), u=u)) for u in range(npg)]
    return pl.pallas_call(
        _cache_means_kernel,
        grid_spec=pltpu.PrefetchScalarGridSpec(
            num_scalar_prefetch=1,
            grid=(dbs, n_blocks // MEAN_BLOCKS),
            in_specs=in_specs,
            out_specs=pl.BlockSpec((1, MEAN_BLOCKS, C_ATT), lambda b, g, pt: (b, g, 0)),
        ),
        out_shape=jax.ShapeDtypeStruct((dbs, n_blocks, C_ATT), F32),
        compiler_params=_cparams(("parallel", "arbitrary")),
        name="cache_means",
    )(page_table, *([cache_k2] * npg))


def _topk_ids_kernel(q_ref, m_ref, o_ref):
    q = q_ref[0]
    means = m_ref[0]
    n_blocks = means.shape[0]
    ds = q.shape[0]
    pad = jnp.zeros((LANES - n_blocks, HD_ATT), F32)
    col = lax.broadcasted_iota(jnp.int32, (ds, LANES), 1)
    for h in range(H_ATT):
        sl = slice(h * HD_ATT, (h + 1) * HD_ATT)
        mh = jnp.concatenate([means[:, sl], pad], axis=0) if n_blocks < LANES else means[:, sl]
        s = _dotp(_split(q[:, sl], 3), _split(mh, 3), _NT, order=3)
        s = jnp.where(col < n_blocks, s, NEG_BIG)
        ids = jnp.zeros((ds, LANES), jnp.int32)
        for t in range(MOBA_TOPK):
            mx = jnp.max(s, axis=-1, keepdims=True)
            idx = jnp.min(jnp.where(s == mx, col, LANES), axis=-1, keepdims=True)
            ids = jnp.where(col == t, idx, ids)
            s = jnp.where(col == idx, NEG_BIG * 2, s)
        o_ref[0, h] = ids


def _topk_ids_call(q, means):
    dbs, ds, _ = q.shape
    n_blocks = means.shape[1]
    return pl.pallas_call(
        _topk_ids_kernel,
        grid=(dbs,),
        in_specs=[pl.BlockSpec((1, ds, C_ATT), lambda b: (b, 0, 0)),
                  pl.BlockSpec((1, n_blocks, C_ATT), lambda b: (b, 0, 0))],
        out_specs=pl.BlockSpec((1, H_ATT, ds, LANES), lambda b: (b, 0, 0, 0)),
        out_shape=jax.ShapeDtypeStruct((dbs, H_ATT, ds, LANES), jnp.int32),
        compiler_params=_cparams(("parallel",)),
        name="topk_ids",
    )(q, means)


N_SEL_PAGES = MOBA_TOPK * PAGES_PER_BLOCK


def _attn_sample_kernel(phys_ref, ids_ref, tbl_ref, q_ref, kn_ref, vn_ref, *refs, past_len):
    ds = q_ref.shape[1]
    n_pg = ds * N_SEL_PAGES
    kp_refs = refs[:n_pg]
    vp_refs = refs[n_pg:2 * n_pg]
    o_ref = refs[2 * n_pg]
    b, h = pl.program_id(0), pl.program_id(1)
    qb = (q_ref[0] * (HD_ATT ** -0.5)).astype(BF16)
    qrow = lax.broadcasted_iota(jnp.int32, (ds, 1), 0)
    lane = lax.broadcasted_iota(jnp.int32, (ds, PAGE_SIZE), 1)
    lo = lax.dot_general(qb, kn_ref[0].astype(BF16), _NT, preferred_element_type=F32)
    rel_o = lax.broadcasted_iota(jnp.int32, (ds, ds), 0) - lax.broadcasted_iota(jnp.int32, (ds, ds), 1)
    lo = jnp.where(rel_o >= 0, lo + _t5_bias(rel_o, tbl_ref, h), NEG_BIG)
    mx_o = jnp.max(lo, axis=-1, keepdims=True)
    vn = vn_ref[0].astype(BF16)
    tiles = []
    for qi in range(ds):
        flat = (b * H_ATT + h) * ds + qi
        tq = []
        for s in range(MOBA_TOPK):
            blk = ids_ref[flat * MOBA_TOPK + s]
            for u in range(PAGES_PER_BLOCK):
                kpg = kp_refs[(qi * MOBA_TOPK + s) * PAGES_PER_BLOCK + u][0].astype(BF16)
                lg = lax.dot_general(qb, kpg, _NT, preferred_element_type=F32)
                rel = (past_len + qi) - (blk * MOBA_BLOCK + u * PAGE_SIZE + lane)
                tq.append(lg + _t5_bias(rel, tbl_ref, h))
        tiles.append(tq)
    mxs = []
    for tq in tiles:
        mx = mx_o
        for t in tq:
            mx = jnp.maximum(mx, jnp.max(t, axis=-1, keepdims=True))
        mxs.append(mx)
    pos = [jnp.exp(lo - mx) for mx in mxs]
    ps = [[jnp.exp(t - mx) for t in tq] for tq, mx in zip(tiles, mxs)]
    accs = [jnp.dot(po.astype(BF16), vn, preferred_element_type=F32) for po in pos]
    for qi in range(ds):
        for i in range(N_SEL_PAGES):
            accs[qi] = accs[qi] + jnp.dot(ps[qi][i].astype(BF16), vp_refs[qi * N_SEL_PAGES + i][0].astype(BF16),
                                          preferred_element_type=F32)
    result = jnp.zeros((ds, HD_ATT), F32)
    for qi in range(ds):
        den = jnp.sum(pos[qi], axis=-1, keepdims=True)
        for p in ps[qi]:
            den = den + jnp.sum(p, axis=-1, keepdims=True)
        result = jnp.where(qrow == qi, accs[qi] / den, result)
    o_ref[0] = result


def _attn_sample_call(phys, ids, rel_bias, q, k_new, v_new, cache_k2, cache_v2, past_len):
    dbs, ds, _ = q.shape
    n_pg = ds * N_SEL_PAGES

    def pmap(b, h, phys_ref, ids_ref, u):
        return (phys_ref[(b * H_ATT + h) * n_pg + u], 0, h)

    page_specs = [pl.BlockSpec((1, PAGE_SIZE, HD_ATT), functools.partial(pmap, u=u)) for u in range(n_pg)]
    nspec = pl.BlockSpec((1, ds, HD_ATT), lambda b, h, p, i: (b, 0, h))
    return pl.pallas_call(
        functools.partial(_attn_sample_kernel, past_len=past_len),
        grid_spec=pltpu.PrefetchScalarGridSpec(
            num_scalar_prefetch=2,
            grid=(dbs, H_ATT),
            in_specs=[pl.BlockSpec(memory_space=pltpu.SMEM), nspec, nspec, nspec] + page_specs + page_specs,
            out_specs=pl.BlockSpec((1, ds, HD_ATT), lambda b, h, p, i: (b, 0, h)),
        ),
        out_shape=jax.ShapeDtypeStruct((dbs, ds, C_ATT), F32),
        compiler_params=_cparams(("parallel", "arbitrary")),
        name="attn_sample",
    )(phys, ids, rel_bias, q, k_new, v_new, *([cache_k2] * n_pg), *([cache_v2] * n_pg))


def _rwkv_pre_kernel(z_ref, zp_ref, sh0_ref, mu_ref, w0_ref, w2_ref, a0_ref, a2_ref, g2_ref, kk_ref, ka_ref,
                     r_o, k_o, v_o, kk_o, b_o, ld_o, g_o):
    i = pl.program_id(1)
    z = z_ref[0]
    tm = z.shape[0]
    first = jnp.where(i == 0, sh0_ref[0], zp_ref[0, 7:8, :])
    row = lax.broadcasted_iota(jnp.int32, (tm, 1), 0)
    zprev = jnp.where(row == 0, first, pltpu.roll(z, 1, 0))
    zs = z + (zprev - z) * mu_ref[...]
    c = C_RWKV
    r, k, v = zs[:, :c], zs[:, c:2 * c], zs[:, 2 * c:3 * c]
    zw = zs[:, 3 * c:3 * c + LORA_PAD]
    za = zs[:, 3 * c + LORA_PAD:3 * c + 2 * LORA_PAD]
    zg = zs[:, 3 * c + 2 * LORA_PAD:]
    wl = w0_ref[...] + jnp.dot(jnp.tanh(zw).astype(BF16), w2_ref[...], preferred_element_type=F32)
    w = -(jnp.maximum(-wl, 0.0) + jnp.log(1.0 + jnp.exp(-jnp.abs(wl)))) - 0.5
    ld_o[0] = -jnp.exp(w)
    a = jax.nn.sigmoid(a0_ref[...] + jnp.dot(za.astype(BF16), a2_ref[...], preferred_element_type=F32))
    g_o[0] = jnp.dot(jax.nn.sigmoid(zg).astype(BF16), g2_ref[...], preferred_element_type=F32)
    kkr = k * kk_ref[...]
    n2 = _segsum64(kkr * kkr, _seg_ones())
    kkn = kkr / jnp.maximum(jnp.sqrt(n2), 1e-12)
    r_o[0] = r
    k_o[0] = k * (1.0 + (a - 1.0) * ka_ref[...])
    v_o[0] = v
    kk_o[0] = kkn
    b_o[0] = kkn * a


def _rwkv_pre_call(z, shift0, p, *, tm):
    bsz, t, zp = z.shape
    nt = t // tm
    c = C_RWKV
    row = lambda b, i: (0, 0)
    vec = lambda n: pl.BlockSpec((1, n), row)
    tile = pl.BlockSpec((1, tm, c), lambda b, i: (b, i, 0))
    return pl.pallas_call(
        _rwkv_pre_kernel,
        grid=(bsz, nt),
        in_specs=[pl.BlockSpec((1, tm, zp), lambda b, i: (b, i, 0)),
                  pl.BlockSpec((1, 8, zp), lambda b, i: (b, jnp.maximum(i * (tm // 8) - 1, 0), 0)),
                  pl.BlockSpec((1, 1, zp), lambda b, i: (b, 0, 0)),
                  vec(zp), vec(c),
                  pl.BlockSpec((LORA_PAD, c), row), vec(c),
                  pl.BlockSpec((LORA_PAD, c), row),
                  pl.BlockSpec((D_GATE_LORA, c), row), vec(c), vec(c)],
        out_specs=[tile] * 7,
        out_shape=[jax.ShapeDtypeStruct((bsz, t, c), F32)] * 7,
        compiler_params=_cparams(("parallel", "arbitrary")),
        name="rwkv_pre",
    )(z, z, shift0.reshape(bsz, 1, zp), p["mu"], p["w0"], p["w2"], p["a0"], p["a2"], p["g2"], p["k_k"], p["k_a"])


CHUNK_PREC = {"gram": 1, "state_read": 1, "mkv": 1, "solve1": 2, "solve_sq": 2, "solve_ap": 2, "out": 1, "state_upd": 1}


def _chunk_pairs(rs, ks, vs, kks, bs, lds, ss):
    c = CHUNK
    lane = lax.broadcasted_iota(jnp.int32, (1, LANES), 1)
    m_a = jnp.where(lane < HS_RWKV, 1.0, 0.0)
    m_b = 1.0 - m_a
    row = lax.broadcasted_iota(jnp.int32, (c, 2 * c), 0)
    coli = lax.broadcasted_iota(jnp.int32, (c, 2 * c), 1) % c
    strict = coli < row
    incl = coli <= row
    lr = lax.broadcasted_iota(jnp.int32, (c, c), 0)
    lc = lax.broadcasted_iota(jnp.int32, (c, c), 1)
    ltri = jnp.where(lc <= lr, 1.0, 0.0).astype(BF16)
    ones = jnp.ones((c, LANES), BF16)
    rr = lax.broadcasted_iota(jnp.int32, (LANES, LANES), 0) // HS_RWKV
    cc = lax.broadcasted_iota(jnp.int32, (LANES, LANES), 1) // HS_RWKV
    same_head = rr == cc

    def each(f, *lists):
        return [f(*args) for args in zip(*lists)]

    def stack2(x):
        return jnp.concatenate([x * m_a, x * m_b], axis=0)

    def prod(site, a, b, dims=_NN):
        n = CHUNK_PREC[site]
        return _dotp(_split(a, n), _split(b, n), dims, order=n)

    def pm(site, mcats, xs):
        return each(lambda m, x: prod(site, m, stack2(x)), mcats, xs)

    ldp = each(lambda x: _split(x, 3), lds)
    cums = each(lambda p: _dotp([ltri], p, _NN), ldp)
    gcols = each(lambda p: jnp.exp(_dotp(p, [ones], _TN)), ldp)
    g_inv = each(lambda cu: jnp.exp(-cu), cums)
    g_end = each(lambda cu: jnp.exp(cu[c - 1:c, :] - cu), cums)
    p_all = each(lambda kk, r, cu, ld: jnp.concatenate([kk * jnp.exp(cu - ld), r * jnp.exp(cu)], axis=0),
                 kks, rs, cums, lds)
    z2 = each(lambda k, b, gi: jnp.concatenate([stack2(k * gi), stack2(b * gi)], axis=0), ks, bs, g_inv)
    g4 = each(lambda p, z: prod("gram", p, z, _NT), p_all, z2)
    mk = each(lambda g: jnp.where(strict, g[:c, :2 * c], 0.0), g4)
    pj = each(lambda g: jnp.where(strict, -g[:c, 2 * c:], 0.0), g4)
    akb = each(lambda g: jnp.concatenate([jnp.where(incl, g[c:, :2 * c], 0.0),
                                          jnp.where(incl, -g[c:, 2 * c:], 0.0)], axis=1), g4)
    ps = each(lambda p, s: prod("state_read", p, s), p_all, ss)
    mkv = pm("mkv", mk, vs)
    rhs = each(lambda p, m: p[:c] + m, ps, mkv)
    us = each(lambda x, d: x + d, rhs, pm("solve1", pj, rhs))
    n = 2
    while n < c:
        pj = pm("solve_sq", pj, pj)
        us = each(lambda x, d: x + d, us, pm("solve_ap", pj, us))
        n *= 2
    ys = each(lambda p, m, v, u: p[c:] + prod("out", m, jnp.concatenate([stack2(v), stack2(u)], axis=0)),
              ps, akb, vs, us)
    upd = each(lambda k, b, ge, v, u: prod("state_upd", jnp.concatenate([k * ge, -(b * ge)], axis=0),
                                           jnp.concatenate([v, u], axis=0), _TN), ks, bs, g_end, vs, us)
    s_new = each(lambda gc, s, up: gc * s + jnp.where(same_head, up, 0.0), gcols, ss, upd)
    return ys, s_new


def _rwkv_chunk_kernel(r_ref, k_ref, v_ref, kk_ref, b_ref, ld_ref, s0_ref, y_ref, so_ref, s_scr):
    ci = pl.program_id(2)

    @pl.when(ci == 0)
    def _():
        s_scr[...] = s0_ref[0]

    sls = [slice(p * LANES, (p + 1) * LANES) for p in range(PAIRS_PER_STEP)]
    pairs = lambda ref: [ref[0, :, sl] for sl in sls]
    ys, s_new = _chunk_pairs(pairs(r_ref), pairs(k_ref), pairs(v_ref), pairs(kk_ref), pairs(b_ref), pairs(ld_ref),
                             [s_scr[p] for p in range(PAIRS_PER_STEP)])
    for p, sl in enumerate(sls):
        y_ref[0, :, sl] = ys[p]
        s_scr[p] = s_new[p]

    @pl.when(ci == pl.num_programs(2) - 1)
    def _():
        so_ref[0] = s_scr[...]


def _rwkv_chunk_call(r, k, v, kk, b, logd, s0_blk):
    bsz, t, c = r.shape
    pp = PAIRS_PER_STEP
    w = pp * LANES
    tile = pl.BlockSpec((1, CHUNK, w), lambda bi, p, ci: (bi, ci, p))
    sspec = pl.BlockSpec((1, pp, LANES, LANES), lambda bi, p, ci: (bi, p, 0, 0))
    return pl.pallas_call(
        _rwkv_chunk_kernel,
        grid=(bsz, c // w, t // CHUNK),
        in_specs=[tile] * 6 + [sspec],
        out_specs=[tile, sspec],
        out_shape=[jax.ShapeDtypeStruct((bsz, t, c), F32),
                   jax.ShapeDtypeStruct(s0_blk.shape, F32)],
        scratch_shapes=[pltpu.VMEM((pp, LANES, LANES), F32)],
        compiler_params=_cparams(("parallel", "parallel", "arbitrary")),
        name="rwkv_chunk",
    )(r, k, v, kk, b, logd, s0_blk)


def _rwkv_post_kernel(y_ref, r_ref, k_ref, v_ref, g_ref, rk_ref, lg_ref, lb_ref, o_ref):
    p128 = _seg_ones()
    y = y_ref[...]
    inv = 1.0 / HS_RWKV
    mu = _segsum64(y, p128) * inv
    yc = y - mu
    var = _segsum64(yc * yc, p128) * inv
    yn = yc * lax.rsqrt(var + GN_EPS) * lg_ref[...] + lb_ref[...]
    bonus = _segsum64(r_ref[...] * k_ref[...] * rk_ref[...], p128) * v_ref[...]
    o_ref[...] = ((yn + bonus) * g_ref[...]).astype(o_ref.dtype)


def _rwkv_post_call(y, r, k, v, g, p, *, tm):
    m, c = y.shape
    tile = pl.BlockSpec((tm, c), lambda i: (i, 0))
    vec = pl.BlockSpec((1, c), lambda i: (0, 0))
    return pl.pallas_call(
        _rwkv_post_kernel,
        grid=(m // tm,),
        in_specs=[tile] * 5 + [vec] * 3,
        out_specs=tile,
        out_shape=jax.ShapeDtypeStruct((m, c), BF16),
        compiler_params=_cparams(("parallel",)),
        name="rwkv_post",
    )(y, r, k, v, g, p["r_k"], p["lnx_g"], p["lnx_b"])


def _combine_kernel(x_ref, ada_ref, lng_ref, lnb_ref, oa_ref, or_ref, ga_ref, gr_ref, wa_ref, wr_ref, wo_ref, o_ref):
    bb, t, d = x_ref.shape
    ua = jnp.dot(oa_ref[...], wa_ref[...], preferred_element_type=F32)
    ur = jnp.dot(or_ref[...], wr_ref[...], preferred_element_type=F32)
    m = jax.nn.sigmoid(ga_ref[...]) * ua + jax.nn.sigmoid(gr_ref[...]) * ur
    mo = jnp.dot(m.astype(BF16), wo_ref[...], preferred_element_type=F32)
    y = DEEPNORM_ALPHA * x_ref[...] + ada_ref[:, 5:6, :] * mo.reshape(bb, t, d)
    o_ref[...] = _ln(y) * lng_ref[1:2, :] + lnb_ref[1:2, :]


def _combine_call(x, ada, ln_g, ln_b, oa, orw, ga, gr, wa, wr, wo, *, blk):
    bsz, t, d = x.shape
    bb, tt = blk
    nt = t // tt
    rows = bb * tt
    xmap = lambda i: (i // nt, i % nt, 0)
    const = lambda i: (0, 0)
    rowt = lambda n: pl.BlockSpec((rows, n), lambda i: (i, 0))
    return pl.pallas_call(
        _combine_kernel,
        grid=((bsz // bb) * nt,),
        in_specs=[pl.BlockSpec((bb, tt, d), xmap),
                  pl.BlockSpec((bb, N_ADA, d), lambda i: (i // nt, 0, 0)),
                  pl.BlockSpec((3, d), const), pl.BlockSpec((3, d), const),
                  rowt(C_ATT), rowt(C_RWKV), rowt(d), rowt(d),
                  pl.BlockSpec((C_ATT, d), const), pl.BlockSpec((C_RWKV, d), const), pl.BlockSpec((d, d), const)],
        out_specs=pl.BlockSpec((bb, tt, d), xmap),
        out_shape=jax.ShapeDtypeStruct(x.shape, F32),
        compiler_params=_cparams(("parallel",)),
        name="combine",
    )(x, ada, ln_g, ln_b, oa, orw, ga, gr, wa, wr, wo)


def _rearrange_z(a):
    c3 = 3 * C_RWKV
    pad = [(0, 0)] * (a.ndim - 1) + [(0, LORA_PAD - D_DECAY_LORA)]
    return jnp.concatenate([a[..., :c3],
                            jnp.pad(a[..., c3:c3 + D_DECAY_LORA], pad),
                            jnp.pad(a[..., c3 + D_DECAY_LORA:c3 + D_DECAY_LORA + D_AAA_LORA], pad),
                            a[..., c3 + D_DECAY_LORA + D_AAA_LORA:]], axis=-1)


def _unarrange_z(a):
    c3 = 3 * C_RWKV
    return jnp.concatenate([a[..., :c3], a[..., c3:c3 + D_DECAY_LORA],
                            a[..., c3 + LORA_PAD:c3 + LORA_PAD + D_AAA_LORA], a[..., c3 + 2 * LORA_PAD:]], axis=-1)


def _state_to_blocks(state):
    bsz = state.shape[0]
    st = jnp.swapaxes(state, -1, -2).reshape(bsz, H_RWKV // 2, 2, HS_RWKV, HS_RWKV)
    blk = jnp.einsum('bphkv,hg->bphkgv', st, jnp.eye(2, dtype=state.dtype))
    return blk.reshape(bsz, H_RWKV // 2, LANES, LANES)


def _blocks_to_state(blk):
    bsz = blk.shape[0]
    b6 = blk.reshape(bsz, H_RWKV // 2, 2, HS_RWKV, 2, HS_RWKV)
    st = jnp.stack([b6[:, :, 0, :, 0, :], b6[:, :, 1, :, 1, :]], axis=2)
    return jnp.swapaxes(st, -1, -2).reshape(bsz, H_RWKV, HS_RWKV, HS_RWKV)


def _trunk(x, ada, attend, shift0, wkv0, w, *, ffn_blk, mm_tm, pre_tm, post_tm, comb_blk):
    bsz, t, d = x.shape
    m = bsz * t
    x1, h2 = _ffn_call(x, ada, w["ln_g"], w["ln_b"], w["wi0"], w["wo0"], sub=0, emit_next=True, blk=ffn_blk)
    h2f = h2.reshape(m, d)
    q = _mm_call(h2f, w["wq"], tm=mm_tm, name="proj_q").reshape(bsz, t, C_ATT)
    k = _mm_call(h2f, w["wk"], tm=mm_tm, name="proj_k").reshape(bsz, t, C_ATT)
    v = _mm_call(h2f, w["wv"], tm=mm_tm, name="proj_v").reshape(bsz, t, C_ATT)
    z = _mm_call(h2f, w["wz"], tm=mm_tm, name="proj_z").reshape(bsz, t, Z_PAD)
    ga = _mm_call(h2f, w["wga"], tm=mm_tm, name="proj_ga")
    gr = _mm_call(h2f, w["wgr"], tm=mm_tm, name="proj_gr")
    o_att = attend(q, k, v)
    r, k2, vv, kk, b, logd, g = _rwkv_pre_call(z, _rearrange_z(shift0), w, tm=pre_tm)
    tp = -(-t // CHUNK) * CHUNK
    seq = [r, k2, vv, kk, b, logd]
    if tp != t:
        seq = [jnp.pad(a, ((0, 0), (0, tp - t), (0, 0))) for a in seq]
    y, s_blk = _rwkv_chunk_call(*seq, _state_to_blocks(wkv0))
    y = y[:, :t]
    fl = lambda a: a.reshape(m, C_RWKV)
    o_rwkv = _rwkv_post_call(fl(y), fl(r), fl(k2), fl(vv), fl(g), w, tm=post_tm)
    x2 = _combine_call(x1, ada, w["ln_g"], w["ln_b"], o_att.reshape(m, C_ATT), o_rwkv, ga, gr,
                       w["wua"], w["wur"], w["wout"], blk=comb_blk)
    x3 = _ffn_call(x2, ada, w["ln_g"], w["ln_b"], w["wi1"], w["wo1"], sub=2, emit_next=False, blk=ffn_blk)
    shift_new = _unarrange_z(z[:, -1])
    return (x3, k.reshape(bsz, t, H_ATT, HD_ATT), v.reshape(bsz, t, H_ATT, HD_ATT), _blocks_to_state(s_blk), shift_new)


def kernel(x_prompt, x_sample, cache_k, cache_v, state_wkv, state_shift, page_table, c_prompt, c_sample, rel_bias, w_ada, b_ada, ln_g, ln_b, ffn_wi, ffn_wo, w_in, mu_shift, w0, w2, a0, a2, g2, k_k, k_a, r_k, lnx_g, lnx_b, w_up_attn, w_up_rwkv, w_out):
    assert w_ada.shape[0] == DEPTH == 1
    bsz, s_len, d = x_prompt.shape
    dbs, ds, _ = x_sample.shape
    past_len = page_table.shape[1] * PAGE_SIZE
    n_phys = cache_k.shape[1]
    l = 0
    win = w_in[l]
    c3 = 3 * C_ATT
    lora_rows = ((0, LORA_PAD - D_DECAY_LORA), (0, 0))
    w = {
        "ln_g": ln_g[l], "ln_b": ln_b[l],
        "wi0": ffn_wi[l, 0].astype(BF16), "wo0": ffn_wo[l, 0].astype(BF16),
        "wi1": ffn_wi[l, 1].astype(BF16), "wo1": ffn_wo[l, 1].astype(BF16),
        "wq": win[:, :C_ATT].astype(BF16), "wk": win[:, C_ATT:2 * C_ATT].astype(BF16),
        "wv": win[:, 2 * C_ATT:c3].astype(BF16),
        "wz": _rearrange_z(win[:, c3:c3 + RWKV_PROJ]).astype(BF16),
        "wga": win[:, c3 + RWKV_PROJ:c3 + RWKV_PROJ + d].astype(BF16),
        "wgr": win[:, c3 + RWKV_PROJ + d:].astype(BF16),
        "mu": _rearrange_z(mu_shift[l])[None, :],
        "w0": w0[l][None, :], "w2": jnp.pad(w2[l], lora_rows).astype(BF16),
        "a0": a0[l][None, :], "a2": jnp.pad(a2[l], lora_rows).astype(BF16),
        "g2": g2[l].astype(BF16), "k_k": k_k[l][None, :], "k_a": k_a[l][None, :],
        "r_k": r_k[l].reshape(1, C_RWKV), "lnx_g": lnx_g[l][None, :], "lnx_b": lnx_b[l][None, :],
        "wua": w_up_attn[l].astype(BF16), "wur": w_up_rwkv[l].astype(BF16), "wout": w_out[l].astype(BF16),
    }
    n_c = bsz + dbs
    c_rows = -(-n_c // 8) * 8
    c_all = jnp.concatenate([c_prompt, c_sample, jnp.zeros((c_rows - n_c, d), F32)], axis=0)
    ada = _ada_call(c_all, w_ada[l], b_ada[l]).reshape(c_rows, N_ADA, d)

    bias_tiles = _bias_tiles_call(rel_bias)
    far = rel_bias[NUM_BUCKETS - 1]
    attend_p = lambda q, k, v: _attn_prompt_call(q, k, v, bias_tiles, far)
    yp, kp, vp, wp, sp = _trunk(
        x_prompt, ada[:bsz], attend_p, jnp.zeros((bsz, RWKV_PROJ), F32),
        jnp.zeros((bsz, H_RWKV, HS_RWKV, HS_RWKV), F32), w,
        ffn_blk=(1, 512), mm_tm=1024, pre_tm=256, post_tm=256, comb_blk=(1, 256))

    cache_k2 = cache_k.reshape(DEPTH * n_phys, PAGE_SIZE, C_ATT)
    cache_v2 = cache_v.reshape(DEPTH * n_phys, PAGE_SIZE, C_ATT)
    page_table = page_table + l * n_phys
    n_full = past_len // MOBA_BLOCK

    def attend_s(q, k, v):
        means = _cache_means_call(page_table, cache_k2, n_full)
        ids = _topk_ids_call(q, means)[..., :MOBA_TOPK]
        b_idx = jnp.arange(dbs)[:, None, None, None, None]
        pg = ids[..., None] * PAGES_PER_BLOCK + jnp.arange(PAGES_PER_BLOCK)
        phys = page_table[b_idx, pg]
        o = _attn_sample_call(phys.reshape(-1), ids.reshape(-1), rel_bias, q, k, v, cache_k2, cache_v2, past_len)
        return o.astype(BF16)

    ys, kn, vn, wn, sn = _trunk(
        x_sample, ada[bsz:n_c], attend_s, state_shift[l], state_wkv[l], w,
        ffn_blk=(dbs, ds), mm_tm=dbs * ds, pre_tm=ds, post_tm=dbs * ds, comb_blk=(dbs, ds))

    return (yp, ys, kp[None], vp[None], kn[None], vn[None], wp[None], wn[None], sp[None], sn[None])
```

```python
import functools
import math

import jax
import jax.numpy as jnp
from jax import lax
from jax.experimental import pallas as pl
from jax.experimental.pallas import tpu as pltpu

F32 = jnp.float32
BF16 = jnp.bfloat16

D_MODEL = 2048
D_FF = 5632
N_ADA = 9
H_ATT = 8
HD_ATT = 128
C_ATT = H_ATT * HD_ATT
MOBA_BLOCK = 256
MOBA_TOPK = 3
NUM_BUCKETS = 32
MAX_DISTANCE = 128
PAGE_SIZE = 128
HS_RWKV = 64
C_RWKV = 1024
H_RWKV = C_RWKV // HS_RWKV
D_DECAY_LORA = 96
D_AAA_LORA = 96
D_GATE_LORA = 256
RWKV_PROJ = 3 * C_RWKV + D_DECAY_LORA + D_AAA_LORA + D_GATE_LORA
LORA_PAD = 128
Z_PAD = 3 * C_RWKV + 2 * LORA_PAD + D_GATE_LORA
GN_EPS = 64e-5
LN_EPS = 1e-5
DEPTH = 1
DEEPNORM_ALPHA = (2 * DEPTH) ** 0.25
NEG_BIG = -1e30

LANES = 128
CHUNK = 64
PAIRS_PER_STEP = 8
VMEM_LIMIT = 56 * 1024 * 1024


def _t5_thresholds():
    max_exact = NUM_BUCKETS // 2
    thr = list(range(1, max_exact + 1))
    for b in range(max_exact + 1, NUM_BUCKETS):
        x = max_exact * (MAX_DISTANCE / max_exact) ** ((b - max_exact) / (NUM_BUCKETS - max_exact))
        thr.append(int(math.ceil(x)))
    return tuple(thr)


T5_THR = _t5_thresholds()


def _cparams(sem, vmem=VMEM_LIMIT):
    return pltpu.CompilerParams(dimension_semantics=sem, vmem_limit_bytes=vmem)


def _ln(x):
    mu = jnp.mean(x, axis=-1, keepdims=True)
    xc = x - mu
    var = jnp.mean(xc * xc, axis=-1, keepdims=True)
    return xc * lax.rsqrt(var + LN_EPS)


def _split(x, n):
    parts = []
    for i in range(n):
        p = x.astype(BF16)
        parts.append(p)
        if i + 1 < n:
            x = x - p.astype(F32)
    return parts


_NN = (((1,), (0,)), ((), ()))
_NT = (((1,), (1,)), ((), ()))
_TN = (((0,), (0,)), ((), ()))


def _dotp(ap, bp, dims=_NN, order=None):
    if order is None:
        order = max(len(ap), len(bp))
    acc = None
    for i, a in enumerate(ap):
        for j, b in enumerate(bp):
            if i + j < order:
                t = lax.dot_general(a, b, dims, preferred_element_type=F32)
                acc = t if acc is None else acc + t
    return acc


def _t5_bias(rel, tbl_ref, h):
    bias = jnp.full(rel.shape, tbl_ref[0, h], F32)
    for b in range(1, NUM_BUCKETS):
        bias = jnp.where(rel >= T5_THR[b - 1], tbl_ref[b, h], bias)
    return bias


def _seg_ones():
    r = lax.broadcasted_iota(jnp.int32, (LANES, LANES), 0) // HS_RWKV
    c = lax.broadcasted_iota(jnp.int32, (LANES, LANES), 1) // HS_RWKV
    return jnp.where(r == c, 1.0, 0.0).astype(BF16)


def _segsum64(x, p128):
    outs = []
    for j in range(x.shape[-1] // LANES):
        xs = x[:, j * LANES:(j + 1) * LANES]
        outs.append(_dotp(_split(xs, 3), [p128]))
    return jnp.concatenate(outs, axis=-1)


ADA_TN = 1024


def _ada_kernel(c_ref, w_ref, b_ref, o_ref):
    c = c_ref[...]
    sc = c * jax.nn.sigmoid(c)
    rows = sc.shape[0]
    s0, s1, s2 = _split(sc, 3)
    w_hi, w_lo = _split(w_ref[...], 2)
    a = jnp.dot(jnp.concatenate([s0, s1, s2], axis=0), w_hi, preferred_element_type=F32)
    b = jnp.dot(jnp.concatenate([s0, s1], axis=0), w_lo, preferred_element_type=F32)
    o_ref[...] = (a[:rows] + a[rows:2 * rows] + a[2 * rows:] + b[:rows] + b[rows:]) + b_ref[...]


def _ada_call(c_all, w_ada, b_ada):
    rows, d = c_all.shape
    n = w_ada.shape[1]
    return pl.pallas_call(
        _ada_kernel,
        grid=(n // ADA_TN,),
        in_specs=[pl.BlockSpec((rows, d), lambda j: (0, 0)),
                  pl.BlockSpec((d, ADA_TN), lambda j: (0, j)),
                  pl.BlockSpec((1, ADA_TN), lambda j: (0, j))],
        out_specs=pl.BlockSpec((rows, ADA_TN), lambda j: (0, j)),
        out_shape=jax.ShapeDtypeStruct((rows, n), F32),
        compiler_params=_cparams(("parallel",)),
        name="ada",
    )(c_all, w_ada, b_ada.reshape(1, n))


FFN_TF = 512


def _ffn_kernel(x_ref, ada_ref, lng_ref, lnb_ref, wig_ref, wiu_ref, wo_ref, *rest, sub, emit_next):
    if emit_next:
        o_ref, h_next_ref, h_scr, acc = rest
    else:
        o_ref, h_scr, acc = rest
    bb, t, d = x_ref.shape
    j = pl.program_id(1)

    @pl.when(j == 0)
    def _():
        h = _ln(x_ref[...]) * (1.0 + ada_ref[:, 3 * sub + 1:3 * sub + 2, :]) + ada_ref[:, 3 * sub:3 * sub + 1, :]
        h_scr[...] = h.reshape(bb * t, d).astype(BF16)
        acc[...] = jnp.zeros_like(acc)

    h = h_scr[...]
    g = jnp.dot(h, wig_ref[...], preferred_element_type=F32)
    u = jnp.dot(h, wiu_ref[...], preferred_element_type=F32)
    act = (g * jax.nn.sigmoid(g)) * u
    acc[...] += jnp.dot(act.astype(BF16), wo_ref[...], preferred_element_type=F32)

    @pl.when(j == pl.num_programs(1) - 1)
    def _():
        gate = ada_ref[:, 3 * sub + 2:3 * sub + 3, :]
        y = DEEPNORM_ALPHA * x_ref[...] + 0.5 * gate * acc[...].reshape(bb, t, d)
        y = _ln(y) * lng_ref[sub:sub + 1, :] + lnb_ref[sub:sub + 1, :]
        o_ref[...] = y
        if emit_next:
            nxt = sub + 1
            hn = _ln(y) * (1.0 + ada_ref[:, 3 * nxt + 1:3 * nxt + 2, :]) + ada_ref[:, 3 * nxt:3 * nxt + 1, :]
            h_next_ref[...] = hn.astype(BF16)


def _ffn_call(x, ada, ln_g, ln_b, wi, wo, *, sub, emit_next, blk):
    bsz, t, d = x.shape
    bb, tt = blk
    nj = D_FF // FFN_TF
    grid = (bsz // bb, t // tt, nj)
    nt = t // tt
    grid = ((bsz // bb) * nt, nj)
    xmap = lambda i, j: (i // nt, i % nt, 0)
    amap = lambda i, j: (i // nt, 0, 0)
    out_shape = [jax.ShapeDtypeStruct(x.shape, F32)]
    out_specs = [pl.BlockSpec((bb, tt, d), xmap)]
    if emit_next:
        out_shape.append(jax.ShapeDtypeStruct(x.shape, BF16))
        out_specs.append(pl.BlockSpec((bb, tt, d), xmap))
    res = pl.pallas_call(
        functools.partial(_ffn_kernel, sub=sub, emit_next=emit_next),
        grid=grid,
        in_specs=[pl.BlockSpec((bb, tt, d), xmap),
                  pl.BlockSpec((bb, N_ADA, d), amap),
                  pl.BlockSpec((3, d), lambda i, j: (0, 0)),
                  pl.BlockSpec((3, d), lambda i, j: (0, 0)),
                  pl.BlockSpec((d, FFN_TF), lambda i, j: (0, j)),
                  pl.BlockSpec((d, FFN_TF), lambda i, j: (0, j + nj)),
                  pl.BlockSpec((FFN_TF, d), lambda i, j: (j, 0))],
        out_specs=out_specs,
        out_shape=out_shape,
        scratch_shapes=[pltpu.VMEM((bb * tt, d), BF16), pltpu.VMEM((bb * tt, d), F32)],
        compiler_params=_cparams(("parallel", "arbitrary")),
        name=f"ffn{sub}",
    )(x, ada, ln_g, ln_b, wi, wi, wo)
    return res if emit_next else res[0]


def _mm_kernel(h_ref, w_ref, o_ref):
    o_ref[...] = jnp.dot(h_ref[...], w_ref[...], preferred_element_type=F32)


def _mm_call(h, w, *, tm, tn=512, name="proj"):
    m, k = h.shape
    n = w.shape[1]
    return pl.pallas_call(
        _mm_kernel,
        grid=(m // tm, n // tn),
        in_specs=[pl.BlockSpec((tm, k), lambda i, j: (i, 0)),
                  pl.BlockSpec((k, tn), lambda i, j: (0, j))],
        out_specs=pl.BlockSpec((tm, tn), lambda i, j: (i, j)),
        out_shape=jax.ShapeDtypeStruct((m, n), F32),
        compiler_params=_cparams(("parallel", "arbitrary")),
        name=name,
    )(h, w)


def _bias_tiles_kernel(tbl_ref, o_ref):
    h = pl.program_id(0)
    r = lax.broadcasted_iota(jnp.int32, (MOBA_BLOCK, MOBA_BLOCK), 0)
    c = lax.broadcasted_iota(jnp.int32, (MOBA_BLOCK, MOBA_BLOCK), 1)
    rel0 = r - c
    o_ref[0, 0] = jnp.where(rel0 >= 0, _t5_bias(rel0, tbl_ref, h), NEG_BIG)
    o_ref[0, 1] = _t5_bias(rel0 + MOBA_BLOCK, tbl_ref, h)


def _bias_tiles_call(rel_bias):
    return pl.pallas_call(
        _bias_tiles_kernel,
        grid=(H_ATT,),
        in_specs=[pl.BlockSpec(memory_space=pltpu.SMEM)],
        out_specs=pl.BlockSpec((1, 2, MOBA_BLOCK, MOBA_BLOCK), lambda h: (h, 0, 0, 0)),
        out_shape=jax.ShapeDtypeStruct((H_ATT, 2, MOBA_BLOCK, MOBA_BLOCK), F32),
        compiler_params=_cparams(("arbitrary",)),
        name="bias_tiles",
    )(rel_bias)


def _rank_select(s, n_valid):
    col = lax.broadcasted_iota(jnp.int32, s.shape, 1)
    cnt = jnp.zeros(s.shape, jnp.int32)
    for m in range(n_valid):
        sm = s[:, m:m + 1]
        beats = (sm > s) | ((sm == s) & (m < col))
        cnt = cnt + jnp.where(beats, 1, 0)
    return (col < n_valid) & (cnt < MOBA_TOPK)


def _attn_prompt_kernel(far_ref, q_ref, k_ref, v_ref, bt_ref, o_ref):
    h = pl.program_id(1)
    s_len = q_ref.shape[1]
    nb = s_len // MOBA_BLOCK
    q = q_ref[0]
    k = k_ref[0]
    kb = k.astype(BF16)
    vb = v_ref[0].astype(BF16)
    qb = (q * (HD_ATT ** -0.5)).astype(BF16)
    means = jnp.mean(k.reshape(nb, MOBA_BLOCK, HD_ATT), axis=1)
    means = jnp.concatenate([means, jnp.zeros((LANES - nb, HD_ATT), F32)], axis=0)
    mparts = _split(means, 3)
    far = far_ref[h]
    for i in range(nb):
        rows = slice(i * MOBA_BLOCK, (i + 1) * MOBA_BLOCK)
        nk = (i + 1) * MOBA_BLOCK
        logits = lax.dot_general(qb[rows], kb[:nk], _NT, preferred_element_type=F32)
        tiles = []
        if i > 0:
            score = _dotp(_split(q[rows], 3), mparts, _NT, order=3)
            sel = _rank_select(score, i)
            negs = jnp.where(sel, 0.0, NEG_BIG)
        for j in range(i + 1):
            tile = logits[:, j * MOBA_BLOCK:(j + 1) * MOBA_BLOCK]
            if j == i:
                tile = tile + bt_ref[0, 0]
            elif j == i - 1:
                tile = tile + bt_ref[0, 1] + negs[:, j:j + 1]
            else:
                tile = tile + (far + negs[:, j:j + 1])
            tiles.append(tile)
        lg = jnp.concatenate(tiles, axis=-1) if len(tiles) > 1 else tiles[0]
        mx = jnp.max(lg, axis=-1, keepdims=True)
        p = jnp.exp(lg - mx)
        den = jnp.sum(p, axis=-1, keepdims=True)
        out = jnp.dot(p.astype(BF16), vb[:nk], preferred_element_type=F32) / den
        o_ref[0, rows, :] = out.astype(o_ref.dtype)


def _attn_prompt_call(q, k, v, bias_tiles, far):
    bsz, s_len, _ = q.shape
    qspec = pl.BlockSpec((1, s_len, HD_ATT), lambda b, h: (b, 0, h))
    return pl.pallas_call(
        _attn_prompt_kernel,
        grid=(bsz, H_ATT),
        in_specs=[pl.BlockSpec(memory_space=pltpu.SMEM), qspec, qspec, qspec,
                  pl.BlockSpec((1, 2, MOBA_BLOCK, MOBA_BLOCK), lambda b, h: (h, 0, 0, 0))],
        out_specs=pl.BlockSpec((1, s_len, HD_ATT), lambda b, h: (b, 0, h)),
        out_shape=jax.ShapeDtypeStruct((bsz, s_len, C_ATT), BF16),
        compiler_params=_cparams(("parallel", "arbitrary")),
        name="attn_prompt",
    )(far, q, k, v, bias_tiles)


MEAN_BLOCKS = 8
PAGES_PER_BLOCK = MOBA_BLOCK // PAGE_SIZE


def _cache_means_kernel(pt_ref, *refs):
    page_refs, o_ref = refs[:-1], refs[-1]
    for m in range(MEAN_BLOCKS):
        tot = None
        for u in range(PAGES_PER_BLOCK):
            s = jnp.sum(page_refs[m * PAGES_PER_BLOCK + u][0], axis=0)
            tot = s if tot is None else tot + s
        o_ref[0, m] = tot * (1.0 / MOBA_BLOCK)


def _cache_means_call(page_table, cache_k4, n_blocks):
    dbs = page_table.shape[0]
    npg = MEAN_BLOCKS * PAGES_PER_BLOCK
    in_specs = [pl.BlockSpec((1, PAGE_SIZE, H_ATT, HD_ATT), functools.partial(
        lambda b, g, pt, u: (pt[b, g * npg + u], 0, 0, 0), u=u)) for u in range(npg)]
    return pl.pallas_call(
        _cache_means_kernel,
        grid_spec=pltpu.PrefetchScalarGridSpec(
            num_scalar_prefetch=1,
            grid=(dbs, n_blocks // MEAN_BLOCKS),
            in_specs=in_specs,
            out_specs=pl.BlockSpec((1, MEAN_BLOCKS, H_ATT, HD_ATT), lambda b, g, pt: (b, g, 0, 0)),
        ),
        out_shape=jax.ShapeDtypeStruct((dbs, n_blocks, H_ATT, HD_ATT), F32),
        compiler_params=_cparams(("parallel", "arbitrary")),
        name="cache_means",
    )(page_table, *([cache_k4] * npg))


def _topk_ids_kernel(q_ref, m_ref, o_ref):
    q = q_ref[0]
    means = m_ref[0]
    n_blocks = means.shape[0]
    ds = q.shape[0]
    pad = jnp.zeros((LANES - n_blocks, HD_ATT), F32)
    col = lax.broadcasted_iota(jnp.int32, (ds, LANES), 1)
    for h in range(H_ATT):
        sl = slice(h * HD_ATT, (h + 1) * HD_ATT)
        mh = jnp.concatenate([means[:, sl], pad], axis=0) if n_blocks < LANES else means[:, sl]
        s = _dotp(_split(q[:, sl], 3), _split(mh, 3), _NT, order=3)
        s = jnp.where(col < n_blocks, s, NEG_BIG)
        ids = jnp.zeros((ds, LANES), jnp.int32)
        for t in range(MOBA_TOPK):
            mx = jnp.max(s, axis=-1, keepdims=True)
            idx = jnp.min(jnp.where(s == mx, col, LANES), axis=-1, keepdims=True)
            ids = jnp.where(col == t, idx, ids)
            s = jnp.where(col == idx, NEG_BIG * 2, s)
        o_ref[0, h] = ids


def _topk_ids_call(q, means):
    dbs, ds, _ = q.shape
    n_blocks = means.shape[1]
    return pl.pallas_call(
        _topk_ids_kernel,
        grid=(dbs,),
        in_specs=[pl.BlockSpec((1, ds, C_ATT), lambda b: (b, 0, 0)),
                  pl.BlockSpec((1, n_blocks, C_ATT), lambda b: (b, 0, 0))],
        out_specs=pl.BlockSpec((1, H_ATT, ds, LANES), lambda b: (b, 0, 0, 0)),
        out_shape=jax.ShapeDtypeStruct((dbs, H_ATT, ds, LANES), jnp.int32),
        compiler_params=_cparams(("parallel",)),
        name="topk_ids",
    )(q, means)


N_SEL_PAGES = MOBA_TOPK * PAGES_PER_BLOCK


def _attn_sample_kernel(pt_ref, ids_ref, tbl_ref, q_ref, kn_ref, vn_ref, ck_hbm, cv_hbm, o_ref,
                        kbuf, vbuf, sems, *, past_len):
    ds = q_ref.shape[1]
    n_pg = ds * N_SEL_PAGES
    b, h = pl.program_id(0), pl.program_id(1)
    nh = pl.num_programs(1)
    step = b * nh + h
    n_steps = pl.num_programs(0) * nh
    slot = step % 2

    def page_copies(bb, hh, sl):
        base = (bb * H_ATT + hh) * ds * MOBA_TOPK
        cps = []
        for u in range(n_pg):
            pg = pt_ref[bb, ids_ref[base + u // PAGES_PER_BLOCK] * PAGES_PER_BLOCK + u % PAGES_PER_BLOCK]
            cps.append(pltpu.make_async_copy(ck_hbm.at[pg, :, hh, :], kbuf.at[sl, u], sems.at[sl, u]))
            cps.append(pltpu.make_async_copy(cv_hbm.at[pg, :, hh, :], vbuf.at[sl, u], sems.at[sl, n_pg + u]))
        return cps

    @pl.when(step == 0)
    def _():
        for cp in page_copies(b, h, slot):
            cp.start()

    @pl.when(step + 1 < n_steps)
    def _():
        nxt = step + 1
        for cp in page_copies(nxt // nh, nxt % nh, 1 - slot):
            cp.start()

    for cp in page_copies(b, h, slot):
        cp.wait()
    kp_refs = [kbuf.at[slot, u] for u in range(n_pg)]
    vp_refs = [vbuf.at[slot, u] for u in range(n_pg)]
    qb = (q_ref[0] * (HD_ATT ** -0.5)).astype(BF16)
    qrow = lax.broadcasted_iota(jnp.int32, (ds, 1), 0)
    lane = lax.broadcasted_iota(jnp.int32, (ds, PAGE_SIZE), 1)
    lo = lax.dot_general(qb, kn_ref[0].astype(BF16), _NT, preferred_element_type=F32)
    rel_o = lax.broadcasted_iota(jnp.int32, (ds, ds), 0) - lax.broadcasted_iota(jnp.int32, (ds, ds), 1)
    lo = jnp.where(rel_o >= 0, lo + _t5_bias(rel_o, tbl_ref, h), NEG_BIG)
    mx_o = jnp.max(lo, axis=-1, keepdims=True)
    vn = vn_ref[0].astype(BF16)
    tiles = []
    for qi in range(ds):
        flat = (b * H_ATT + h) * ds + qi
        tq = []
        for s in range(MOBA_TOPK):
            blk = ids_ref[flat * MOBA_TOPK + s]
            for u in range(PAGES_PER_BLOCK):
                kpg = kp_refs[(qi * MOBA_TOPK + s) * PAGES_PER_BLOCK + u][...].astype(BF16)
                lg = lax.dot_general(qb, kpg, _NT, preferred_element_type=F32)
                rel = (past_len + qi) - (blk * MOBA_BLOCK + u * PAGE_SIZE + lane)
                tq.append(lg + _t5_bias(rel, tbl_ref, h))
        tiles.append(tq)
    mxs = []
    for tq in tiles:
        mx = mx_o
        for t in tq:
            mx = jnp.maximum(mx, jnp.max(t, axis=-1, keepdims=True))
        mxs.append(mx)
    pos = [jnp.exp(lo - mx) for mx in mxs]
    ps = [[jnp.exp(t - mx) for t in tq] for tq, mx in zip(tiles, mxs)]
    accs = [jnp.dot(po.astype(BF16), vn, preferred_element_type=F32) for po in pos]
    for qi in range(ds):
        for i in range(N_SEL_PAGES):
            accs[qi] = accs[qi] + jnp.dot(ps[qi][i].astype(BF16), vp_refs[qi * N_SEL_PAGES + i][...].astype(BF16),
                                          preferred_element_type=F32)
    result = jnp.zeros((ds, HD_ATT), F32)
    for qi in range(ds):
        den = jnp.sum(pos[qi], axis=-1, keepdims=True)
        for p in ps[qi]:
            den = den + jnp.sum(p, axis=-1, keepdims=True)
        result = jnp.where(qrow == qi, accs[qi] / den, result)
    o_ref[0] = result


def _attn_sample_call(page_table, ids, rel_bias, q, k_new, v_new, cache_k2, cache_v2, past_len):
    dbs, ds, _ = q.shape
    n_pg = ds * N_SEL_PAGES

    nspec = pl.BlockSpec((1, ds, HD_ATT), lambda b, h, p, i: (b, 0, h))
    pool = pl.BlockSpec(memory_space=pl.ANY)
    return pl.pallas_call(
        functools.partial(_attn_sample_kernel, past_len=past_len),
        grid_spec=pltpu.PrefetchScalarGridSpec(
            num_scalar_prefetch=2,
            grid=(dbs, H_ATT),
            in_specs=[pl.BlockSpec(memory_space=pltpu.SMEM), nspec, nspec, nspec, pool, pool],
            out_specs=pl.BlockSpec((1, ds, HD_ATT), lambda b, h, p, i: (b, 0, h)),
            scratch_shapes=[pltpu.VMEM((2, n_pg, PAGE_SIZE, HD_ATT), F32),
                            pltpu.VMEM((2, n_pg, PAGE_SIZE, HD_ATT), F32),
                            pltpu.SemaphoreType.DMA((2, 2 * n_pg))],
        ),
        out_shape=jax.ShapeDtypeStruct((dbs, ds, C_ATT), F32),
        compiler_params=_cparams(("arbitrary", "arbitrary")),
        name="attn_sample",
    )(page_table, ids, rel_bias, q, k_new, v_new, cache_k2, cache_v2)


def _rwkv_pre_kernel(z_ref, zp_ref, sh0_ref, mu_ref, w0_ref, w2_ref, a0_ref, a2_ref, g2_ref, kk_ref, ka_ref,
                     r_o, k_o, v_o, kk_o, b_o, ld_o, g_o):
    i = pl.program_id(1)
    z = z_ref[0]
    tm = z.shape[0]
    first = jnp.where(i == 0, sh0_ref[0], zp_ref[0, 7:8, :])
    row = lax.broadcasted_iota(jnp.int32, (tm, 1), 0)
    zprev = jnp.where(row == 0, first, pltpu.roll(z, 1, 0))
    zs = z + (zprev - z) * mu_ref[...]
    c = C_RWKV
    r, k, v = zs[:, :c], zs[:, c:2 * c], zs[:, 2 * c:3 * c]
    zw = zs[:, 3 * c:3 * c + LORA_PAD]
    za = zs[:, 3 * c + LORA_PAD:3 * c + 2 * LORA_PAD]
    zg = zs[:, 3 * c + 2 * LORA_PAD:]
    wl = w0_ref[...] + jnp.dot(jnp.tanh(zw).astype(BF16), w2_ref[...], preferred_element_type=F32)
    w = -(jnp.maximum(-wl, 0.0) + jnp.log(1.0 + jnp.exp(-jnp.abs(wl)))) - 0.5
    ld_o[0] = -jnp.exp(w)
    a = jax.nn.sigmoid(a0_ref[...] + jnp.dot(za.astype(BF16), a2_ref[...], preferred_element_type=F32))
    g_o[0] = jnp.dot(jax.nn.sigmoid(zg).astype(BF16), g2_ref[...], preferred_element_type=F32)
    kkr = k * kk_ref[...]
    n2 = _segsum64(kkr * kkr, _seg_ones())
    kkn = kkr / jnp.maximum(jnp.sqrt(n2), 1e-12)
    r_o[0] = r
    k_o[0] = k * (1.0 + (a - 1.0) * ka_ref[...])
    v_o[0] = v
    kk_o[0] = kkn
    b_o[0] = kkn * a


def _rwkv_pre_call(z, shift0, p, *, tm):
    bsz, t, zp = z.shape
    nt = t // tm
    c = C_RWKV
    row = lambda b, i: (0, 0)
    vec = lambda n: pl.BlockSpec((1, n), row)
    tile = pl.BlockSpec((1, tm, c), lambda b, i: (b, i, 0))
    return pl.pallas_call(
        _rwkv_pre_kernel,
        grid=(bsz, nt),
        in_specs=[pl.BlockSpec((1, tm, zp), lambda b, i: (b, i, 0)),
                  pl.BlockSpec((1, 8, zp), lambda b, i: (b, jnp.maximum(i * (tm // 8) - 1, 0), 0)),
                  pl.BlockSpec((1, 1, zp), lambda b, i: (b, 0, 0)),
                  vec(zp), vec(c),
                  pl.BlockSpec((LORA_PAD, c), row), vec(c),
                  pl.BlockSpec((LORA_PAD, c), row),
                  pl.BlockSpec((D_GATE_LORA, c), row), vec(c), vec(c)],
        out_specs=[tile] * 7,
        out_shape=[jax.ShapeDtypeStruct((bsz, t, c), F32)] * 7,
        compiler_params=_cparams(("parallel", "arbitrary")),
        name="rwkv_pre",
    )(z, z, shift0.reshape(bsz, 1, zp), p["mu"], p["w0"], p["w2"], p["a0"], p["a2"], p["g2"], p["k_k"], p["k_a"])


CHUNK_PREC = {"gram": 1, "state_read": 1, "mkv": 1, "solve1": 2, "solve_sq": 2, "solve_ap": 2, "out": 1, "state_upd": 1}


def _chunk_pairs(rs, ks, vs, kks, bs, lds, ss):
    c = CHUNK
    lane = lax.broadcasted_iota(jnp.int32, (1, LANES), 1)
    m_a = jnp.where(lane < HS_RWKV, 1.0, 0.0)
    m_b = 1.0 - m_a
    row = lax.broadcasted_iota(jnp.int32, (c, 2 * c), 0)
    coli = lax.broadcasted_iota(jnp.int32, (c, 2 * c), 1) % c
    strict = coli < row
    incl = coli <= row
    lr = lax.broadcasted_iota(jnp.int32, (c, c), 0)
    lc = lax.broadcasted_iota(jnp.int32, (c, c), 1)
    ltri = jnp.where(lc <= lr, 1.0, 0.0).astype(BF16)
    ones = jnp.ones((c, LANES), BF16)
    rr = lax.broadcasted_iota(jnp.int32, (LANES, LANES), 0) // HS_RWKV
    cc = lax.broadcasted_iota(jnp.int32, (LANES, LANES), 1) // HS_RWKV
    same_head = rr == cc

    def each(f, *lists):
        return [f(*args) for args in zip(*lists)]

    def stack2(x):
        return jnp.concatenate([x * m_a, x * m_b], axis=0)

    def prod(site, a, b, dims=_NN):
        n = CHUNK_PREC[site]
        return _dotp(_split(a, n), _split(b, n), dims, order=n)

    def pm(site, mcats, xs):
        return each(lambda m, x: prod(site, m, stack2(x)), mcats, xs)

    ldp = each(lambda x: _split(x, 3), lds)
    cums = each(lambda p: _dotp([ltri], p, _NN), ldp)
    gcols = each(lambda p: jnp.exp(_dotp(p, [ones], _TN)), ldp)
    g_inv = each(lambda cu: jnp.exp(-cu), cums)
    g_end = each(lambda cu: jnp.exp(cu[c - 1:c, :] - cu), cums)
    p_all = each(lambda kk, r, cu, ld: jnp.concatenate([kk * jnp.exp(cu - ld), r * jnp.exp(cu)], axis=0),
                 kks, rs, cums, lds)
    z2 = each(lambda k, b, gi: jnp.concatenate([stack2(k * gi), stack2(b * gi)], axis=0), ks, bs, g_inv)
    g4 = each(lambda p, z: prod("gram", p, z, _NT), p_all, z2)
    mk = each(lambda g: jnp.where(strict, g[:c, :2 * c], 0.0), g4)
    pj = each(lambda g: jnp.where(strict, -g[:c, 2 * c:], 0.0), g4)
    akb = each(lambda g: jnp.concatenate([jnp.where(incl, g[c:, :2 * c], 0.0),
                                          jnp.where(incl, -g[c:, 2 * c:], 0.0)], axis=1), g4)
    ps = each(lambda p, s: prod("state_read", p, s), p_all, ss)
    mkv = pm("mkv", mk, vs)
    rhs = each(lambda p, m: p[:c] + m, ps, mkv)
    us = each(lambda x, d: x + d, rhs, pm("solve1", pj, rhs))
    n = 2
    while n < c:
        pj = pm("solve_sq", pj, pj)
        us = each(lambda x, d: x + d, us, pm("solve_ap", pj, us))
        n *= 2
    ys = each(lambda p, m, v, u: p[c:] + prod("out", m, jnp.concatenate([stack2(v), stack2(u)], axis=0)),
              ps, akb, vs, us)
    upd = each(lambda k, b, ge, v, u: prod("state_upd", jnp.concatenate([k * ge, -(b * ge)], axis=0),
                                           jnp.concatenate([v, u], axis=0), _TN), ks, bs, g_end, vs, us)
    s_new = each(lambda gc, s, up: gc * s + jnp.where(same_head, up, 0.0), gcols, ss, upd)
    return ys, s_new


def _rwkv_chunk_kernel(r_ref, k_ref, v_ref, kk_ref, b_ref, ld_ref, s0_ref, y_ref, so_ref, s_scr):
    ci = pl.program_id(2)

    @pl.when(ci == 0)
    def _():
        s_scr[...] = s0_ref[0]

    sls = [slice(p * LANES, (p + 1) * LANES) for p in range(PAIRS_PER_STEP)]
    pairs = lambda ref: [ref[0, :, sl] for sl in sls]
    ys, s_new = _chunk_pairs(pairs(r_ref), pairs(k_ref), pairs(v_ref), pairs(kk_ref), pairs(b_ref), pairs(ld_ref),
                             [s_scr[p] for p in range(PAIRS_PER_STEP)])
    for p, sl in enumerate(sls):
        y_ref[0, :, sl] = ys[p]
        s_scr[p] = s_new[p]
        so_ref[0, p] = s_new[p]


def _rwkv_chunk_call(r, k, v, kk, b, logd, s0_blk):
    bsz, t, c = r.shape
    pp = PAIRS_PER_STEP
    w = pp * LANES
    tile = pl.BlockSpec((1, CHUNK, w), lambda bi, p, ci: (bi, ci, p))
    sspec = pl.BlockSpec((1, pp, LANES, LANES), lambda bi, p, ci: (bi, p, 0, 0))
    return pl.pallas_call(
        _rwkv_chunk_kernel,
        grid=(bsz, c // w, t // CHUNK),
        in_specs=[tile] * 6 + [sspec],
        out_specs=[tile, sspec],
        out_shape=[jax.ShapeDtypeStruct((bsz, t, c), F32),
                   jax.ShapeDtypeStruct(s0_blk.shape, F32)],
        scratch_shapes=[pltpu.VMEM((pp, LANES, LANES), F32)],
        compiler_params=_cparams(("parallel", "parallel", "arbitrary")),
        name="rwkv_chunk",
    )(r, k, v, kk, b, logd, s0_blk)


def _rwkv_post_kernel(y_ref, r_ref, k_ref, v_ref, g_ref, rk_ref, lg_ref, lb_ref, o_ref):
    p128 = _seg_ones()
    y = y_ref[...]
    inv = 1.0 / HS_RWKV
    mu = _segsum64(y, p128) * inv
    yc = y - mu
    var = _segsum64(yc * yc, p128) * inv
    yn = yc * lax.rsqrt(var + GN_EPS) * lg_ref[...] + lb_ref[...]
    bonus = _segsum64(r_ref[...] * k_ref[...] * rk_ref[...], p128) * v_ref[...]
    o_ref[...] = ((yn + bonus) * g_ref[...]).astype(o_ref.dtype)


def _rwkv_post_call(y, r, k, v, g, p, *, tm):
    m, c = y.shape
    tile = pl.BlockSpec((tm, c), lambda i: (i, 0))
    vec = pl.BlockSpec((1, c), lambda i: (0, 0))
    return pl.pallas_call(
        _rwkv_post_kernel,
        grid=(m // tm,),
        in_specs=[tile] * 5 + [vec] * 3,
        out_specs=tile,
        out_shape=jax.ShapeDtypeStruct((m, c), BF16),
        compiler_params=_cparams(("parallel",)),
        name="rwkv_post",
    )(y, r, k, v, g, p["r_k"], p["lnx_g"], p["lnx_b"])


def _combine_kernel(x_ref, ada_ref, lng_ref, lnb_ref, oa_ref, or_ref, ga_ref, gr_ref, wa_ref, wr_ref, wo_ref, o_ref):
    bb, t, d = x_ref.shape
    ua = jnp.dot(oa_ref[...], wa_ref[...], preferred_element_type=F32)
    ur = jnp.dot(or_ref[...], wr_ref[...], preferred_element_type=F32)
    m = jax.nn.sigmoid(ga_ref[...]) * ua + jax.nn.sigmoid(gr_ref[...]) * ur
    mo = jnp.dot(m.astype(BF16), wo_ref[...], preferred_element_type=F32)
    y = DEEPNORM_ALPHA * x_ref[...] + ada_ref[:, 5:6, :] * mo.reshape(bb, t, d)
    o_ref[...] = _ln(y) * lng_ref[1:2, :] + lnb_ref[1:2, :]


def _combine_call(x, ada, ln_g, ln_b, oa, orw, ga, gr, wa, wr, wo, *, blk):
    bsz, t, d = x.shape
    bb, tt = blk
    nt = t // tt
    rows = bb * tt
    xmap = lambda i: (i // nt, i % nt, 0)
    const = lambda i: (0, 0)
    rowt = lambda n: pl.BlockSpec((rows, n), lambda i: (i, 0))
    return pl.pallas_call(
        _combine_kernel,
        grid=((bsz // bb) * nt,),
        in_specs=[pl.BlockSpec((bb, tt, d), xmap),
                  pl.BlockSpec((bb, N_ADA, d), lambda i: (i // nt, 0, 0)),
                  pl.BlockSpec((3, d), const), pl.BlockSpec((3, d), const),
                  rowt(C_ATT), rowt(C_RWKV), rowt(d), rowt(d),
                  pl.BlockSpec((C_ATT, d), const), pl.BlockSpec((C_RWKV, d), const), pl.BlockSpec((d, d), const)],
        out_specs=pl.BlockSpec((bb, tt, d), xmap),
        out_shape=jax.ShapeDtypeStruct(x.shape, F32),
        compiler_params=_cparams(("parallel",)),
        name="combine",
    )(x, ada, ln_g, ln_b, oa, orw, ga, gr, wa, wr, wo)


def _rearrange_z(a):
    c3 = 3 * C_RWKV
    pad = [(0, 0)] * (a.ndim - 1) + [(0, LORA_PAD - D_DECAY_LORA)]
    return jnp.concatenate([a[..., :c3],
                            jnp.pad(a[..., c3:c3 + D_DECAY_LORA], pad),
                            jnp.pad(a[..., c3 + D_DECAY_LORA:c3 + D_DECAY_LORA + D_AAA_LORA], pad),
                            a[..., c3 + D_DECAY_LORA + D_AAA_LORA:]], axis=-1)


def _unarrange_z(a):
    c3 = 3 * C_RWKV
    return jnp.concatenate([a[..., :c3], a[..., c3:c3 + D_DECAY_LORA],
                            a[..., c3 + LORA_PAD:c3 + LORA_PAD + D_AAA_LORA], a[..., c3 + 2 * LORA_PAD:]], axis=-1)


def _state_to_blocks(state):
    bsz = state.shape[0]
    st = jnp.swapaxes(state, -1, -2).reshape(bsz, H_RWKV // 2, 2, HS_RWKV, HS_RWKV)
    blk = jnp.einsum('bphkv,hg->bphkgv', st, jnp.eye(2, dtype=state.dtype))
    return blk.reshape(bsz, H_RWKV // 2, LANES, LANES)


def _blocks_to_state(blk):
    bsz = blk.shape[0]
    b6 = blk.reshape(bsz, H_RWKV // 2, 2, HS_RWKV, 2, HS_RWKV)
    st = jnp.stack([b6[:, :, 0, :, 0, :], b6[:, :, 1, :, 1, :]], axis=2)
    return jnp.swapaxes(st, -1, -2).reshape(bsz, H_RWKV, HS_RWKV, HS_RWKV)


def _trunk(x, ada, attend, shift0, wkv0, w, *, ffn_blk, mm_tm, pre_tm, post_tm, comb_blk):
    bsz, t, d = x.shape
    m = bsz * t
    x1, h2 = _ffn_call(x, ada, w["ln_g"], w["ln_b"], w["wi0"], w["wo0"], sub=0, emit_next=True, blk=ffn_blk)
    h2f = h2.reshape(m, d)
    q = _mm_call(h2f, w["wq"], tm=mm_tm, tn=C_ATT, name="proj_q").reshape(bsz, t, C_ATT)
    k = _mm_call(h2f, w["wk"], tm=mm_tm, tn=C_ATT, name="proj_k").reshape(bsz, t, C_ATT)
    v = _mm_call(h2f, w["wv"], tm=mm_tm, tn=C_ATT, name="proj_v").reshape(bsz, t, C_ATT)
    z = _mm_call(h2f, w["wz"], tm=mm_tm, tn=Z_PAD // 2, name="proj_z").reshape(bsz, t, Z_PAD)
    ga = _mm_call(h2f, w["wga"], tm=mm_tm, tn=d // 2, name="proj_ga")
    gr = _mm_call(h2f, w["wgr"], tm=mm_tm, tn=d // 2, name="proj_gr")
    o_att = attend(q, k, v)
    r, k2, vv, kk, b, logd, g = _rwkv_pre_call(z, _rearrange_z(shift0), w, tm=pre_tm)
    tp = -(-t // CHUNK) * CHUNK
    seq = [r, k2, vv, kk, b, logd]
    if tp != t:
        seq = [jnp.pad(a, ((0, 0), (0, tp - t), (0, 0))) for a in seq]
    y, s_blk = _rwkv_chunk_call(*seq, _state_to_blocks(wkv0))
    y = y[:, :t]
    fl = lambda a: a.reshape(m, C_RWKV)
    o_rwkv = _rwkv_post_call(fl(y), fl(r), fl(k2), fl(vv), fl(g), w, tm=post_tm)
    x2 = _combine_call(x1, ada, w["ln_g"], w["ln_b"], o_att.reshape(m, C_ATT), o_rwkv, ga, gr,
                       w["wua"], w["wur"], w["wout"], blk=comb_blk)
    x3 = _ffn_call(x2, ada, w["ln_g"], w["ln_b"], w["wi1"], w["wo1"], sub=2, emit_next=False, blk=ffn_blk)
    shift_new = _unarrange_z(z[:, -1])
    return (x3, k.reshape(bsz, t, H_ATT, HD_ATT), v.reshape(bsz, t, H_ATT, HD_ATT), _blocks_to_state(s_blk), shift_new)


def kernel(x_prompt, x_sample, cache_k, cache_v, state_wkv, state_shift, page_table, c_prompt, c_sample, rel_bias, w_ada, b_ada, ln_g, ln_b, ffn_wi, ffn_wo, w_in, mu_shift, w0, w2, a0, a2, g2, k_k, k_a, r_k, lnx_g, lnx_b, w_up_attn, w_up_rwkv, w_out):
    assert w_ada.shape[0] == DEPTH == 1
    bsz, s_len, d = x_prompt.shape
    dbs, ds, _ = x_sample.shape
    past_len = page_table.shape[1] * PAGE_SIZE
    n_phys = cache_k.shape[1]
    l = 0
    win = w_in[l]
    c3 = 3 * C_ATT
    lora_rows = ((0, LORA_PAD - D_DECAY_LORA), (0, 0))
    w = {
        "ln_g": ln_g[l], "ln_b": ln_b[l],
        "wi0": ffn_wi[l, 0].astype(BF16), "wo0": ffn_wo[l, 0].astype(BF16),
        "wi1": ffn_wi[l, 1].astype(BF16), "wo1": ffn_wo[l, 1].astype(BF16),
        "wq": win[:, :C_ATT].astype(BF16), "wk": win[:, C_ATT:2 * C_ATT].astype(BF16),
        "wv": win[:, 2 * C_ATT:c3].astype(BF16),
        "wz": _rearrange_z(win[:, c3:c3 + RWKV_PROJ]).astype(BF16),
        "wga": win[:, c3 + RWKV_PROJ:c3 + RWKV_PROJ + d].astype(BF16),
        "wgr": win[:, c3 + RWKV_PROJ + d:].astype(BF16),
        "mu": _rearrange_z(mu_shift[l])[None, :],
        "w0": w0[l][None, :], "w2": jnp.pad(w2[l], lora_rows).astype(BF16),
        "a0": a0[l][None, :], "a2": jnp.pad(a2[l], lora_rows).astype(BF16),
        "g2": g2[l].astype(BF16), "k_k": k_k[l][None, :], "k_a": k_a[l][None, :],
        "r_k": r_k[l].reshape(1, C_RWKV), "lnx_g": lnx_g[l][None, :], "lnx_b": lnx_b[l][None, :],
        "wua": w_up_attn[l].astype(BF16), "wur": w_up_rwkv[l].astype(BF16), "wout": w_out[l].astype(BF16),
    }
    n_c = bsz + dbs
    c_rows = -(-n_c // 8) * 8
    c_all = jnp.concatenate([c_prompt, c_sample, jnp.zeros((c_rows - n_c, d), F32)], axis=0)
    ada = _ada_call(c_all, w_ada[l], b_ada[l]).reshape(c_rows, N_ADA, d)

    bias_tiles = _bias_tiles_call(rel_bias)
    far = rel_bias[NUM_BUCKETS - 1]
    attend_p = lambda q, k, v: _attn_prompt_call(q, k, v, bias_tiles, far)
    yp, kp, vp, wp, sp = _trunk(
        x_prompt, ada[:bsz], attend_p, jnp.zeros((bsz, RWKV_PROJ), F32),
        jnp.zeros((bsz, H_RWKV, HS_RWKV, HS_RWKV), F32), w,
        ffn_blk=(1, 512), mm_tm=1024, pre_tm=256, post_tm=256, comb_blk=(1, 256))

    cache_k2 = cache_k.reshape(DEPTH * n_phys, PAGE_SIZE, H_ATT, HD_ATT)
    cache_v2 = cache_v.reshape(DEPTH * n_phys, PAGE_SIZE, H_ATT, HD_ATT)
    page_table = page_table + l * n_phys
    n_full = past_len // MOBA_BLOCK

    def attend_s(q, k, v):
        means = _cache_means_call(page_table, cache_k2, n_full).reshape(dbs, n_full, C_ATT)
        ids = _topk_ids_call(q, means)[..., :MOBA_TOPK]
        o = _attn_sample_call(page_table, ids.reshape(-1), rel_bias, q, k, v, cache_k2, cache_v2, past_len)
        return o.astype(BF16)

    ys, kn, vn, wn, sn = _trunk(
        x_sample, ada[bsz:n_c], attend_s, state_shift[l], state_wkv[l], w,
        ffn_blk=(dbs, ds), mm_tm=dbs * ds, pre_tm=ds, post_tm=dbs * ds, comb_blk=(dbs, ds))

    return (yp, ys, kp[None], vp[None], kn[None], vn[None], wp[None], wn[None], sp[None], sn[None])
```

```python
import functools
import math

import jax
import jax.numpy as jnp
from jax import lax
from jax.experimental import pallas as pl
from jax.experimental.pallas import tpu as pltpu

F32 = jnp.float32
BF16 = jnp.bfloat16

D_MODEL = 2048
D_FF = 5632
N_ADA = 9
H_ATT = 8
HD_ATT = 128
C_ATT = H_ATT * HD_ATT
MOBA_BLOCK = 256
MOBA_TOPK = 3
NUM_BUCKETS = 32
MAX_DISTANCE = 128
PAGE_SIZE = 128
HS_RWKV = 64
C_RWKV = 1024
H_RWKV = C_RWKV // HS_RWKV
D_DECAY_LORA = 96
D_AAA_LORA = 96
D_GATE_LORA = 256
RWKV_PROJ = 3 * C_RWKV + D_DECAY_LORA + D_AAA_LORA + D_GATE_LORA
LORA_PAD = 128
Z_PAD = 3 * C_RWKV + 2 * LORA_PAD + D_GATE_LORA
GN_EPS = 64e-5
LN_EPS = 1e-5
DEPTH = 1
DEEPNORM_ALPHA = (2 * DEPTH) ** 0.25
NEG_BIG = -1e30

LANES = 128
CHUNK = 64
PAIRS_PER_STEP = 8
VMEM_LIMIT = 56 * 1024 * 1024


def _t5_thresholds():
    max_exact = NUM_BUCKETS // 2
    thr = list(range(1, max_exact + 1))
    for b in range(max_exact + 1, NUM_BUCKETS):
        x = max_exact * (MAX_DISTANCE / max_exact) ** ((b - max_exact) / (NUM_BUCKETS - max_exact))
        thr.append(int(math.ceil(x)))
    return tuple(thr)


T5_THR = _t5_thresholds()


def _cparams(sem, vmem=VMEM_LIMIT):
    return pltpu.CompilerParams(dimension_semantics=sem, vmem_limit_bytes=vmem)


def _ln(x):
    mu = jnp.mean(x, axis=-1, keepdims=True)
    xc = x - mu
    var = jnp.mean(xc * xc, axis=-1, keepdims=True)
    return xc * lax.rsqrt(var + LN_EPS)


def _split(x, n):
    parts = []
    for i in range(n):
        p = x.astype(BF16)
        parts.append(p)
        if i + 1 < n:
            x = x - p.astype(F32)
    return parts


_NN = (((1,), (0,)), ((), ()))
_NT = (((1,), (1,)), ((), ()))
_TN = (((0,), (0,)), ((), ()))


def _dotp(ap, bp, dims=_NN, order=None):
    if order is None:
        order = max(len(ap), len(bp))
    acc = None
    for i, a in enumerate(ap):
        for j, b in enumerate(bp):
            if i + j < order:
                t = lax.dot_general(a, b, dims, preferred_element_type=F32)
                acc = t if acc is None else acc + t
    return acc


def _t5_bias(rel, tbl_ref, h):
    bias = jnp.full(rel.shape, tbl_ref[0, h], F32)
    for b in range(1, NUM_BUCKETS):
        bias = jnp.where(rel >= T5_THR[b - 1], tbl_ref[b, h], bias)
    return bias


def _seg_ones():
    r = lax.broadcasted_iota(jnp.int32, (LANES, LANES), 0) // HS_RWKV
    c = lax.broadcasted_iota(jnp.int32, (LANES, LANES), 1) // HS_RWKV
    return jnp.where(r == c, 1.0, 0.0).astype(BF16)


def _segsum64(x, p128):
    outs = []
    for j in range(x.shape[-1] // LANES):
        xs = x[:, j * LANES:(j + 1) * LANES]
        outs.append(_dotp(_split(xs, 3), [p128]))
    return jnp.concatenate(outs, axis=-1)


ADA_TN = 1024


def _ada_kernel(c_ref, w_ref, b_ref, o_ref):
    c = c_ref[...]
    sc = c * jax.nn.sigmoid(c)
    rows = sc.shape[0]
    s0, s1, s2 = _split(sc, 3)
    w_hi, w_lo = _split(w_ref[...], 2)
    a = jnp.dot(jnp.concatenate([s0, s1, s2], axis=0), w_hi, preferred_element_type=F32)
    b = jnp.dot(jnp.concatenate([s0, s1], axis=0), w_lo, preferred_element_type=F32)
    o_ref[...] = (a[:rows] + a[rows:2 * rows] + a[2 * rows:] + b[:rows] + b[rows:]) + b_ref[...]


def _ada_call(c_all, w_ada, b_ada):
    rows, d = c_all.shape
    n = w_ada.shape[1]
    return pl.pallas_call(
        _ada_kernel,
        grid=(n // ADA_TN,),
        in_specs=[pl.BlockSpec((rows, d), lambda j: (0, 0)),
                  pl.BlockSpec((d, ADA_TN), lambda j: (0, j)),
                  pl.BlockSpec((1, ADA_TN), lambda j: (0, j))],
        out_specs=pl.BlockSpec((rows, ADA_TN), lambda j: (0, j)),
        out_shape=jax.ShapeDtypeStruct((rows, n), F32),
        compiler_params=_cparams(("parallel",)),
        name="ada",
    )(c_all, w_ada, b_ada.reshape(1, n))


FFN_TF = 512


def _ffn_kernel(x_ref, ada_ref, lng_ref, lnb_ref, wig_ref, wiu_ref, wo_ref, *rest, sub, emit_next):
    if emit_next:
        o_ref, h_next_ref, h_scr, acc = rest
    else:
        o_ref, h_scr, acc = rest
    bb, t, d = x_ref.shape
    j = pl.program_id(1)

    @pl.when(j == 0)
    def _():
        h = _ln(x_ref[...]) * (1.0 + ada_ref[:, 3 * sub + 1:3 * sub + 2, :]) + ada_ref[:, 3 * sub:3 * sub + 1, :]
        h_scr[...] = h.reshape(bb * t, d).astype(BF16)
        acc[...] = jnp.zeros_like(acc)

    h = h_scr[...]
    g = jnp.dot(h, wig_ref[...], preferred_element_type=F32)
    u = jnp.dot(h, wiu_ref[...], preferred_element_type=F32)
    act = (g * jax.nn.sigmoid(g)) * u
    acc[...] += jnp.dot(act.astype(BF16), wo_ref[...], preferred_element_type=F32)

    @pl.when(j == pl.num_programs(1) - 1)
    def _():
        gate = ada_ref[:, 3 * sub + 2:3 * sub + 3, :]
        y = DEEPNORM_ALPHA * x_ref[...] + 0.5 * gate * acc[...].reshape(bb, t, d)
        y = _ln(y) * lng_ref[sub:sub + 1, :] + lnb_ref[sub:sub + 1, :]
        o_ref[...] = y
        if emit_next:
            nxt = sub + 1
            hn = _ln(y) * (1.0 + ada_ref[:, 3 * nxt + 1:3 * nxt + 2, :]) + ada_ref[:, 3 * nxt:3 * nxt + 1, :]
            h_next_ref[...] = hn.astype(BF16)


def _ffn_call(x, ada, ln_g, ln_b, wi, wo, *, sub, emit_next, blk):
    bsz, t, d = x.shape
    bb, tt = blk
    nj = D_FF // FFN_TF
    grid = (bsz // bb, t // tt, nj)
    nt = t // tt
    grid = ((bsz // bb) * nt, nj)
    xmap = lambda i, j: (i // nt, i % nt, 0)
    amap = lambda i, j: (i // nt, 0, 0)
    out_shape = [jax.ShapeDtypeStruct(x.shape, F32)]
    out_specs = [pl.BlockSpec((bb, tt, d), xmap)]
    if emit_next:
        out_shape.append(jax.ShapeDtypeStruct(x.shape, BF16))
        out_specs.append(pl.BlockSpec((bb, tt, d), xmap))
    res = pl.pallas_call(
        functools.partial(_ffn_kernel, sub=sub, emit_next=emit_next),
        grid=grid,
        in_specs=[pl.BlockSpec((bb, tt, d), xmap),
                  pl.BlockSpec((bb, N_ADA, d), amap),
                  pl.BlockSpec((3, d), lambda i, j: (0, 0)),
                  pl.BlockSpec((3, d), lambda i, j: (0, 0)),
                  pl.BlockSpec((d, FFN_TF), lambda i, j: (0, j)),
                  pl.BlockSpec((d, FFN_TF), lambda i, j: (0, j + nj)),
                  pl.BlockSpec((FFN_TF, d), lambda i, j: (j, 0))],
        out_specs=out_specs,
        out_shape=out_shape,
        scratch_shapes=[pltpu.VMEM((bb * tt, d), BF16), pltpu.VMEM((bb * tt, d), F32)],
        compiler_params=_cparams(("parallel", "arbitrary")),
        name=f"ffn{sub}",
    )(x, ada, ln_g, ln_b, wi, wi, wo)
    return res if emit_next else res[0]


def _mm_kernel(h_ref, w_ref, o_ref):
    o_ref[...] = jnp.dot(h_ref[...], w_ref[...], preferred_element_type=F32)


def _mm_call(h, w, *, tm, tn, col0, n, name="proj"):
    m, k = h.shape
    assert col0 % tn == 0 and n % tn == 0
    cb = col0 // tn
    return pl.pallas_call(
        _mm_kernel,
        grid=(m // tm, n // tn),
        in_specs=[pl.BlockSpec((tm, k), lambda i, j: (i, 0)),
                  pl.BlockSpec((k, tn), lambda i, j: (0, j + cb))],
        out_specs=pl.BlockSpec((tm, tn), lambda i, j: (i, j)),
        out_shape=jax.ShapeDtypeStruct((m, n), F32),
        compiler_params=_cparams(("parallel", "arbitrary")),
        name=name,
    )(h, w)


def _bias_tiles_kernel(tbl_ref, o_ref):
    h = pl.program_id(0)
    r = lax.broadcasted_iota(jnp.int32, (MOBA_BLOCK, MOBA_BLOCK), 0)
    c = lax.broadcasted_iota(jnp.int32, (MOBA_BLOCK, MOBA_BLOCK), 1)
    rel0 = r - c
    o_ref[0, 0] = jnp.where(rel0 >= 0, _t5_bias(rel0, tbl_ref, h), NEG_BIG)
    o_ref[0, 1] = _t5_bias(rel0 + MOBA_BLOCK, tbl_ref, h)


def _bias_tiles_call(rel_bias):
    return pl.pallas_call(
        _bias_tiles_kernel,
        grid=(H_ATT,),
        in_specs=[pl.BlockSpec(memory_space=pltpu.SMEM)],
        out_specs=pl.BlockSpec((1, 2, MOBA_BLOCK, MOBA_BLOCK), lambda h: (h, 0, 0, 0)),
        out_shape=jax.ShapeDtypeStruct((H_ATT, 2, MOBA_BLOCK, MOBA_BLOCK), F32),
        compiler_params=_cparams(("arbitrary",)),
        name="bias_tiles",
    )(rel_bias)


def _rank_select(s, n_valid):
    col = lax.broadcasted_iota(jnp.int32, s.shape, 1)
    cnt = jnp.zeros(s.shape, jnp.int32)
    for m in range(n_valid):
        sm = s[:, m:m + 1]
        beats = (sm > s) | ((sm == s) & (m < col))
        cnt = cnt + jnp.where(beats, 1, 0)
    return (col < n_valid) & (cnt < MOBA_TOPK)


def _attn_prompt_kernel(far_ref, q_ref, k_ref, v_ref, bt_ref, o_ref):
    h = pl.program_id(1)
    s_len = q_ref.shape[1]
    nb = s_len // MOBA_BLOCK
    q = q_ref[0]
    k = k_ref[0]
    kb = k.astype(BF16)
    vb = v_ref[0].astype(BF16)
    qb = (q * (HD_ATT ** -0.5)).astype(BF16)
    means = jnp.mean(k.reshape(nb, MOBA_BLOCK, HD_ATT), axis=1)
    means = jnp.concatenate([means, jnp.zeros((LANES - nb, HD_ATT), F32)], axis=0)
    mparts = _split(means, 3)
    far = far_ref[h]
    rows = [slice(i * MOBA_BLOCK, (i + 1) * MOBA_BLOCK) for i in range(nb)]
    logits = [lax.dot_general(qb[rows[i]], kb[:(i + 1) * MOBA_BLOCK], _NT, preferred_element_type=F32)
              for i in range(nb)]
    scores = [None] + [_dotp(_split(q[rows[i]], 3), mparts, _NT, order=3) for i in range(1, nb)]
    negs = [None] + [jnp.where(_rank_select(scores[i], i), 0.0, NEG_BIG) for i in range(1, nb)]
    probs, dens = [], []
    for i in range(nb):
        tiles = []
        for j in range(i + 1):
            tile = logits[i][:, j * MOBA_BLOCK:(j + 1) * MOBA_BLOCK]
            if j == i:
                tile = tile + bt_ref[0, 0]
            elif j == i - 1:
                tile = tile + bt_ref[0, 1] + negs[i][:, j:j + 1]
            else:
                tile = tile + (far + negs[i][:, j:j + 1])
            tiles.append(tile)
        lg = jnp.concatenate(tiles, axis=-1) if len(tiles) > 1 else tiles[0]
        mx = jnp.max(lg, axis=-1, keepdims=True)
        p = jnp.exp(lg - mx)
        dens.append(jnp.sum(p, axis=-1, keepdims=True))
        probs.append(p.astype(BF16))
    for i in range(nb):
        out = jnp.dot(probs[i], vb[:(i + 1) * MOBA_BLOCK], preferred_element_type=F32) / dens[i]
        o_ref[0, rows[i], :] = out.astype(o_ref.dtype)


def _attn_prompt_call(q, k, v, bias_tiles, far):
    bsz, s_len, _ = q.shape
    qspec = pl.BlockSpec((1, s_len, HD_ATT), lambda b, h: (b, 0, h))
    return pl.pallas_call(
        _attn_prompt_kernel,
        grid=(bsz, H_ATT),
        in_specs=[pl.BlockSpec(memory_space=pltpu.SMEM), qspec, qspec, qspec,
                  pl.BlockSpec((1, 2, MOBA_BLOCK, MOBA_BLOCK), lambda b, h: (h, 0, 0, 0))],
        out_specs=pl.BlockSpec((1, s_len, HD_ATT), lambda b, h: (b, 0, h)),
        out_shape=jax.ShapeDtypeStruct((bsz, s_len, C_ATT), BF16),
        compiler_params=_cparams(("parallel", "arbitrary")),
        name="attn_prompt",
    )(far, q, k, v, bias_tiles)


MEAN_BLOCKS = 8
PAGES_PER_BLOCK = MOBA_BLOCK // PAGE_SIZE


def _cache_means_kernel(pt_ref, *refs):
    page_refs, o_ref = refs[:-1], refs[-1]
    for m in range(MEAN_BLOCKS):
        tot = None
        for u in range(PAGES_PER_BLOCK):
            s = jnp.sum(page_refs[m * PAGES_PER_BLOCK + u][0], axis=0)
            tot = s if tot is None else tot + s
        o_ref[0, m] = tot * (1.0 / MOBA_BLOCK)


def _cache_means_call(page_table, cache_k4, n_blocks):
    dbs = page_table.shape[0]
    npg = MEAN_BLOCKS * PAGES_PER_BLOCK
    in_specs = [pl.BlockSpec((1, PAGE_SIZE, H_ATT, HD_ATT), functools.partial(
        lambda b, g, pt, u: (pt[b, g * npg + u], 0, 0, 0), u=u)) for u in range(npg)]
    return pl.pallas_call(
        _cache_means_kernel,
        grid_spec=pltpu.PrefetchScalarGridSpec(
            num_scalar_prefetch=1,
            grid=(dbs, n_blocks // MEAN_BLOCKS),
            in_specs=in_specs,
            out_specs=pl.BlockSpec((1, MEAN_BLOCKS, H_ATT, HD_ATT), lambda b, g, pt: (b, g, 0, 0)),
        ),
        out_shape=jax.ShapeDtypeStruct((dbs, n_blocks, H_ATT, HD_ATT), F32),
        compiler_params=_cparams(("parallel", "arbitrary")),
        name="cache_means",
    )(page_table, *([cache_k4] * npg))


def _topk_ids_kernel(q_ref, m_ref, o_ref):
    q = q_ref[0]
    means = m_ref[0]
    n_blocks = means.shape[0]
    ds = q.shape[0]
    pad = jnp.zeros((LANES - n_blocks, HD_ATT), F32)
    col = lax.broadcasted_iota(jnp.int32, (ds, LANES), 1)
    for h in range(H_ATT):
        sl = slice(h * HD_ATT, (h + 1) * HD_ATT)
        mh = jnp.concatenate([means[:, sl], pad], axis=0) if n_blocks < LANES else means[:, sl]
        s = _dotp(_split(q[:, sl], 3), _split(mh, 3), _NT, order=3)
        s = jnp.where(col < n_blocks, s, NEG_BIG)
        ids = jnp.zeros((ds, LANES), jnp.int32)
        for t in range(MOBA_TOPK):
            mx = jnp.max(s, axis=-1, keepdims=True)
            idx = jnp.min(jnp.where(s == mx, col, LANES), axis=-1, keepdims=True)
            ids = jnp.where(col == t, idx, ids)
            s = jnp.where(col == idx, NEG_BIG * 2, s)
        o_ref[0, h] = ids


def _topk_ids_call(q, means):
    dbs, ds, _ = q.shape
    n_blocks = means.shape[1]
    return pl.pallas_call(
        _topk_ids_kernel,
        grid=(dbs,),
        in_specs=[pl.BlockSpec((1, ds, C_ATT), lambda b: (b, 0, 0)),
                  pl.BlockSpec((1, n_blocks, C_ATT), lambda b: (b, 0, 0))],
        out_specs=pl.BlockSpec((1, H_ATT, ds, LANES), lambda b: (b, 0, 0, 0)),
        out_shape=jax.ShapeDtypeStruct((dbs, H_ATT, ds, LANES), jnp.int32),
        compiler_params=_cparams(("parallel",)),
        name="topk_ids",
    )(q, means)


N_SEL_PAGES = MOBA_TOPK * PAGES_PER_BLOCK


def _attn_sample_kernel(pt_ref, ids_ref, tbl_ref, q_ref, kn_ref, vn_ref, ck_hbm, cv_hbm, o_ref,
                        kbuf, vbuf, sems, *, past_len):
    ds = q_ref.shape[1]
    n_pg = ds * N_SEL_PAGES
    b, h = pl.program_id(0), pl.program_id(1)
    nh = pl.num_programs(1)
    step = b * nh + h
    n_steps = pl.num_programs(0) * nh
    slot = step % 2

    def page_copies(bb, hh, sl):
        base = (bb * H_ATT + hh) * ds * MOBA_TOPK
        cps = []
        for u in range(n_pg):
            pg = pt_ref[bb, ids_ref[base + u // PAGES_PER_BLOCK] * PAGES_PER_BLOCK + u % PAGES_PER_BLOCK]
            cps.append(pltpu.make_async_copy(ck_hbm.at[pg, :, hh, :], kbuf.at[sl, u], sems.at[sl, u]))
            cps.append(pltpu.make_async_copy(cv_hbm.at[pg, :, hh, :], vbuf.at[sl, u], sems.at[sl, n_pg + u]))
        return cps

    @pl.when(step == 0)
    def _():
        for cp in page_copies(b, h, slot):
            cp.start()

    @pl.when(step + 1 < n_steps)
    def _():
        nxt = step + 1
        for cp in page_copies(nxt // nh, nxt % nh, 1 - slot):
            cp.start()

    for cp in page_copies(b, h, slot):
        cp.wait()
    kp_refs = [kbuf.at[slot, u] for u in range(n_pg)]
    vp_refs = [vbuf.at[slot, u] for u in range(n_pg)]
    qb = (q_ref[0] * (HD_ATT ** -0.5)).astype(BF16)
    qrow = lax.broadcasted_iota(jnp.int32, (ds, 1), 0)
    lane = lax.broadcasted_iota(jnp.int32, (ds, PAGE_SIZE), 1)
    lo = lax.dot_general(qb, kn_ref[0].astype(BF16), _NT, preferred_element_type=F32)
    rel_o = lax.broadcasted_iota(jnp.int32, (ds, ds), 0) - lax.broadcasted_iota(jnp.int32, (ds, ds), 1)
    lo = jnp.where(rel_o >= 0, lo + _t5_bias(rel_o, tbl_ref, h), NEG_BIG)
    mx_o = jnp.max(lo, axis=-1, keepdims=True)
    vn = vn_ref[0].astype(BF16)
    tiles = []
    for qi in range(ds):
        flat = (b * H_ATT + h) * ds + qi
        tq = []
        for s in range(MOBA_TOPK):
            blk = ids_ref[flat * MOBA_TOPK + s]
            for u in range(PAGES_PER_BLOCK):
                kpg = kp_refs[(qi * MOBA_TOPK + s) * PAGES_PER_BLOCK + u][...].astype(BF16)
                lg = lax.dot_general(qb, kpg, _NT, preferred_element_type=F32)
                rel = (past_len + qi) - (blk * MOBA_BLOCK + u * PAGE_SIZE + lane)
                tq.append(lg + _t5_bias(rel, tbl_ref, h))
        tiles.append(tq)
    mxs = []
    for tq in tiles:
        mx = mx_o
        for t in tq:
            mx = jnp.maximum(mx, jnp.max(t, axis=-1, keepdims=True))
        mxs.append(mx)
    pos = [jnp.exp(lo - mx) for mx in mxs]
    ps = [[jnp.exp(t - mx) for t in tq] for tq, mx in zip(tiles, mxs)]
    accs = [jnp.dot(po.astype(BF16), vn, preferred_element_type=F32) for po in pos]
    for qi in range(ds):
        for i in range(N_SEL_PAGES):
            accs[qi] = accs[qi] + jnp.dot(ps[qi][i].astype(BF16), vp_refs[qi * N_SEL_PAGES + i][...].astype(BF16),
                                          preferred_element_type=F32)
    result = jnp.zeros((ds, HD_ATT), F32)
    for qi in range(ds):
        den = jnp.sum(pos[qi], axis=-1, keepdims=True)
        for p in ps[qi]:
            den = den + jnp.sum(p, axis=-1, keepdims=True)
        result = jnp.where(qrow == qi, accs[qi] / den, result)
    o_ref[0] = result


def _attn_sample_call(page_table, ids, rel_bias, q, k_new, v_new, cache_k2, cache_v2, past_len):
    dbs, ds, _ = q.shape
    n_pg = ds * N_SEL_PAGES

    nspec = pl.BlockSpec((1, ds, HD_ATT), lambda b, h, p, i: (b, 0, h))
    pool = pl.BlockSpec(memory_space=pl.ANY)
    return pl.pallas_call(
        functools.partial(_attn_sample_kernel, past_len=past_len),
        grid_spec=pltpu.PrefetchScalarGridSpec(
            num_scalar_prefetch=2,
            grid=(dbs, H_ATT),
            in_specs=[pl.BlockSpec(memory_space=pltpu.SMEM), nspec, nspec, nspec, pool, pool],
            out_specs=pl.BlockSpec((1, ds, HD_ATT), lambda b, h, p, i: (b, 0, h)),
            scratch_shapes=[pltpu.VMEM((2, n_pg, PAGE_SIZE, HD_ATT), F32),
                            pltpu.VMEM((2, n_pg, PAGE_SIZE, HD_ATT), F32),
                            pltpu.SemaphoreType.DMA((2, 2 * n_pg))],
        ),
        out_shape=jax.ShapeDtypeStruct((dbs, ds, C_ATT), F32),
        compiler_params=_cparams(("arbitrary", "arbitrary")),
        name="attn_sample",
    )(page_table, ids, rel_bias, q, k_new, v_new, cache_k2, cache_v2)


def _rwkv_pre_kernel(z_ref, zp_ref, sh0_ref, mu_ref, w0_ref, w2_ref, a0_ref, a2_ref, g2_ref, kk_ref, ka_ref,
                     r_o, k_o, v_o, kk_o, b_o, ld_o, g_o):
    i = pl.program_id(1)
    z = z_ref[0]
    tm = z.shape[0]
    first = jnp.where(i == 0, sh0_ref[0], zp_ref[0, 7:8, :])
    row = lax.broadcasted_iota(jnp.int32, (tm, 1), 0)
    zprev = jnp.where(row == 0, first, pltpu.roll(z, 1, 0))
    zs = z + (zprev - z) * mu_ref[...]
    c = C_RWKV
    r, k, v = zs[:, :c], zs[:, c:2 * c], zs[:, 2 * c:3 * c]
    zw = zs[:, 3 * c:3 * c + LORA_PAD]
    za = zs[:, 3 * c + LORA_PAD:3 * c + 2 * LORA_PAD]
    zg = zs[:, 3 * c + 2 * LORA_PAD:]
    wl = w0_ref[...] + jnp.dot(jnp.tanh(zw).astype(BF16), w2_ref[...], preferred_element_type=F32)
    w = -(jnp.maximum(-wl, 0.0) + jnp.log(1.0 + jnp.exp(-jnp.abs(wl)))) - 0.5
    ld_o[0] = -jnp.exp(w)
    a = jax.nn.sigmoid(a0_ref[...] + jnp.dot(za.astype(BF16), a2_ref[...], preferred_element_type=F32))
    g_o[0] = jnp.dot(jax.nn.sigmoid(zg).astype(BF16), g2_ref[...], preferred_element_type=F32)
    kkr = k * kk_ref[...]
    n2 = _segsum64(kkr * kkr, _seg_ones())
    kkn = kkr / jnp.maximum(jnp.sqrt(n2), 1e-12)
    r_o[0] = r
    k_o[0] = k * (1.0 + (a - 1.0) * ka_ref[...])
    v_o[0] = v
    kk_o[0] = kkn
    b_o[0] = kkn * a


def _rwkv_pre_call(z, shift0, p, *, tm):
    bsz, t, zp = z.shape
    nt = t // tm
    c = C_RWKV
    row = lambda b, i: (0, 0)
    vec = lambda n: pl.BlockSpec((1, n), row)
    tile = pl.BlockSpec((1, tm, c), lambda b, i: (b, i, 0))
    return pl.pallas_call(
        _rwkv_pre_kernel,
        grid=(bsz, nt),
        in_specs=[pl.BlockSpec((1, tm, zp), lambda b, i: (b, i, 0)),
                  pl.BlockSpec((1, 8, zp), lambda b, i: (b, jnp.maximum(i * (tm // 8) - 1, 0), 0)),
                  pl.BlockSpec((1, 1, zp), lambda b, i: (b, 0, 0)),
                  vec(zp), vec(c),
                  pl.BlockSpec((LORA_PAD, c), row), vec(c),
                  pl.BlockSpec((LORA_PAD, c), row),
                  pl.BlockSpec((D_GATE_LORA, c), row), vec(c), vec(c)],
        out_specs=[tile] * 7,
        out_shape=[jax.ShapeDtypeStruct((bsz, t, c), F32)] * 7,
        compiler_params=_cparams(("parallel", "arbitrary")),
        name="rwkv_pre",
    )(z, z, shift0.reshape(bsz, 1, zp), p["mu"], p["w0"], p["w2"], p["a0"], p["a2"], p["g2"], p["k_k"], p["k_a"])


CHUNK_PREC = {"gram": (1, 1), "state_read": (1, 1), "mkv": (1, 1), "solve1": (2, 2), "solve_sq": (2, 2),
              "solve_ap": (2, 2), "out": (1, 1), "state_upd": (1, 1)}


def _chunk_pairs(rs, ks, vs, kks, bs, lds, ss):
    c = CHUNK
    lane = lax.broadcasted_iota(jnp.int32, (1, LANES), 1)
    m_a = jnp.where(lane < HS_RWKV, 1.0, 0.0)
    m_b = 1.0 - m_a
    row = lax.broadcasted_iota(jnp.int32, (c, 2 * c), 0)
    coli = lax.broadcasted_iota(jnp.int32, (c, 2 * c), 1) % c
    strict = coli < row
    incl = coli <= row
    lr = lax.broadcasted_iota(jnp.int32, (c, c), 0)
    lc = lax.broadcasted_iota(jnp.int32, (c, c), 1)
    ltri = jnp.where(lc <= lr, 1.0, 0.0).astype(BF16)
    ones = jnp.ones((c, LANES), BF16)
    rr = lax.broadcasted_iota(jnp.int32, (LANES, LANES), 0) // HS_RWKV
    cc = lax.broadcasted_iota(jnp.int32, (LANES, LANES), 1) // HS_RWKV
    same_head = rr == cc

    def each(f, *lists):
        return [f(*args) for args in zip(*lists)]

    def stack2(x):
        return jnp.concatenate([x * m_a, x * m_b], axis=0)

    def prod(site, a, b, dims=_NN):
        na, nb = CHUNK_PREC[site]
        return _dotp(_split(a, na), _split(b, nb), dims, order=max(na, nb))

    def pm(site, mcats, xs):
        return each(lambda m, x: prod(site, m, stack2(x)), mcats, xs)

    ldp = each(lambda x: _split(x, 3), lds)
    cums = each(lambda p: _dotp([ltri], p, _NN), ldp)
    gcols = each(lambda p: jnp.exp(_dotp(p, [ones], _TN)), ldp)
    g_inv = each(lambda cu: jnp.exp(-cu), cums)
    g_end = each(lambda cu: jnp.exp(cu[c - 1:c, :] - cu), cums)
    p_all = each(lambda kk, r, cu, ld: jnp.concatenate([kk * jnp.exp(cu - ld), r * jnp.exp(cu)], axis=0),
                 kks, rs, cums, lds)
    z2 = each(lambda k, b, gi: jnp.concatenate([stack2(k * gi), stack2(b * gi)], axis=0), ks, bs, g_inv)
    g4 = each(lambda p, z: prod("gram", p, z, _NT), p_all, z2)
    mk = each(lambda g: jnp.where(strict, g[:c, :2 * c], 0.0), g4)
    pj = each(lambda g: jnp.where(strict, -g[:c, 2 * c:], 0.0), g4)
    akb = each(lambda g: jnp.concatenate([jnp.where(incl, g[c:, :2 * c], 0.0),
                                          jnp.where(incl, -g[c:, 2 * c:], 0.0)], axis=1), g4)
    ps = each(lambda p, s: prod("state_read", p, s), p_all, ss)
    mkv = pm("mkv", mk, vs)
    rhs = each(lambda p, m: p[:c] + m, ps, mkv)
    us = each(lambda x, d: x + d, rhs, pm("solve1", pj, rhs))
    n = 2
    while n < c:
        pj = pm("solve_sq", pj, pj)
        us = each(lambda x, d: x + d, us, pm("solve_ap", pj, us))
        n *= 2
    ys = each(lambda p, m, v, u: p[c:] + prod("out", m, jnp.concatenate([stack2(v), stack2(u)], axis=0)),
              ps, akb, vs, us)
    upd = each(lambda k, b, ge, v, u: prod("state_upd", jnp.concatenate([k * ge, -(b * ge)], axis=0),
                                           jnp.concatenate([v, u], axis=0), _TN), ks, bs, g_end, vs, us)
    s_new = each(lambda gc, s, up: gc * s + jnp.where(same_head, up, 0.0), gcols, ss, upd)
    return ys, s_new


def _rwkv_chunk_kernel(r_ref, k_ref, v_ref, kk_ref, b_ref, ld_ref, s0_ref, y_ref, so_ref, s_scr):
    ci = pl.program_id(2)

    @pl.when(ci == 0)
    def _():
        s_scr[...] = s0_ref[0]

    sls = [slice(p * LANES, (p + 1) * LANES) for p in range(PAIRS_PER_STEP)]
    pairs = lambda ref: [ref[0, :, sl] for sl in sls]
    ys, s_new = _chunk_pairs(pairs(r_ref), pairs(k_ref), pairs(v_ref), pairs(kk_ref), pairs(b_ref), pairs(ld_ref),
                             [s_scr[p] for p in range(PAIRS_PER_STEP)])
    for p, sl in enumerate(sls):
        y_ref[0, :, sl] = ys[p]
        s_scr[p] = s_new[p]
        so_ref[0, p] = s_new[p]


def _rwkv_chunk_call(r, k, v, kk, b, logd, s0_blk):
    bsz, t, c = r.shape
    pp = PAIRS_PER_STEP
    w = pp * LANES
    tile = pl.BlockSpec((1, CHUNK, w), lambda bi, p, ci: (bi, ci, p))
    sspec = pl.BlockSpec((1, pp, LANES, LANES), lambda bi, p, ci: (bi, p, 0, 0))
    return pl.pallas_call(
        _rwkv_chunk_kernel,
        grid=(bsz, c // w, t // CHUNK),
        in_specs=[tile] * 6 + [sspec],
        out_specs=[tile, sspec],
        out_shape=[jax.ShapeDtypeStruct((bsz, t, c), F32),
                   jax.ShapeDtypeStruct(s0_blk.shape, F32)],
        scratch_shapes=[pltpu.VMEM((pp, LANES, LANES), F32)],
        compiler_params=_cparams(("parallel", "parallel", "arbitrary")),
        name="rwkv_chunk",
    )(r, k, v, kk, b, logd, s0_blk)


def _rwkv_post_kernel(y_ref, r_ref, k_ref, v_ref, g_ref, rk_ref, lg_ref, lb_ref, o_ref):
    p128 = _seg_ones()
    y = y_ref[...]
    inv = 1.0 / HS_RWKV
    mu = _segsum64(y, p128) * inv
    yc = y - mu
    var = _segsum64(yc * yc, p128) * inv
    yn = yc * lax.rsqrt(var + GN_EPS) * lg_ref[...] + lb_ref[...]
    bonus = _segsum64(r_ref[...] * k_ref[...] * rk_ref[...], p128) * v_ref[...]
    o_ref[...] = ((yn + bonus) * g_ref[...]).astype(o_ref.dtype)


def _rwkv_post_call(y, r, k, v, g, p, *, tm):
    m, c = y.shape
    tile = pl.BlockSpec((tm, c), lambda i: (i, 0))
    vec = pl.BlockSpec((1, c), lambda i: (0, 0))
    return pl.pallas_call(
        _rwkv_post_kernel,
        grid=(m // tm,),
        in_specs=[tile] * 5 + [vec] * 3,
        out_specs=tile,
        out_shape=jax.ShapeDtypeStruct((m, c), BF16),
        compiler_params=_cparams(("parallel",)),
        name="rwkv_post",
    )(y, r, k, v, g, p["r_k"], p["lnx_g"], p["lnx_b"])


def _combine_kernel(x_ref, ada_ref, lng_ref, lnb_ref, oa_ref, or_ref, ga_ref, gr_ref, wa_ref, wr_ref, wo_ref, o_ref):
    bb, t, d = x_ref.shape
    ua = jnp.dot(oa_ref[...], wa_ref[...], preferred_element_type=F32)
    ur = jnp.dot(or_ref[...], wr_ref[...], preferred_element_type=F32)
    m = jax.nn.sigmoid(ga_ref[...]) * ua + jax.nn.sigmoid(gr_ref[...]) * ur
    mo = jnp.dot(m.astype(BF16), wo_ref[...], preferred_element_type=F32)
    y = DEEPNORM_ALPHA * x_ref[...] + ada_ref[:, 5:6, :] * mo.reshape(bb, t, d)
    o_ref[...] = _ln(y) * lng_ref[1:2, :] + lnb_ref[1:2, :]


def _combine_call(x, ada, ln_g, ln_b, oa, orw, ga, gr, wa, wr, wo, *, blk):
    bsz, t, d = x.shape
    bb, tt = blk
    nt = t // tt
    rows = bb * tt
    xmap = lambda i: (i // nt, i % nt, 0)
    const = lambda i: (0, 0)
    rowt = lambda n: pl.BlockSpec((rows, n), lambda i: (i, 0))
    return pl.pallas_call(
        _combine_kernel,
        grid=((bsz // bb) * nt,),
        in_specs=[pl.BlockSpec((bb, tt, d), xmap),
                  pl.BlockSpec((bb, N_ADA, d), lambda i: (i // nt, 0, 0)),
                  pl.BlockSpec((3, d), const), pl.BlockSpec((3, d), const),
                  rowt(C_ATT), rowt(C_RWKV), rowt(d), rowt(d),
                  pl.BlockSpec((C_ATT, d), const), pl.BlockSpec((C_RWKV, d), const), pl.BlockSpec((d, d), const)],
        out_specs=pl.BlockSpec((bb, tt, d), xmap),
        out_shape=jax.ShapeDtypeStruct(x.shape, F32),
        compiler_params=_cparams(("parallel",)),
        name="combine",
    )(x, ada, ln_g, ln_b, oa, orw, ga, gr, wa, wr, wo)


def _rearrange_z(a):
    c3 = 3 * C_RWKV
    pad = [(0, 0)] * (a.ndim - 1) + [(0, LORA_PAD - D_DECAY_LORA)]
    return jnp.concatenate([a[..., :c3],
                            jnp.pad(a[..., c3:c3 + D_DECAY_LORA], pad),
                            jnp.pad(a[..., c3 + D_DECAY_LORA:c3 + D_DECAY_LORA + D_AAA_LORA], pad),
                            a[..., c3 + D_DECAY_LORA + D_AAA_LORA:]], axis=-1)


def _unarrange_z(a):
    c3 = 3 * C_RWKV
    return jnp.concatenate([a[..., :c3], a[..., c3:c3 + D_DECAY_LORA],
                            a[..., c3 + LORA_PAD:c3 + LORA_PAD + D_AAA_LORA], a[..., c3 + 2 * LORA_PAD:]], axis=-1)


def _state_to_blocks(state):
    bsz = state.shape[0]
    st = jnp.swapaxes(state, -1, -2).reshape(bsz, H_RWKV // 2, 2, HS_RWKV, HS_RWKV)
    blk = jnp.einsum('bphkv,hg->bphkgv', st, jnp.eye(2, dtype=state.dtype))
    return blk.reshape(bsz, H_RWKV // 2, LANES, LANES)


def _blocks_to_state(blk):
    bsz = blk.shape[0]
    b6 = blk.reshape(bsz, H_RWKV // 2, 2, HS_RWKV, 2, HS_RWKV)
    st = jnp.stack([b6[:, :, 0, :, 0, :], b6[:, :, 1, :, 1, :]], axis=2)
    return jnp.swapaxes(st, -1, -2).reshape(bsz, H_RWKV, HS_RWKV, HS_RWKV)


def _trunk(x, ada, attend, shift0, wkv0, w, *, ffn_blk, mm_tm, pre_tm, post_tm, comb_blk):
    bsz, t, d = x.shape
    m = bsz * t
    x1, h2 = _ffn_call(x, ada, w["ln_g"], w["ln_b"], w["wi0"], w["wo0"], sub=0, emit_next=True, blk=ffn_blk)
    h2f = h2.reshape(m, d)
    win = w["win"]
    proj = functools.partial(_mm_call, h2f, win, tm=mm_tm)
    q = proj(tn=C_ATT, col0=0, n=C_ATT, name="proj_q").reshape(bsz, t, C_ATT)
    k = proj(tn=C_ATT, col0=C_ATT, n=C_ATT, name="proj_k").reshape(bsz, t, C_ATT)
    v = proj(tn=C_ATT, col0=2 * C_ATT, n=C_ATT, name="proj_v").reshape(bsz, t, C_ATT)
    ga = proj(tn=d // 2, col0=3 * C_ATT, n=d, name="proj_ga")
    gr = proj(tn=d // 2, col0=3 * C_ATT + d, n=d, name="proj_gr")
    z = proj(tn=Z_PAD // 2, col0=3 * C_ATT + 2 * d, n=Z_PAD, name="proj_z").reshape(bsz, t, Z_PAD)
    o_att = attend(q, k, v)
    r, k2, vv, kk, b, logd, g = _rwkv_pre_call(z, _rearrange_z(shift0), w, tm=pre_tm)
    tp = -(-t // CHUNK) * CHUNK
    seq = [r, k2, vv, kk, b, logd]
    if tp != t:
        seq = [jnp.pad(a, ((0, 0), (0, tp - t), (0, 0))) for a in seq]
    y, s_blk = _rwkv_chunk_call(*seq, _state_to_blocks(wkv0))
    y = y[:, :t]
    fl = lambda a: a.reshape(m, C_RWKV)
    o_rwkv = _rwkv_post_call(fl(y), fl(r), fl(k2), fl(vv), fl(g), w, tm=post_tm)
    x2 = _combine_call(x1, ada, w["ln_g"], w["ln_b"], o_att.reshape(m, C_ATT), o_rwkv, ga, gr,
                       w["wua"], w["wur"], w["wout"], blk=comb_blk)
    x3 = _ffn_call(x2, ada, w["ln_g"], w["ln_b"], w["wi1"], w["wo1"], sub=2, emit_next=False, blk=ffn_blk)
    shift_new = _unarrange_z(z[:, -1])
    return (x3, k.reshape(bsz, t, H_ATT, HD_ATT), v.reshape(bsz, t, H_ATT, HD_ATT), _blocks_to_state(s_blk), shift_new)


def kernel(x_prompt, x_sample, cache_k, cache_v, state_wkv, state_shift, page_table, c_prompt, c_sample, rel_bias, w_ada, b_ada, ln_g, ln_b, ffn_wi, ffn_wo, w_in, mu_shift, w0, w2, a0, a2, g2, k_k, k_a, r_k, lnx_g, lnx_b, w_up_attn, w_up_rwkv, w_out):
    assert w_ada.shape[0] == DEPTH == 1
    bsz, s_len, d = x_prompt.shape
    dbs, ds, _ = x_sample.shape
    past_len = page_table.shape[1] * PAGE_SIZE
    n_phys = cache_k.shape[1]
    l = 0
    win = w_in[l]
    c3 = 3 * C_ATT
    lora_rows = ((0, LORA_PAD - D_DECAY_LORA), (0, 0))
    w = {
        "ln_g": ln_g[l], "ln_b": ln_b[l],
        "wi0": ffn_wi[l, 0].astype(BF16), "wo0": ffn_wo[l, 0].astype(BF16),
        "wi1": ffn_wi[l, 1].astype(BF16), "wo1": ffn_wo[l, 1].astype(BF16),
        "win": jnp.concatenate([win[:, :c3], win[:, c3 + RWKV_PROJ:],
                                _rearrange_z(win[:, c3:c3 + RWKV_PROJ])], axis=1).astype(BF16),
        "mu": _rearrange_z(mu_shift[l])[None, :],
        "w0": w0[l][None, :], "w2": jnp.pad(w2[l], lora_rows).astype(BF16),
        "a0": a0[l][None, :], "a2": jnp.pad(a2[l], lora_rows).astype(BF16),
        "g2": g2[l].astype(BF16), "k_k": k_k[l][None, :], "k_a": k_a[l][None, :],
        "r_k": r_k[l].reshape(1, C_RWKV), "lnx_g": lnx_g[l][None, :], "lnx_b": lnx_b[l][None, :],
        "wua": w_up_attn[l].astype(BF16), "wur": w_up_rwkv[l].astype(BF16), "wout": w_out[l].astype(BF16),
    }
    n_c = bsz + dbs
    c_rows = -(-n_c // 8) * 8
    c_all = jnp.concatenate([c_prompt, c_sample, jnp.zeros((c_rows - n_c, d), F32)], axis=0)
    ada = _ada_call(c_all, w_ada[l], b_ada[l]).reshape(c_rows, N_ADA, d)

    bias_tiles = _bias_tiles_call(rel_bias)
    far = rel_bias[NUM_BUCKETS - 1]
    attend_p = lambda q, k, v: _attn_prompt_call(q, k, v, bias_tiles, far)
    yp, kp, vp, wp, sp = _trunk(
        x_prompt, ada[:bsz], attend_p, jnp.zeros((bsz, RWKV_PROJ), F32),
        jnp.zeros((bsz, H_RWKV, HS_RWKV, HS_RWKV), F32), w,
        ffn_blk=(1, 512), mm_tm=1024, pre_tm=256, post_tm=256, comb_blk=(1, 256))

    cache_k2 = cache_k.reshape(DEPTH * n_phys, PAGE_SIZE, H_ATT, HD_ATT)
    cache_v2 = cache_v.reshape(DEPTH * n_phys, PAGE_SIZE, H_ATT, HD_ATT)
    page_table = page_table + l * n_phys
    n_full = past_len // MOBA_BLOCK

    def attend_s(q, k, v):
        means = _cache_means_call(page_table, cache_k2, n_full).reshape(dbs, n_full, C_ATT)
        ids = _topk_ids_call(q, means)[..., :MOBA_TOPK]
        o = _attn_sample_call(page_table, ids.reshape(-1), rel_bias, q, k, v, cache_k2, cache_v2, past_len)
        return o.astype(BF16)

    ys, kn, vn, wn, sn = _trunk(
        x_sample, ada[bsz:n_c], attend_s, state_shift[l], state_wkv[l], w,
        ffn_blk=(dbs, ds), mm_tm=dbs * ds, pre_tm=ds, post_tm=dbs * ds, comb_blk=(dbs, ds))

    return (yp, ys, kp[None], vp[None], kn[None], vn[None], wp[None], wn[None], sp[None], sn[None])
```

```python
import functools
import math

import jax
import jax.numpy as jnp
from jax import lax
from jax.experimental import pallas as pl
from jax.experimental.pallas import tpu as pltpu

F32 = jnp.float32
BF16 = jnp.bfloat16

D_MODEL = 2048
D_FF = 5632
N_ADA = 9
H_ATT = 8
HD_ATT = 128
C_ATT = H_ATT * HD_ATT
MOBA_BLOCK = 256
MOBA_TOPK = 3
NUM_BUCKETS = 32
MAX_DISTANCE = 128
PAGE_SIZE = 128
HS_RWKV = 64
C_RWKV = 1024
H_RWKV = C_RWKV // HS_RWKV
D_DECAY_LORA = 96
D_AAA_LORA = 96
D_GATE_LORA = 256
RWKV_PROJ = 3 * C_RWKV + D_DECAY_LORA + D_AAA_LORA + D_GATE_LORA
LORA_PAD = 128
Z_PAD = 3 * C_RWKV + 2 * LORA_PAD + D_GATE_LORA
GN_EPS = 64e-5
LN_EPS = 1e-5
DEPTH = 1
DEEPNORM_ALPHA = (2 * DEPTH) ** 0.25
NEG_BIG = -1e30

LANES = 128
CHUNK = 64
PAIRS_PER_STEP = 8
VMEM_LIMIT = 56 * 1024 * 1024


def _t5_thresholds():
    max_exact = NUM_BUCKETS // 2
    thr = list(range(1, max_exact + 1))
    for b in range(max_exact + 1, NUM_BUCKETS):
        x = max_exact * (MAX_DISTANCE / max_exact) ** ((b - max_exact) / (NUM_BUCKETS - max_exact))
        thr.append(int(math.ceil(x)))
    return tuple(thr)


T5_THR = _t5_thresholds()


def _cparams(sem, vmem=VMEM_LIMIT):
    return pltpu.CompilerParams(dimension_semantics=sem, vmem_limit_bytes=vmem)


def _ln(x):
    mu = jnp.mean(x, axis=-1, keepdims=True)
    xc = x - mu
    var = jnp.mean(xc * xc, axis=-1, keepdims=True)
    return xc * lax.rsqrt(var + LN_EPS)


def _split(x, n):
    parts = []
    for i in range(n):
        p = x.astype(BF16)
        parts.append(p)
        if i + 1 < n:
            x = x - p.astype(F32)
    return parts


_NN = (((1,), (0,)), ((), ()))
_NT = (((1,), (1,)), ((), ()))
_TN = (((0,), (0,)), ((), ()))


def _dotp(ap, bp, dims=_NN, order=None):
    if order is None:
        order = max(len(ap), len(bp))
    acc = None
    if dims == _NN and len(ap) > 1:
        m = ap[0].shape[0]
        for j, b in enumerate(bp):
            sel = [a for i, a in enumerate(ap) if i + j < order]
            if not sel:
                continue
            lhs = sel[0] if len(sel) == 1 else jnp.concatenate(sel, axis=0)
            t = lax.dot_general(lhs, b, dims, preferred_element_type=F32)
            for r in range(len(sel)):
                part = t[r * m:(r + 1) * m]
                acc = part if acc is None else acc + part
        return acc
    for i, a in enumerate(ap):
        for j, b in enumerate(bp):
            if i + j < order:
                t = lax.dot_general(a, b, dims, preferred_element_type=F32)
                acc = t if acc is None else acc + t
    return acc


def _t5_bias(rel, tbl_ref, h):
    bias = jnp.full(rel.shape, tbl_ref[0, h], F32)
    for b in range(1, NUM_BUCKETS):
        bias = jnp.where(rel >= T5_THR[b - 1], tbl_ref[b, h], bias)
    return bias


def _seg_ones():
    r = lax.broadcasted_iota(jnp.int32, (LANES, LANES), 0) // HS_RWKV
    c = lax.broadcasted_iota(jnp.int32, (LANES, LANES), 1) // HS_RWKV
    return jnp.where(r == c, 1.0, 0.0).astype(BF16)


def _segsum64(x, p128):
    outs = []
    for j in range(x.shape[-1] // LANES):
        xs = x[:, j * LANES:(j + 1) * LANES]
        outs.append(_dotp(_split(xs, 3), [p128]))
    return jnp.concatenate(outs, axis=-1)


ADA_TN = 1024


def _ada_kernel(c_ref, w_ref, b_ref, o_ref):
    c = c_ref[...]
    sc = c * jax.nn.sigmoid(c)
    rows = sc.shape[0]
    s0, s1, s2 = _split(sc, 3)
    w_hi, w_lo = _split(w_ref[...], 2)
    a = jnp.dot(jnp.concatenate([s0, s1, s2], axis=0), w_hi, preferred_element_type=F32)
    b = jnp.dot(jnp.concatenate([s0, s1], axis=0), w_lo, preferred_element_type=F32)
    o_ref[...] = (a[:rows] + a[rows:2 * rows] + a[2 * rows:] + b[:rows] + b[rows:]) + b_ref[...]


def _ada_call(c_all, w_ada, b_ada):
    rows, d = c_all.shape
    n = w_ada.shape[1]
    return pl.pallas_call(
        _ada_kernel,
        grid=(n // ADA_TN,),
        in_specs=[pl.BlockSpec((rows, d), lambda j: (0, 0)),
                  pl.BlockSpec((d, ADA_TN), lambda j: (0, j)),
                  pl.BlockSpec((1, ADA_TN), lambda j: (0, j))],
        out_specs=pl.BlockSpec((rows, ADA_TN), lambda j: (0, j)),
        out_shape=jax.ShapeDtypeStruct((rows, n), F32),
        compiler_params=_cparams(("parallel",)),
        name="ada",
    )(c_all, w_ada, b_ada.reshape(1, n))


FFN_TF = 512


def _ffn_kernel(x_ref, ada_ref, lng_ref, lnb_ref, wig_ref, wiu_ref, wo_ref, *rest, sub, emit_next):
    if emit_next:
        o_ref, h_next_ref, h_scr, acc = rest
    else:
        o_ref, h_scr, acc = rest
    bb, t, d = x_ref.shape
    j = pl.program_id(1)

    @pl.when(j == 0)
    def _():
        h = _ln(x_ref[...]) * (1.0 + ada_ref[:, 3 * sub + 1:3 * sub + 2, :]) + ada_ref[:, 3 * sub:3 * sub + 1, :]
        h_scr[...] = h.reshape(bb * t, d).astype(BF16)
        acc[...] = jnp.zeros_like(acc)

    h = h_scr[...]
    g = jnp.dot(h, wig_ref[...], preferred_element_type=F32)
    u = jnp.dot(h, wiu_ref[...], preferred_element_type=F32)
    act = (g * jax.nn.sigmoid(g)) * u
    acc[...] += jnp.dot(act.astype(BF16), wo_ref[...], preferred_element_type=F32)

    @pl.when(j == pl.num_programs(1) - 1)
    def _():
        gate = ada_ref[:, 3 * sub + 2:3 * sub + 3, :]
        y = DEEPNORM_ALPHA * x_ref[...] + 0.5 * gate * acc[...].reshape(bb, t, d)
        y = _ln(y) * lng_ref[sub:sub + 1, :] + lnb_ref[sub:sub + 1, :]
        o_ref[...] = y
        if emit_next:
            nxt = sub + 1
            hn = _ln(y) * (1.0 + ada_ref[:, 3 * nxt + 1:3 * nxt + 2, :]) + ada_ref[:, 3 * nxt:3 * nxt + 1, :]
            h_next_ref[...] = hn.astype(BF16)


def _ffn_call(x, ada, ln_g, ln_b, wi, wo, *, layer, which, sub, emit_next, blk):
    bsz, t, d = x.shape
    bb, tt = blk
    nj = D_FF // FFN_TF
    nt = t // tt
    grid = ((bsz // bb) * nt, nj)
    xmap = lambda i, j: (i // nt, i % nt, 0)
    amap = lambda i, j: (i // nt, 0, 0)
    out_shape = [jax.ShapeDtypeStruct(x.shape, F32)]
    out_specs = [pl.BlockSpec((bb, tt, d), xmap)]
    if emit_next:
        out_shape.append(jax.ShapeDtypeStruct(x.shape, BF16))
        out_specs.append(pl.BlockSpec((bb, tt, d), xmap))
    res = pl.pallas_call(
        functools.partial(_ffn_kernel, sub=sub, emit_next=emit_next),
        grid=grid,
        in_specs=[pl.BlockSpec((bb, tt, d), xmap),
                  pl.BlockSpec((bb, N_ADA, d), amap),
                  pl.BlockSpec((3, d), lambda i, j: (0, 0)),
                  pl.BlockSpec((3, d), lambda i, j: (0, 0)),
                  pl.BlockSpec((None, None, d, FFN_TF), lambda i, j: (layer, which, 0, j)),
                  pl.BlockSpec((None, None, d, FFN_TF), lambda i, j: (layer, which, 0, j + nj)),
                  pl.BlockSpec((None, None, FFN_TF, d), lambda i, j: (layer, which, j, 0))],
        out_specs=out_specs,
        out_shape=out_shape,
        scratch_shapes=[pltpu.VMEM((bb * tt, d), BF16), pltpu.VMEM((bb * tt, d), F32)],
        compiler_params=_cparams(("parallel", "arbitrary")),
        name=f"ffn{sub}",
    )(x, ada, ln_g, ln_b, wi, wi, wo)
    return res if emit_next else res[0]


def _mm_kernel(h_ref, w_ref, o_ref):
    o_ref[...] = jnp.dot(h_ref[...], w_ref[...], preferred_element_type=F32)


def _mm_call(h, w, *, tm, tn, col0, n, name="proj"):
    m, k = h.shape
    assert col0 % tn == 0 and n % tn == 0
    cb = col0 // tn
    return pl.pallas_call(
        _mm_kernel,
        grid=(m // tm, n // tn),
        in_specs=[pl.BlockSpec((tm, k), lambda i, j: (i, 0)),
                  pl.BlockSpec((k, tn), lambda i, j: (0, j + cb))],
        out_specs=pl.BlockSpec((tm, tn), lambda i, j: (i, j)),
        out_shape=jax.ShapeDtypeStruct((m, n), F32),
        compiler_params=_cparams(("parallel", "arbitrary")),
        name=name,
    )(h, w)


def _bias_tiles_kernel(tbl_ref, o_ref):
    h = pl.program_id(0)
    r = lax.broadcasted_iota(jnp.int32, (MOBA_BLOCK, MOBA_BLOCK), 0)
    c = lax.broadcasted_iota(jnp.int32, (MOBA_BLOCK, MOBA_BLOCK), 1)
    rel0 = r - c
    o_ref[0, 0] = jnp.where(rel0 >= 0, _t5_bias(rel0, tbl_ref, h), NEG_BIG)
    o_ref[0, 1] = _t5_bias(rel0 + MOBA_BLOCK, tbl_ref, h)


def _bias_tiles_call(rel_bias):
    return pl.pallas_call(
        _bias_tiles_kernel,
        grid=(H_ATT,),
        in_specs=[pl.BlockSpec(memory_space=pltpu.SMEM)],
        out_specs=pl.BlockSpec((1, 2, MOBA_BLOCK, MOBA_BLOCK), lambda h: (h, 0, 0, 0)),
        out_shape=jax.ShapeDtypeStruct((H_ATT, 2, MOBA_BLOCK, MOBA_BLOCK), F32),
        compiler_params=_cparams(("arbitrary",)),
        name="bias_tiles",
    )(rel_bias)


def _rank_select(s, n_valid):
    col = lax.broadcasted_iota(jnp.int32, s.shape, 1)
    cnt = jnp.zeros(s.shape, jnp.int32)
    for m in range(n_valid):
        sm = s[:, m:m + 1]
        beats = (sm > s) | ((sm == s) & (m < col))
        cnt = cnt + jnp.where(beats, 1, 0)
    return (col < n_valid) & (cnt < MOBA_TOPK)


def _attn_prompt_kernel(far_ref, q_ref, k_ref, v_ref, bt_ref, o_ref):
    h = pl.program_id(1)
    s_len = q_ref.shape[1]
    nb = s_len // MOBA_BLOCK
    q = q_ref[0]
    k = k_ref[0]
    kb = k.astype(BF16)
    vb = v_ref[0].astype(BF16)
    qb = (q * (HD_ATT ** -0.5)).astype(BF16)
    means = jnp.mean(k.reshape(nb, MOBA_BLOCK, HD_ATT), axis=1)
    means = jnp.concatenate([means, jnp.zeros((LANES - nb, HD_ATT), F32)], axis=0)
    mparts = _split(means, 3)
    far = far_ref[h]
    rows = [slice(i * MOBA_BLOCK, (i + 1) * MOBA_BLOCK) for i in range(nb)]
    logits = [lax.dot_general(qb[rows[i]], kb[:(i + 1) * MOBA_BLOCK], _NT, preferred_element_type=F32)
              for i in range(nb)]
    scores = [None] + [_dotp(_split(q[rows[i]], 3), mparts, _NT, order=3) for i in range(1, nb)]
    negs = [None] + [jnp.where(_rank_select(scores[i], i), 0.0, NEG_BIG) for i in range(1, nb)]
    probs, dens = [], []
    for i in range(nb):
        tiles = []
        for j in range(i + 1):
            tile = logits[i][:, j * MOBA_BLOCK:(j + 1) * MOBA_BLOCK]
            if j == i:
                tile = tile + bt_ref[0, 0]
            elif j == i - 1:
                tile = tile + bt_ref[0, 1] + negs[i][:, j:j + 1]
            else:
                tile = tile + (far + negs[i][:, j:j + 1])
            tiles.append(tile)
        lg = jnp.concatenate(tiles, axis=-1) if len(tiles) > 1 else tiles[0]
        mx = jnp.max(lg, axis=-1, keepdims=True)
        p = jnp.exp(lg - mx)
        dens.append(jnp.sum(p, axis=-1, keepdims=True))
        probs.append(p.astype(BF16))
    for i in range(nb):
        out = jnp.dot(probs[i], vb[:(i + 1) * MOBA_BLOCK], preferred_element_type=F32) / dens[i]
        o_ref[0, rows[i], :] = out.astype(o_ref.dtype)


def _attn_prompt_call(q, k, v, bias_tiles, far):
    bsz, s_len, _ = q.shape
    qspec = pl.BlockSpec((1, s_len, HD_ATT), lambda b, h: (b, 0, h))
    return pl.pallas_call(
        _attn_prompt_kernel,
        grid=(bsz, H_ATT),
        in_specs=[pl.BlockSpec(memory_space=pltpu.SMEM), qspec, qspec, qspec,
                  pl.BlockSpec((1, 2, MOBA_BLOCK, MOBA_BLOCK), lambda b, h: (h, 0, 0, 0))],
        out_specs=pl.BlockSpec((1, s_len, HD_ATT), lambda b, h: (b, 0, h)),
        out_shape=jax.ShapeDtypeStruct((bsz, s_len, C_ATT), BF16),
        compiler_params=_cparams(("parallel", "arbitrary")),
        name="attn_prompt",
    )(far, q, k, v, bias_tiles)


MEAN_BLOCKS = 8
PAGES_PER_BLOCK = MOBA_BLOCK // PAGE_SIZE


def _cache_means_kernel(pt_ref, *refs):
    page_refs, o_ref = refs[:-1], refs[-1]
    for m in range(MEAN_BLOCKS):
        tot = None
        for u in range(PAGES_PER_BLOCK):
            s = jnp.sum(page_refs[m * PAGES_PER_BLOCK + u][0], axis=0)
            tot = s if tot is None else tot + s
        o_ref[0, m] = tot * (1.0 / MOBA_BLOCK)


def _cache_means_call(page_table, cache_k4, n_blocks):
    dbs = page_table.shape[0]
    npg = MEAN_BLOCKS * PAGES_PER_BLOCK
    in_specs = [pl.BlockSpec((1, PAGE_SIZE, H_ATT, HD_ATT), functools.partial(
        lambda b, g, pt, u: (pt[b, g * npg + u], 0, 0, 0), u=u)) for u in range(npg)]
    return pl.pallas_call(
        _cache_means_kernel,
        grid_spec=pltpu.PrefetchScalarGridSpec(
            num_scalar_prefetch=1,
            grid=(dbs, n_blocks // MEAN_BLOCKS),
            in_specs=in_specs,
            out_specs=pl.BlockSpec((1, MEAN_BLOCKS, H_ATT, HD_ATT), lambda b, g, pt: (b, g, 0, 0)),
        ),
        out_shape=jax.ShapeDtypeStruct((dbs, n_blocks, H_ATT, HD_ATT), F32),
        compiler_params=_cparams(("parallel", "arbitrary")),
        name="cache_means",
    )(page_table, *([cache_k4] * npg))


def _topk_ids_kernel(q_ref, m_ref, o_ref):
    q = q_ref[0]
    means = m_ref[0]
    n_blocks = means.shape[0]
    ds = q.shape[0]
    pad = jnp.zeros((LANES - n_blocks, HD_ATT), F32)
    col = lax.broadcasted_iota(jnp.int32, (ds, LANES), 1)
    for h in range(H_ATT):
        sl = slice(h * HD_ATT, (h + 1) * HD_ATT)
        mh = jnp.concatenate([means[:, sl], pad], axis=0) if n_blocks < LANES else means[:, sl]
        s = _dotp(_split(q[:, sl], 3), _split(mh, 3), _NT, order=3)
        s = jnp.where(col < n_blocks, s, NEG_BIG)
        ids = jnp.zeros((ds, LANES), jnp.int32)
        for t in range(MOBA_TOPK):
            mx = jnp.max(s, axis=-1, keepdims=True)
            idx = jnp.min(jnp.where(s == mx, col, LANES), axis=-1, keepdims=True)
            ids = jnp.where(col == t, idx, ids)
            s = jnp.where(col == idx, NEG_BIG * 2, s)
        o_ref[0, h] = ids


def _topk_ids_call(q, means):
    dbs, ds, _ = q.shape
    n_blocks = means.shape[1]
    return pl.pallas_call(
        _topk_ids_kernel,
        grid=(dbs,),
        in_specs=[pl.BlockSpec((1, ds, C_ATT), lambda b: (b, 0, 0)),
                  pl.BlockSpec((1, n_blocks, C_ATT), lambda b: (b, 0, 0))],
        out_specs=pl.BlockSpec((1, H_ATT, ds, LANES), lambda b: (b, 0, 0, 0)),
        out_shape=jax.ShapeDtypeStruct((dbs, H_ATT, ds, LANES), jnp.int32),
        compiler_params=_cparams(("parallel",)),
        name="topk_ids",
    )(q, means)


N_SEL_PAGES = MOBA_TOPK * PAGES_PER_BLOCK


def _attn_sample_kernel(pt_ref, ids_ref, tbl_ref, q_ref, kn_ref, vn_ref, ck_hbm, cv_hbm, o_ref,
                        kbuf, vbuf, sems, *, past_len):
    ds = q_ref.shape[1]
    n_pg = ds * N_SEL_PAGES
    b, h = pl.program_id(0), pl.program_id(1)
    nh = pl.num_programs(1)
    step = b * nh + h
    n_steps = pl.num_programs(0) * nh
    slot = step % 2

    def page_copies(bb, hh, sl):
        base = (bb * H_ATT + hh) * ds * MOBA_TOPK
        cps = []
        for u in range(n_pg):
            pg = pt_ref[bb, ids_ref[base + u // PAGES_PER_BLOCK] * PAGES_PER_BLOCK + u % PAGES_PER_BLOCK]
            cps.append(pltpu.make_async_copy(ck_hbm.at[pg, :, hh, :], kbuf.at[sl, u], sems.at[sl, u]))
            cps.append(pltpu.make_async_copy(cv_hbm.at[pg, :, hh, :], vbuf.at[sl, u], sems.at[sl, n_pg + u]))
        return cps

    @pl.when(step == 0)
    def _():
        for cp in page_copies(b, h, slot):
            cp.start()

    @pl.when(step + 1 < n_steps)
    def _():
        nxt = step + 1
        for cp in page_copies(nxt // nh, nxt % nh, 1 - slot):
            cp.start()

    for cp in page_copies(b, h, slot):
        cp.wait()
    kp_refs = [kbuf.at[slot, u] for u in range(n_pg)]
    vp_refs = [vbuf.at[slot, u] for u in range(n_pg)]
    qb = (q_ref[0] * (HD_ATT ** -0.5)).astype(BF16)
    qrow = lax.broadcasted_iota(jnp.int32, (ds, 1), 0)
    lane = lax.broadcasted_iota(jnp.int32, (ds, PAGE_SIZE), 1)
    lo = lax.dot_general(qb, kn_ref[0].astype(BF16), _NT, preferred_element_type=F32)
    rel_o = lax.broadcasted_iota(jnp.int32, (ds, ds), 0) - lax.broadcasted_iota(jnp.int32, (ds, ds), 1)
    lo = jnp.where(rel_o >= 0, lo + _t5_bias(rel_o, tbl_ref, h), NEG_BIG)
    mx_o = jnp.max(lo, axis=-1, keepdims=True)
    vn = vn_ref[0].astype(BF16)
    tiles = []
    for qi in range(ds):
        flat = (b * H_ATT + h) * ds + qi
        tq = []
        for s in range(MOBA_TOPK):
            blk = ids_ref[flat * MOBA_TOPK + s]
            for u in range(PAGES_PER_BLOCK):
                kpg = kp_refs[(qi * MOBA_TOPK + s) * PAGES_PER_BLOCK + u][...].astype(BF16)
                lg = lax.dot_general(qb, kpg, _NT, preferred_element_type=F32)
                rel = (past_len + qi) - (blk * MOBA_BLOCK + u * PAGE_SIZE + lane)
                tq.append(lg + _t5_bias(rel, tbl_ref, h))
        tiles.append(tq)
    mxs = []
    for tq in tiles:
        mx = mx_o
        for t in tq:
            mx = jnp.maximum(mx, jnp.max(t, axis=-1, keepdims=True))
        mxs.append(mx)
    pos = [jnp.exp(lo - mx) for mx in mxs]
    ps = [[jnp.exp(t - mx) for t in tq] for tq, mx in zip(tiles, mxs)]
    accs = [jnp.dot(po.astype(BF16), vn, preferred_element_type=F32) for po in pos]
    for qi in range(ds):
        for i in range(N_SEL_PAGES):
            accs[qi] = accs[qi] + jnp.dot(ps[qi][i].astype(BF16), vp_refs[qi * N_SEL_PAGES + i][...].astype(BF16),
                                          preferred_element_type=F32)
    result = jnp.zeros((ds, HD_ATT), F32)
    for qi in range(ds):
        den = jnp.sum(pos[qi], axis=-1, keepdims=True)
        for p in ps[qi]:
            den = den + jnp.sum(p, axis=-1, keepdims=True)
        result = jnp.where(qrow == qi, accs[qi] / den, result)
    o_ref[0] = result


def _attn_sample_call(page_table, ids, rel_bias, q, k_new, v_new, cache_k2, cache_v2, past_len):
    dbs, ds, _ = q.shape
    n_pg = ds * N_SEL_PAGES

    nspec = pl.BlockSpec((1, ds, HD_ATT), lambda b, h, p, i: (b, 0, h))
    pool = pl.BlockSpec(memory_space=pl.ANY)
    return pl.pallas_call(
        functools.partial(_attn_sample_kernel, past_len=past_len),
        grid_spec=pltpu.PrefetchScalarGridSpec(
            num_scalar_prefetch=2,
            grid=(dbs, H_ATT),
            in_specs=[pl.BlockSpec(memory_space=pltpu.SMEM), nspec, nspec, nspec, pool, pool],
            out_specs=pl.BlockSpec((1, ds, HD_ATT), lambda b, h, p, i: (b, 0, h)),
            scratch_shapes=[pltpu.VMEM((2, n_pg, PAGE_SIZE, HD_ATT), F32),
                            pltpu.VMEM((2, n_pg, PAGE_SIZE, HD_ATT), F32),
                            pltpu.SemaphoreType.DMA((2, 2 * n_pg))],
        ),
        out_shape=jax.ShapeDtypeStruct((dbs, ds, C_ATT), F32),
        compiler_params=_cparams(("arbitrary", "arbitrary")),
        name="attn_sample",
    )(page_table, ids, rel_bias, q, k_new, v_new, cache_k2, cache_v2)


def _rwkv_pre_kernel(z_ref, zp_ref, sh0_ref, mu_ref, w0_ref, w2_ref, a0_ref, a2_ref, g2_ref, kk_ref, ka_ref,
                     r_o, k_o, v_o, kk_o, b_o, ld_o, g_o):
    i = pl.program_id(1)
    z = z_ref[0]
    tm = z.shape[0]
    first = jnp.where(i == 0, sh0_ref[0], zp_ref[0, 7:8, :])
    row = lax.broadcasted_iota(jnp.int32, (tm, 1), 0)
    zprev = jnp.where(row == 0, first, pltpu.roll(z, 1, 0))
    zs = z + (zprev - z) * mu_ref[...]
    c = C_RWKV
    r, k, v = zs[:, :c], zs[:, c:2 * c], zs[:, 2 * c:3 * c]
    zw = zs[:, 3 * c:3 * c + LORA_PAD]
    za = zs[:, 3 * c + LORA_PAD:3 * c + 2 * LORA_PAD]
    zg = zs[:, 3 * c + 2 * LORA_PAD:]
    wl = w0_ref[...] + jnp.dot(jnp.tanh(zw).astype(BF16), w2_ref[...], preferred_element_type=F32)
    w = -(jnp.maximum(-wl, 0.0) + jnp.log(1.0 + jnp.exp(-jnp.abs(wl)))) - 0.5
    ld_o[0] = -jnp.exp(w)
    a = jax.nn.sigmoid(a0_ref[...] + jnp.dot(za.astype(BF16), a2_ref[...], preferred_element_type=F32))
    g_o[0] = jnp.dot(jax.nn.sigmoid(zg).astype(BF16), g2_ref[...], preferred_element_type=F32)
    kkr = k * kk_ref[...]
    n2 = _segsum64(kkr * kkr, _seg_ones())
    kkn = kkr / jnp.maximum(jnp.sqrt(n2), 1e-12)
    r_o[0] = r
    k_o[0] = k * (1.0 + (a - 1.0) * ka_ref[...])
    v_o[0] = v
    kk_o[0] = kkn
    b_o[0] = kkn * a


def _rwkv_pre_call(z, shift0, p, *, tm):
    bsz, t, zp = z.shape
    nt = t // tm
    c = C_RWKV
    row = lambda b, i: (0, 0)
    vec = lambda n: pl.BlockSpec((1, n), row)
    tile = pl.BlockSpec((1, tm, c), lambda b, i: (b, i, 0))
    return pl.pallas_call(
        _rwkv_pre_kernel,
        grid=(bsz, nt),
        in_specs=[pl.BlockSpec((1, tm, zp), lambda b, i: (b, i, 0)),
                  pl.BlockSpec((1, 8, zp), lambda b, i: (b, jnp.maximum(i * (tm // 8) - 1, 0), 0)),
                  pl.BlockSpec((1, 1, zp), lambda b, i: (b, 0, 0)),
                  vec(zp), vec(c),
                  pl.BlockSpec((LORA_PAD, c), row), vec(c),
                  pl.BlockSpec((LORA_PAD, c), row),
                  pl.BlockSpec((D_GATE_LORA, c), row), vec(c), vec(c)],
        out_specs=[tile] * 7,
        out_shape=[jax.ShapeDtypeStruct((bsz, t, c), F32)] * 7,
        compiler_params=_cparams(("parallel", "arbitrary")),
        name="rwkv_pre",
    )(z, z, shift0.reshape(bsz, 1, zp), p["mu"], p["w0"], p["w2"], p["a0"], p["a2"], p["g2"], p["k_k"], p["k_a"])


CHUNK_PREC = {"gram": (1, 1), "state_read": (1, 1), "mkv": (1, 1), "solve1": (2, 2), "solve_sq": (2, 2),
              "solve_ap": (2, 2), "out": (1, 1), "state_upd": (1, 1)}


def _chunk_pairs(rs, ks, vs, kks, bs, lds, ss):
    c = CHUNK
    lane = lax.broadcasted_iota(jnp.int32, (1, LANES), 1)
    m_a = jnp.where(lane < HS_RWKV, 1.0, 0.0)
    m_b = 1.0 - m_a
    row = lax.broadcasted_iota(jnp.int32, (c, 2 * c), 0)
    coli = lax.broadcasted_iota(jnp.int32, (c, 2 * c), 1) % c
    strict = coli < row
    incl = coli <= row
    lr = lax.broadcasted_iota(jnp.int32, (c, c), 0)
    lc = lax.broadcasted_iota(jnp.int32, (c, c), 1)
    ltri = jnp.where(lc <= lr, 1.0, 0.0).astype(BF16)
    ones = jnp.ones((c, LANES), BF16)
    rr = lax.broadcasted_iota(jnp.int32, (LANES, LANES), 0) // HS_RWKV
    cc = lax.broadcasted_iota(jnp.int32, (LANES, LANES), 1) // HS_RWKV
    same_head = rr == cc

    def each(f, *lists):
        return [f(*args) for args in zip(*lists)]

    def stack2(x):
        return jnp.concatenate([x * m_a, x * m_b], axis=0)

    def prod(site, a, b, dims=_NN):
        na, nb = CHUNK_PREC[site]
        return _dotp(_split(a, na), _split(b, nb), dims, order=max(na, nb))

    def pm(site, mcats, xs):
        return each(lambda m, x: prod(site, m, stack2(x)), mcats, xs)

    ldp = each(lambda x: _split(x, 3), lds)
    cums = each(lambda p: _dotp([ltri], p, _NN), ldp)
    gcols = each(lambda p: jnp.exp(_dotp(p, [ones], _TN)), ldp)
    g_inv = each(lambda cu: jnp.exp(-cu), cums)
    g_end = each(lambda cu: jnp.exp(cu[c - 1:c, :] - cu), cums)
    p_all = each(lambda kk, r, cu, ld: jnp.concatenate([kk * jnp.exp(cu - ld), r * jnp.exp(cu)], axis=0),
                 kks, rs, cums, lds)
    z2 = each(lambda k, b, gi: jnp.concatenate([stack2(k * gi), stack2(b * gi)], axis=0), ks, bs, g_inv)
    g4 = each(lambda p, z: prod("gram", p, z, _NT), p_all, z2)
    mk = each(lambda g: jnp.where(strict, g[:c, :2 * c], 0.0), g4)
    pj = each(lambda g: jnp.where(strict, -g[:c, 2 * c:], 0.0), g4)
    akb = each(lambda g: jnp.concatenate([jnp.where(incl, g[c:, :2 * c], 0.0),
                                          jnp.where(incl, -g[c:, 2 * c:], 0.0)], axis=1), g4)
    ps = each(lambda p, s: prod("state_read", p, s), p_all, ss)
    mkv = pm("mkv", mk, vs)
    rhs = each(lambda p, m: p[:c] + m, ps, mkv)
    us = each(lambda x, d: x + d, rhs, pm("solve1", pj, rhs))
    n = 2
    while n < c:
        pj = pm("solve_sq", pj, pj)
        us = each(lambda x, d: x + d, us, pm("solve_ap", pj, us))
        n *= 2
    ys = each(lambda p, m, v, u: p[c:] + prod("out", m, jnp.concatenate([stack2(v), stack2(u)], axis=0)),
              ps, akb, vs, us)
    upd = each(lambda k, b, ge, v, u: prod("state_upd", jnp.concatenate([k * ge, -(b * ge)], axis=0),
                                           jnp.concatenate([v, u], axis=0), _TN), ks, bs, g_end, vs, us)
    s_new = each(lambda gc, s, up: gc * s + jnp.where(same_head, up, 0.0), gcols, ss, upd)
    return ys, s_new


def _rwkv_chunk_kernel(r_ref, k_ref, v_ref, kk_ref, b_ref, ld_ref, s0_ref, y_ref, so_ref, s_scr):
    ci = pl.program_id(2)

    @pl.when(ci == 0)
    def _():
        s_scr[...] = s0_ref[0]

    sls = [slice(p * LANES, (p + 1) * LANES) for p in range(PAIRS_PER_STEP)]
    pairs = lambda ref: [ref[0, :, sl] for sl in sls]
    ys, s_new = _chunk_pairs(pairs(r_ref), pairs(k_ref), pairs(v_ref), pairs(kk_ref), pairs(b_ref), pairs(ld_ref),
                             [s_scr[p] for p in range(PAIRS_PER_STEP)])
    for p, sl in enumerate(sls):
        y_ref[0, :, sl] = ys[p]
        s_scr[p] = s_new[p]
        so_ref[0, p] = s_new[p]


def _rwkv_chunk_call(r, k, v, kk, b, logd, s0_blk):
    bsz, t, c = r.shape
    pp = PAIRS_PER_STEP
    w = pp * LANES
    tile = pl.BlockSpec((1, CHUNK, w), lambda bi, p, ci: (bi, ci, p))
    sspec = pl.BlockSpec((1, pp, LANES, LANES), lambda bi, p, ci: (bi, p, 0, 0))
    return pl.pallas_call(
        _rwkv_chunk_kernel,
        grid=(bsz, c // w, t // CHUNK),
        in_specs=[tile] * 6 + [sspec],
        out_specs=[tile, sspec],
        out_shape=[jax.ShapeDtypeStruct((bsz, t, c), F32),
                   jax.ShapeDtypeStruct(s0_blk.shape, F32)],
        scratch_shapes=[pltpu.VMEM((pp, LANES, LANES), F32)],
        compiler_params=_cparams(("parallel", "parallel", "arbitrary")),
        name="rwkv_chunk",
    )(r, k, v, kk, b, logd, s0_blk)


def _rwkv_post_kernel(y_ref, r_ref, k_ref, v_ref, g_ref, rk_ref, lg_ref, lb_ref, o_ref):
    p128 = _seg_ones()
    y = y_ref[...]
    inv = 1.0 / HS_RWKV
    mu = _segsum64(y, p128) * inv
    yc = y - mu
    var = _segsum64(yc * yc, p128) * inv
    yn = yc * lax.rsqrt(var + GN_EPS) * lg_ref[...] + lb_ref[...]
    bonus = _segsum64(r_ref[...] * k_ref[...] * rk_ref[...], p128) * v_ref[...]
    o_ref[...] = ((yn + bonus) * g_ref[...]).astype(o_ref.dtype)


def _rwkv_post_call(y, r, k, v, g, p, *, tm):
    m, c = y.shape
    tile = pl.BlockSpec((tm, c), lambda i: (i, 0))
    vec = pl.BlockSpec((1, c), lambda i: (0, 0))
    return pl.pallas_call(
        _rwkv_post_kernel,
        grid=(m // tm,),
        in_specs=[tile] * 5 + [vec] * 3,
        out_specs=tile,
        out_shape=jax.ShapeDtypeStruct((m, c), BF16),
        compiler_params=_cparams(("parallel",)),
        name="rwkv_post",
    )(y, r, k, v, g, p["r_k"], p["lnx_g"], p["lnx_b"])


def _combine_kernel(x_ref, ada_ref, lng_ref, lnb_ref, oa_ref, or_ref, ga_ref, gr_ref, wa_ref, wr_ref, wo_ref, o_ref):
    bb, t, d = x_ref.shape
    ua = jnp.dot(oa_ref[...], wa_ref[...], preferred_element_type=F32)
    ur = jnp.dot(or_ref[...], wr_ref[...], preferred_element_type=F32)
    m = jax.nn.sigmoid(ga_ref[...]) * ua + jax.nn.sigmoid(gr_ref[...]) * ur
    mo = jnp.dot(m.astype(BF16), wo_ref[...], preferred_element_type=F32)
    y = DEEPNORM_ALPHA * x_ref[...] + ada_ref[:, 5:6, :] * mo.reshape(bb, t, d)
    o_ref[...] = _ln(y) * lng_ref[1:2, :] + lnb_ref[1:2, :]


def _combine_call(x, ada, ln_g, ln_b, oa, orw, ga, gr, wa, wr, wo, *, blk):
    bsz, t, d = x.shape
    bb, tt = blk
    nt = t // tt
    rows = bb * tt
    xmap = lambda i: (i // nt, i % nt, 0)
    const = lambda i: (0, 0)
    rowt = lambda n: pl.BlockSpec((rows, n), lambda i: (i, 0))
    return pl.pallas_call(
        _combine_kernel,
        grid=((bsz // bb) * nt,),
        in_specs=[pl.BlockSpec((bb, tt, d), xmap),
                  pl.BlockSpec((bb, N_ADA, d), lambda i: (i // nt, 0, 0)),
                  pl.BlockSpec((3, d), const), pl.BlockSpec((3, d), const),
                  rowt(C_ATT), rowt(C_RWKV), rowt(d), rowt(d),
                  pl.BlockSpec((C_ATT, d), const), pl.BlockSpec((C_RWKV, d), const), pl.BlockSpec((d, d), const)],
        out_specs=pl.BlockSpec((bb, tt, d), xmap),
        out_shape=jax.ShapeDtypeStruct(x.shape, F32),
        compiler_params=_cparams(("parallel",)),
        name="combine",
    )(x, ada, ln_g, ln_b, oa, orw, ga, gr, wa, wr, wo)


def _rearrange_z(a):
    c3 = 3 * C_RWKV
    pad = [(0, 0)] * (a.ndim - 1) + [(0, LORA_PAD - D_DECAY_LORA)]
    return jnp.concatenate([a[..., :c3],
                            jnp.pad(a[..., c3:c3 + D_DECAY_LORA], pad),
                            jnp.pad(a[..., c3 + D_DECAY_LORA:c3 + D_DECAY_LORA + D_AAA_LORA], pad),
                            a[..., c3 + D_DECAY_LORA + D_AAA_LORA:]], axis=-1)


def _unarrange_z(a):
    c3 = 3 * C_RWKV
    return jnp.concatenate([a[..., :c3], a[..., c3:c3 + D_DECAY_LORA],
                            a[..., c3 + LORA_PAD:c3 + LORA_PAD + D_AAA_LORA], a[..., c3 + 2 * LORA_PAD:]], axis=-1)


def _state_to_blocks(state):
    bsz = state.shape[0]
    st = jnp.swapaxes(state, -1, -2).reshape(bsz, H_RWKV // 2, 2, HS_RWKV, HS_RWKV)
    blk = jnp.einsum('bphkv,hg->bphkgv', st, jnp.eye(2, dtype=state.dtype))
    return blk.reshape(bsz, H_RWKV // 2, LANES, LANES)


def _blocks_to_state(blk):
    bsz = blk.shape[0]
    b6 = blk.reshape(bsz, H_RWKV // 2, 2, HS_RWKV, 2, HS_RWKV)
    st = jnp.stack([b6[:, :, 0, :, 0, :], b6[:, :, 1, :, 1, :]], axis=2)
    return jnp.swapaxes(st, -1, -2).reshape(bsz, H_RWKV, HS_RWKV, HS_RWKV)


def _trunk(x, ada, attend, shift0, wkv0, w, *, ffn_blk, mm_tm, pre_tm, post_tm, comb_blk):
    bsz, t, d = x.shape
    m = bsz * t
    ffn = functools.partial(_ffn_call, ada=ada, ln_g=w["ln_g"], ln_b=w["ln_b"], wi=w["ffn_wi"], wo=w["ffn_wo"],
                            layer=w["layer"], blk=ffn_blk)
    x1, h2 = ffn(x, which=0, sub=0, emit_next=True)
    h2f = h2.reshape(m, d)
    win = w["win"]
    proj = functools.partial(_mm_call, h2f, win, tm=mm_tm)
    q = proj(tn=C_ATT, col0=0, n=C_ATT, name="proj_q").reshape(bsz, t, C_ATT)
    k = proj(tn=C_ATT, col0=C_ATT, n=C_ATT, name="proj_k").reshape(bsz, t, C_ATT)
    v = proj(tn=C_ATT, col0=2 * C_ATT, n=C_ATT, name="proj_v").reshape(bsz, t, C_ATT)
    ga = proj(tn=d // 2, col0=3 * C_ATT, n=d, name="proj_ga")
    gr = proj(tn=d // 2, col0=3 * C_ATT + d, n=d, name="proj_gr")
    z = proj(tn=Z_PAD // 2, col0=3 * C_ATT + 2 * d, n=Z_PAD, name="proj_z").reshape(bsz, t, Z_PAD)
    o_att = attend(q, k, v)
    r, k2, vv, kk, b, logd, g = _rwkv_pre_call(z, _rearrange_z(shift0), w, tm=pre_tm)
    tp = -(-t // CHUNK) * CHUNK
    seq = [r, k2, vv, kk, b, logd]
    if tp != t:
        seq = [jnp.pad(a, ((0, 0), (0, tp - t), (0, 0))) for a in seq]
    y, s_blk = _rwkv_chunk_call(*seq, _state_to_blocks(wkv0))
    y = y[:, :t]
    fl = lambda a: a.reshape(m, C_RWKV)
    o_rwkv = _rwkv_post_call(fl(y), fl(r), fl(k2), fl(vv), fl(g), w, tm=post_tm)
    x2 = _combine_call(x1, ada, w["ln_g"], w["ln_b"], o_att.reshape(m, C_ATT), o_rwkv, ga, gr,
                       w["wua"], w["wur"], w["wout"], blk=comb_blk)
    x3 = ffn(x2, which=1, sub=2, emit_next=False)
    shift_new = _unarrange_z(z[:, -1])
    return (x3, k.reshape(bsz, t, H_ATT, HD_ATT), v.reshape(bsz, t, H_ATT, HD_ATT), _blocks_to_state(s_blk), shift_new)


def kernel(x_prompt, x_sample, cache_k, cache_v, state_wkv, state_shift, page_table, c_prompt, c_sample, rel_bias, w_ada, b_ada, ln_g, ln_b, ffn_wi, ffn_wo, w_in, mu_shift, w0, w2, a0, a2, g2, k_k, k_a, r_k, lnx_g, lnx_b, w_up_attn, w_up_rwkv, w_out):
    assert w_ada.shape[0] == DEPTH == 1
    bsz, s_len, d = x_prompt.shape
    dbs, ds, _ = x_sample.shape
    past_len = page_table.shape[1] * PAGE_SIZE
    n_phys = cache_k.shape[1]
    l = 0
    win = w_in[l]
    c3 = 3 * C_ATT
    lora_rows = ((0, LORA_PAD - D_DECAY_LORA), (0, 0))
    w = {
        "ln_g": ln_g[l], "ln_b": ln_b[l],
        "layer": l, "ffn_wi": ffn_wi.astype(BF16), "ffn_wo": ffn_wo.astype(BF16),
        "win": jnp.concatenate([win[:, :c3], win[:, c3 + RWKV_PROJ:],
                                _rearrange_z(win[:, c3:c3 + RWKV_PROJ])], axis=1).astype(BF16),
        "mu": _rearrange_z(mu_shift[l])[None, :],
        "w0": w0[l][None, :], "w2": jnp.pad(w2[l], lora_rows).astype(BF16),
        "a0": a0[l][None, :], "a2": jnp.pad(a2[l], lora_rows).astype(BF16),
        "g2": g2[l].astype(BF16), "k_k": k_k[l][None, :], "k_a": k_a[l][None, :],
        "r_k": r_k[l].reshape(1, C_RWKV), "lnx_g": lnx_g[l][None, :], "lnx_b": lnx_b[l][None, :],
        "wua": w_up_attn[l].astype(BF16), "wur": w_up_rwkv[l].astype(BF16), "wout": w_out[l].astype(BF16),
    }
    n_c = bsz + dbs
    c_rows = -(-n_c // 8) * 8
    c_all = jnp.concatenate([c_prompt, c_sample, jnp.zeros((c_rows - n_c, d), F32)], axis=0)
    ada = _ada_call(c_all, w_ada[l], b_ada[l]).reshape(c_rows, N_ADA, d)

    bias_tiles = _bias_tiles_call(rel_bias)
    far = rel_bias[NUM_BUCKETS - 1]
    attend_p = lambda q, k, v: _attn_prompt_call(q, k, v, bias_tiles, far)
    yp, kp, vp, wp, sp = _trunk(
        x_prompt, ada[:bsz], attend_p, jnp.zeros((bsz, RWKV_PROJ), F32),
        jnp.zeros((bsz, H_RWKV, HS_RWKV, HS_RWKV), F32), w,
        ffn_blk=(1, 512), mm_tm=1024, pre_tm=256, post_tm=256, comb_blk=(1, 256))

    cache_k2 = cache_k.reshape(DEPTH * n_phys, PAGE_SIZE, H_ATT, HD_ATT)
    cache_v2 = cache_v.reshape(DEPTH * n_phys, PAGE_SIZE, H_ATT, HD_ATT)
    page_table = page_table + l * n_phys
    n_full = past_len // MOBA_BLOCK

    def attend_s(q, k, v):
        means = _cache_means_call(page_table, cache_k2, n_full).reshape(dbs, n_full, C_ATT)
        ids = _topk_ids_call(q, means)[..., :MOBA_TOPK]
        o = _attn_sample_call(page_table, ids.reshape(-1), rel_bias, q, k, v, cache_k2, cache_v2, past_len)
        return o.astype(BF16)

    ys, kn, vn, wn, sn = _trunk(
        x_sample, ada[bsz:n_c], attend_s, state_shift[l], state_wkv[l], w,
        ffn_blk=(dbs, ds), mm_tm=dbs * ds, pre_tm=ds, post_tm=dbs * ds, comb_blk=(dbs, ds))

    return (yp, ys, kp[None], vp[None], kn[None], vn[None], wp[None], wn[None], sp[None], sn[None])
```

```python
import functools
import math

import jax
import jax.numpy as jnp
from jax import lax
from jax.experimental import pallas as pl
from jax.experimental.pallas import tpu as pltpu

F32 = jnp.float32
BF16 = jnp.bfloat16

D_MODEL = 2048
D_FF = 5632
N_ADA = 9
H_ATT = 8
HD_ATT = 128
C_ATT = H_ATT * HD_ATT
MOBA_BLOCK = 256
MOBA_TOPK = 3
NUM_BUCKETS = 32
MAX_DISTANCE = 128
PAGE_SIZE = 128
HS_RWKV = 64
C_RWKV = 1024
H_RWKV = C_RWKV // HS_RWKV
D_DECAY_LORA = 96
D_AAA_LORA = 96
D_GATE_LORA = 256
RWKV_PROJ = 3 * C_RWKV + D_DECAY_LORA + D_AAA_LORA + D_GATE_LORA
LORA_PAD = 128
Z_PAD = 3 * C_RWKV + 2 * LORA_PAD + D_GATE_LORA
GN_EPS = 64e-5
LN_EPS = 1e-5
DEPTH = 1
DEEPNORM_ALPHA = (2 * DEPTH) ** 0.25
NEG_BIG = -1e30

LANES = 128
CHUNK = 64
PAIRS_PER_STEP = 8
VMEM_LIMIT = 56 * 1024 * 1024


def _t5_thresholds():
    max_exact = NUM_BUCKETS // 2
    thr = list(range(1, max_exact + 1))
    for b in range(max_exact + 1, NUM_BUCKETS):
        x = max_exact * (MAX_DISTANCE / max_exact) ** ((b - max_exact) / (NUM_BUCKETS - max_exact))
        thr.append(int(math.ceil(x)))
    return tuple(thr)


T5_THR = _t5_thresholds()


def _cparams(sem, vmem=VMEM_LIMIT):
    return pltpu.CompilerParams(dimension_semantics=sem, vmem_limit_bytes=vmem)


def _ln(x):
    mu = jnp.mean(x, axis=-1, keepdims=True)
    xc = x - mu
    var = jnp.mean(xc * xc, axis=-1, keepdims=True)
    return xc * lax.rsqrt(var + LN_EPS)


def _split(x, n):
    parts = []
    for i in range(n):
        p = x.astype(BF16)
        parts.append(p)
        if i + 1 < n:
            x = x - p.astype(F32)
    return parts


_NN = (((1,), (0,)), ((), ()))
_NT = (((1,), (1,)), ((), ()))
_TN = (((0,), (0,)), ((), ()))


def _dotp(ap, bp, dims=_NN, order=None):
    if order is None:
        order = max(len(ap), len(bp))
    acc = None
    if dims == _NN and len(ap) > 1:
        m = ap[0].shape[0]
        for j, b in enumerate(bp):
            sel = [a for i, a in enumerate(ap) if i + j < order]
            if not sel:
                continue
            lhs = sel[0] if len(sel) == 1 else jnp.concatenate(sel, axis=0)
            t = lax.dot_general(lhs, b, dims, preferred_element_type=F32)
            for r in range(len(sel)):
                part = t[r * m:(r + 1) * m]
                acc = part if acc is None else acc + part
        return acc
    for i, a in enumerate(ap):
        for j, b in enumerate(bp):
            if i + j < order:
                t = lax.dot_general(a, b, dims, preferred_element_type=F32)
                acc = t if acc is None else acc + t
    return acc


def _t5_bias(rel, tbl_ref, h):
    bias = jnp.full(rel.shape, tbl_ref[0, h], F32)
    for b in range(1, NUM_BUCKETS):
        bias = jnp.where(rel >= T5_THR[b - 1], tbl_ref[b, h], bias)
    return bias


def _seg_ones():
    r = lax.broadcasted_iota(jnp.int32, (LANES, LANES), 0) // HS_RWKV
    c = lax.broadcasted_iota(jnp.int32, (LANES, LANES), 1) // HS_RWKV
    return jnp.where(r == c, 1.0, 0.0).astype(BF16)


def _segsum64(x, p128):
    outs = []
    for j in range(x.shape[-1] // LANES):
        xs = x[:, j * LANES:(j + 1) * LANES]
        outs.append(_dotp(_split(xs, 3), [p128]))
    return jnp.concatenate(outs, axis=-1)


ADA_TN = 1024


def _ada_kernel(c_ref, w_ref, b_ref, o_ref):
    c = c_ref[...]
    sc = c * jax.nn.sigmoid(c)
    rows = sc.shape[0]
    s0, s1, s2 = _split(sc, 3)
    w_hi, w_lo = _split(w_ref[...], 2)
    a = jnp.dot(jnp.concatenate([s0, s1, s2], axis=0), w_hi, preferred_element_type=F32)
    b = jnp.dot(jnp.concatenate([s0, s1], axis=0), w_lo, preferred_element_type=F32)
    o_ref[...] = (a[:rows] + a[rows:2 * rows] + a[2 * rows:] + b[:rows] + b[rows:]) + b_ref[...]


def _ada_call(c_all, w_ada, b_ada):
    rows, d = c_all.shape
    n = w_ada.shape[1]
    return pl.pallas_call(
        _ada_kernel,
        grid=(n // ADA_TN,),
        in_specs=[pl.BlockSpec((rows, d), lambda j: (0, 0)),
                  pl.BlockSpec((d, ADA_TN), lambda j: (0, j)),
                  pl.BlockSpec((1, ADA_TN), lambda j: (0, j))],
        out_specs=pl.BlockSpec((rows, ADA_TN), lambda j: (0, j)),
        out_shape=jax.ShapeDtypeStruct((rows, n), F32),
        compiler_params=_cparams(("parallel",)),
        name="ada",
    )(c_all, w_ada, b_ada.reshape(1, n))


FFN_TF = 512


def _ffn_kernel(x_ref, ada_ref, lng_ref, lnb_ref, wig_ref, wiu_ref, wo_ref, *rest, sub, emit_next):
    if emit_next:
        o_ref, h_next_ref, h_scr, acc = rest
    else:
        o_ref, h_scr, acc = rest
    bb, t, d = x_ref.shape
    j = pl.program_id(1)

    @pl.when(j == 0)
    def _():
        h = _ln(x_ref[...]) * (1.0 + ada_ref[:, 3 * sub + 1:3 * sub + 2, :]) + ada_ref[:, 3 * sub:3 * sub + 1, :]
        h_scr[...] = h.reshape(bb * t, d).astype(BF16)
        acc[...] = jnp.zeros_like(acc)

    h = h_scr[...]
    g = jnp.dot(h, wig_ref[...], preferred_element_type=F32)
    u = jnp.dot(h, wiu_ref[...], preferred_element_type=F32)
    act = (g * jax.nn.sigmoid(g)) * u
    acc[...] += jnp.dot(act.astype(BF16), wo_ref[...], preferred_element_type=F32)

    @pl.when(j == pl.num_programs(1) - 1)
    def _():
        gate = ada_ref[:, 3 * sub + 2:3 * sub + 3, :]
        y = DEEPNORM_ALPHA * x_ref[...] + 0.5 * gate * acc[...].reshape(bb, t, d)
        y = _ln(y) * lng_ref[sub:sub + 1, :] + lnb_ref[sub:sub + 1, :]
        o_ref[...] = y
        if emit_next:
            nxt = sub + 1
            hn = _ln(y) * (1.0 + ada_ref[:, 3 * nxt + 1:3 * nxt + 2, :]) + ada_ref[:, 3 * nxt:3 * nxt + 1, :]
            h_next_ref[...] = hn.astype(BF16)


def _ffn_call(x, ada, ln_g, ln_b, wi, wo, *, layer, which, sub, emit_next, blk):
    bsz, t, d = x.shape
    bb, tt = blk
    nj = D_FF // FFN_TF
    nt = t // tt
    grid = ((bsz // bb) * nt, nj)
    xmap = lambda i, j: (i // nt, i % nt, 0)
    amap = lambda i, j: (i // nt, 0, 0)
    out_shape = [jax.ShapeDtypeStruct(x.shape, F32)]
    out_specs = [pl.BlockSpec((bb, tt, d), xmap)]
    if emit_next:
        out_shape.append(jax.ShapeDtypeStruct(x.shape, BF16))
        out_specs.append(pl.BlockSpec((bb, tt, d), xmap))
    res = pl.pallas_call(
        functools.partial(_ffn_kernel, sub=sub, emit_next=emit_next),
        grid=grid,
        in_specs=[pl.BlockSpec((bb, tt, d), xmap),
                  pl.BlockSpec((bb, N_ADA, d), amap),
                  pl.BlockSpec((3, d), lambda i, j: (0, 0)),
                  pl.BlockSpec((3, d), lambda i, j: (0, 0)),
                  pl.BlockSpec((None, None, d, FFN_TF), lambda i, j: (layer, which, 0, j)),
                  pl.BlockSpec((None, None, d, FFN_TF), lambda i, j: (layer, which, 0, j + nj)),
                  pl.BlockSpec((None, None, FFN_TF, d), lambda i, j: (layer, which, j, 0))],
        out_specs=out_specs,
        out_shape=out_shape,
        scratch_shapes=[pltpu.VMEM((bb * tt, d), BF16), pltpu.VMEM((bb * tt, d), F32)],
        compiler_params=_cparams(("parallel", "arbitrary")),
        name=f"ffn{sub}",
    )(x, ada, ln_g, ln_b, wi, wi, wo)
    return res if emit_next else res[0]


def _mm_kernel(h_ref, w_ref, o_ref):
    o_ref[...] = jnp.dot(h_ref[...], w_ref[...], preferred_element_type=F32)


def _mm_call(h, w, *, tm, tn, col0, n, name="proj"):
    m, k = h.shape
    assert col0 % tn == 0 and n % tn == 0
    cb = col0 // tn
    return pl.pallas_call(
        _mm_kernel,
        grid=(m // tm, n // tn),
        in_specs=[pl.BlockSpec((tm, k), lambda i, j: (i, 0)),
                  pl.BlockSpec((k, tn), lambda i, j: (0, j + cb))],
        out_specs=pl.BlockSpec((tm, tn), lambda i, j: (i, j)),
        out_shape=jax.ShapeDtypeStruct((m, n), F32),
        compiler_params=_cparams(("parallel", "arbitrary")),
        name=name,
    )(h, w)


def _bias_tiles_kernel(tbl_ref, o_ref):
    h = pl.program_id(0)
    r = lax.broadcasted_iota(jnp.int32, (MOBA_BLOCK, MOBA_BLOCK), 0)
    c = lax.broadcasted_iota(jnp.int32, (MOBA_BLOCK, MOBA_BLOCK), 1)
    rel0 = r - c
    o_ref[0, 0] = jnp.where(rel0 >= 0, _t5_bias(rel0, tbl_ref, h), NEG_BIG)
    o_ref[0, 1] = _t5_bias(rel0 + MOBA_BLOCK, tbl_ref, h)


def _bias_tiles_call(rel_bias):
    return pl.pallas_call(
        _bias_tiles_kernel,
        grid=(H_ATT,),
        in_specs=[pl.BlockSpec(memory_space=pltpu.SMEM)],
        out_specs=pl.BlockSpec((1, 2, MOBA_BLOCK, MOBA_BLOCK), lambda h: (h, 0, 0, 0)),
        out_shape=jax.ShapeDtypeStruct((H_ATT, 2, MOBA_BLOCK, MOBA_BLOCK), F32),
        compiler_params=_cparams(("arbitrary",)),
        name="bias_tiles",
    )(rel_bias)


def _rank_select(s, n_valid):
    col = lax.broadcasted_iota(jnp.int32, s.shape, 1)
    cnt = jnp.zeros(s.shape, jnp.int32)
    for m in range(n_valid):
        sm = s[:, m:m + 1]
        beats = (sm > s) | ((sm == s) & (m < col))
        cnt = cnt + jnp.where(beats, 1, 0)
    return (col < n_valid) & (cnt < MOBA_TOPK)


def _attn_prompt_kernel(far_ref, q_ref, k_ref, v_ref, bt_ref, o_ref):
    h = pl.program_id(1)
    s_len = q_ref.shape[1]
    nb = s_len // MOBA_BLOCK
    q = q_ref[0]
    k = k_ref[0]
    kb = k.astype(BF16)
    vb = v_ref[0].astype(BF16)
    qb = (q * (HD_ATT ** -0.5)).astype(BF16)
    means = jnp.mean(k.reshape(nb, MOBA_BLOCK, HD_ATT), axis=1)
    means = jnp.concatenate([means, jnp.zeros((LANES - nb, HD_ATT), F32)], axis=0)
    mparts = _split(means, 3)
    far = far_ref[h]
    rows = [slice(i * MOBA_BLOCK, (i + 1) * MOBA_BLOCK) for i in range(nb)]
    logits = [lax.dot_general(qb[rows[i]], kb[:(i + 1) * MOBA_BLOCK], _NT, preferred_element_type=F32)
              for i in range(nb)]
    scores = [None] + [_dotp(_split(q[rows[i]], 3), mparts, _NT, order=3) for i in range(1, nb)]
    negs = [None] + [jnp.where(_rank_select(scores[i], i), 0.0, NEG_BIG) for i in range(1, nb)]
    probs, dens = [], []
    for i in range(nb):
        tiles = []
        for j in range(i + 1):
            tile = logits[i][:, j * MOBA_BLOCK:(j + 1) * MOBA_BLOCK]
            if j == i:
                tile = tile + bt_ref[0, 0]
            elif j == i - 1:
                tile = tile + bt_ref[0, 1] + negs[i][:, j:j + 1]
            else:
                tile = tile + (far + negs[i][:, j:j + 1])
            tiles.append(tile)
        lg = jnp.concatenate(tiles, axis=-1) if len(tiles) > 1 else tiles[0]
        mx = jnp.max(lg, axis=-1, keepdims=True)
        p = jnp.exp(lg - mx)
        dens.append(jnp.sum(p, axis=-1, keepdims=True))
        probs.append(p.astype(BF16))
    for i in range(nb):
        out = jnp.dot(probs[i], vb[:(i + 1) * MOBA_BLOCK], preferred_element_type=F32) / dens[i]
        o_ref[0, rows[i], :] = out.astype(o_ref.dtype)


def _attn_prompt_call(q, k, v, bias_tiles, far):
    bsz, s_len, _ = q.shape
    qspec = pl.BlockSpec((1, s_len, HD_ATT), lambda b, h: (b, 0, h))
    return pl.pallas_call(
        _attn_prompt_kernel,
        grid=(bsz, H_ATT),
        in_specs=[pl.BlockSpec(memory_space=pltpu.SMEM), qspec, qspec, qspec,
                  pl.BlockSpec((1, 2, MOBA_BLOCK, MOBA_BLOCK), lambda b, h: (h, 0, 0, 0))],
        out_specs=pl.BlockSpec((1, s_len, HD_ATT), lambda b, h: (b, 0, h)),
        out_shape=jax.ShapeDtypeStruct((bsz, s_len, C_ATT), BF16),
        compiler_params=_cparams(("parallel", "arbitrary")),
        name="attn_prompt",
    )(far, q, k, v, bias_tiles)


MEAN_BLOCKS = 8
PAGES_PER_BLOCK = MOBA_BLOCK // PAGE_SIZE


def _cache_means_kernel(pt_ref, *refs):
    page_refs, o_ref = refs[:-1], refs[-1]
    for m in range(MEAN_BLOCKS):
        tot = None
        for u in range(PAGES_PER_BLOCK):
            s = jnp.sum(page_refs[m * PAGES_PER_BLOCK + u][0], axis=0)
            tot = s if tot is None else tot + s
        o_ref[0, m] = tot * (1.0 / MOBA_BLOCK)


def _cache_means_call(page_table, cache_k4, n_blocks):
    dbs = page_table.shape[0]
    npg = MEAN_BLOCKS * PAGES_PER_BLOCK
    in_specs = [pl.BlockSpec((1, PAGE_SIZE, H_ATT, HD_ATT), functools.partial(
        lambda b, g, pt, u: (pt[b, g * npg + u], 0, 0, 0), u=u)) for u in range(npg)]
    return pl.pallas_call(
        _cache_means_kernel,
        grid_spec=pltpu.PrefetchScalarGridSpec(
            num_scalar_prefetch=1,
            grid=(dbs, n_blocks // MEAN_BLOCKS),
            in_specs=in_specs,
            out_specs=pl.BlockSpec((1, MEAN_BLOCKS, H_ATT, HD_ATT), lambda b, g, pt: (b, g, 0, 0)),
        ),
        out_shape=jax.ShapeDtypeStruct((dbs, n_blocks, H_ATT, HD_ATT), F32),
        compiler_params=_cparams(("parallel", "arbitrary")),
        name="cache_means",
    )(page_table, *([cache_k4] * npg))


def _topk_ids_kernel(q_ref, m_ref, o_ref):
    q = q_ref[0]
    means = m_ref[0]
    n_blocks = means.shape[0]
    ds = q.shape[0]
    pad = jnp.zeros((LANES - n_blocks, HD_ATT), F32)
    col = lax.broadcasted_iota(jnp.int32, (ds, LANES), 1)
    for h in range(H_ATT):
        sl = slice(h * HD_ATT, (h + 1) * HD_ATT)
        mh = jnp.concatenate([means[:, sl], pad], axis=0) if n_blocks < LANES else means[:, sl]
        s = _dotp(_split(q[:, sl], 3), _split(mh, 3), _NT, order=3)
        s = jnp.where(col < n_blocks, s, NEG_BIG)
        ids = jnp.zeros((ds, LANES), jnp.int32)
        for t in range(MOBA_TOPK):
            mx = jnp.max(s, axis=-1, keepdims=True)
            idx = jnp.min(jnp.where(s == mx, col, LANES), axis=-1, keepdims=True)
            ids = jnp.where(col == t, idx, ids)
            s = jnp.where(col == idx, NEG_BIG * 2, s)
        o_ref[0, h] = ids


def _topk_ids_call(q, means):
    dbs, ds, _ = q.shape
    n_blocks = means.shape[1]
    return pl.pallas_call(
        _topk_ids_kernel,
        grid=(dbs,),
        in_specs=[pl.BlockSpec((1, ds, C_ATT), lambda b: (b, 0, 0)),
                  pl.BlockSpec((1, n_blocks, C_ATT), lambda b: (b, 0, 0))],
        out_specs=pl.BlockSpec((1, H_ATT, ds, LANES), lambda b: (b, 0, 0, 0)),
        out_shape=jax.ShapeDtypeStruct((dbs, H_ATT, ds, LANES), jnp.int32),
        compiler_params=_cparams(("parallel",)),
        name="topk_ids",
    )(q, means)


N_SEL_PAGES = MOBA_TOPK * PAGES_PER_BLOCK


def _attn_sample_kernel(pt_ref, ids_ref, tbl_ref, q_ref, kn_ref, vn_ref, ck_hbm, cv_hbm, o_ref,
                        kbuf, vbuf, sems, *, past_len):
    ds = q_ref.shape[1]
    n_pg = ds * N_SEL_PAGES
    b, h = pl.program_id(0), pl.program_id(1)
    nh = pl.num_programs(1)
    step = b * nh + h
    n_steps = pl.num_programs(0) * nh
    slot = step % 2

    def page_copies(bb, hh, sl):
        base = (bb * H_ATT + hh) * ds * MOBA_TOPK
        cps = []
        for u in range(n_pg):
            pg = pt_ref[bb, ids_ref[base + u // PAGES_PER_BLOCK] * PAGES_PER_BLOCK + u % PAGES_PER_BLOCK]
            cps.append(pltpu.make_async_copy(ck_hbm.at[pg, :, hh, :], kbuf.at[sl, u], sems.at[sl, u]))
            cps.append(pltpu.make_async_copy(cv_hbm.at[pg, :, hh, :], vbuf.at[sl, u], sems.at[sl, n_pg + u]))
        return cps

    @pl.when(step == 0)
    def _():
        for cp in page_copies(b, h, slot):
            cp.start()

    @pl.when(step + 1 < n_steps)
    def _():
        nxt = step + 1
        for cp in page_copies(nxt // nh, nxt % nh, 1 - slot):
            cp.start()

    for cp in page_copies(b, h, slot):
        cp.wait()
    kp_refs = [kbuf.at[slot, u] for u in range(n_pg)]
    vp_refs = [vbuf.at[slot, u] for u in range(n_pg)]
    qb = (q_ref[0] * (HD_ATT ** -0.5)).astype(BF16)
    qrow = lax.broadcasted_iota(jnp.int32, (ds, 1), 0)
    lane = lax.broadcasted_iota(jnp.int32, (ds, PAGE_SIZE), 1)
    lo = lax.dot_general(qb, kn_ref[0].astype(BF16), _NT, preferred_element_type=F32)
    rel_o = lax.broadcasted_iota(jnp.int32, (ds, ds), 0) - lax.broadcasted_iota(jnp.int32, (ds, ds), 1)
    lo = jnp.where(rel_o >= 0, lo + _t5_bias(rel_o, tbl_ref, h), NEG_BIG)
    mx_o = jnp.max(lo, axis=-1, keepdims=True)
    vn = vn_ref[0].astype(BF16)
    tiles = []
    for qi in range(ds):
        flat = (b * H_ATT + h) * ds + qi
        tq = []
        for s in range(MOBA_TOPK):
            blk = ids_ref[flat * MOBA_TOPK + s]
            for u in range(PAGES_PER_BLOCK):
                kpg = kp_refs[(qi * MOBA_TOPK + s) * PAGES_PER_BLOCK + u][...].astype(BF16)
                lg = lax.dot_general(qb, kpg, _NT, preferred_element_type=F32)
                rel = (past_len + qi) - (blk * MOBA_BLOCK + u * PAGE_SIZE + lane)
                tq.append(lg + _t5_bias(rel, tbl_ref, h))
        tiles.append(tq)
    mxs = []
    for tq in tiles:
        mx = mx_o
        for t in tq:
            mx = jnp.maximum(mx, jnp.max(t, axis=-1, keepdims=True))
        mxs.append(mx)
    pos = [jnp.exp(lo - mx) for mx in mxs]
    ps = [[jnp.exp(t - mx) for t in tq] for tq, mx in zip(tiles, mxs)]
    accs = [jnp.dot(po.astype(BF16), vn, preferred_element_type=F32) for po in pos]
    for qi in range(ds):
        for i in range(N_SEL_PAGES):
            accs[qi] = accs[qi] + jnp.dot(ps[qi][i].astype(BF16), vp_refs[qi * N_SEL_PAGES + i][...].astype(BF16),
                                          preferred_element_type=F32)
    result = jnp.zeros((ds, HD_ATT), F32)
    for qi in range(ds):
        den = jnp.sum(pos[qi], axis=-1, keepdims=True)
        for p in ps[qi]:
            den = den + jnp.sum(p, axis=-1, keepdims=True)
        result = jnp.where(qrow == qi, accs[qi] / den, result)
    o_ref[0] = result


def _attn_sample_call(page_table, ids, rel_bias, q, k_new, v_new, cache_k2, cache_v2, past_len):
    dbs, ds, _ = q.shape
    n_pg = ds * N_SEL_PAGES

    nspec = pl.BlockSpec((1, ds, HD_ATT), lambda b, h, p, i: (b, 0, h))
    pool = pl.BlockSpec(memory_space=pl.ANY)
    return pl.pallas_call(
        functools.partial(_attn_sample_kernel, past_len=past_len),
        grid_spec=pltpu.PrefetchScalarGridSpec(
            num_scalar_prefetch=2,
            grid=(dbs, H_ATT),
            in_specs=[pl.BlockSpec(memory_space=pltpu.SMEM), nspec, nspec, nspec, pool, pool],
            out_specs=pl.BlockSpec((1, ds, HD_ATT), lambda b, h, p, i: (b, 0, h)),
            scratch_shapes=[pltpu.VMEM((2, n_pg, PAGE_SIZE, HD_ATT), F32),
                            pltpu.VMEM((2, n_pg, PAGE_SIZE, HD_ATT), F32),
                            pltpu.SemaphoreType.DMA((2, 2 * n_pg))],
        ),
        out_shape=jax.ShapeDtypeStruct((dbs, ds, C_ATT), F32),
        compiler_params=_cparams(("arbitrary", "arbitrary")),
        name="attn_sample",
    )(page_table, ids, rel_bias, q, k_new, v_new, cache_k2, cache_v2)


def _rwkv_pre_kernel(h_ref, wz_ref, sh0_ref, mu_ref, w0_ref, w2_ref, a0_ref, a2_ref, g2_ref, kk_ref, ka_ref,
                     r_o, k_o, v_o, kk_o, b_o, ld_o, g_o, zl_o, carry):
    i = pl.program_id(1)
    z = jnp.dot(h_ref[0], wz_ref[...], preferred_element_type=F32)
    tm = z.shape[0]

    @pl.when(i == 0)
    def _():
        carry[...] = sh0_ref[0]

    first = carry[...]
    carry[...] = z[tm - 1:tm, :]
    zl_o[0, 0] = z[tm - 8:tm, :]
    row = lax.broadcasted_iota(jnp.int32, (tm, 1), 0)
    zprev = jnp.where(row == 0, first, pltpu.roll(z, 1, 0))
    zs = z + (zprev - z) * mu_ref[...]
    c = C_RWKV
    r, k, v = zs[:, :c], zs[:, c:2 * c], zs[:, 2 * c:3 * c]
    zw = zs[:, 3 * c:3 * c + LORA_PAD]
    za = zs[:, 3 * c + LORA_PAD:3 * c + 2 * LORA_PAD]
    zg = zs[:, 3 * c + 2 * LORA_PAD:]
    wl = w0_ref[...] + jnp.dot(jnp.tanh(zw).astype(BF16), w2_ref[...], preferred_element_type=F32)
    w = -(jnp.maximum(-wl, 0.0) + jnp.log(1.0 + jnp.exp(-jnp.abs(wl)))) - 0.5
    ld_o[0] = -jnp.exp(w)
    a = jax.nn.sigmoid(a0_ref[...] + jnp.dot(za.astype(BF16), a2_ref[...], preferred_element_type=F32))
    g_o[0] = jnp.dot(jax.nn.sigmoid(zg).astype(BF16), g2_ref[...], preferred_element_type=F32)
    kkr = k * kk_ref[...]
    n2 = _segsum64(kkr * kkr, _seg_ones())
    kkn = kkr / jnp.maximum(jnp.sqrt(n2), 1e-12)
    r_o[0] = r
    k_o[0] = k * (1.0 + (a - 1.0) * ka_ref[...])
    v_o[0] = v
    kk_o[0] = kkn
    b_o[0] = kkn * a


def _rwkv_pre_call(h, win, z_col0, shift0, p, *, tm):
    bsz, t, d = h.shape
    zp = Z_PAD
    assert z_col0 % zp == 0
    nt = t // tm
    c = C_RWKV
    row = lambda b, i: (0, 0)
    vec = lambda n: pl.BlockSpec((1, n), row)
    tile = pl.BlockSpec((1, tm, c), lambda b, i: (b, i, 0))
    return pl.pallas_call(
        _rwkv_pre_kernel,
        grid=(bsz, nt),
        in_specs=[pl.BlockSpec((1, tm, d), lambda b, i: (b, i, 0)),
                  pl.BlockSpec((d, zp), lambda b, i: (0, z_col0 // zp)),
                  pl.BlockSpec((1, 1, zp), lambda b, i: (b, 0, 0)),
                  vec(zp), vec(c),
                  pl.BlockSpec((LORA_PAD, c), row), vec(c),
                  pl.BlockSpec((LORA_PAD, c), row),
                  pl.BlockSpec((D_GATE_LORA, c), row), vec(c), vec(c)],
        out_specs=[tile] * 7 + [pl.BlockSpec((1, 1, 8, zp), lambda b, i: (b, i, 0, 0))],
        out_shape=[jax.ShapeDtypeStruct((bsz, t, c), F32)] * 7 + [jax.ShapeDtypeStruct((bsz, nt, 8, zp), F32)],
        scratch_shapes=[pltpu.VMEM((1, zp), F32)],
        compiler_params=_cparams(("parallel", "arbitrary")),
        name="rwkv_pre",
    )(h, win, shift0.reshape(bsz, 1, zp), p["mu"], p["w0"], p["w2"], p["a0"], p["a2"], p["g2"], p["k_k"], p["k_a"])


CHUNK_PREC = {"gram": (1, 1), "state_read": (1, 1), "mkv": (1, 1), "solve1": (2, 2), "solve_sq": (2, 2),
              "solve_ap": (2, 2), "out": (1, 1), "state_upd": (1, 1)}


def _chunk_pairs(rs, ks, vs, kks, bs, lds, ss):
    c = CHUNK
    lane = lax.broadcasted_iota(jnp.int32, (1, LANES), 1)
    m_a = jnp.where(lane < HS_RWKV, 1.0, 0.0)
    m_b = 1.0 - m_a
    row = lax.broadcasted_iota(jnp.int32, (c, 2 * c), 0)
    coli = lax.broadcasted_iota(jnp.int32, (c, 2 * c), 1) % c
    strict = coli < row
    incl = coli <= row
    lr = lax.broadcasted_iota(jnp.int32, (c, c), 0)
    lc = lax.broadcasted_iota(jnp.int32, (c, c), 1)
    ltri = jnp.where(lc <= lr, 1.0, 0.0).astype(BF16)
    ones = jnp.ones((c, LANES), BF16)
    rr = lax.broadcasted_iota(jnp.int32, (LANES, LANES), 0) // HS_RWKV
    cc = lax.broadcasted_iota(jnp.int32, (LANES, LANES), 1) // HS_RWKV
    same_head = rr == cc

    def each(f, *lists):
        return [f(*args) for args in zip(*lists)]

    def stack2(x):
        return jnp.concatenate([x * m_a, x * m_b], axis=0)

    def prod(site, a, b, dims=_NN):
        na, nb = CHUNK_PREC[site]
        return _dotp(_split(a, na), _split(b, nb), dims, order=max(na, nb))

    def pm(site, mcats, xs):
        return each(lambda m, x: prod(site, m, stack2(x)), mcats, xs)

    ldp = each(lambda x: _split(x, 3), lds)
    cums = each(lambda p: _dotp([ltri], p, _NN), ldp)
    gcols = each(lambda p: jnp.exp(_dotp(p, [ones], _TN)), ldp)
    g_inv = each(lambda cu: jnp.exp(-cu), cums)
    g_end = each(lambda cu: jnp.exp(cu[c - 1:c, :] - cu), cums)
    p_all = each(lambda kk, r, cu, ld: jnp.concatenate([kk * jnp.exp(cu - ld), r * jnp.exp(cu)], axis=0),
                 kks, rs, cums, lds)
    z2 = each(lambda k, b, gi: jnp.concatenate([stack2(k * gi), stack2(b * gi)], axis=0), ks, bs, g_inv)
    g4 = each(lambda p, z: prod("gram", p, z, _NT), p_all, z2)
    mk = each(lambda g: jnp.where(strict, g[:c, :2 * c], 0.0), g4)
    pj = each(lambda g: jnp.where(strict, -g[:c, 2 * c:], 0.0), g4)
    akb = each(lambda g: jnp.concatenate([jnp.where(incl, g[c:, :2 * c], 0.0),
                                          jnp.where(incl, -g[c:, 2 * c:], 0.0)], axis=1), g4)
    ps = each(lambda p, s: prod("state_read", p, s), p_all, ss)
    mkv = pm("mkv", mk, vs)
    rhs = each(lambda p, m: p[:c] + m, ps, mkv)
    us = each(lambda x, d: x + d, rhs, pm("solve1", pj, rhs))
    n = 2
    while n < c:
        pj = pm("solve_sq", pj, pj)
        us = each(lambda x, d: x + d, us, pm("solve_ap", pj, us))
        n *= 2
    ys = each(lambda p, m, v, u: p[c:] + prod("out", m, jnp.concatenate([stack2(v), stack2(u)], axis=0)),
              ps, akb, vs, us)
    upd = each(lambda k, b, ge, v, u: prod("state_upd", jnp.concatenate([k * ge, -(b * ge)], axis=0),
                                           jnp.concatenate([v, u], axis=0), _TN), ks, bs, g_end, vs, us)
    s_new = each(lambda gc, s, up: gc * s + jnp.where(same_head, up, 0.0), gcols, ss, upd)
    return ys, s_new


def _rwkv_out_pairs(ys, rs, ks, vs, gs, rks, lgs, lbs):
    p128 = _seg_ones()
    c = CHUNK
    inv = 1.0 / HS_RWKV

    def seg3(x):
        return _split(x, 3)

    def sum3(t, o):
        return t[o * c:(o + 1) * c] + t[(o + 1) * c:(o + 2) * c] + t[(o + 2) * c:(o + 3) * c]

    t1 = [jnp.dot(jnp.concatenate(seg3(y) + seg3(r * k * rk), axis=0), p128, preferred_element_type=F32)
          for y, r, k, rk in zip(ys, rs, ks, rks)]
    ycs = [y - sum3(t, 0) * inv for y, t in zip(ys, t1)]
    t2 = [jnp.dot(jnp.concatenate(seg3(yc * yc), axis=0), p128, preferred_element_type=F32) for yc in ycs]
    outs = []
    for yc, ta, tb, v, g, lg, lb in zip(ycs, t1, t2, vs, gs, lgs, lbs):
        yn = yc * lax.rsqrt(sum3(tb, 0) * inv + GN_EPS) * lg + lb
        outs.append(((yn + sum3(ta, 3) * v) * g).astype(BF16))
    return outs


def _rwkv_chunk_kernel(r_ref, k_ref, v_ref, kk_ref, b_ref, ld_ref, g_ref, rk_ref, lg_ref, lb_ref, s0_ref,
                       o_ref, so_ref, s_scr):
    ci = pl.program_id(2)

    @pl.when(ci == 0)
    def _():
        s_scr[...] = s0_ref[0]

    sls = [slice(p * LANES, (p + 1) * LANES) for p in range(PAIRS_PER_STEP)]
    pairs = lambda ref: [ref[0, :, sl] for sl in sls]
    vecs = lambda ref: [ref[:, sl] for sl in sls]
    rs, ks, vs = pairs(r_ref), pairs(k_ref), pairs(v_ref)
    ys, s_new = _chunk_pairs(rs, ks, vs, pairs(kk_ref), pairs(b_ref), pairs(ld_ref),
                             [s_scr[p] for p in range(PAIRS_PER_STEP)])
    outs = _rwkv_out_pairs(ys, rs, ks, vs, pairs(g_ref), vecs(rk_ref), vecs(lg_ref), vecs(lb_ref))
    for p, sl in enumerate(sls):
        o_ref[0, :, sl] = outs[p]
        s_scr[p] = s_new[p]
        so_ref[0, p] = s_new[p]


def _rwkv_chunk_call(r, k, v, kk, b, logd, g, s0_blk, p):
    bsz, t, c = r.shape
    pp = PAIRS_PER_STEP
    w = pp * LANES
    tile = pl.BlockSpec((1, CHUNK, w), lambda bi, pi, ci: (bi, ci, pi))
    vec = pl.BlockSpec((1, w), lambda bi, pi, ci: (0, pi))
    sspec = pl.BlockSpec((1, pp, LANES, LANES), lambda bi, pi, ci: (bi, pi, 0, 0))
    return pl.pallas_call(
        _rwkv_chunk_kernel,
        grid=(bsz, c // w, t // CHUNK),
        in_specs=[tile] * 7 + [vec] * 3 + [sspec],
        out_specs=[tile, sspec],
        out_shape=[jax.ShapeDtypeStruct((bsz, t, c), BF16),
                   jax.ShapeDtypeStruct(s0_blk.shape, F32)],
        scratch_shapes=[pltpu.VMEM((pp, LANES, LANES), F32)],
        compiler_params=_cparams(("parallel", "parallel", "arbitrary")),
        name="rwkv_chunk",
    )(r, k, v, kk, b, logd, g, p["r_k"], p["lnx_g"], p["lnx_b"], s0_blk)


def _combine_kernel(x_ref, ada_ref, lng_ref, lnb_ref, oa_ref, or_ref, ga_ref, gr_ref, wa_ref, wr_ref, wo_ref, o_ref):
    bb, t, d = x_ref.shape
    ua = jnp.dot(oa_ref[...], wa_ref[...], preferred_element_type=F32)
    ur = jnp.dot(or_ref[...], wr_ref[...], preferred_element_type=F32)
    m = jax.nn.sigmoid(ga_ref[...]) * ua + jax.nn.sigmoid(gr_ref[...]) * ur
    mo = jnp.dot(m.astype(BF16), wo_ref[...], preferred_element_type=F32)
    y = DEEPNORM_ALPHA * x_ref[...] + ada_ref[:, 5:6, :] * mo.reshape(bb, t, d)
    o_ref[...] = _ln(y) * lng_ref[1:2, :] + lnb_ref[1:2, :]


def _combine_call(x, ada, ln_g, ln_b, oa, orw, ga, gr, wa, wr, wo, *, blk):
    bsz, t, d = x.shape
    bb, tt = blk
    nt = t // tt
    rows = bb * tt
    xmap = lambda i: (i // nt, i % nt, 0)
    const = lambda i: (0, 0)
    rowt = lambda n: pl.BlockSpec((rows, n), lambda i: (i, 0))
    return pl.pallas_call(
        _combine_kernel,
        grid=((bsz // bb) * nt,),
        in_specs=[pl.BlockSpec((bb, tt, d), xmap),
                  pl.BlockSpec((bb, N_ADA, d), lambda i: (i // nt, 0, 0)),
                  pl.BlockSpec((3, d), const), pl.BlockSpec((3, d), const),
                  rowt(C_ATT), rowt(C_RWKV), rowt(d), rowt(d),
                  pl.BlockSpec((C_ATT, d), const), pl.BlockSpec((C_RWKV, d), const), pl.BlockSpec((d, d), const)],
        out_specs=pl.BlockSpec((bb, tt, d), xmap),
        out_shape=jax.ShapeDtypeStruct(x.shape, F32),
        compiler_params=_cparams(("parallel",)),
        name="combine",
    )(x, ada, ln_g, ln_b, oa, orw, ga, gr, wa, wr, wo)


def _rearrange_z(a):
    c3 = 3 * C_RWKV
    pad = [(0, 0)] * (a.ndim - 1) + [(0, LORA_PAD - D_DECAY_LORA)]
    return jnp.concatenate([a[..., :c3],
                            jnp.pad(a[..., c3:c3 + D_DECAY_LORA], pad),
                            jnp.pad(a[..., c3 + D_DECAY_LORA:c3 + D_DECAY_LORA + D_AAA_LORA], pad),
                            a[..., c3 + D_DECAY_LORA + D_AAA_LORA:]], axis=-1)


def _unarrange_z(a):
    c3 = 3 * C_RWKV
    return jnp.concatenate([a[..., :c3], a[..., c3:c3 + D_DECAY_LORA],
                            a[..., c3 + LORA_PAD:c3 + LORA_PAD + D_AAA_LORA], a[..., c3 + 2 * LORA_PAD:]], axis=-1)


def _state_to_blocks(state):
    bsz = state.shape[0]
    st = jnp.swapaxes(state, -1, -2).reshape(bsz, H_RWKV // 2, 2, HS_RWKV, HS_RWKV)
    blk = jnp.einsum('bphkv,hg->bphkgv', st, jnp.eye(2, dtype=state.dtype))
    return blk.reshape(bsz, H_RWKV // 2, LANES, LANES)


def _blocks_to_state(blk):
    bsz = blk.shape[0]
    b6 = blk.reshape(bsz, H_RWKV // 2, 2, HS_RWKV, 2, HS_RWKV)
    st = jnp.stack([b6[:, :, 0, :, 0, :], b6[:, :, 1, :, 1, :]], axis=2)
    return jnp.swapaxes(st, -1, -2).reshape(bsz, H_RWKV, HS_RWKV, HS_RWKV)


def _trunk(x, ada, attend, shift0, wkv0, w, *, ffn_blk, mm_tm, pre_tm, comb_blk):
    bsz, t, d = x.shape
    m = bsz * t
    ffn = functools.partial(_ffn_call, ada=ada, ln_g=w["ln_g"], ln_b=w["ln_b"], wi=w["ffn_wi"], wo=w["ffn_wo"],
                            layer=w["layer"], blk=ffn_blk)
    x1, h2 = ffn(x, which=0, sub=0, emit_next=True)
    h2f = h2.reshape(m, d)
    win = w["win"]
    proj = functools.partial(_mm_call, h2f, win, tm=mm_tm)
    q = proj(tn=C_ATT, col0=0, n=C_ATT, name="proj_q").reshape(bsz, t, C_ATT)
    k = proj(tn=C_ATT, col0=C_ATT, n=C_ATT, name="proj_k").reshape(bsz, t, C_ATT)
    v = proj(tn=C_ATT, col0=2 * C_ATT, n=C_ATT, name="proj_v").reshape(bsz, t, C_ATT)
    ga = proj(tn=d // 2, col0=3 * C_ATT, n=d, name="proj_ga")
    gr = proj(tn=d // 2, col0=3 * C_ATT + d, n=d, name="proj_gr")
    o_att = attend(q, k, v)
    r, k2, vv, kk, b, logd, g, z_last = _rwkv_pre_call(h2, win, 3 * C_ATT + 2 * d, _rearrange_z(shift0), w, tm=pre_tm)
    tp = -(-t // CHUNK) * CHUNK
    seq = [r, k2, vv, kk, b, logd, g]
    if tp != t:
        seq = [jnp.pad(a, ((0, 0), (0, tp - t), (0, 0))) for a in seq]
    o_rwkv, s_blk = _rwkv_chunk_call(*seq, _state_to_blocks(wkv0), w)
    o_rwkv = o_rwkv[:, :t].reshape(m, C_RWKV)
    x2 = _combine_call(x1, ada, w["ln_g"], w["ln_b"], o_att.reshape(m, C_ATT), o_rwkv, ga, gr,
                       w["wua"], w["wur"], w["wout"], blk=comb_blk)
    x3 = ffn(x2, which=1, sub=2, emit_next=False)
    shift_new = _unarrange_z(z_last[:, -1, 7])
    return (x3, k.reshape(bsz, t, H_ATT, HD_ATT), v.reshape(bsz, t, H_ATT, HD_ATT), _blocks_to_state(s_blk), shift_new)


def kernel(x_prompt, x_sample, cache_k, cache_v, state_wkv, state_shift, page_table, c_prompt, c_sample, rel_bias, w_ada, b_ada, ln_g, ln_b, ffn_wi, ffn_wo, w_in, mu_shift, w0, w2, a0, a2, g2, k_k, k_a, r_k, lnx_g, lnx_b, w_up_attn, w_up_rwkv, w_out):
    assert w_ada.shape[0] == DEPTH == 1
    bsz, s_len, d = x_prompt.shape
    dbs, ds, _ = x_sample.shape
    past_len = page_table.shape[1] * PAGE_SIZE
    n_phys = cache_k.shape[1]
    l = 0
    win = w_in[l]
    c3 = 3 * C_ATT
    lora_rows = ((0, LORA_PAD - D_DECAY_LORA), (0, 0))
    w = {
        "ln_g": ln_g[l], "ln_b": ln_b[l],
        "layer": l, "ffn_wi": ffn_wi.astype(BF16), "ffn_wo": ffn_wo.astype(BF16),
        "win": jnp.concatenate([win[:, :c3], win[:, c3 + RWKV_PROJ:],
                                _rearrange_z(win[:, c3:c3 + RWKV_PROJ])], axis=1).astype(BF16),
        "mu": _rearrange_z(mu_shift[l])[None, :],
        "w0": w0[l][None, :], "w2": jnp.pad(w2[l], lora_rows).astype(BF16),
        "a0": a0[l][None, :], "a2": jnp.pad(a2[l], lora_rows).astype(BF16),
        "g2": g2[l].astype(BF16), "k_k": k_k[l][None, :], "k_a": k_a[l][None, :],
        "r_k": r_k[l].reshape(1, C_RWKV), "lnx_g": lnx_g[l][None, :], "lnx_b": lnx_b[l][None, :],
        "wua": w_up_attn[l].astype(BF16), "wur": w_up_rwkv[l].astype(BF16), "wout": w_out[l].astype(BF16),
    }
    n_c = bsz + dbs
    c_rows = -(-n_c // 8) * 8
    c_all = jnp.concatenate([c_prompt, c_sample, jnp.zeros((c_rows - n_c, d), F32)], axis=0)
    ada = _ada_call(c_all, w_ada[l], b_ada[l]).reshape(c_rows, N_ADA, d)

    bias_tiles = _bias_tiles_call(rel_bias)
    far = rel_bias[NUM_BUCKETS - 1]
    attend_p = lambda q, k, v: _attn_prompt_call(q, k, v, bias_tiles, far)
    yp, kp, vp, wp, sp = _trunk(
        x_prompt, ada[:bsz], attend_p, jnp.zeros((bsz, RWKV_PROJ), F32),
        jnp.zeros((bsz, H_RWKV, HS_RWKV, HS_RWKV), F32), w,
        ffn_blk=(1, 512), mm_tm=1024, pre_tm=256, comb_blk=(1, 256))

    cache_k2 = cache_k.reshape(DEPTH * n_phys, PAGE_SIZE, H_ATT, HD_ATT)
    cache_v2 = cache_v.reshape(DEPTH * n_phys, PAGE_SIZE, H_ATT, HD_ATT)
    page_table = page_table + l * n_phys
    n_full = past_len // MOBA_BLOCK

    def attend_s(q, k, v):
        means = _cache_means_call(page_table, cache_k2, n_full).reshape(dbs, n_full, C_ATT)
        ids = _topk_ids_call(q, means)[..., :MOBA_TOPK]
        o = _attn_sample_call(page_table, ids.reshape(-1), rel_bias, q, k, v, cache_k2, cache_v2, past_len)
        return o.astype(BF16)

    ys, kn, vn, wn, sn = _trunk(
        x_sample, ada[bsz:n_c], attend_s, state_shift[l], state_wkv[l], w,
        ffn_blk=(dbs, ds), mm_tm=dbs * ds, pre_tm=ds, comb_blk=(dbs, ds))

    return (yp, ys, kp[None], vp[None], kn[None], vn[None], wp[None], wn[None], sp[None], sn[None])
```

```python
import functools
import math

import jax
import jax.numpy as jnp
from jax import lax
from jax.experimental import pallas as pl
from jax.experimental.pallas import tpu as pltpu

F32 = jnp.float32
BF16 = jnp.bfloat16

D_MODEL = 2048
D_FF = 5632
N_ADA = 9
H_ATT = 8
HD_ATT = 128
C_ATT = H_ATT * HD_ATT
MOBA_BLOCK = 256
MOBA_TOPK = 3
NUM_BUCKETS = 32
MAX_DISTANCE = 128
PAGE_SIZE = 128
HS_RWKV = 64
C_RWKV = 1024
H_RWKV = C_RWKV // HS_RWKV
D_DECAY_LORA = 96
D_AAA_LORA = 96
D_GATE_LORA = 256
RWKV_PROJ = 3 * C_RWKV + D_DECAY_LORA + D_AAA_LORA + D_GATE_LORA
LORA_PAD = 128
Z_PAD = 3 * C_RWKV + 2 * LORA_PAD + D_GATE_LORA
GN_EPS = 64e-5
LN_EPS = 1e-5
DEPTH = 1
DEEPNORM_ALPHA = (2 * DEPTH) ** 0.25
NEG_BIG = -1e30

LANES = 128
CHUNK = 64
PAIRS_PER_STEP = 8
VMEM_LIMIT = 56 * 1024 * 1024


def _t5_thresholds():
    max_exact = NUM_BUCKETS // 2
    thr = list(range(1, max_exact + 1))
    for b in range(max_exact + 1, NUM_BUCKETS):
        x = max_exact * (MAX_DISTANCE / max_exact) ** ((b - max_exact) / (NUM_BUCKETS - max_exact))
        thr.append(int(math.ceil(x)))
    return tuple(thr)


T5_THR = _t5_thresholds()


def _cparams(sem, vmem=VMEM_LIMIT):
    return pltpu.CompilerParams(dimension_semantics=sem, vmem_limit_bytes=vmem)


def _ln(x):
    mu = jnp.mean(x, axis=-1, keepdims=True)
    xc = x - mu
    var = jnp.mean(xc * xc, axis=-1, keepdims=True)
    return xc * lax.rsqrt(var + LN_EPS)


def _split(x, n):
    parts = []
    for i in range(n):
        p = x.astype(BF16)
        parts.append(p)
        if i + 1 < n:
            x = x - p.astype(F32)
    return parts


_NN = (((1,), (0,)), ((), ()))
_NT = (((1,), (1,)), ((), ()))
_TN = (((0,), (0,)), ((), ()))


def _dotp(ap, bp, dims=_NN, order=None):
    if order is None:
        order = max(len(ap), len(bp))
    acc = None
    if dims == _NN and len(ap) > 1:
        m = ap[0].shape[0]
        for j, b in enumerate(bp):
            sel = [a for i, a in enumerate(ap) if i + j < order]
            if not sel:
                continue
            lhs = sel[0] if len(sel) == 1 else jnp.concatenate(sel, axis=0)
            t = lax.dot_general(lhs, b, dims, preferred_element_type=F32)
            for r in range(len(sel)):
                part = t[r * m:(r + 1) * m]
                acc = part if acc is None else acc + part
        return acc
    for i, a in enumerate(ap):
        for j, b in enumerate(bp):
            if i + j < order:
                t = lax.dot_general(a, b, dims, preferred_element_type=F32)
                acc = t if acc is None else acc + t
    return acc


def _t5_bias(rel, tbl_ref, h):
    bias = jnp.full(rel.shape, tbl_ref[0, h], F32)
    for b in range(1, NUM_BUCKETS):
        bias = jnp.where(rel >= T5_THR[b - 1], tbl_ref[b, h], bias)
    return bias


def _seg_ones():
    r = lax.broadcasted_iota(jnp.int32, (LANES, LANES), 0) // HS_RWKV
    c = lax.broadcasted_iota(jnp.int32, (LANES, LANES), 1) // HS_RWKV
    return jnp.where(r == c, 1.0, 0.0).astype(BF16)


def _segsum64(x, p128):
    outs = []
    for j in range(x.shape[-1] // LANES):
        xs = x[:, j * LANES:(j + 1) * LANES]
        outs.append(_dotp(_split(xs, 3), [p128]))
    return jnp.concatenate(outs, axis=-1)


ADA_TN = 1024


def _ada_kernel(c_ref, w_ref, b_ref, o_ref):
    c = c_ref[...]
    sc = c * jax.nn.sigmoid(c)
    rows = sc.shape[0]
    s0, s1, s2 = _split(sc, 3)
    w_hi, w_lo = _split(w_ref[...], 2)
    a = jnp.dot(jnp.concatenate([s0, s1, s2], axis=0), w_hi, preferred_element_type=F32)
    b = jnp.dot(jnp.concatenate([s0, s1], axis=0), w_lo, preferred_element_type=F32)
    o_ref[...] = (a[:rows] + a[rows:2 * rows] + a[2 * rows:] + b[:rows] + b[rows:]) + b_ref[...]


def _ada_call(c_all, w_ada, b_ada):
    rows, d = c_all.shape
    n = w_ada.shape[1]
    return pl.pallas_call(
        _ada_kernel,
        grid=(n // ADA_TN,),
        in_specs=[pl.BlockSpec((rows, d), lambda j: (0, 0)),
                  pl.BlockSpec((d, ADA_TN), lambda j: (0, j)),
                  pl.BlockSpec((1, ADA_TN), lambda j: (0, j))],
        out_specs=pl.BlockSpec((rows, ADA_TN), lambda j: (0, j)),
        out_shape=jax.ShapeDtypeStruct((rows, n), F32),
        compiler_params=_cparams(("parallel",)),
        name="ada",
    )(c_all, w_ada, b_ada.reshape(1, n))


FFN_TF = 512


EPI_CHUNKS = 8
FFN_VMEM_LIMIT = 60 * 1024 * 1024


def _ffn_kernel(xc_ref, xp_ref, adac_ref, adap_ref, lng_ref, lnb_ref, wig_ref, wiu_ref, wo_ref, *rest,
                sub, emit_next, n_tiles):
    if emit_next:
        o_ref, h_next_ref, h_scr, acc = rest
    else:
        o_ref, h_scr, acc = rest
    bb, t, d = xc_ref.shape
    rows = bb * t
    i, j = pl.program_id(0), pl.program_id(1)
    cur = i % 2

    @pl.when(jnp.logical_and(i == 0, j == 0))
    def _():
        acc[...] = jnp.zeros_like(acc)

    def epilogue_chunk():
        c = jnp.minimum(j, EPI_CHUNKS - 1)
        if bb == 1:
            rc = t // EPI_CHUNKS
            r0 = pl.multiple_of(c * rc, rc)
            x_rows = xp_ref[0, pl.ds(r0, rc), :]
            a_rows = acc[1 - cur, pl.ds(r0, rc), :]
            ada = adap_ref[0]
        else:
            rc = t
            x_rows = xp_ref[pl.ds(c, 1), :, :].reshape(t, d)
            a_rows = acc[1 - cur, pl.ds(pl.multiple_of(c * t, t), t), :]
            ada = adap_ref[pl.ds(c, 1), :, :].reshape(N_ADA, d)
        gate = ada[3 * sub + 2:3 * sub + 3, :]
        y = DEEPNORM_ALPHA * x_rows + 0.5 * gate * a_rows
        y = _ln(y) * lng_ref[sub:sub + 1, :] + lnb_ref[sub:sub + 1, :]
        if emit_next:
            nxt = sub + 1
            hn = (_ln(y) * (1.0 + ada[3 * nxt + 1:3 * nxt + 2, :]) + ada[3 * nxt:3 * nxt + 1, :]).astype(BF16)
        if bb == 1:
            o_ref[0, pl.ds(r0, rc), :] = y
            if emit_next:
                h_next_ref[0, pl.ds(r0, rc), :] = hn
        else:
            o_ref[pl.ds(c, 1), :, :] = y.reshape(1, t, d)
            if emit_next:
                h_next_ref[pl.ds(c, 1), :, :] = hn.reshape(1, t, d)

    @pl.when(jnp.logical_and(j == 0, i < n_tiles))
    def _():
        h = _ln(xc_ref[...]) * (1.0 + adac_ref[:, 3 * sub + 1:3 * sub + 2, :]) + adac_ref[:, 3 * sub:3 * sub + 1, :]
        h_scr[...] = h.reshape(rows, d).astype(BF16)

    @pl.when(i < n_tiles)
    def _():
        epilogue_chunk()
        h = h_scr[...]
        g = jnp.dot(h, wig_ref[...], preferred_element_type=F32)
        u = jnp.dot(h, wiu_ref[...], preferred_element_type=F32)
        act = (g * jax.nn.sigmoid(g)) * u
        part = jnp.dot(act.astype(BF16), wo_ref[...], preferred_element_type=F32)
        acc[cur] = jnp.where(j == 0, part, acc[cur] + part)

    @pl.when(i == n_tiles)
    def _():
        epilogue_chunk()


def _ffn_call(x, ada, ln_g, ln_b, wi, wo, *, layer, which, sub, emit_next, blk):
    bsz, t, d = x.shape
    bb, tt = blk
    assert (bb == 1 and tt % (16 * EPI_CHUNKS) == 0) or (bb == EPI_CHUNKS and tt == t)
    nj = D_FF // FFN_TF
    nt = t // tt
    n_tiles = (bsz // bb) * nt
    cur_tile = lambda i: jnp.minimum(i, n_tiles - 1)
    prev_tile = lambda i: jnp.maximum(i - 1, 0)
    xmap = lambda tile: (lambda i, j: (tile(i) // nt, tile(i) % nt, 0))
    amap = lambda tile: (lambda i, j: (tile(i) // nt, 0, 0))
    wstep = lambda i, j: jnp.where(i < n_tiles, j, nj - 1)
    out_shape = [jax.ShapeDtypeStruct(x.shape, F32)]
    out_specs = [pl.BlockSpec((bb, tt, d), xmap(prev_tile))]
    if emit_next:
        out_shape.append(jax.ShapeDtypeStruct(x.shape, BF16))
        out_specs.append(pl.BlockSpec((bb, tt, d), xmap(prev_tile)))
    res = pl.pallas_call(
        functools.partial(_ffn_kernel, sub=sub, emit_next=emit_next, n_tiles=n_tiles),
        grid=(n_tiles + 1, nj),
        in_specs=[pl.BlockSpec((bb, tt, d), xmap(cur_tile)),
                  pl.BlockSpec((bb, tt, d), xmap(prev_tile)),
                  pl.BlockSpec((bb, N_ADA, d), amap(cur_tile)),
                  pl.BlockSpec((bb, N_ADA, d), amap(prev_tile)),
                  pl.BlockSpec((3, d), lambda i, j: (0, 0)),
                  pl.BlockSpec((3, d), lambda i, j: (0, 0)),
                  pl.BlockSpec((None, None, d, FFN_TF), lambda i, j: (layer, which, 0, wstep(i, j))),
                  pl.BlockSpec((None, None, d, FFN_TF), lambda i, j: (layer, which, 0, wstep(i, j) + nj)),
                  pl.BlockSpec((None, None, FFN_TF, d), lambda i, j: (layer, which, wstep(i, j), 0))],
        out_specs=out_specs,
        out_shape=out_shape,
        scratch_shapes=[pltpu.VMEM((bb * tt, d), BF16), pltpu.VMEM((2, bb * tt, d), F32)],
        compiler_params=_cparams(("arbitrary", "arbitrary"), vmem=FFN_VMEM_LIMIT),
        name=f"ffn{sub}",
    )(x, x, ada, ada, ln_g, ln_b, wi, wi, wo)
    return res if emit_next else res[0]


def _mm_kernel(h_ref, w_ref, o_ref):
    o_ref[...] = jnp.dot(h_ref[...], w_ref[...], preferred_element_type=F32)


def _mm_call(h, w, *, tm, tn, col0, n, name="proj"):
    m, k = h.shape
    assert col0 % tn == 0 and n % tn == 0
    cb = col0 // tn
    return pl.pallas_call(
        _mm_kernel,
        grid=(m // tm, n // tn),
        in_specs=[pl.BlockSpec((tm, k), lambda i, j: (i, 0)),
                  pl.BlockSpec((k, tn), lambda i, j: (0, j + cb))],
        out_specs=pl.BlockSpec((tm, tn), lambda i, j: (i, j)),
        out_shape=jax.ShapeDtypeStruct((m, n), F32),
        compiler_params=_cparams(("parallel", "arbitrary")),
        name=name,
    )(h, w)


def _bias_tiles_kernel(tbl_ref, o_ref):
    h = pl.program_id(0)
    r = lax.broadcasted_iota(jnp.int32, (MOBA_BLOCK, MOBA_BLOCK), 0)
    c = lax.broadcasted_iota(jnp.int32, (MOBA_BLOCK, MOBA_BLOCK), 1)
    rel0 = r - c
    o_ref[0, 0] = jnp.where(rel0 >= 0, _t5_bias(rel0, tbl_ref, h), NEG_BIG)
    o_ref[0, 1] = _t5_bias(rel0 + MOBA_BLOCK, tbl_ref, h)


def _bias_tiles_call(rel_bias):
    return pl.pallas_call(
        _bias_tiles_kernel,
        grid=(H_ATT,),
        in_specs=[pl.BlockSpec(memory_space=pltpu.SMEM)],
        out_specs=pl.BlockSpec((1, 2, MOBA_BLOCK, MOBA_BLOCK), lambda h: (h, 0, 0, 0)),
        out_shape=jax.ShapeDtypeStruct((H_ATT, 2, MOBA_BLOCK, MOBA_BLOCK), F32),
        compiler_params=_cparams(("arbitrary",)),
        name="bias_tiles",
    )(rel_bias)


def _rank_select(s, n_valid):
    col = lax.broadcasted_iota(jnp.int32, s.shape, 1)
    cnt = jnp.zeros(s.shape, jnp.int32)
    for m in range(n_valid):
        sm = s[:, m:m + 1]
        beats = (sm > s) | ((sm == s) & (m < col))
        cnt = cnt + jnp.where(beats, 1, 0)
    return (col < n_valid) & (cnt < MOBA_TOPK)


def _attn_prompt_kernel(far_ref, q_ref, k_ref, v_ref, bt_ref, o_ref):
    h = pl.program_id(1)
    s_len = q_ref.shape[1]
    nb = s_len // MOBA_BLOCK
    q = q_ref[0]
    k = k_ref[0]
    kb = k.astype(BF16)
    vb = v_ref[0].astype(BF16)
    qb = (q * (HD_ATT ** -0.5)).astype(BF16)
    means = jnp.mean(k.reshape(nb, MOBA_BLOCK, HD_ATT), axis=1)
    means = jnp.concatenate([means, jnp.zeros((LANES - nb, HD_ATT), F32)], axis=0)
    mparts = _split(means, 3)
    far = far_ref[h]
    rows = [slice(i * MOBA_BLOCK, (i + 1) * MOBA_BLOCK) for i in range(nb)]
    logits = [lax.dot_general(qb[rows[i]], kb[:(i + 1) * MOBA_BLOCK], _NT, preferred_element_type=F32)
              for i in range(nb)]
    scores = [None] + [_dotp(_split(q[rows[i]], 3), mparts, _NT, order=3) for i in range(1, nb)]
    negs = [None] + [jnp.where(_rank_select(scores[i], i), 0.0, NEG_BIG) for i in range(1, nb)]
    probs, dens = [], []
    for i in range(nb):
        tiles = []
        for j in range(i + 1):
            tile = logits[i][:, j * MOBA_BLOCK:(j + 1) * MOBA_BLOCK]
            if j == i:
                tile = tile + bt_ref[0, 0]
            elif j == i - 1:
                tile = tile + bt_ref[0, 1] + negs[i][:, j:j + 1]
            else:
                tile = tile + (far + negs[i][:, j:j + 1])
            tiles.append(tile)
        lg = jnp.concatenate(tiles, axis=-1) if len(tiles) > 1 else tiles[0]
        mx = jnp.max(lg, axis=-1, keepdims=True)
        p = jnp.exp(lg - mx)
        dens.append(jnp.sum(p, axis=-1, keepdims=True))
        probs.append(p.astype(BF16))
    for i in range(nb):
        out = jnp.dot(probs[i], vb[:(i + 1) * MOBA_BLOCK], preferred_element_type=F32) / dens[i]
        o_ref[0, rows[i], :] = out.astype(o_ref.dtype)


def _attn_prompt_call(q, k, v, bias_tiles, far):
    bsz, s_len, _ = q.shape
    qspec = pl.BlockSpec((1, s_len, HD_ATT), lambda b, h: (b, 0, h))
    return pl.pallas_call(
        _attn_prompt_kernel,
        grid=(bsz, H_ATT),
        in_specs=[pl.BlockSpec(memory_space=pltpu.SMEM), qspec, qspec, qspec,
                  pl.BlockSpec((1, 2, MOBA_BLOCK, MOBA_BLOCK), lambda b, h: (h, 0, 0, 0))],
        out_specs=pl.BlockSpec((1, s_len, HD_ATT), lambda b, h: (b, 0, h)),
        out_shape=jax.ShapeDtypeStruct((bsz, s_len, C_ATT), BF16),
        compiler_params=_cparams(("parallel", "arbitrary")),
        name="attn_prompt",
    )(far, q, k, v, bias_tiles)


MEAN_BLOCKS = 8
PAGES_PER_BLOCK = MOBA_BLOCK // PAGE_SIZE


def _cache_means_kernel(pt_ref, *refs):
    page_refs, o_ref = refs[:-1], refs[-1]
    for m in range(MEAN_BLOCKS):
        tot = None
        for u in range(PAGES_PER_BLOCK):
            s = jnp.sum(page_refs[m * PAGES_PER_BLOCK + u][0], axis=0)
            tot = s if tot is None else tot + s
        o_ref[0, m] = tot * (1.0 / MOBA_BLOCK)


def _cache_means_call(page_table, cache_k4, n_blocks):
    dbs = page_table.shape[0]
    npg = MEAN_BLOCKS * PAGES_PER_BLOCK
    in_specs = [pl.BlockSpec((1, PAGE_SIZE, H_ATT, HD_ATT), functools.partial(
        lambda b, g, pt, u: (pt[b, g * npg + u], 0, 0, 0), u=u)) for u in range(npg)]
    return pl.pallas_call(
        _cache_means_kernel,
        grid_spec=pltpu.PrefetchScalarGridSpec(
            num_scalar_prefetch=1,
            grid=(dbs, n_blocks // MEAN_BLOCKS),
            in_specs=in_specs,
            out_specs=pl.BlockSpec((1, MEAN_BLOCKS, H_ATT, HD_ATT), lambda b, g, pt: (b, g, 0, 0)),
        ),
        out_shape=jax.ShapeDtypeStruct((dbs, n_blocks, H_ATT, HD_ATT), F32),
        compiler_params=_cparams(("parallel", "arbitrary")),
        name="cache_means",
    )(page_table, *([cache_k4] * npg))


def _topk_ids_kernel(q_ref, m_ref, o_ref):
    q = q_ref[0]
    means = m_ref[0]
    n_blocks = means.shape[0]
    ds = q.shape[0]
    pad = jnp.zeros((LANES - n_blocks, HD_ATT), F32)
    col = lax.broadcasted_iota(jnp.int32, (ds, LANES), 1)
    for h in range(H_ATT):
        sl = slice(h * HD_ATT, (h + 1) * HD_ATT)
        mh = jnp.concatenate([means[:, sl], pad], axis=0) if n_blocks < LANES else means[:, sl]
        s = _dotp(_split(q[:, sl], 3), _split(mh, 3), _NT, order=3)
        s = jnp.where(col < n_blocks, s, NEG_BIG)
        ids = jnp.zeros((ds, LANES), jnp.int32)
        for t in range(MOBA_TOPK):
            mx = jnp.max(s, axis=-1, keepdims=True)
            idx = jnp.min(jnp.where(s == mx, col, LANES), axis=-1, keepdims=True)
            ids = jnp.where(col == t, idx, ids)
            s = jnp.where(col == idx, NEG_BIG * 2, s)
        o_ref[0, h] = ids


def _topk_ids_call(q, means):
    dbs, ds, _ = q.shape
    n_blocks = means.shape[1]
    return pl.pallas_call(
        _topk_ids_kernel,
        grid=(dbs,),
        in_specs=[pl.BlockSpec((1, ds, C_ATT), lambda b: (b, 0, 0)),
                  pl.BlockSpec((1, n_blocks, C_ATT), lambda b: (b, 0, 0))],
        out_specs=pl.BlockSpec((1, H_ATT, ds, LANES), lambda b: (b, 0, 0, 0)),
        out_shape=jax.ShapeDtypeStruct((dbs, H_ATT, ds, LANES), jnp.int32),
        compiler_params=_cparams(("parallel",)),
        name="topk_ids",
    )(q, means)


N_SEL_PAGES = MOBA_TOPK * PAGES_PER_BLOCK


def _attn_sample_kernel(pt_ref, ids_ref, tbl_ref, q_ref, kn_ref, vn_ref, ck_hbm, cv_hbm, o_ref,
                        kbuf, vbuf, sems, *, past_len):
    ds = q_ref.shape[1]
    n_pg = ds * N_SEL_PAGES
    b, h = pl.program_id(0), pl.program_id(1)
    nh = pl.num_programs(1)
    step = b * nh + h
    n_steps = pl.num_programs(0) * nh
    slot = step % 2

    def page_copies(bb, hh, sl):
        base = (bb * H_ATT + hh) * ds * MOBA_TOPK
        cps = []
        for u in range(n_pg):
            pg = pt_ref[bb, ids_ref[base + u // PAGES_PER_BLOCK] * PAGES_PER_BLOCK + u % PAGES_PER_BLOCK]
            cps.append(pltpu.make_async_copy(ck_hbm.at[pg, :, hh, :], kbuf.at[sl, u], sems.at[sl, u]))
            cps.append(pltpu.make_async_copy(cv_hbm.at[pg, :, hh, :], vbuf.at[sl, u], sems.at[sl, n_pg + u]))
        return cps

    @pl.when(step == 0)
    def _():
        for cp in page_copies(b, h, slot):
            cp.start()

    @pl.when(step + 1 < n_steps)
    def _():
        nxt = step + 1
        for cp in page_copies(nxt // nh, nxt % nh, 1 - slot):
            cp.start()

    for cp in page_copies(b, h, slot):
        cp.wait()
    kp_refs = [kbuf.at[slot, u] for u in range(n_pg)]
    vp_refs = [vbuf.at[slot, u] for u in range(n_pg)]
    qb = (q_ref[0] * (HD_ATT ** -0.5)).astype(BF16)
    qrow = lax.broadcasted_iota(jnp.int32, (ds, 1), 0)
    lane = lax.broadcasted_iota(jnp.int32, (ds, PAGE_SIZE), 1)
    lo = lax.dot_general(qb, kn_ref[0].astype(BF16), _NT, preferred_element_type=F32)
    rel_o = lax.broadcasted_iota(jnp.int32, (ds, ds), 0) - lax.broadcasted_iota(jnp.int32, (ds, ds), 1)
    lo = jnp.where(rel_o >= 0, lo + _t5_bias(rel_o, tbl_ref, h), NEG_BIG)
    mx_o = jnp.max(lo, axis=-1, keepdims=True)
    vn = vn_ref[0].astype(BF16)
    tiles = []
    for qi in range(ds):
        flat = (b * H_ATT + h) * ds + qi
        tq = []
        for s in range(MOBA_TOPK):
            blk = ids_ref[flat * MOBA_TOPK + s]
            for u in range(PAGES_PER_BLOCK):
                kpg = kp_refs[(qi * MOBA_TOPK + s) * PAGES_PER_BLOCK + u][...].astype(BF16)
                lg = lax.dot_general(qb, kpg, _NT, preferred_element_type=F32)
                rel = (past_len + qi) - (blk * MOBA_BLOCK + u * PAGE_SIZE + lane)
                tq.append(lg + _t5_bias(rel, tbl_ref, h))
        tiles.append(tq)
    mxs = []
    for tq in tiles:
        mx = mx_o
        for t in tq:
            mx = jnp.maximum(mx, jnp.max(t, axis=-1, keepdims=True))
        mxs.append(mx)
    pos = [jnp.exp(lo - mx) for mx in mxs]
    ps = [[jnp.exp(t - mx) for t in tq] for tq, mx in zip(tiles, mxs)]
    accs = [jnp.dot(po.astype(BF16), vn, preferred_element_type=F32) for po in pos]
    for qi in range(ds):
        for i in range(N_SEL_PAGES):
            accs[qi] = accs[qi] + jnp.dot(ps[qi][i].astype(BF16), vp_refs[qi * N_SEL_PAGES + i][...].astype(BF16),
                                          preferred_element_type=F32)
    result = jnp.zeros((ds, HD_ATT), F32)
    for qi in range(ds):
        den = jnp.sum(pos[qi], axis=-1, keepdims=True)
        for p in ps[qi]:
            den = den + jnp.sum(p, axis=-1, keepdims=True)
        result = jnp.where(qrow == qi, accs[qi] / den, result)
    o_ref[0] = result


def _attn_sample_call(page_table, ids, rel_bias, q, k_new, v_new, cache_k2, cache_v2, past_len):
    dbs, ds, _ = q.shape
    n_pg = ds * N_SEL_PAGES

    nspec = pl.BlockSpec((1, ds, HD_ATT), lambda b, h, p, i: (b, 0, h))
    pool = pl.BlockSpec(memory_space=pl.ANY)
    return pl.pallas_call(
        functools.partial(_attn_sample_kernel, past_len=past_len),
        grid_spec=pltpu.PrefetchScalarGridSpec(
            num_scalar_prefetch=2,
            grid=(dbs, H_ATT),
            in_specs=[pl.BlockSpec(memory_space=pltpu.SMEM), nspec, nspec, nspec, pool, pool],
            out_specs=pl.BlockSpec((1, ds, HD_ATT), lambda b, h, p, i: (b, 0, h)),
            scratch_shapes=[pltpu.VMEM((2, n_pg, PAGE_SIZE, HD_ATT), F32),
                            pltpu.VMEM((2, n_pg, PAGE_SIZE, HD_ATT), F32),
                            pltpu.SemaphoreType.DMA((2, 2 * n_pg))],
        ),
        out_shape=jax.ShapeDtypeStruct((dbs, ds, C_ATT), F32),
        compiler_params=_cparams(("arbitrary", "arbitrary")),
        name="attn_sample",
    )(page_table, ids, rel_bias, q, k_new, v_new, cache_k2, cache_v2)


def _rwkv_pre_kernel(h_ref, wz_ref, sh0_ref, mu_ref, w0_ref, w2_ref, a0_ref, a2_ref, g2_ref, kk_ref, ka_ref,
                     r_o, k_o, v_o, kk_o, b_o, ld_o, g_o, zl_o, carry):
    i = pl.program_id(1)
    z = jnp.dot(h_ref[0], wz_ref[...], preferred_element_type=F32)
    tm = z.shape[0]

    @pl.when(i == 0)
    def _():
        carry[...] = sh0_ref[0]

    first = carry[...]
    carry[...] = z[tm - 1:tm, :]
    zl_o[0, 0] = z[tm - 8:tm, :]
    row = lax.broadcasted_iota(jnp.int32, (tm, 1), 0)
    zprev = jnp.where(row == 0, first, pltpu.roll(z, 1, 0))
    zs = z + (zprev - z) * mu_ref[...]
    c = C_RWKV
    r, k, v = zs[:, :c], zs[:, c:2 * c], zs[:, 2 * c:3 * c]
    zw = zs[:, 3 * c:3 * c + LORA_PAD]
    za = zs[:, 3 * c + LORA_PAD:3 * c + 2 * LORA_PAD]
    zg = zs[:, 3 * c + 2 * LORA_PAD:]
    wl = w0_ref[...] + jnp.dot(jnp.tanh(zw).astype(BF16), w2_ref[...], preferred_element_type=F32)
    w = -(jnp.maximum(-wl, 0.0) + jnp.log(1.0 + jnp.exp(-jnp.abs(wl)))) - 0.5
    ld_o[0] = -jnp.exp(w)
    a = jax.nn.sigmoid(a0_ref[...] + jnp.dot(za.astype(BF16), a2_ref[...], preferred_element_type=F32))
    g_o[0] = jnp.dot(jax.nn.sigmoid(zg).astype(BF16), g2_ref[...], preferred_element_type=F32)
    kkr = k * kk_ref[...]
    n2 = _segsum64(kkr * kkr, _seg_ones())
    kkn = kkr / jnp.maximum(jnp.sqrt(n2), 1e-12)
    r_o[0] = r
    k_o[0] = k * (1.0 + (a - 1.0) * ka_ref[...])
    v_o[0] = v
    kk_o[0] = kkn
    b_o[0] = kkn * a


def _rwkv_pre_call(h, win, z_col0, shift0, p, *, tm):
    bsz, t, d = h.shape
    zp = Z_PAD
    assert z_col0 % zp == 0
    nt = t // tm
    c = C_RWKV
    row = lambda b, i: (0, 0)
    vec = lambda n: pl.BlockSpec((1, n), row)
    tile = pl.BlockSpec((1, tm, c), lambda b, i: (b, i, 0))
    return pl.pallas_call(
        _rwkv_pre_kernel,
        grid=(bsz, nt),
        in_specs=[pl.BlockSpec((1, tm, d), lambda b, i: (b, i, 0)),
                  pl.BlockSpec((d, zp), lambda b, i: (0, z_col0 // zp)),
                  pl.BlockSpec((1, 1, zp), lambda b, i: (b, 0, 0)),
                  vec(zp), vec(c),
                  pl.BlockSpec((LORA_PAD, c), row), vec(c),
                  pl.BlockSpec((LORA_PAD, c), row),
                  pl.BlockSpec((D_GATE_LORA, c), row), vec(c), vec(c)],
        out_specs=[tile] * 7 + [pl.BlockSpec((1, 1, 8, zp), lambda b, i: (b, i, 0, 0))],
        out_shape=[jax.ShapeDtypeStruct((bsz, t, c), F32)] * 7 + [jax.ShapeDtypeStruct((bsz, nt, 8, zp), F32)],
        scratch_shapes=[pltpu.VMEM((1, zp), F32)],
        compiler_params=_cparams(("parallel", "arbitrary")),
        name="rwkv_pre",
    )(h, win, shift0.reshape(bsz, 1, zp), p["mu"], p["w0"], p["w2"], p["a0"], p["a2"], p["g2"], p["k_k"], p["k_a"])


CHUNK_PREC = {"gram": (1, 1), "state_read": (1, 1), "mkv": (1, 1), "solve1": (2, 2), "solve_sq": (2, 2),
              "solve_ap": (2, 2), "out": (1, 1), "state_upd": (1, 1)}


def _chunk_pairs(rs, ks, vs, kks, bs, lds, ss):
    c = CHUNK
    lane = lax.broadcasted_iota(jnp.int32, (1, LANES), 1)
    m_a = jnp.where(lane < HS_RWKV, 1.0, 0.0)
    m_b = 1.0 - m_a
    row = lax.broadcasted_iota(jnp.int32, (c, 2 * c), 0)
    coli = lax.broadcasted_iota(jnp.int32, (c, 2 * c), 1) % c
    strict = coli < row
    incl = coli <= row
    lr = lax.broadcasted_iota(jnp.int32, (c, c), 0)
    lc = lax.broadcasted_iota(jnp.int32, (c, c), 1)
    ltri = jnp.where(lc <= lr, 1.0, 0.0).astype(BF16)
    ones = jnp.ones((c, LANES), BF16)
    rr = lax.broadcasted_iota(jnp.int32, (LANES, LANES), 0) // HS_RWKV
    cc = lax.broadcasted_iota(jnp.int32, (LANES, LANES), 1) // HS_RWKV
    same_head = rr == cc

    def each(f, *lists):
        return [f(*args) for args in zip(*lists)]

    def stack2(x):
        return jnp.concatenate([x * m_a, x * m_b], axis=0)

    def prod(site, a, b, dims=_NN):
        na, nb = CHUNK_PREC[site]
        return _dotp(_split(a, na), _split(b, nb), dims, order=max(na, nb))

    def pm(site, mcats, xs):
        return each(lambda m, x: prod(site, m, stack2(x)), mcats, xs)

    ldp = each(lambda x: _split(x, 3), lds)
    cums = each(lambda p: _dotp([ltri], p, _NN), ldp)
    gcols = each(lambda p: jnp.exp(_dotp(p, [ones], _TN)), ldp)
    g_inv = each(lambda cu: jnp.exp(-cu), cums)
    g_end = each(lambda cu: jnp.exp(cu[c - 1:c, :] - cu), cums)
    p_all = each(lambda kk, r, cu, ld: jnp.concatenate([kk * jnp.exp(cu - ld), r * jnp.exp(cu)], axis=0),
                 kks, rs, cums, lds)
    z2 = each(lambda k, b, gi: jnp.concatenate([stack2(k * gi), stack2(b * gi)], axis=0), ks, bs, g_inv)
    g4 = each(lambda p, z: prod("gram", p, z, _NT), p_all, z2)
    mk = each(lambda g: jnp.where(strict, g[:c, :2 * c], 0.0), g4)
    pj = each(lambda g: jnp.where(strict, -g[:c, 2 * c:], 0.0), g4)
    akb = each(lambda g: jnp.concatenate([jnp.where(incl, g[c:, :2 * c], 0.0),
                                          jnp.where(incl, -g[c:, 2 * c:], 0.0)], axis=1), g4)
    ps = each(lambda p, s: prod("state_read", p, s), p_all, ss)
    mkv = pm("mkv", mk, vs)
    rhs = each(lambda p, m: p[:c] + m, ps, mkv)
    us = each(lambda x, d: x + d, rhs, pm("solve1", pj, rhs))
    n = 2
    while n < c:
        pj = pm("solve_sq", pj, pj)
        us = each(lambda x, d: x + d, us, pm("solve_ap", pj, us))
        n *= 2
    ys = each(lambda p, m, v, u: p[c:] + prod("out", m, jnp.concatenate([stack2(v), stack2(u)], axis=0)),
              ps, akb, vs, us)
    upd = each(lambda k, b, ge, v, u: prod("state_upd", jnp.concatenate([k * ge, -(b * ge)], axis=0),
                                           jnp.concatenate([v, u], axis=0), _TN), ks, bs, g_end, vs, us)
    s_new = each(lambda gc, s, up: gc * s + jnp.where(same_head, up, 0.0), gcols, ss, upd)
    return ys, s_new


def _rwkv_out_pairs(ys, rs, ks, vs, gs, rks, lgs, lbs):
    p128 = _seg_ones()
    c = CHUNK
    inv = 1.0 / HS_RWKV

    def seg3(x):
        return _split(x, 3)

    def sum3(t, o):
        return t[o * c:(o + 1) * c] + t[(o + 1) * c:(o + 2) * c] + t[(o + 2) * c:(o + 3) * c]

    t1 = [jnp.dot(jnp.concatenate(seg3(y) + seg3(r * k * rk), axis=0), p128, preferred_element_type=F32)
          for y, r, k, rk in zip(ys, rs, ks, rks)]
    ycs = [y - sum3(t, 0) * inv for y, t in zip(ys, t1)]
    t2 = [jnp.dot(jnp.concatenate(seg3(yc * yc), axis=0), p128, preferred_element_type=F32) for yc in ycs]
    outs = []
    for yc, ta, tb, v, g, lg, lb in zip(ycs, t1, t2, vs, gs, lgs, lbs):
        yn = yc * lax.rsqrt(sum3(tb, 0) * inv + GN_EPS) * lg + lb
        outs.append(((yn + sum3(ta, 3) * v) * g).astype(BF16))
    return outs


def _rwkv_chunk_kernel(r_ref, k_ref, v_ref, kk_ref, b_ref, ld_ref, g_ref, rk_ref, lg_ref, lb_ref, s0_ref,
                       o_ref, so_ref, s_scr):
    ci = pl.program_id(2)

    @pl.when(ci == 0)
    def _():
        s_scr[...] = s0_ref[0]

    sls = [slice(p * LANES, (p + 1) * LANES) for p in range(PAIRS_PER_STEP)]
    pairs = lambda ref: [ref[0, :, sl] for sl in sls]
    vecs = lambda ref: [ref[:, sl] for sl in sls]
    rs, ks, vs = pairs(r_ref), pairs(k_ref), pairs(v_ref)
    ys, s_new = _chunk_pairs(rs, ks, vs, pairs(kk_ref), pairs(b_ref), pairs(ld_ref),
                             [s_scr[p] for p in range(PAIRS_PER_STEP)])
    outs = _rwkv_out_pairs(ys, rs, ks, vs, pairs(g_ref), vecs(rk_ref), vecs(lg_ref), vecs(lb_ref))
    for p, sl in enumerate(sls):
        o_ref[0, :, sl] = outs[p]
        s_scr[p] = s_new[p]
        so_ref[0, p] = s_new[p]


def _rwkv_chunk_call(r, k, v, kk, b, logd, g, s0_blk, p):
    bsz, t, c = r.shape
    pp = PAIRS_PER_STEP
    w = pp * LANES
    tile = pl.BlockSpec((1, CHUNK, w), lambda bi, pi, ci: (bi, ci, pi))
    vec = pl.BlockSpec((1, w), lambda bi, pi, ci: (0, pi))
    sspec = pl.BlockSpec((1, pp, LANES, LANES), lambda bi, pi, ci: (bi, pi, 0, 0))
    return pl.pallas_call(
        _rwkv_chunk_kernel,
        grid=(bsz, c // w, t // CHUNK),
        in_specs=[tile] * 7 + [vec] * 3 + [sspec],
        out_specs=[tile, sspec],
        out_shape=[jax.ShapeDtypeStruct((bsz, t, c), BF16),
                   jax.ShapeDtypeStruct(s0_blk.shape, F32)],
        scratch_shapes=[pltpu.VMEM((pp, LANES, LANES), F32)],
        compiler_params=_cparams(("parallel", "parallel", "arbitrary")),
        name="rwkv_chunk",
    )(r, k, v, kk, b, logd, g, p["r_k"], p["lnx_g"], p["lnx_b"], s0_blk)


def _combine_kernel(x_ref, ada_ref, lng_ref, lnb_ref, oa_ref, or_ref, ga_ref, gr_ref, wa_ref, wr_ref, wo_ref, o_ref):
    bb, t, d = x_ref.shape
    ua = jnp.dot(oa_ref[...], wa_ref[...], preferred_element_type=F32)
    ur = jnp.dot(or_ref[...], wr_ref[...], preferred_element_type=F32)
    m = jax.nn.sigmoid(ga_ref[...]) * ua + jax.nn.sigmoid(gr_ref[...]) * ur
    mo = jnp.dot(m.astype(BF16), wo_ref[...], preferred_element_type=F32)
    y = DEEPNORM_ALPHA * x_ref[...] + ada_ref[:, 5:6, :] * mo.reshape(bb, t, d)
    o_ref[...] = _ln(y) * lng_ref[1:2, :] + lnb_ref[1:2, :]


def _combine_call(x, ada, ln_g, ln_b, oa, orw, ga, gr, wa, wr, wo, *, blk):
    bsz, t, d = x.shape
    bb, tt = blk
    nt = t // tt
    rows = bb * tt
    xmap = lambda i: (i // nt, i % nt, 0)
    const = lambda i: (0, 0)
    rowt = lambda n: pl.BlockSpec((rows, n), lambda i: (i, 0))
    return pl.pallas_call(
        _combine_kernel,
        grid=((bsz // bb) * nt,),
        in_specs=[pl.BlockSpec((bb, tt, d), xmap),
                  pl.BlockSpec((bb, N_ADA, d), lambda i: (i // nt, 0, 0)),
                  pl.BlockSpec((3, d), const), pl.BlockSpec((3, d), const),
                  rowt(C_ATT), rowt(C_RWKV), rowt(d), rowt(d),
                  pl.BlockSpec((C_ATT, d), const), pl.BlockSpec((C_RWKV, d), const), pl.BlockSpec((d, d), const)],
        out_specs=pl.BlockSpec((bb, tt, d), xmap),
        out_shape=jax.ShapeDtypeStruct(x.shape, F32),
        compiler_params=_cparams(("parallel",)),
        name="combine",
    )(x, ada, ln_g, ln_b, oa, orw, ga, gr, wa, wr, wo)


def _rearrange_z(a):
    c3 = 3 * C_RWKV
    pad = [(0, 0)] * (a.ndim - 1) + [(0, LORA_PAD - D_DECAY_LORA)]
    return jnp.concatenate([a[..., :c3],
                            jnp.pad(a[..., c3:c3 + D_DECAY_LORA], pad),
                            jnp.pad(a[..., c3 + D_DECAY_LORA:c3 + D_DECAY_LORA + D_AAA_LORA], pad),
                            a[..., c3 + D_DECAY_LORA + D_AAA_LORA:]], axis=-1)


def _unarrange_z(a):
    c3 = 3 * C_RWKV
    return jnp.concatenate([a[..., :c3], a[..., c3:c3 + D_DECAY_LORA],
                            a[..., c3 + LORA_PAD:c3 + LORA_PAD + D_AAA_LORA], a[..., c3 + 2 * LORA_PAD:]], axis=-1)


def _state_to_blocks(state):
    bsz = state.shape[0]
    st = jnp.swapaxes(state, -1, -2).reshape(bsz, H_RWKV // 2, 2, HS_RWKV, HS_RWKV)
    blk = jnp.einsum('bphkv,hg->bphkgv', st, jnp.eye(2, dtype=state.dtype))
    return blk.reshape(bsz, H_RWKV // 2, LANES, LANES)


def _blocks_to_state(blk):
    bsz = blk.shape[0]
    b6 = blk.reshape(bsz, H_RWKV // 2, 2, HS_RWKV, 2, HS_RWKV)
    st = jnp.stack([b6[:, :, 0, :, 0, :], b6[:, :, 1, :, 1, :]], axis=2)
    return jnp.swapaxes(st, -1, -2).reshape(bsz, H_RWKV, HS_RWKV, HS_RWKV)


def _trunk(x, ada, attend, shift0, wkv0, w, *, ffn_blk, mm_tm, pre_tm, comb_blk):
    bsz, t, d = x.shape
    m = bsz * t
    ffn = functools.partial(_ffn_call, ada=ada, ln_g=w["ln_g"], ln_b=w["ln_b"], wi=w["ffn_wi"], wo=w["ffn_wo"],
                            layer=w["layer"], blk=ffn_blk)
    x1, h2 = ffn(x, which=0, sub=0, emit_next=True)
    h2f = h2.reshape(m, d)
    win = w["win"]
    proj = functools.partial(_mm_call, h2f, win, tm=mm_tm)
    q = proj(tn=C_ATT, col0=0, n=C_ATT, name="proj_q").reshape(bsz, t, C_ATT)
    k = proj(tn=C_ATT, col0=C_ATT, n=C_ATT, name="proj_k").reshape(bsz, t, C_ATT)
    v = proj(tn=C_ATT, col0=2 * C_ATT, n=C_ATT, name="proj_v").reshape(bsz, t, C_ATT)
    ga = proj(tn=d // 2, col0=3 * C_ATT, n=d, name="proj_ga")
    gr = proj(tn=d // 2, col0=3 * C_ATT + d, n=d, name="proj_gr")
    o_att = attend(q, k, v)
    r, k2, vv, kk, b, logd, g, z_last = _rwkv_pre_call(h2, win, 3 * C_ATT + 2 * d, _rearrange_z(shift0), w, tm=pre_tm)
    tp = -(-t // CHUNK) * CHUNK
    seq = [r, k2, vv, kk, b, logd, g]
    if tp != t:
        seq = [jnp.pad(a, ((0, 0), (0, tp - t), (0, 0))) for a in seq]
    o_rwkv, s_blk = _rwkv_chunk_call(*seq, _state_to_blocks(wkv0), w)
    o_rwkv = o_rwkv[:, :t].reshape(m, C_RWKV)
    x2 = _combine_call(x1, ada, w["ln_g"], w["ln_b"], o_att.reshape(m, C_ATT), o_rwkv, ga, gr,
                       w["wua"], w["wur"], w["wout"], blk=comb_blk)
    x3 = ffn(x2, which=1, sub=2, emit_next=False)
    shift_new = _unarrange_z(z_last[:, -1, 7])
    return (x3, k.reshape(bsz, t, H_ATT, HD_ATT), v.reshape(bsz, t, H_ATT, HD_ATT), _blocks_to_state(s_blk), shift_new)


def kernel(x_prompt, x_sample, cache_k, cache_v, state_wkv, state_shift, page_table, c_prompt, c_sample, rel_bias, w_ada, b_ada, ln_g, ln_b, ffn_wi, ffn_wo, w_in, mu_shift, w0, w2, a0, a2, g2, k_k, k_a, r_k, lnx_g, lnx_b, w_up_attn, w_up_rwkv, w_out):
    assert w_ada.shape[0] == DEPTH == 1
    bsz, s_len, d = x_prompt.shape
    dbs, ds, _ = x_sample.shape
    past_len = page_table.shape[1] * PAGE_SIZE
    n_phys = cache_k.shape[1]
    l = 0
    win = w_in[l]
    c3 = 3 * C_ATT
    lora_rows = ((0, LORA_PAD - D_DECAY_LORA), (0, 0))
    w = {
        "ln_g": ln_g[l], "ln_b": ln_b[l],
        "layer": l, "ffn_wi": ffn_wi.astype(BF16), "ffn_wo": ffn_wo.astype(BF16),
        "win": jnp.concatenate([win[:, :c3], win[:, c3 + RWKV_PROJ:],
                                _rearrange_z(win[:, c3:c3 + RWKV_PROJ])], axis=1).astype(BF16),
        "mu": _rearrange_z(mu_shift[l])[None, :],
        "w0": w0[l][None, :], "w2": jnp.pad(w2[l], lora_rows).astype(BF16),
        "a0": a0[l][None, :], "a2": jnp.pad(a2[l], lora_rows).astype(BF16),
        "g2": g2[l].astype(BF16), "k_k": k_k[l][None, :], "k_a": k_a[l][None, :],
        "r_k": r_k[l].reshape(1, C_RWKV), "lnx_g": lnx_g[l][None, :], "lnx_b": lnx_b[l][None, :],
        "wua": w_up_attn[l].astype(BF16), "wur": w_up_rwkv[l].astype(BF16), "wout": w_out[l].astype(BF16),
    }
    n_c = bsz + dbs
    c_rows = -(-n_c // 8) * 8
    c_all = jnp.concatenate([c_prompt, c_sample, jnp.zeros((c_rows - n_c, d), F32)], axis=0)
    ada = _ada_call(c_all, w_ada[l], b_ada[l]).reshape(c_rows, N_ADA, d)

    bias_tiles = _bias_tiles_call(rel_bias)
    far = rel_bias[NUM_BUCKETS - 1]
    attend_p = lambda q, k, v: _attn_prompt_call(q, k, v, bias_tiles, far)
    yp, kp, vp, wp, sp = _trunk(
        x_prompt, ada[:bsz], attend_p, jnp.zeros((bsz, RWKV_PROJ), F32),
        jnp.zeros((bsz, H_RWKV, HS_RWKV, HS_RWKV), F32), w,
        ffn_blk=(1, 512), mm_tm=1024, pre_tm=256, comb_blk=(1, 256))

    cache_k2 = cache_k.reshape(DEPTH * n_phys, PAGE_SIZE, H_ATT, HD_ATT)
    cache_v2 = cache_v.reshape(DEPTH * n_phys, PAGE_SIZE, H_ATT, HD_ATT)
    page_table = page_table + l * n_phys
    n_full = past_len // MOBA_BLOCK

    def attend_s(q, k, v):
        means = _cache_means_call(page_table, cache_k2, n_full).reshape(dbs, n_full, C_ATT)
        ids = _topk_ids_call(q, means)[..., :MOBA_TOPK]
        o = _attn_sample_call(page_table, ids.reshape(-1), rel_bias, q, k, v, cache_k2, cache_v2, past_len)
        return o.astype(BF16)

    ys, kn, vn, wn, sn = _trunk(
        x_sample, ada[bsz:n_c], attend_s, state_shift[l], state_wkv[l], w,
        ffn_blk=(dbs, ds), mm_tm=dbs * ds, pre_tm=ds, comb_blk=(dbs, ds))

    return (yp, ys, kp[None], vp[None], kn[None], vn[None], wp[None], wn[None], sp[None], sn[None])
```

```python
import functools
import math

import jax
import jax.numpy as jnp
from jax import lax
from jax.experimental import pallas as pl
from jax.experimental.pallas import tpu as pltpu

F32 = jnp.float32
BF16 = jnp.bfloat16

D_MODEL = 2048
D_FF = 5632
N_ADA = 9
H_ATT = 8
HD_ATT = 128
C_ATT = H_ATT * HD_ATT
MOBA_BLOCK = 256
MOBA_TOPK = 3
NUM_BUCKETS = 32
MAX_DISTANCE = 128
PAGE_SIZE = 128
HS_RWKV = 64
C_RWKV = 1024
H_RWKV = C_RWKV // HS_RWKV
D_DECAY_LORA = 96
D_AAA_LORA = 96
D_GATE_LORA = 256
RWKV_PROJ = 3 * C_RWKV + D_DECAY_LORA + D_AAA_LORA + D_GATE_LORA
LORA_PAD = 128
Z_PAD = 3 * C_RWKV + 2 * LORA_PAD + D_GATE_LORA
GN_EPS = 64e-5
LN_EPS = 1e-5
DEPTH = 1
DEEPNORM_ALPHA = (2 * DEPTH) ** 0.25
NEG_BIG = -1e30

LANES = 128
CHUNK = 64
PAIRS_PER_STEP = 8
VMEM_LIMIT = 56 * 1024 * 1024


def _t5_thresholds():
    max_exact = NUM_BUCKETS // 2
    thr = list(range(1, max_exact + 1))
    for b in range(max_exact + 1, NUM_BUCKETS):
        x = max_exact * (MAX_DISTANCE / max_exact) ** ((b - max_exact) / (NUM_BUCKETS - max_exact))
        thr.append(int(math.ceil(x)))
    return tuple(thr)


T5_THR = _t5_thresholds()


def _cparams(sem, vmem=VMEM_LIMIT):
    return pltpu.CompilerParams(dimension_semantics=sem, vmem_limit_bytes=vmem)


def _ln(x):
    mu = jnp.mean(x, axis=-1, keepdims=True)
    xc = x - mu
    var = jnp.mean(xc * xc, axis=-1, keepdims=True)
    return xc * lax.rsqrt(var + LN_EPS)


def _split(x, n):
    parts = []
    for i in range(n):
        p = x.astype(BF16)
        parts.append(p)
        if i + 1 < n:
            x = x - p.astype(F32)
    return parts


_NN = (((1,), (0,)), ((), ()))
_NT = (((1,), (1,)), ((), ()))
_TN = (((0,), (0,)), ((), ()))


def _dotp(ap, bp, dims=_NN, order=None):
    if order is None:
        order = max(len(ap), len(bp))
    acc = None
    if dims == _NN and len(ap) > 1:
        m = ap[0].shape[0]
        for j, b in enumerate(bp):
            sel = [a for i, a in enumerate(ap) if i + j < order]
            if not sel:
                continue
            lhs = sel[0] if len(sel) == 1 else jnp.concatenate(sel, axis=0)
            t = lax.dot_general(lhs, b, dims, preferred_element_type=F32)
            for r in range(len(sel)):
                part = t[r * m:(r + 1) * m]
                acc = part if acc is None else acc + part
        return acc
    for i, a in enumerate(ap):
        for j, b in enumerate(bp):
            if i + j < order:
                t = lax.dot_general(a, b, dims, preferred_element_type=F32)
                acc = t if acc is None else acc + t
    return acc


def _t5_bias(rel, tbl_ref, h):
    bias = jnp.full(rel.shape, tbl_ref[0, h], F32)
    for b in range(1, NUM_BUCKETS):
        bias = jnp.where(rel >= T5_THR[b - 1], tbl_ref[b, h], bias)
    return bias


def _seg_ones():
    r = lax.broadcasted_iota(jnp.int32, (LANES, LANES), 0) // HS_RWKV
    c = lax.broadcasted_iota(jnp.int32, (LANES, LANES), 1) // HS_RWKV
    return jnp.where(r == c, 1.0, 0.0).astype(BF16)


def _segsum64(x, p128):
    outs = []
    for j in range(x.shape[-1] // LANES):
        xs = x[:, j * LANES:(j + 1) * LANES]
        outs.append(_dotp(_split(xs, 3), [p128]))
    return jnp.concatenate(outs, axis=-1)


ADA_TN = 1024


def _ada_kernel(c_ref, w_ref, b_ref, o_ref):
    c = c_ref[...]
    sc = c * jax.nn.sigmoid(c)
    rows = sc.shape[0]
    s0, s1, s2 = _split(sc, 3)
    w_hi, w_lo = _split(w_ref[...], 2)
    a = jnp.dot(jnp.concatenate([s0, s1, s2], axis=0), w_hi, preferred_element_type=F32)
    b = jnp.dot(jnp.concatenate([s0, s1], axis=0), w_lo, preferred_element_type=F32)
    o_ref[...] = (a[:rows] + a[rows:2 * rows] + a[2 * rows:] + b[:rows] + b[rows:]) + b_ref[...]


def _ada_call(c_all, w_ada, b_ada):
    rows, d = c_all.shape
    n = w_ada.shape[1]
    return pl.pallas_call(
        _ada_kernel,
        grid=(n // ADA_TN,),
        in_specs=[pl.BlockSpec((rows, d), lambda j: (0, 0)),
                  pl.BlockSpec((d, ADA_TN), lambda j: (0, j)),
                  pl.BlockSpec((1, ADA_TN), lambda j: (0, j))],
        out_specs=pl.BlockSpec((rows, ADA_TN), lambda j: (0, j)),
        out_shape=jax.ShapeDtypeStruct((rows, n), F32),
        compiler_params=_cparams(("parallel",)),
        name="ada",
    )(c_all, w_ada, b_ada.reshape(1, n))


FFN_TF = 512


def _ffn_kernel(x_ref, ada_ref, lng_ref, lnb_ref, wig_ref, wiu_ref, wo_ref, *rest, sub, emit_next):
    if emit_next:
        o_ref, h_next_ref, h_scr, acc = rest
    else:
        o_ref, h_scr, acc = rest
    bb, t, d = x_ref.shape
    j = pl.program_id(1)

    @pl.when(j == 0)
    def _():
        h = _ln(x_ref[...]) * (1.0 + ada_ref[:, 3 * sub + 1:3 * sub + 2, :]) + ada_ref[:, 3 * sub:3 * sub + 1, :]
        h_scr[...] = h.reshape(bb * t, d).astype(BF16)
        acc[...] = jnp.zeros_like(acc)

    h = h_scr[...]
    g = jnp.dot(h, wig_ref[...], preferred_element_type=F32)
    u = jnp.dot(h, wiu_ref[...], preferred_element_type=F32)
    act = (g * jax.nn.sigmoid(g)) * u
    acc[...] += jnp.dot(act.astype(BF16), wo_ref[...], preferred_element_type=F32)

    @pl.when(j == pl.num_programs(1) - 1)
    def _():
        gate = ada_ref[:, 3 * sub + 2:3 * sub + 3, :]
        y = DEEPNORM_ALPHA * x_ref[...] + 0.5 * gate * acc[...].reshape(bb, t, d)
        y = _ln(y) * lng_ref[sub:sub + 1, :] + lnb_ref[sub:sub + 1, :]
        o_ref[...] = y
        if emit_next:
            nxt = sub + 1
            hn = _ln(y) * (1.0 + ada_ref[:, 3 * nxt + 1:3 * nxt + 2, :]) + ada_ref[:, 3 * nxt:3 * nxt + 1, :]
            h_next_ref[...] = hn.astype(BF16)


def _ffn_call(x, ada, ln_g, ln_b, wi, wo, *, layer, which, sub, emit_next, blk):
    bsz, t, d = x.shape
    bb, tt = blk
    nj = D_FF // FFN_TF
    nt = t // tt
    grid = ((bsz // bb) * nt, nj)
    xmap = lambda i, j: (i // nt, i % nt, 0)
    amap = lambda i, j: (i // nt, 0, 0)
    out_shape = [jax.ShapeDtypeStruct(x.shape, F32)]
    out_specs = [pl.BlockSpec((bb, tt, d), xmap)]
    if emit_next:
        out_shape.append(jax.ShapeDtypeStruct(x.shape, BF16))
        out_specs.append(pl.BlockSpec((bb, tt, d), xmap))
    res = pl.pallas_call(
        functools.partial(_ffn_kernel, sub=sub, emit_next=emit_next),
        grid=grid,
        in_specs=[pl.BlockSpec((bb, tt, d), xmap),
                  pl.BlockSpec((bb, N_ADA, d), amap),
                  pl.BlockSpec((3, d), lambda i, j: (0, 0)),
                  pl.BlockSpec((3, d), lambda i, j: (0, 0)),
                  pl.BlockSpec((None, None, d, FFN_TF), lambda i, j: (layer, which, 0, j)),
                  pl.BlockSpec((None, None, d, FFN_TF), lambda i, j: (layer, which, 0, j + nj)),
                  pl.BlockSpec((None, None, FFN_TF, d), lambda i, j: (layer, which, j, 0))],
        out_specs=out_specs,
        out_shape=out_shape,
        scratch_shapes=[pltpu.VMEM((bb * tt, d), BF16), pltpu.VMEM((bb * tt, d), F32)],
        compiler_params=_cparams(("parallel", "arbitrary")),
        name=f"ffn{sub}",
    )(x, ada, ln_g, ln_b, wi, wi, wo)
    return res if emit_next else res[0]


def _mm_kernel(h_ref, w_ref, o_ref):
    o_ref[...] = jnp.dot(h_ref[...], w_ref[...], preferred_element_type=F32)


def _mm_call(h, w, *, tm, tn, col0, n, name="proj"):
    m, k = h.shape
    assert col0 % tn == 0 and n % tn == 0
    cb = col0 // tn
    return pl.pallas_call(
        _mm_kernel,
        grid=(m // tm, n // tn),
        in_specs=[pl.BlockSpec((tm, k), lambda i, j: (i, 0)),
                  pl.BlockSpec((k, tn), lambda i, j: (0, j + cb))],
        out_specs=pl.BlockSpec((tm, tn), lambda i, j: (i, j)),
        out_shape=jax.ShapeDtypeStruct((m, n), F32),
        compiler_params=_cparams(("parallel", "arbitrary")),
        name=name,
    )(h, w)


def _bias_tiles_kernel(tbl_ref, o_ref):
    h = pl.program_id(0)
    r = lax.broadcasted_iota(jnp.int32, (MOBA_BLOCK, MOBA_BLOCK), 0)
    c = lax.broadcasted_iota(jnp.int32, (MOBA_BLOCK, MOBA_BLOCK), 1)
    rel0 = r - c
    o_ref[0, 0] = jnp.where(rel0 >= 0, _t5_bias(rel0, tbl_ref, h), NEG_BIG)
    o_ref[0, 1] = _t5_bias(rel0 + MOBA_BLOCK, tbl_ref, h)


def _bias_tiles_call(rel_bias):
    return pl.pallas_call(
        _bias_tiles_kernel,
        grid=(H_ATT,),
        in_specs=[pl.BlockSpec(memory_space=pltpu.SMEM)],
        out_specs=pl.BlockSpec((1, 2, MOBA_BLOCK, MOBA_BLOCK), lambda h: (h, 0, 0, 0)),
        out_shape=jax.ShapeDtypeStruct((H_ATT, 2, MOBA_BLOCK, MOBA_BLOCK), F32),
        compiler_params=_cparams(("arbitrary",)),
        name="bias_tiles",
    )(rel_bias)


def _rank_select(s, n_valid):
    col = lax.broadcasted_iota(jnp.int32, s.shape, 1)
    cnt = jnp.zeros(s.shape, jnp.int32)
    for m in range(n_valid):
        sm = s[:, m:m + 1]
        beats = (sm > s) | ((sm == s) & (m < col))
        cnt = cnt + jnp.where(beats, 1, 0)
    return (col < n_valid) & (cnt < MOBA_TOPK)


def _attn_prompt_kernel(far_ref, q_ref, k_ref, v_ref, bt_ref, o_ref):
    h = pl.program_id(1)
    s_len = q_ref.shape[1]
    nb = s_len // MOBA_BLOCK
    q = q_ref[0]
    k = k_ref[0]
    kb = k.astype(BF16)
    vb = v_ref[0].astype(BF16)
    qb = (q * (HD_ATT ** -0.5)).astype(BF16)
    means = jnp.mean(k.reshape(nb, MOBA_BLOCK, HD_ATT), axis=1)
    means = jnp.concatenate([means, jnp.zeros((LANES - nb, HD_ATT), F32)], axis=0)
    mparts = _split(means, 3)
    far = far_ref[h]
    rows = [slice(i * MOBA_BLOCK, (i + 1) * MOBA_BLOCK) for i in range(nb)]
    logits = [lax.dot_general(qb[rows[i]], kb[:(i + 1) * MOBA_BLOCK], _NT, preferred_element_type=F32)
              for i in range(nb)]
    scores = [None] + [_dotp(_split(q[rows[i]], 3), mparts, _NT, order=3) for i in range(1, nb)]
    negs = [None] + [jnp.where(_rank_select(scores[i], i), 0.0, NEG_BIG) for i in range(1, nb)]
    probs, dens = [], []
    for i in range(nb):
        tiles = []
        for j in range(i + 1):
            tile = logits[i][:, j * MOBA_BLOCK:(j + 1) * MOBA_BLOCK]
            if j == i:
                tile = tile + bt_ref[0, 0]
            elif j == i - 1:
                tile = tile + bt_ref[0, 1] + negs[i][:, j:j + 1]
            else:
                tile = tile + (far + negs[i][:, j:j + 1])
            tiles.append(tile)
        lg = jnp.concatenate(tiles, axis=-1) if len(tiles) > 1 else tiles[0]
        mx = jnp.max(lg, axis=-1, keepdims=True)
        p = jnp.exp(lg - mx)
        dens.append(jnp.sum(p, axis=-1, keepdims=True))
        probs.append(p.astype(BF16))
    for i in range(nb):
        out = jnp.dot(probs[i], vb[:(i + 1) * MOBA_BLOCK], preferred_element_type=F32) / dens[i]
        o_ref[0, rows[i], :] = out.astype(o_ref.dtype)


def _attn_prompt_call(q, k, v, bias_tiles, far):
    bsz, s_len, _ = q.shape
    qspec = pl.BlockSpec((1, s_len, HD_ATT), lambda b, h: (b, 0, h))
    return pl.pallas_call(
        _attn_prompt_kernel,
        grid=(bsz, H_ATT),
        in_specs=[pl.BlockSpec(memory_space=pltpu.SMEM), qspec, qspec, qspec,
                  pl.BlockSpec((1, 2, MOBA_BLOCK, MOBA_BLOCK), lambda b, h: (h, 0, 0, 0))],
        out_specs=pl.BlockSpec((1, s_len, HD_ATT), lambda b, h: (b, 0, h)),
        out_shape=jax.ShapeDtypeStruct((bsz, s_len, C_ATT), BF16),
        compiler_params=_cparams(("parallel", "arbitrary")),
        name="attn_prompt",
    )(far, q, k, v, bias_tiles)


MEAN_BLOCKS = 16
PAGES_PER_BLOCK = MOBA_BLOCK // PAGE_SIZE


def _cache_means_kernel(pt_ref, *refs):
    page_refs, o_ref = refs[:-1], refs[-1]
    for m in range(MEAN_BLOCKS):
        tot = None
        for u in range(PAGES_PER_BLOCK):
            s = jnp.sum(page_refs[m * PAGES_PER_BLOCK + u][0], axis=0)
            tot = s if tot is None else tot + s
        o_ref[0, m] = tot * (1.0 / MOBA_BLOCK)


def _cache_means_call(page_table, cache_k4, n_blocks):
    dbs = page_table.shape[0]
    npg = MEAN_BLOCKS * PAGES_PER_BLOCK
    in_specs = [pl.BlockSpec((1, PAGE_SIZE, H_ATT, HD_ATT), functools.partial(
        lambda b, g, pt, u: (pt[b, g * npg + u], 0, 0, 0), u=u)) for u in range(npg)]
    return pl.pallas_call(
        _cache_means_kernel,
        grid_spec=pltpu.PrefetchScalarGridSpec(
            num_scalar_prefetch=1,
            grid=(dbs, n_blocks // MEAN_BLOCKS),
            in_specs=in_specs,
            out_specs=pl.BlockSpec((1, MEAN_BLOCKS, H_ATT, HD_ATT), lambda b, g, pt: (b, g, 0, 0)),
        ),
        out_shape=jax.ShapeDtypeStruct((dbs, n_blocks, H_ATT, HD_ATT), F32),
        compiler_params=_cparams(("parallel", "arbitrary")),
        name="cache_means",
    )(page_table, *([cache_k4] * npg))


def _topk_ids_kernel(q_ref, m_ref, o_ref):
    q = q_ref[0]
    means = m_ref[0]
    n_blocks = means.shape[0]
    ds = q.shape[0]
    pad = jnp.zeros((LANES - n_blocks, HD_ATT), F32)
    col = lax.broadcasted_iota(jnp.int32, (ds, LANES), 1)
    for h in range(H_ATT):
        sl = slice(h * HD_ATT, (h + 1) * HD_ATT)
        mh = jnp.concatenate([means[:, sl], pad], axis=0) if n_blocks < LANES else means[:, sl]
        s = _dotp(_split(q[:, sl], 3), _split(mh, 3), _NT, order=3)
        s = jnp.where(col < n_blocks, s, NEG_BIG)
        ids = jnp.zeros((ds, LANES), jnp.int32)
        for t in range(MOBA_TOPK):
            mx = jnp.max(s, axis=-1, keepdims=True)
            idx = jnp.min(jnp.where(s == mx, col, LANES), axis=-1, keepdims=True)
            ids = jnp.where(col == t, idx, ids)
            s = jnp.where(col == idx, NEG_BIG * 2, s)
        o_ref[0, h] = ids


def _topk_ids_call(q, means):
    dbs, ds, _ = q.shape
    n_blocks = means.shape[1]
    return pl.pallas_call(
        _topk_ids_kernel,
        grid=(dbs,),
        in_specs=[pl.BlockSpec((1, ds, C_ATT), lambda b: (b, 0, 0)),
                  pl.BlockSpec((1, n_blocks, C_ATT), lambda b: (b, 0, 0))],
        out_specs=pl.BlockSpec((1, H_ATT, ds, LANES), lambda b: (b, 0, 0, 0)),
        out_shape=jax.ShapeDtypeStruct((dbs, H_ATT, ds, LANES), jnp.int32),
        compiler_params=_cparams(("parallel",)),
        name="topk_ids",
    )(q, means)


N_SEL_PAGES = MOBA_TOPK * PAGES_PER_BLOCK


def _attn_sample_kernel(pt_ref, ids_ref, tbl_ref, q_ref, kn_ref, vn_ref, ck_hbm, cv_hbm, o_ref,
                        kbuf, vbuf, sems, *, past_len):
    ds = q_ref.shape[1]
    n_pg = ds * N_SEL_PAGES
    b, h = pl.program_id(0), pl.program_id(1)
    nh = pl.num_programs(1)
    step = b * nh + h
    n_steps = pl.num_programs(0) * nh
    slot = step % 2

    def page_copies(bb, hh, sl):
        base = (bb * H_ATT + hh) * ds * MOBA_TOPK
        cps = []
        for u in range(n_pg):
            pg = pt_ref[bb, ids_ref[base + u // PAGES_PER_BLOCK] * PAGES_PER_BLOCK + u % PAGES_PER_BLOCK]
            cps.append(pltpu.make_async_copy(ck_hbm.at[pg, :, hh, :], kbuf.at[sl, u], sems.at[sl, u]))
            cps.append(pltpu.make_async_copy(cv_hbm.at[pg, :, hh, :], vbuf.at[sl, u], sems.at[sl, n_pg + u]))
        return cps

    @pl.when(step == 0)
    def _():
        for cp in page_copies(b, h, slot):
            cp.start()

    @pl.when(step + 1 < n_steps)
    def _():
        nxt = step + 1
        for cp in page_copies(nxt // nh, nxt % nh, 1 - slot):
            cp.start()

    for cp in page_copies(b, h, slot):
        cp.wait()
    kp_refs = [kbuf.at[slot, u] for u in range(n_pg)]
    vp_refs = [vbuf.at[slot, u] for u in range(n_pg)]
    qb = (q_ref[0] * (HD_ATT ** -0.5)).astype(BF16)
    qrow = lax.broadcasted_iota(jnp.int32, (ds, 1), 0)
    lane = lax.broadcasted_iota(jnp.int32, (ds, PAGE_SIZE), 1)
    lo = lax.dot_general(qb, kn_ref[0].astype(BF16), _NT, preferred_element_type=F32)
    rel_o = lax.broadcasted_iota(jnp.int32, (ds, ds), 0) - lax.broadcasted_iota(jnp.int32, (ds, ds), 1)
    lo = jnp.where(rel_o >= 0, lo + _t5_bias(rel_o, tbl_ref, h), NEG_BIG)
    mx_o = jnp.max(lo, axis=-1, keepdims=True)
    vn = vn_ref[0].astype(BF16)
    tiles = []
    for qi in range(ds):
        flat = (b * H_ATT + h) * ds + qi
        tq = []
        for s in range(MOBA_TOPK):
            blk = ids_ref[flat * MOBA_TOPK + s]
            for u in range(PAGES_PER_BLOCK):
                kpg = kp_refs[(qi * MOBA_TOPK + s) * PAGES_PER_BLOCK + u][...].astype(BF16)
                lg = lax.dot_general(qb, kpg, _NT, preferred_element_type=F32)
                rel = (past_len + qi) - (blk * MOBA_BLOCK + u * PAGE_SIZE + lane)
                tq.append(lg + _t5_bias(rel, tbl_ref, h))
        tiles.append(tq)
    mxs = []
    for tq in tiles:
        mx = mx_o
        for t in tq:
            mx = jnp.maximum(mx, jnp.max(t, axis=-1, keepdims=True))
        mxs.append(mx)
    pos = [jnp.exp(lo - mx) for mx in mxs]
    ps = [[jnp.exp(t - mx) for t in tq] for tq, mx in zip(tiles, mxs)]
    accs = [jnp.dot(po.astype(BF16), vn, preferred_element_type=F32) for po in pos]
    for qi in range(ds):
        for i in range(N_SEL_PAGES):
            accs[qi] = accs[qi] + jnp.dot(ps[qi][i].astype(BF16), vp_refs[qi * N_SEL_PAGES + i][...].astype(BF16),
                                          preferred_element_type=F32)
    result = jnp.zeros((ds, HD_ATT), F32)
    for qi in range(ds):
        den = jnp.sum(pos[qi], axis=-1, keepdims=True)
        for p in ps[qi]:
            den = den + jnp.sum(p, axis=-1, keepdims=True)
        result = jnp.where(qrow == qi, accs[qi] / den, result)
    o_ref[0] = result


def _attn_sample_call(page_table, ids, rel_bias, q, k_new, v_new, cache_k2, cache_v2, past_len):
    dbs, ds, _ = q.shape
    n_pg = ds * N_SEL_PAGES

    nspec = pl.BlockSpec((1, ds, HD_ATT), lambda b, h, p, i: (b, 0, h))
    pool = pl.BlockSpec(memory_space=pl.ANY)
    return pl.pallas_call(
        functools.partial(_attn_sample_kernel, past_len=past_len),
        grid_spec=pltpu.PrefetchScalarGridSpec(
            num_scalar_prefetch=2,
            grid=(dbs, H_ATT),
            in_specs=[pl.BlockSpec(memory_space=pltpu.SMEM), nspec, nspec, nspec, pool, pool],
            out_specs=pl.BlockSpec((1, ds, HD_ATT), lambda b, h, p, i: (b, 0, h)),
            scratch_shapes=[pltpu.VMEM((2, n_pg, PAGE_SIZE, HD_ATT), F32),
                            pltpu.VMEM((2, n_pg, PAGE_SIZE, HD_ATT), F32),
                            pltpu.SemaphoreType.DMA((2, 2 * n_pg))],
        ),
        out_shape=jax.ShapeDtypeStruct((dbs, ds, C_ATT), F32),
        compiler_params=_cparams(("arbitrary", "arbitrary")),
        name="attn_sample",
    )(page_table, ids, rel_bias, q, k_new, v_new, cache_k2, cache_v2)


def _rwkv_pre_kernel(h_ref, wz_ref, sh0_ref, mu_ref, w0_ref, w2_ref, a0_ref, a2_ref, g2_ref, kk_ref, ka_ref,
                     r_o, k_o, v_o, kk_o, b_o, ld_o, g_o, zl_o, carry):
    i = pl.program_id(1)
    z = jnp.dot(h_ref[0], wz_ref[...], preferred_element_type=F32)
    tm = z.shape[0]

    @pl.when(i == 0)
    def _():
        carry[...] = sh0_ref[0]

    first = carry[...]
    carry[...] = z[tm - 1:tm, :]
    zl_o[0, 0] = z[tm - 8:tm, :]
    row = lax.broadcasted_iota(jnp.int32, (tm, 1), 0)
    zprev = jnp.where(row == 0, first, pltpu.roll(z, 1, 0))
    zs = z + (zprev - z) * mu_ref[...]
    c = C_RWKV
    r, k, v = zs[:, :c], zs[:, c:2 * c], zs[:, 2 * c:3 * c]
    zw = zs[:, 3 * c:3 * c + LORA_PAD]
    za = zs[:, 3 * c + LORA_PAD:3 * c + 2 * LORA_PAD]
    zg = zs[:, 3 * c + 2 * LORA_PAD:]
    wl = w0_ref[...] + jnp.dot(jnp.tanh(zw).astype(BF16), w2_ref[...], preferred_element_type=F32)
    w = -(jnp.maximum(-wl, 0.0) + jnp.log(1.0 + jnp.exp(-jnp.abs(wl)))) - 0.5
    ld_o[0] = -jnp.exp(w)
    a = jax.nn.sigmoid(a0_ref[...] + jnp.dot(za.astype(BF16), a2_ref[...], preferred_element_type=F32))
    g_o[0] = jnp.dot(jax.nn.sigmoid(zg).astype(BF16), g2_ref[...], preferred_element_type=F32)
    kkr = k * kk_ref[...]
    n2 = _segsum64(kkr * kkr, _seg_ones())
    kkn = kkr / jnp.maximum(jnp.sqrt(n2), 1e-12)
    r_o[0] = r
    k_o[0] = k * (1.0 + (a - 1.0) * ka_ref[...])
    v_o[0] = v
    kk_o[0] = kkn
    b_o[0] = kkn * a


def _rwkv_pre_call(h, win, z_col0, shift0, p, *, tm):
    bsz, t, d = h.shape
    zp = Z_PAD
    assert z_col0 % zp == 0
    nt = t // tm
    c = C_RWKV
    row = lambda b, i: (0, 0)
    vec = lambda n: pl.BlockSpec((1, n), row)
    tile = pl.BlockSpec((1, tm, c), lambda b, i: (b, i, 0))
    return pl.pallas_call(
        _rwkv_pre_kernel,
        grid=(bsz, nt),
        in_specs=[pl.BlockSpec((1, tm, d), lambda b, i: (b, i, 0)),
                  pl.BlockSpec((d, zp), lambda b, i: (0, z_col0 // zp)),
                  pl.BlockSpec((1, 1, zp), lambda b, i: (b, 0, 0)),
                  vec(zp), vec(c),
                  pl.BlockSpec((LORA_PAD, c), row), vec(c),
                  pl.BlockSpec((LORA_PAD, c), row),
                  pl.BlockSpec((D_GATE_LORA, c), row), vec(c), vec(c)],
        out_specs=[tile] * 7 + [pl.BlockSpec((1, 1, 8, zp), lambda b, i: (b, i, 0, 0))],
        out_shape=[jax.ShapeDtypeStruct((bsz, t, c), F32)] * 7 + [jax.ShapeDtypeStruct((bsz, nt, 8, zp), F32)],
        scratch_shapes=[pltpu.VMEM((1, zp), F32)],
        compiler_params=_cparams(("parallel", "arbitrary")),
        name="rwkv_pre",
    )(h, win, shift0.reshape(bsz, 1, zp), p["mu"], p["w0"], p["w2"], p["a0"], p["a2"], p["g2"], p["k_k"], p["k_a"])


CHUNK_PREC = {"gram": (1, 1), "state_read": (1, 1), "mkv": (1, 1), "solve1": (2, 2), "solve_sq": (2, 2),
              "solve_ap": (2, 2), "out": (1, 1), "state_upd": (1, 1)}


def _chunk_pairs(rs, ks, vs, kks, bs, lds, ss):
    c = CHUNK
    lane = lax.broadcasted_iota(jnp.int32, (1, LANES), 1)
    m_a = jnp.where(lane < HS_RWKV, 1.0, 0.0)
    m_b = 1.0 - m_a
    row = lax.broadcasted_iota(jnp.int32, (c, 2 * c), 0)
    coli = lax.broadcasted_iota(jnp.int32, (c, 2 * c), 1) % c
    strict = coli < row
    incl = coli <= row
    lr = lax.broadcasted_iota(jnp.int32, (c, c), 0)
    lc = lax.broadcasted_iota(jnp.int32, (c, c), 1)
    ltri = jnp.where(lc <= lr, 1.0, 0.0).astype(BF16)
    ones = jnp.ones((c, LANES), BF16)
    rr = lax.broadcasted_iota(jnp.int32, (LANES, LANES), 0) // HS_RWKV
    cc = lax.broadcasted_iota(jnp.int32, (LANES, LANES), 1) // HS_RWKV
    same_head = rr == cc

    def each(f, *lists):
        return [f(*args) for args in zip(*lists)]

    def stack2(x):
        return jnp.concatenate([x * m_a, x * m_b], axis=0)

    def prod(site, a, b, dims=_NN):
        na, nb = CHUNK_PREC[site]
        return _dotp(_split(a, na), _split(b, nb), dims, order=max(na, nb))

    def pm(site, mcats, xs):
        return each(lambda m, x: prod(site, m, stack2(x)), mcats, xs)

    ldp = each(lambda x: _split(x, 2), lds)
    cums = each(lambda p: _dotp([ltri], p, _NN), ldp)
    gcols = each(lambda p: jnp.exp(_dotp(p, [ones], _TN)), ldp)
    g_inv = each(lambda cu: jnp.exp(-cu), cums)
    g_end = each(lambda cu: jnp.exp(cu[c - 1:c, :] - cu), cums)
    p_all = each(lambda kk, r, cu, ld: jnp.concatenate([kk * jnp.exp(cu - ld), r * jnp.exp(cu)], axis=0),
                 kks, rs, cums, lds)
    z2 = each(lambda k, b, gi: jnp.concatenate([stack2(k * gi), stack2(b * gi)], axis=0), ks, bs, g_inv)
    g4 = each(lambda p, z: prod("gram", p, z, _NT), p_all, z2)
    mk = each(lambda g: jnp.where(strict, g[:c, :2 * c], 0.0), g4)
    pj = each(lambda g: jnp.where(strict, -g[:c, 2 * c:], 0.0), g4)
    akb = each(lambda g: jnp.concatenate([jnp.where(incl, g[c:, :2 * c], 0.0),
                                          jnp.where(incl, -g[c:, 2 * c:], 0.0)], axis=1), g4)
    ps = each(lambda p, s: prod("state_read", p, s), p_all, ss)
    mkv = pm("mkv", mk, vs)
    rhs = each(lambda p, m: p[:c] + m, ps, mkv)
    us = each(lambda x, d: x + d, rhs, pm("solve1", pj, rhs))
    n = 2
    while n < c:
        pj = pm("solve_sq", pj, pj)
        us = each(lambda x, d: x + d, us, pm("solve_ap", pj, us))
        n *= 2
    ys = each(lambda p, m, v, u: p[c:] + prod("out", m, jnp.concatenate([stack2(v), stack2(u)], axis=0)),
              ps, akb, vs, us)
    upd = each(lambda k, b, ge, v, u: prod("state_upd", jnp.concatenate([k * ge, -(b * ge)], axis=0),
                                           jnp.concatenate([v, u], axis=0), _TN), ks, bs, g_end, vs, us)
    s_new = each(lambda gc, s, up: gc * s + jnp.where(same_head, up, 0.0), gcols, ss, upd)
    return ys, s_new


def _rwkv_out_pairs(ys, rs, ks, vs, gs, rks, lgs, lbs):
    p128 = _seg_ones()
    c = CHUNK
    inv = 1.0 / HS_RWKV

    def seg3(x):
        return _split(x, 3)

    def sum3(t, o):
        return t[o * c:(o + 1) * c] + t[(o + 1) * c:(o + 2) * c] + t[(o + 2) * c:(o + 3) * c]

    t1 = [jnp.dot(jnp.concatenate(seg3(y) + seg3(r * k * rk), axis=0), p128, preferred_element_type=F32)
          for y, r, k, rk in zip(ys, rs, ks, rks)]
    ycs = [y - sum3(t, 0) * inv for y, t in zip(ys, t1)]
    t2 = [jnp.dot(jnp.concatenate(seg3(yc * yc), axis=0), p128, preferred_element_type=F32) for yc in ycs]
    outs = []
    for yc, ta, tb, v, g, lg, lb in zip(ycs, t1, t2, vs, gs, lgs, lbs):
        yn = yc * lax.rsqrt(sum3(tb, 0) * inv + GN_EPS) * lg + lb
        outs.append(((yn + sum3(ta, 3) * v) * g).astype(BF16))
    return outs


def _rwkv_chunk_kernel(r_ref, k_ref, v_ref, kk_ref, b_ref, ld_ref, g_ref, rk_ref, lg_ref, lb_ref, s0_ref,
                       o_ref, so_ref, s_scr):
    ci = pl.program_id(2)

    @pl.when(ci == 0)
    def _():
        s_scr[...] = s0_ref[0]

    sls = [slice(p * LANES, (p + 1) * LANES) for p in range(PAIRS_PER_STEP)]
    pairs = lambda ref: [ref[0, :, sl] for sl in sls]
    vecs = lambda ref: [ref[:, sl] for sl in sls]
    rs, ks, vs = pairs(r_ref), pairs(k_ref), pairs(v_ref)
    ys, s_new = _chunk_pairs(rs, ks, vs, pairs(kk_ref), pairs(b_ref), pairs(ld_ref),
                             [s_scr[p] for p in range(PAIRS_PER_STEP)])
    outs = _rwkv_out_pairs(ys, rs, ks, vs, pairs(g_ref), vecs(rk_ref), vecs(lg_ref), vecs(lb_ref))
    for p, sl in enumerate(sls):
        o_ref[0, :, sl] = outs[p]
        s_scr[p] = s_new[p]
        so_ref[0, p] = s_new[p]


def _rwkv_chunk_call(r, k, v, kk, b, logd, g, s0_blk, p):
    bsz, t, c = r.shape
    pp = PAIRS_PER_STEP
    w = pp * LANES
    tile = pl.BlockSpec((1, CHUNK, w), lambda bi, pi, ci: (bi, ci, pi))
    vec = pl.BlockSpec((1, w), lambda bi, pi, ci: (0, pi))
    sspec = pl.BlockSpec((1, pp, LANES, LANES), lambda bi, pi, ci: (bi, pi, 0, 0))
    return pl.pallas_call(
        _rwkv_chunk_kernel,
        grid=(bsz, c // w, t // CHUNK),
        in_specs=[tile] * 7 + [vec] * 3 + [sspec],
        out_specs=[tile, sspec],
        out_shape=[jax.ShapeDtypeStruct((bsz, t, c), BF16),
                   jax.ShapeDtypeStruct(s0_blk.shape, F32)],
        scratch_shapes=[pltpu.VMEM((pp, LANES, LANES), F32)],
        compiler_params=_cparams(("parallel", "parallel", "arbitrary")),
        name="rwkv_chunk",
    )(r, k, v, kk, b, logd, g, p["r_k"], p["lnx_g"], p["lnx_b"], s0_blk)


def _combine_kernel(x_ref, ada_ref, lng_ref, lnb_ref, oa_ref, or_ref, ga_ref, gr_ref, wa_ref, wr_ref, wo_ref, o_ref):
    bb, t, d = x_ref.shape
    ua = jnp.dot(oa_ref[...], wa_ref[...], preferred_element_type=F32)
    ur = jnp.dot(or_ref[...], wr_ref[...], preferred_element_type=F32)
    m = jax.nn.sigmoid(ga_ref[...]) * ua + jax.nn.sigmoid(gr_ref[...]) * ur
    mo = jnp.dot(m.astype(BF16), wo_ref[...], preferred_element_type=F32)
    y = DEEPNORM_ALPHA * x_ref[...] + ada_ref[:, 5:6, :] * mo.reshape(bb, t, d)
    o_ref[...] = _ln(y) * lng_ref[1:2, :] + lnb_ref[1:2, :]


def _combine_call(x, ada, ln_g, ln_b, oa, orw, ga, gr, wa, wr, wo, *, blk):
    bsz, t, d = x.shape
    bb, tt = blk
    nt = t // tt
    rows = bb * tt
    xmap = lambda i: (i // nt, i % nt, 0)
    const = lambda i: (0, 0)
    rowt = lambda n: pl.BlockSpec((rows, n), lambda i: (i, 0))
    return pl.pallas_call(
        _combine_kernel,
        grid=((bsz // bb) * nt,),
        in_specs=[pl.BlockSpec((bb, tt, d), xmap),
                  pl.BlockSpec((bb, N_ADA, d), lambda i: (i // nt, 0, 0)),
                  pl.BlockSpec((3, d), const), pl.BlockSpec((3, d), const),
                  rowt(C_ATT), rowt(C_RWKV), rowt(d), rowt(d),
                  pl.BlockSpec((C_ATT, d), const), pl.BlockSpec((C_RWKV, d), const), pl.BlockSpec((d, d), const)],
        out_specs=pl.BlockSpec((bb, tt, d), xmap),
        out_shape=jax.ShapeDtypeStruct(x.shape, F32),
        compiler_params=_cparams(("parallel",)),
        name="combine",
    )(x, ada, ln_g, ln_b, oa, orw, ga, gr, wa, wr, wo)


def _rearrange_z(a):
    c3 = 3 * C_RWKV
    pad = [(0, 0)] * (a.ndim - 1) + [(0, LORA_PAD - D_DECAY_LORA)]
    return jnp.concatenate([a[..., :c3],
                            jnp.pad(a[..., c3:c3 + D_DECAY_LORA], pad),
                            jnp.pad(a[..., c3 + D_DECAY_LORA:c3 + D_DECAY_LORA + D_AAA_LORA], pad),
                            a[..., c3 + D_DECAY_LORA + D_AAA_LORA:]], axis=-1)


def _unarrange_z(a):
    c3 = 3 * C_RWKV
    return jnp.concatenate([a[..., :c3], a[..., c3:c3 + D_DECAY_LORA],
                            a[..., c3 + LORA_PAD:c3 + LORA_PAD + D_AAA_LORA], a[..., c3 + 2 * LORA_PAD:]], axis=-1)


def _state_to_blocks(state):
    bsz = state.shape[0]
    st = jnp.swapaxes(state, -1, -2).reshape(bsz, H_RWKV // 2, 2, HS_RWKV, HS_RWKV)
    blk = jnp.einsum('bphkv,hg->bphkgv', st, jnp.eye(2, dtype=state.dtype))
    return blk.reshape(bsz, H_RWKV // 2, LANES, LANES)


def _blocks_to_state(blk):
    bsz = blk.shape[0]
    b6 = blk.reshape(bsz, H_RWKV // 2, 2, HS_RWKV, 2, HS_RWKV)
    st = jnp.stack([b6[:, :, 0, :, 0, :], b6[:, :, 1, :, 1, :]], axis=2)
    return jnp.swapaxes(st, -1, -2).reshape(bsz, H_RWKV, HS_RWKV, HS_RWKV)


def _trunk(x, ada, attend, shift0, wkv0, w, *, ffn_blk, mm_tm, pre_tm, comb_blk):
    bsz, t, d = x.shape
    m = bsz * t
    ffn = functools.partial(_ffn_call, ada=ada, ln_g=w["ln_g"], ln_b=w["ln_b"], wi=w["ffn_wi"], wo=w["ffn_wo"],
                            layer=w["layer"], blk=ffn_blk)
    x1, h2 = ffn(x, which=0, sub=0, emit_next=True)
    h2f = h2.reshape(m, d)
    win = w["win"]
    proj = functools.partial(_mm_call, h2f, win, tm=mm_tm)
    q = proj(tn=C_ATT, col0=0, n=C_ATT, name="proj_q").reshape(bsz, t, C_ATT)
    k = proj(tn=C_ATT, col0=C_ATT, n=C_ATT, name="proj_k").reshape(bsz, t, C_ATT)
    v = proj(tn=C_ATT, col0=2 * C_ATT, n=C_ATT, name="proj_v").reshape(bsz, t, C_ATT)
    ga = proj(tn=d // 2, col0=3 * C_ATT, n=d, name="proj_ga")
    gr = proj(tn=d // 2, col0=3 * C_ATT + d, n=d, name="proj_gr")
    o_att = attend(q, k, v)
    r, k2, vv, kk, b, logd, g, z_last = _rwkv_pre_call(h2, win, 3 * C_ATT + 2 * d, _rearrange_z(shift0), w, tm=pre_tm)
    tp = -(-t // CHUNK) * CHUNK
    seq = [r, k2, vv, kk, b, logd, g]
    if tp != t:
        seq = [jnp.pad(a, ((0, 0), (0, tp - t), (0, 0))) for a in seq]
    o_rwkv, s_blk = _rwkv_chunk_call(*seq, _state_to_blocks(wkv0), w)
    o_rwkv = o_rwkv[:, :t].reshape(m, C_RWKV)
    x2 = _combine_call(x1, ada, w["ln_g"], w["ln_b"], o_att.reshape(m, C_ATT), o_rwkv, ga, gr,
                       w["wua"], w["wur"], w["wout"], blk=comb_blk)
    x3 = ffn(x2, which=1, sub=2, emit_next=False)
    shift_new = _unarrange_z(z_last[:, -1, 7])
    return (x3, k.reshape(bsz, t, H_ATT, HD_ATT), v.reshape(bsz, t, H_ATT, HD_ATT), _blocks_to_state(s_blk), shift_new)


def kernel(x_prompt, x_sample, cache_k, cache_v, state_wkv, state_shift, page_table, c_prompt, c_sample, rel_bias, w_ada, b_ada, ln_g, ln_b, ffn_wi, ffn_wo, w_in, mu_shift, w0, w2, a0, a2, g2, k_k, k_a, r_k, lnx_g, lnx_b, w_up_attn, w_up_rwkv, w_out):
    assert w_ada.shape[0] == DEPTH == 1
    bsz, s_len, d = x_prompt.shape
    dbs, ds, _ = x_sample.shape
    past_len = page_table.shape[1] * PAGE_SIZE
    n_phys = cache_k.shape[1]
    l = 0
    win = w_in[l]
    c3 = 3 * C_ATT
    lora_rows = ((0, LORA_PAD - D_DECAY_LORA), (0, 0))
    w = {
        "ln_g": ln_g[l], "ln_b": ln_b[l],
        "layer": l, "ffn_wi": ffn_wi.astype(BF16), "ffn_wo": ffn_wo.astype(BF16),
        "win": jnp.concatenate([win[:, :c3], win[:, c3 + RWKV_PROJ:],
                                _rearrange_z(win[:, c3:c3 + RWKV_PROJ])], axis=1).astype(BF16),
        "mu": _rearrange_z(mu_shift[l])[None, :],
        "w0": w0[l][None, :], "w2": jnp.pad(w2[l], lora_rows).astype(BF16),
        "a0": a0[l][None, :], "a2": jnp.pad(a2[l], lora_rows).astype(BF16),
        "g2": g2[l].astype(BF16), "k_k": k_k[l][None, :], "k_a": k_a[l][None, :],
        "r_k": r_k[l].reshape(1, C_RWKV), "lnx_g": lnx_g[l][None, :], "lnx_b": lnx_b[l][None, :],
        "wua": w_up_attn[l].astype(BF16), "wur": w_up_rwkv[l].astype(BF16), "wout": w_out[l].astype(BF16),
    }
    n_c = bsz + dbs
    c_rows = -(-n_c // 8) * 8
    c_all = jnp.concatenate([c_prompt, c_sample, jnp.zeros((c_rows - n_c, d), F32)], axis=0)
    ada = _ada_call(c_all, w_ada[l], b_ada[l]).reshape(c_rows, N_ADA, d)

    bias_tiles = _bias_tiles_call(rel_bias)
    far = rel_bias[NUM_BUCKETS - 1]
    attend_p = lambda q, k, v: _attn_prompt_call(q, k, v, bias_tiles, far)
    yp, kp, vp, wp, sp = _trunk(
        x_prompt, ada[:bsz], attend_p, jnp.zeros((bsz, RWKV_PROJ), F32),
        jnp.zeros((bsz, H_RWKV, HS_RWKV, HS_RWKV), F32), w,
        ffn_blk=(1, 512), mm_tm=1024, pre_tm=256, comb_blk=(1, 256))

    cache_k2 = cache_k.reshape(DEPTH * n_phys, PAGE_SIZE, H_ATT, HD_ATT)
    cache_v2 = cache_v.reshape(DEPTH * n_phys, PAGE_SIZE, H_ATT, HD_ATT)
    page_table = page_table + l * n_phys
    n_full = past_len // MOBA_BLOCK

    def attend_s(q, k, v):
        means = _cache_means_call(page_table, cache_k2, n_full).reshape(dbs, n_full, C_ATT)
        ids = _topk_ids_call(q, means)[..., :MOBA_TOPK]
        o = _attn_sample_call(page_table, ids.reshape(-1), rel_bias, q, k, v, cache_k2, cache_v2, past_len)
        return o.astype(BF16)

    ys, kn, vn, wn, sn = _trunk(
        x_sample, ada[bsz:n_c], attend_s, state_shift[l], state_wkv[l], w,
        ffn_blk=(dbs, ds), mm_tm=dbs * ds, pre_tm=ds, comb_blk=(dbs, ds))

    return (yp, ys, kp[None], vp[None], kn[None], vn[None], wp[None], wn[None], sp[None], sn[None])
```

```python
import functools
import math

import jax
import jax.numpy as jnp
from jax import lax
from jax.experimental import pallas as pl
from jax.experimental.pallas import tpu as pltpu

F32 = jnp.float32
BF16 = jnp.bfloat16

D_MODEL = 2048
D_FF = 5632
N_ADA = 9
H_ATT = 8
HD_ATT = 128
C_ATT = H_ATT * HD_ATT
MOBA_BLOCK = 256
MOBA_TOPK = 3
NUM_BUCKETS = 32
MAX_DISTANCE = 128
PAGE_SIZE = 128
HS_RWKV = 64
C_RWKV = 1024
H_RWKV = C_RWKV // HS_RWKV
D_DECAY_LORA = 96
D_AAA_LORA = 96
D_GATE_LORA = 256
RWKV_PROJ = 3 * C_RWKV + D_DECAY_LORA + D_AAA_LORA + D_GATE_LORA
LORA_PAD = 128
Z_PAD = 3 * C_RWKV + 2 * LORA_PAD + D_GATE_LORA
GN_EPS = 64e-5
LN_EPS = 1e-5
DEPTH = 1
DEEPNORM_ALPHA = (2 * DEPTH) ** 0.25
NEG_BIG = -1e30

LANES = 128
CHUNK = 64
BATCH_PER_STEP = 2
PAIRS_PER_STEP = 8
VMEM_LIMIT = 56 * 1024 * 1024


def _t5_thresholds():
    max_exact = NUM_BUCKETS // 2
    thr = list(range(1, max_exact + 1))
    for b in range(max_exact + 1, NUM_BUCKETS):
        x = max_exact * (MAX_DISTANCE / max_exact) ** ((b - max_exact) / (NUM_BUCKETS - max_exact))
        thr.append(int(math.ceil(x)))
    return tuple(thr)


T5_THR = _t5_thresholds()


def _cparams(sem, vmem=VMEM_LIMIT):
    return pltpu.CompilerParams(dimension_semantics=sem, vmem_limit_bytes=vmem)


def _ln(x):
    mu = jnp.mean(x, axis=-1, keepdims=True)
    xc = x - mu
    var = jnp.mean(xc * xc, axis=-1, keepdims=True)
    return xc * lax.rsqrt(var + LN_EPS)


def _split(x, n):
    parts = []
    for i in range(n):
        p = x.astype(BF16)
        parts.append(p)
        if i + 1 < n:
            x = x - p.astype(F32)
    return parts


_NN = (((1,), (0,)), ((), ()))
_NT = (((1,), (1,)), ((), ()))
_TN = (((0,), (0,)), ((), ()))


def _dotp(ap, bp, dims=_NN, order=None):
    if order is None:
        order = max(len(ap), len(bp))
    acc = None
    if dims == _NN and len(ap) > 1:
        m = ap[0].shape[0]
        for j, b in enumerate(bp):
            sel = [a for i, a in enumerate(ap) if i + j < order]
            if not sel:
                continue
            lhs = sel[0] if len(sel) == 1 else jnp.concatenate(sel, axis=0)
            t = lax.dot_general(lhs, b, dims, preferred_element_type=F32)
            for r in range(len(sel)):
                part = t[r * m:(r + 1) * m]
                acc = part if acc is None else acc + part
        return acc
    for i, a in enumerate(ap):
        for j, b in enumerate(bp):
            if i + j < order:
                t = lax.dot_general(a, b, dims, preferred_element_type=F32)
                acc = t if acc is None else acc + t
    return acc


def _t5_bias(rel, tbl_ref, h):
    bias = jnp.full(rel.shape, tbl_ref[0, h], F32)
    for b in range(1, NUM_BUCKETS):
        bias = jnp.where(rel >= T5_THR[b - 1], tbl_ref[b, h], bias)
    return bias


def _seg_ones():
    r = lax.broadcasted_iota(jnp.int32, (LANES, LANES), 0) // HS_RWKV
    c = lax.broadcasted_iota(jnp.int32, (LANES, LANES), 1) // HS_RWKV
    return jnp.where(r == c, 1.0, 0.0).astype(BF16)


def _segsum64(x, p128):
    outs = []
    for j in range(x.shape[-1] // LANES):
        xs = x[:, j * LANES:(j + 1) * LANES]
        outs.append(_dotp(_split(xs, 3), [p128]))
    return jnp.concatenate(outs, axis=-1)


ADA_TN = 1024


def _ada_kernel(c_ref, w_ref, b_ref, o_ref):
    c = c_ref[...]
    sc = c * jax.nn.sigmoid(c)
    rows = sc.shape[0]
    s0, s1, s2 = _split(sc, 3)
    w_hi, w_lo = _split(w_ref[...], 2)
    a = jnp.dot(jnp.concatenate([s0, s1, s2], axis=0), w_hi, preferred_element_type=F32)
    b = jnp.dot(jnp.concatenate([s0, s1], axis=0), w_lo, preferred_element_type=F32)
    o_ref[...] = (a[:rows] + a[rows:2 * rows] + a[2 * rows:] + b[:rows] + b[rows:]) + b_ref[...]


def _ada_call(c_all, w_ada, b_ada):
    rows, d = c_all.shape
    n = w_ada.shape[1]
    return pl.pallas_call(
        _ada_kernel,
        grid=(n // ADA_TN,),
        in_specs=[pl.BlockSpec((rows, d), lambda j: (0, 0)),
                  pl.BlockSpec((d, ADA_TN), lambda j: (0, j)),
                  pl.BlockSpec((1, ADA_TN), lambda j: (0, j))],
        out_specs=pl.BlockSpec((rows, ADA_TN), lambda j: (0, j)),
        out_shape=jax.ShapeDtypeStruct((rows, n), F32),
        compiler_params=_cparams(("parallel",)),
        name="ada",
    )(c_all, w_ada, b_ada.reshape(1, n))


FFN_TF = 512
FFN_TF_SMALL_M = 1408


def _ffn_kernel(x_ref, ada_ref, lng_ref, lnb_ref, wig_ref, wiu_ref, wo_ref, *rest, sub, emit_next):
    if emit_next:
        o_ref, h_next_ref, h_scr, acc = rest
    else:
        o_ref, h_scr, acc = rest
    bb, t, d = x_ref.shape
    j = pl.program_id(1)

    @pl.when(j == 0)
    def _():
        h = _ln(x_ref[...]) * (1.0 + ada_ref[:, 3 * sub + 1:3 * sub + 2, :]) + ada_ref[:, 3 * sub:3 * sub + 1, :]
        h_scr[...] = h.reshape(bb * t, d).astype(BF16)
        acc[...] = jnp.zeros_like(acc)

    h = h_scr[...]
    g = jnp.dot(h, wig_ref[...], preferred_element_type=F32)
    u = jnp.dot(h, wiu_ref[...], preferred_element_type=F32)
    act = (g * jax.nn.sigmoid(g)) * u
    acc[...] += jnp.dot(act.astype(BF16), wo_ref[...], preferred_element_type=F32)

    @pl.when(j == pl.num_programs(1) - 1)
    def _():
        gate = ada_ref[:, 3 * sub + 2:3 * sub + 3, :]
        y = DEEPNORM_ALPHA * x_ref[...] + 0.5 * gate * acc[...].reshape(bb, t, d)
        y = _ln(y) * lng_ref[sub:sub + 1, :] + lnb_ref[sub:sub + 1, :]
        o_ref[...] = y
        if emit_next:
            nxt = sub + 1
            hn = _ln(y) * (1.0 + ada_ref[:, 3 * nxt + 1:3 * nxt + 2, :]) + ada_ref[:, 3 * nxt:3 * nxt + 1, :]
            h_next_ref[...] = hn.astype(BF16)


def _ffn_call(x, ada, ln_g, ln_b, wi, wo, *, layer, which, sub, emit_next, blk):
    bsz, t, d = x.shape
    bb, tt = blk
    tf = FFN_TF if bb * tt >= FFN_TF else FFN_TF_SMALL_M
    assert D_FF % tf == 0
    nj = D_FF // tf
    nt = t // tt
    grid = ((bsz // bb) * nt, nj)
    xmap = lambda i, j: (i // nt, i % nt, 0)
    amap = lambda i, j: (i // nt, 0, 0)
    out_shape = [jax.ShapeDtypeStruct(x.shape, F32)]
    out_specs = [pl.BlockSpec((bb, tt, d), xmap)]
    if emit_next:
        out_shape.append(jax.ShapeDtypeStruct(x.shape, BF16))
        out_specs.append(pl.BlockSpec((bb, tt, d), xmap))
    res = pl.pallas_call(
        functools.partial(_ffn_kernel, sub=sub, emit_next=emit_next),
        grid=grid,
        in_specs=[pl.BlockSpec((bb, tt, d), xmap),
                  pl.BlockSpec((bb, N_ADA, d), amap),
                  pl.BlockSpec((3, d), lambda i, j: (0, 0)),
                  pl.BlockSpec((3, d), lambda i, j: (0, 0)),
                  pl.BlockSpec((None, None, d, tf), lambda i, j: (layer, which, 0, j)),
                  pl.BlockSpec((None, None, d, tf), lambda i, j: (layer, which, 0, j + nj)),
                  pl.BlockSpec((None, None, tf, d), lambda i, j: (layer, which, j, 0))],
        out_specs=out_specs,
        out_shape=out_shape,
        scratch_shapes=[pltpu.VMEM((bb * tt, d), BF16), pltpu.VMEM((bb * tt, d), F32)],
        compiler_params=_cparams(("parallel", "arbitrary")),
        name=f"ffn{sub}",
    )(x, ada, ln_g, ln_b, wi, wi, wo)
    return res if emit_next else res[0]


def _mm_kernel(h_ref, w_ref, o_ref):
    o_ref[...] = jnp.dot(h_ref[...], w_ref[...], preferred_element_type=F32)


def _mm_call(h, w, *, tm, tn, col0, n, name="proj"):
    m, k = h.shape
    assert col0 % tn == 0 and n % tn == 0
    cb = col0 // tn
    return pl.pallas_call(
        _mm_kernel,
        grid=(m // tm, n // tn),
        in_specs=[pl.BlockSpec((tm, k), lambda i, j: (i, 0)),
                  pl.BlockSpec((k, tn), lambda i, j: (0, j + cb))],
        out_specs=pl.BlockSpec((tm, tn), lambda i, j: (i, j)),
        out_shape=jax.ShapeDtypeStruct((m, n), F32),
        compiler_params=_cparams(("parallel", "arbitrary")),
        name=name,
    )(h, w)


def _bias_tiles_kernel(tbl_ref, o_ref):
    h = pl.program_id(0)
    r = lax.broadcasted_iota(jnp.int32, (MOBA_BLOCK, MOBA_BLOCK), 0)
    c = lax.broadcasted_iota(jnp.int32, (MOBA_BLOCK, MOBA_BLOCK), 1)
    rel0 = r - c
    o_ref[0, 0] = jnp.where(rel0 >= 0, _t5_bias(rel0, tbl_ref, h), NEG_BIG)
    o_ref[0, 1] = _t5_bias(rel0 + MOBA_BLOCK, tbl_ref, h)


def _bias_tiles_call(rel_bias):
    return pl.pallas_call(
        _bias_tiles_kernel,
        grid=(H_ATT,),
        in_specs=[pl.BlockSpec(memory_space=pltpu.SMEM)],
        out_specs=pl.BlockSpec((1, 2, MOBA_BLOCK, MOBA_BLOCK), lambda h: (h, 0, 0, 0)),
        out_shape=jax.ShapeDtypeStruct((H_ATT, 2, MOBA_BLOCK, MOBA_BLOCK), F32),
        compiler_params=_cparams(("arbitrary",)),
        name="bias_tiles",
    )(rel_bias)


def _rank_select(s, n_valid):
    col = lax.broadcasted_iota(jnp.int32, s.shape, 1)
    cnt = jnp.zeros(s.shape, jnp.int32)
    for m in range(n_valid):
        sm = s[:, m:m + 1]
        beats = (sm > s) | ((sm == s) & (m < col))
        cnt = cnt + jnp.where(beats, 1, 0)
    return (col < n_valid) & (cnt < MOBA_TOPK)


def _attn_prompt_kernel(far_ref, q_ref, k_ref, v_ref, bt_ref, o_ref):
    h = pl.program_id(1)
    s_len = q_ref.shape[1]
    nb = s_len // MOBA_BLOCK
    q = q_ref[0]
    k = k_ref[0]
    kb = k.astype(BF16)
    vb = v_ref[0].astype(BF16)
    qb = (q * (HD_ATT ** -0.5)).astype(BF16)
    means = jnp.mean(k.reshape(nb, MOBA_BLOCK, HD_ATT), axis=1)
    means = jnp.concatenate([means, jnp.zeros((LANES - nb, HD_ATT), F32)], axis=0)
    mparts = _split(means, 3)
    far = far_ref[h]
    rows = [slice(i * MOBA_BLOCK, (i + 1) * MOBA_BLOCK) for i in range(nb)]
    logits = [lax.dot_general(qb[rows[i]], kb[:(i + 1) * MOBA_BLOCK], _NT, preferred_element_type=F32)
              for i in range(nb)]
    scores = [None] + [_dotp(_split(q[rows[i]], 3), mparts, _NT, order=3) for i in range(1, nb)]
    negs = [None] + [jnp.where(_rank_select(scores[i], i), 0.0, NEG_BIG) for i in range(1, nb)]
    probs, dens = [], []
    for i in range(nb):
        tiles = []
        for j in range(i + 1):
            tile = logits[i][:, j * MOBA_BLOCK:(j + 1) * MOBA_BLOCK]
            if j == i:
                tile = tile + bt_ref[0, 0]
            elif j == i - 1:
                tile = tile + bt_ref[0, 1] + negs[i][:, j:j + 1]
            else:
                tile = tile + (far + negs[i][:, j:j + 1])
            tiles.append(tile)
        lg = jnp.concatenate(tiles, axis=-1) if len(tiles) > 1 else tiles[0]
        mx = jnp.max(lg, axis=-1, keepdims=True)
        p = jnp.exp(lg - mx)
        dens.append(jnp.sum(p, axis=-1, keepdims=True))
        probs.append(p.astype(BF16))
    for i in range(nb):
        out = jnp.dot(probs[i], vb[:(i + 1) * MOBA_BLOCK], preferred_element_type=F32) / dens[i]
        o_ref[0, rows[i], :] = out.astype(o_ref.dtype)


def _attn_prompt_call(q, k, v, bias_tiles, far):
    bsz, s_len, _ = q.shape
    qspec = pl.BlockSpec((1, s_len, HD_ATT), lambda b, h: (b, 0, h))
    return pl.pallas_call(
        _attn_prompt_kernel,
        grid=(bsz, H_ATT),
        in_specs=[pl.BlockSpec(memory_space=pltpu.SMEM), qspec, qspec, qspec,
                  pl.BlockSpec((1, 2, MOBA_BLOCK, MOBA_BLOCK), lambda b, h: (h, 0, 0, 0))],
        out_specs=pl.BlockSpec((1, s_len, HD_ATT), lambda b, h: (b, 0, h)),
        out_shape=jax.ShapeDtypeStruct((bsz, s_len, C_ATT), BF16),
        compiler_params=_cparams(("parallel", "arbitrary")),
        name="attn_prompt",
    )(far, q, k, v, bias_tiles)


MEAN_BLOCKS = 16
PAGES_PER_BLOCK = MOBA_BLOCK // PAGE_SIZE


def _cache_means_kernel(pt_ref, *refs):
    page_refs, o_ref = refs[:-1], refs[-1]
    for m in range(MEAN_BLOCKS):
        tot = None
        for u in range(PAGES_PER_BLOCK):
            s = jnp.sum(page_refs[m * PAGES_PER_BLOCK + u][0], axis=0)
            tot = s if tot is None else tot + s
        o_ref[0, m] = tot * (1.0 / MOBA_BLOCK)


def _cache_means_call(page_table, cache_k4, n_blocks):
    dbs = page_table.shape[0]
    npg = MEAN_BLOCKS * PAGES_PER_BLOCK
    in_specs = [pl.BlockSpec((1, PAGE_SIZE, H_ATT, HD_ATT), functools.partial(
        lambda b, g, pt, u: (pt[b, g * npg + u], 0, 0, 0), u=u)) for u in range(npg)]
    return pl.pallas_call(
        _cache_means_kernel,
        grid_spec=pltpu.PrefetchScalarGridSpec(
            num_scalar_prefetch=1,
            grid=(dbs, n_blocks // MEAN_BLOCKS),
            in_specs=in_specs,
            out_specs=pl.BlockSpec((1, MEAN_BLOCKS, H_ATT, HD_ATT), lambda b, g, pt: (b, g, 0, 0)),
        ),
        out_shape=jax.ShapeDtypeStruct((dbs, n_blocks, H_ATT, HD_ATT), F32),
        compiler_params=_cparams(("parallel", "arbitrary")),
        name="cache_means",
    )(page_table, *([cache_k4] * npg))


def _topk_ids_kernel(q_ref, m_ref, o_ref):
    q = q_ref[0]
    means = m_ref[0]
    n_blocks = means.shape[0]
    ds = q.shape[0]
    pad = jnp.zeros((LANES - n_blocks, HD_ATT), F32)
    col = lax.broadcasted_iota(jnp.int32, (ds, LANES), 1)
    for h in range(H_ATT):
        sl = slice(h * HD_ATT, (h + 1) * HD_ATT)
        mh = jnp.concatenate([means[:, sl], pad], axis=0) if n_blocks < LANES else means[:, sl]
        s = _dotp(_split(q[:, sl], 3), _split(mh, 3), _NT, order=3)
        s = jnp.where(col < n_blocks, s, NEG_BIG)
        ids = jnp.zeros((ds, LANES), jnp.int32)
        for t in range(MOBA_TOPK):
            mx = jnp.max(s, axis=-1, keepdims=True)
            idx = jnp.min(jnp.where(s == mx, col, LANES), axis=-1, keepdims=True)
            ids = jnp.where(col == t, idx, ids)
            s = jnp.where(col == idx, NEG_BIG * 2, s)
        o_ref[0, h] = ids


def _topk_ids_call(q, means):
    dbs, ds, _ = q.shape
    n_blocks = means.shape[1]
    return pl.pallas_call(
        _topk_ids_kernel,
        grid=(dbs,),
        in_specs=[pl.BlockSpec((1, ds, C_ATT), lambda b: (b, 0, 0)),
                  pl.BlockSpec((1, n_blocks, C_ATT), lambda b: (b, 0, 0))],
        out_specs=pl.BlockSpec((1, H_ATT, ds, LANES), lambda b: (b, 0, 0, 0)),
        out_shape=jax.ShapeDtypeStruct((dbs, H_ATT, ds, LANES), jnp.int32),
        compiler_params=_cparams(("parallel",)),
        name="topk_ids",
    )(q, means)


N_SEL_PAGES = MOBA_TOPK * PAGES_PER_BLOCK


def _attn_sample_kernel(pt_ref, ids_ref, tbl_ref, q_ref, kn_ref, vn_ref, ck_hbm, cv_hbm, o_ref,
                        kbuf, vbuf, sems, *, past_len):
    ds = q_ref.shape[1]
    n_pg = ds * N_SEL_PAGES
    b, h = pl.program_id(0), pl.program_id(1)
    nh = pl.num_programs(1)
    step = b * nh + h
    n_steps = pl.num_programs(0) * nh
    slot = step % 2

    def page_copies(bb, hh, sl):
        base = (bb * H_ATT + hh) * ds * MOBA_TOPK
        cps = []
        for u in range(n_pg):
            pg = pt_ref[bb, ids_ref[base + u // PAGES_PER_BLOCK] * PAGES_PER_BLOCK + u % PAGES_PER_BLOCK]
            cps.append(pltpu.make_async_copy(ck_hbm.at[pg, :, hh, :], kbuf.at[sl, u], sems.at[sl, u]))
            cps.append(pltpu.make_async_copy(cv_hbm.at[pg, :, hh, :], vbuf.at[sl, u], sems.at[sl, n_pg + u]))
        return cps

    @pl.when(step == 0)
    def _():
        for cp in page_copies(b, h, slot):
            cp.start()

    @pl.when(step + 1 < n_steps)
    def _():
        nxt = step + 1
        for cp in page_copies(nxt // nh, nxt % nh, 1 - slot):
            cp.start()

    for cp in page_copies(b, h, slot):
        cp.wait()
    kp_refs = [kbuf.at[slot, u] for u in range(n_pg)]
    vp_refs = [vbuf.at[slot, u] for u in range(n_pg)]
    qb = (q_ref[0] * (HD_ATT ** -0.5)).astype(BF16)
    qrow = lax.broadcasted_iota(jnp.int32, (ds, 1), 0)
    lane = lax.broadcasted_iota(jnp.int32, (ds, PAGE_SIZE), 1)
    lo = lax.dot_general(qb, kn_ref[0].astype(BF16), _NT, preferred_element_type=F32)
    rel_o = lax.broadcasted_iota(jnp.int32, (ds, ds), 0) - lax.broadcasted_iota(jnp.int32, (ds, ds), 1)
    lo = jnp.where(rel_o >= 0, lo + _t5_bias(rel_o, tbl_ref, h), NEG_BIG)
    mx_o = jnp.max(lo, axis=-1, keepdims=True)
    vn = vn_ref[0].astype(BF16)
    tiles = []
    for qi in range(ds):
        flat = (b * H_ATT + h) * ds + qi
        tq = []
        for s in range(MOBA_TOPK):
            blk = ids_ref[flat * MOBA_TOPK + s]
            for u in range(PAGES_PER_BLOCK):
                kpg = kp_refs[(qi * MOBA_TOPK + s) * PAGES_PER_BLOCK + u][...].astype(BF16)
                lg = lax.dot_general(qb, kpg, _NT, preferred_element_type=F32)
                rel = (past_len + qi) - (blk * MOBA_BLOCK + u * PAGE_SIZE + lane)
                tq.append(lg + _t5_bias(rel, tbl_ref, h))
        tiles.append(tq)
    mxs = []
    for tq in tiles:
        mx = mx_o
        for t in tq:
            mx = jnp.maximum(mx, jnp.max(t, axis=-1, keepdims=True))
        mxs.append(mx)
    pos = [jnp.exp(lo - mx) for mx in mxs]
    ps = [[jnp.exp(t - mx) for t in tq] for tq, mx in zip(tiles, mxs)]
    accs = [jnp.dot(po.astype(BF16), vn, preferred_element_type=F32) for po in pos]
    for qi in range(ds):
        for i in range(N_SEL_PAGES):
            accs[qi] = accs[qi] + jnp.dot(ps[qi][i].astype(BF16), vp_refs[qi * N_SEL_PAGES + i][...].astype(BF16),
                                          preferred_element_type=F32)
    result = jnp.zeros((ds, HD_ATT), F32)
    for qi in range(ds):
        den = jnp.sum(pos[qi], axis=-1, keepdims=True)
        for p in ps[qi]:
            den = den + jnp.sum(p, axis=-1, keepdims=True)
        result = jnp.where(qrow == qi, accs[qi] / den, result)
    o_ref[0] = result


def _attn_sample_call(page_table, ids, rel_bias, q, k_new, v_new, cache_k2, cache_v2, past_len):
    dbs, ds, _ = q.shape
    n_pg = ds * N_SEL_PAGES

    nspec = pl.BlockSpec((1, ds, HD_ATT), lambda b, h, p, i: (b, 0, h))
    pool = pl.BlockSpec(memory_space=pl.ANY)
    return pl.pallas_call(
        functools.partial(_attn_sample_kernel, past_len=past_len),
        grid_spec=pltpu.PrefetchScalarGridSpec(
            num_scalar_prefetch=2,
            grid=(dbs, H_ATT),
            in_specs=[pl.BlockSpec(memory_space=pltpu.SMEM), nspec, nspec, nspec, pool, pool],
            out_specs=pl.BlockSpec((1, ds, HD_ATT), lambda b, h, p, i: (b, 0, h)),
            scratch_shapes=[pltpu.VMEM((2, n_pg, PAGE_SIZE, HD_ATT), F32),
                            pltpu.VMEM((2, n_pg, PAGE_SIZE, HD_ATT), F32),
                            pltpu.SemaphoreType.DMA((2, 2 * n_pg))],
        ),
        out_shape=jax.ShapeDtypeStruct((dbs, ds, C_ATT), F32),
        compiler_params=_cparams(("arbitrary", "arbitrary")),
        name="attn_sample",
    )(page_table, ids, rel_bias, q, k_new, v_new, cache_k2, cache_v2)


def _rwkv_pre_kernel(h_ref, wz_ref, sh0_ref, mu_ref, w0_ref, w2_ref, a0_ref, a2_ref, g2_ref, kk_ref, ka_ref,
                     r_o, k_o, v_o, kk_o, b_o, ld_o, g_o, zl_o, carry):
    i = pl.program_id(1)
    z = jnp.dot(h_ref[0], wz_ref[...], preferred_element_type=F32)
    tm = z.shape[0]

    @pl.when(i == 0)
    def _():
        carry[...] = sh0_ref[0]

    first = carry[...]
    carry[...] = z[tm - 1:tm, :]
    zl_o[0, 0] = z[tm - 8:tm, :]
    row = lax.broadcasted_iota(jnp.int32, (tm, 1), 0)
    zprev = jnp.where(row == 0, first, pltpu.roll(z, 1, 0))
    zs = z + (zprev - z) * mu_ref[...]
    c = C_RWKV
    r, k, v = zs[:, :c], zs[:, c:2 * c], zs[:, 2 * c:3 * c]
    zw = zs[:, 3 * c:3 * c + LORA_PAD]
    za = zs[:, 3 * c + LORA_PAD:3 * c + 2 * LORA_PAD]
    zg = zs[:, 3 * c + 2 * LORA_PAD:]
    wl = w0_ref[...] + jnp.dot(jnp.tanh(zw).astype(BF16), w2_ref[...], preferred_element_type=F32)
    w = -(jnp.maximum(-wl, 0.0) + jnp.log(1.0 + jnp.exp(-jnp.abs(wl)))) - 0.5
    ld_o[0] = -jnp.exp(w)
    a = jax.nn.sigmoid(a0_ref[...] + jnp.dot(za.astype(BF16), a2_ref[...], preferred_element_type=F32))
    g_o[0] = jnp.dot(jax.nn.sigmoid(zg).astype(BF16), g2_ref[...], preferred_element_type=F32)
    kkr = k * kk_ref[...]
    n2 = _segsum64(kkr * kkr, _seg_ones())
    kkn = kkr / jnp.maximum(jnp.sqrt(n2), 1e-12)
    r_o[0] = r
    k_o[0] = k * (1.0 + (a - 1.0) * ka_ref[...])
    v_o[0] = v
    kk_o[0] = kkn
    b_o[0] = kkn * a


def _rwkv_pre_call(h, win, z_col0, shift0, p, *, tm):
    bsz, t, d = h.shape
    zp = Z_PAD
    assert z_col0 % zp == 0
    nt = t // tm
    c = C_RWKV
    row = lambda b, i: (0, 0)
    vec = lambda n: pl.BlockSpec((1, n), row)
    tile = pl.BlockSpec((1, tm, c), lambda b, i: (b, i, 0))
    return pl.pallas_call(
        _rwkv_pre_kernel,
        grid=(bsz, nt),
        in_specs=[pl.BlockSpec((1, tm, d), lambda b, i: (b, i, 0)),
                  pl.BlockSpec((d, zp), lambda b, i: (0, z_col0 // zp)),
                  pl.BlockSpec((1, 1, zp), lambda b, i: (b, 0, 0)),
                  vec(zp), vec(c),
                  pl.BlockSpec((LORA_PAD, c), row), vec(c),
                  pl.BlockSpec((LORA_PAD, c), row),
                  pl.BlockSpec((D_GATE_LORA, c), row), vec(c), vec(c)],
        out_specs=[tile] * 7 + [pl.BlockSpec((1, 1, 8, zp), lambda b, i: (b, i, 0, 0))],
        out_shape=[jax.ShapeDtypeStruct((bsz, t, c), F32)] * 7 + [jax.ShapeDtypeStruct((bsz, nt, 8, zp), F32)],
        scratch_shapes=[pltpu.VMEM((1, zp), F32)],
        compiler_params=_cparams(("parallel", "arbitrary")),
        name="rwkv_pre",
    )(h, win, shift0.reshape(bsz, 1, zp), p["mu"], p["w0"], p["w2"], p["a0"], p["a2"], p["g2"], p["k_k"], p["k_a"])


SOLVE_BASE = 8
CHUNK_PREC = {"gram": (1, 1), "state_read": (1, 1), "mkv": (1, 1), "solve1": (1, 1), "solve_sq": (1, 1),
              "solve_ap": (1, 1), "out": (1, 1), "state_upd": (1, 1)}


def _chunk_pairs(rs, ks, vs, kks, bs, lds, ss):
    c = CHUNK
    lane = lax.broadcasted_iota(jnp.int32, (1, LANES), 1)
    m_a = jnp.where(lane < HS_RWKV, 1.0, 0.0)
    m_b = 1.0 - m_a
    row = lax.broadcasted_iota(jnp.int32, (c, 2 * c), 0)
    coli = lax.broadcasted_iota(jnp.int32, (c, 2 * c), 1) % c
    strict = coli < row
    incl = coli <= row
    lr = lax.broadcasted_iota(jnp.int32, (c, c), 0)
    lc = lax.broadcasted_iota(jnp.int32, (c, c), 1)
    ltri = jnp.where(lc <= lr, 1.0, 0.0).astype(BF16)
    ones = jnp.ones((c, LANES), BF16)
    rr = lax.broadcasted_iota(jnp.int32, (LANES, LANES), 0) // HS_RWKV
    cc = lax.broadcasted_iota(jnp.int32, (LANES, LANES), 1) // HS_RWKV
    same_head = rr == cc

    def each(f, *lists):
        return [f(*args) for args in zip(*lists)]

    def stack2(x):
        return jnp.concatenate([x * m_a, x * m_b], axis=0)

    def prod(site, a, b, dims=_NN):
        na, nb = CHUNK_PREC[site]
        return _dotp(_split(a, na), _split(b, nb), dims, order=max(na, nb))

    def pm(site, mcats, xs):
        return each(lambda m, x: prod(site, m, stack2(x)), mcats, xs)

    ldp = each(lambda x: _split(x, 2), lds)
    cums = each(lambda p: _dotp([ltri], p, _NN), ldp)
    gcols = each(lambda p: jnp.exp(_dotp(p, [ones], _TN)), ldp)
    g_inv = each(lambda cu: jnp.exp(-cu), cums)
    g_end = each(lambda cu: jnp.exp(cu[c - 1:c, :] - cu), cums)
    p_all = each(lambda kk, r, cu, ld: jnp.concatenate([kk * jnp.exp(cu - ld), r * jnp.exp(cu)], axis=0),
                 kks, rs, cums, lds)
    z2 = each(lambda k, b, gi: jnp.concatenate([stack2(k * gi), stack2(b * gi)], axis=0), ks, bs, g_inv)
    g4 = each(lambda p, z: prod("gram", p, z, _NT), p_all, z2)
    mk = each(lambda g: jnp.where(strict, g[:c, :2 * c], 0.0), g4)
    pj = each(lambda g: jnp.where(strict, -g[:c, 2 * c:], 0.0), g4)
    akb = each(lambda g: jnp.concatenate([jnp.where(incl, g[c:, :2 * c], 0.0),
                                          jnp.where(incl, -g[c:, 2 * c:], 0.0)], axis=1), g4)
    ps = each(lambda p, s: prod("state_read", p, s), p_all, ss)
    mkv = pm("mkv", mk, vs)
    rhs = each(lambda p, m: p[:c] + m, ps, mkv)
    same_blk = lambda s: (row // s) == (coli // s)
    eye = jnp.where(coli == row, 1.0, 0.0)
    pk = each(lambda n_: jnp.where(same_blk(SOLVE_BASE), n_, 0.0), pj)
    ts = each(lambda p: eye + p, pk)
    n = 2
    while n < SOLVE_BASE:
        pk = pm("solve_sq", pk, pk)
        ts = each(lambda t, d: t + d, ts, pm("solve_ap", ts, pk))
        n *= 2
    s = SOLVE_BASE
    while s < c:
        low = same_blk(2 * s) & jnp.logical_not(same_blk(s))
        cs = each(lambda n_: jnp.where(low, -n_, 0.0), pj)
        ts = each(lambda t, d: t - d, ts, pm("solve_ap", ts, pm("solve_sq", cs, ts)))
        s *= 2
    us = pm("solve1", ts, rhs)
    ys = each(lambda p, m, v, u: p[c:] + prod("out", m, jnp.concatenate([stack2(v), stack2(u)], axis=0)),
              ps, akb, vs, us)
    upd = each(lambda k, b, ge, v, u: prod("state_upd", jnp.concatenate([k * ge, -(b * ge)], axis=0),
                                           jnp.concatenate([v, u], axis=0), _TN), ks, bs, g_end, vs, us)
    s_new = each(lambda gc, s, up: gc * s + jnp.where(same_head, up, 0.0), gcols, ss, upd)
    return ys, s_new


def _rwkv_out_pairs(ys, rs, ks, vs, gs, rks, lgs, lbs):
    p128 = _seg_ones()
    c = CHUNK
    inv = 1.0 / HS_RWKV

    def seg3(x):
        return _split(x, 3)

    def sum3(t, o):
        return t[o * c:(o + 1) * c] + t[(o + 1) * c:(o + 2) * c] + t[(o + 2) * c:(o + 3) * c]

    t1 = [jnp.dot(jnp.concatenate(seg3(y) + seg3(r * k * rk), axis=0), p128, preferred_element_type=F32)
          for y, r, k, rk in zip(ys, rs, ks, rks)]
    ycs = [y - sum3(t, 0) * inv for y, t in zip(ys, t1)]
    t2 = [jnp.dot(jnp.concatenate(seg3(yc * yc), axis=0), p128, preferred_element_type=F32) for yc in ycs]
    outs = []
    for yc, ta, tb, v, g, lg, lb in zip(ycs, t1, t2, vs, gs, lgs, lbs):
        yn = yc * lax.rsqrt(sum3(tb, 0) * inv + GN_EPS) * lg + lb
        outs.append(((yn + sum3(ta, 3) * v) * g).astype(BF16))
    return outs


def _rwkv_chunk_kernel(r_ref, k_ref, v_ref, kk_ref, b_ref, ld_ref, g_ref, rk_ref, lg_ref, lb_ref, s0_ref,
                       o_ref, so_ref, s_scr):
    ci = pl.program_id(2)

    @pl.when(ci == 0)
    def _():
        s_scr[...] = s0_ref[...]

    nb = r_ref.shape[0]
    units = [(bi, p, slice(p * LANES, (p + 1) * LANES)) for bi in range(nb) for p in range(PAIRS_PER_STEP)]
    pairs = lambda ref: [ref[bi, :, sl] for bi, _, sl in units]
    vecs = lambda ref: [ref[:, sl] for _, _, sl in units]
    rs, ks, vs = pairs(r_ref), pairs(k_ref), pairs(v_ref)
    ys, s_new = _chunk_pairs(rs, ks, vs, pairs(kk_ref), pairs(b_ref), pairs(ld_ref),
                             [s_scr[bi, p] for bi, p, _ in units])
    outs = _rwkv_out_pairs(ys, rs, ks, vs, pairs(g_ref), vecs(rk_ref), vecs(lg_ref), vecs(lb_ref))
    for u, (bi, p, sl) in enumerate(units):
        o_ref[bi, :, sl] = outs[u]
        s_scr[bi, p] = s_new[u]
        so_ref[bi, p] = s_new[u]


def _rwkv_chunk_call(r, k, v, kk, b, logd, g, s0_blk, p):
    bsz, t, c = r.shape
    pp = PAIRS_PER_STEP
    nb = BATCH_PER_STEP
    assert bsz % nb == 0
    w = pp * LANES
    tile = pl.BlockSpec((nb, CHUNK, w), lambda bi, pi, ci: (bi, ci, pi))
    vec = pl.BlockSpec((1, w), lambda bi, pi, ci: (0, pi))
    sspec = pl.BlockSpec((nb, pp, LANES, LANES), lambda bi, pi, ci: (bi, pi, 0, 0))
    return pl.pallas_call(
        _rwkv_chunk_kernel,
        grid=(bsz // nb, c // w, t // CHUNK),
        in_specs=[tile] * 7 + [vec] * 3 + [sspec],
        out_specs=[tile, sspec],
        out_shape=[jax.ShapeDtypeStruct((bsz, t, c), BF16),
                   jax.ShapeDtypeStruct(s0_blk.shape, F32)],
        scratch_shapes=[pltpu.VMEM((nb, pp, LANES, LANES), F32)],
        compiler_params=_cparams(("parallel", "parallel", "arbitrary")),
        name="rwkv_chunk",
    )(r, k, v, kk, b, logd, g, p["r_k"], p["lnx_g"], p["lnx_b"], s0_blk)


def _combine_kernel(x_ref, ada_ref, lng_ref, lnb_ref, oa_ref, or_ref, ga_ref, gr_ref, wa_ref, wr_ref, wo_ref, o_ref):
    bb, t, d = x_ref.shape
    ua = jnp.dot(oa_ref[...], wa_ref[...], preferred_element_type=F32)
    ur = jnp.dot(or_ref[...], wr_ref[...], preferred_element_type=F32)
    m = jax.nn.sigmoid(ga_ref[...]) * ua + jax.nn.sigmoid(gr_ref[...]) * ur
    mo = jnp.dot(m.astype(BF16), wo_ref[...], preferred_element_type=F32)
    y = DEEPNORM_ALPHA * x_ref[...] + ada_ref[:, 5:6, :] * mo.reshape(bb, t, d)
    o_ref[...] = _ln(y) * lng_ref[1:2, :] + lnb_ref[1:2, :]


def _combine_call(x, ada, ln_g, ln_b, oa, orw, ga, gr, wa, wr, wo, *, blk):
    bsz, t, d = x.shape
    bb, tt = blk
    nt = t // tt
    rows = bb * tt
    xmap = lambda i: (i // nt, i % nt, 0)
    const = lambda i: (0, 0)
    rowt = lambda n: pl.BlockSpec((rows, n), lambda i: (i, 0))
    return pl.pallas_call(
        _combine_kernel,
        grid=((bsz // bb) * nt,),
        in_specs=[pl.BlockSpec((bb, tt, d), xmap),
                  pl.BlockSpec((bb, N_ADA, d), lambda i: (i // nt, 0, 0)),
                  pl.BlockSpec((3, d), const), pl.BlockSpec((3, d), const),
                  rowt(C_ATT), rowt(C_RWKV), rowt(d), rowt(d),
                  pl.BlockSpec((C_ATT, d), const), pl.BlockSpec((C_RWKV, d), const), pl.BlockSpec((d, d), const)],
        out_specs=pl.BlockSpec((bb, tt, d), xmap),
        out_shape=jax.ShapeDtypeStruct(x.shape, F32),
        compiler_params=_cparams(("parallel",)),
        name="combine",
    )(x, ada, ln_g, ln_b, oa, orw, ga, gr, wa, wr, wo)


def _rearrange_z(a):
    c3 = 3 * C_RWKV
    pad = [(0, 0)] * (a.ndim - 1) + [(0, LORA_PAD - D_DECAY_LORA)]
    return jnp.concatenate([a[..., :c3],
                            jnp.pad(a[..., c3:c3 + D_DECAY_LORA], pad),
                            jnp.pad(a[..., c3 + D_DECAY_LORA:c3 + D_DECAY_LORA + D_AAA_LORA], pad),
                            a[..., c3 + D_DECAY_LORA + D_AAA_LORA:]], axis=-1)


def _unarrange_z(a):
    c3 = 3 * C_RWKV
    return jnp.concatenate([a[..., :c3], a[..., c3:c3 + D_DECAY_LORA],
                            a[..., c3 + LORA_PAD:c3 + LORA_PAD + D_AAA_LORA], a[..., c3 + 2 * LORA_PAD:]], axis=-1)


def _state_to_blocks(state):
    bsz = state.shape[0]
    st = jnp.swapaxes(state, -1, -2).reshape(bsz, H_RWKV // 2, 2, HS_RWKV, HS_RWKV)
    blk = jnp.einsum('bphkv,hg->bphkgv', st, jnp.eye(2, dtype=state.dtype))
    return blk.reshape(bsz, H_RWKV // 2, LANES, LANES)


def _blocks_to_state(blk):
    bsz = blk.shape[0]
    b6 = blk.reshape(bsz, H_RWKV // 2, 2, HS_RWKV, 2, HS_RWKV)
    st = jnp.stack([b6[:, :, 0, :, 0, :], b6[:, :, 1, :, 1, :]], axis=2)
    return jnp.swapaxes(st, -1, -2).reshape(bsz, H_RWKV, HS_RWKV, HS_RWKV)


def _trunk(x, ada, attend, shift0, wkv0, w, *, ffn_blk, mm_tm, pre_tm, comb_blk):
    bsz, t, d = x.shape
    m = bsz * t
    ffn = functools.partial(_ffn_call, ada=ada, ln_g=w["ln_g"], ln_b=w["ln_b"], wi=w["ffn_wi"], wo=w["ffn_wo"],
                            layer=w["layer"], blk=ffn_blk)
    x1, h2 = ffn(x, which=0, sub=0, emit_next=True)
    h2f = h2.reshape(m, d)
    win = w["win"]
    proj = functools.partial(_mm_call, h2f, win, tm=mm_tm)
    q = proj(tn=C_ATT, col0=0, n=C_ATT, name="proj_q").reshape(bsz, t, C_ATT)
    k = proj(tn=C_ATT, col0=C_ATT, n=C_ATT, name="proj_k").reshape(bsz, t, C_ATT)
    v = proj(tn=C_ATT, col0=2 * C_ATT, n=C_ATT, name="proj_v").reshape(bsz, t, C_ATT)
    ga = proj(tn=d // 2, col0=3 * C_ATT, n=d, name="proj_ga")
    gr = proj(tn=d // 2, col0=3 * C_ATT + d, n=d, name="proj_gr")
    o_att = attend(q, k, v)
    r, k2, vv, kk, b, logd, g, z_last = _rwkv_pre_call(h2, win, 3 * C_ATT + 2 * d, _rearrange_z(shift0), w, tm=pre_tm)
    tp = -(-t // CHUNK) * CHUNK
    seq = [r, k2, vv, kk, b, logd, g]
    if tp != t:
        seq = [jnp.pad(a, ((0, 0), (0, tp - t), (0, 0))) for a in seq]
    o_rwkv, s_blk = _rwkv_chunk_call(*seq, _state_to_blocks(wkv0), w)
    o_rwkv = o_rwkv[:, :t].reshape(m, C_RWKV)
    x2 = _combine_call(x1, ada, w["ln_g"], w["ln_b"], o_att.reshape(m, C_ATT), o_rwkv, ga, gr,
                       w["wua"], w["wur"], w["wout"], blk=comb_blk)
    x3 = ffn(x2, which=1, sub=2, emit_next=False)
    shift_new = _unarrange_z(z_last[:, -1, 7])
    return (x3, k.reshape(bsz, t, H_ATT, HD_ATT), v.reshape(bsz, t, H_ATT, HD_ATT), _blocks_to_state(s_blk), shift_new)


def kernel(x_prompt, x_sample, cache_k, cache_v, state_wkv, state_shift, page_table, c_prompt, c_sample, rel_bias, w_ada, b_ada, ln_g, ln_b, ffn_wi, ffn_wo, w_in, mu_shift, w0, w2, a0, a2, g2, k_k, k_a, r_k, lnx_g, lnx_b, w_up_attn, w_up_rwkv, w_out):
    assert w_ada.shape[0] == DEPTH == 1
    bsz, s_len, d = x_prompt.shape
    dbs, ds, _ = x_sample.shape
    past_len = page_table.shape[1] * PAGE_SIZE
    n_phys = cache_k.shape[1]
    l = 0
    win = w_in[l]
    c3 = 3 * C_ATT
    lora_rows = ((0, LORA_PAD - D_DECAY_LORA), (0, 0))
    w = {
        "ln_g": ln_g[l], "ln_b": ln_b[l],
        "layer": l, "ffn_wi": ffn_wi.astype(BF16), "ffn_wo": ffn_wo.astype(BF16),
        "win": jnp.concatenate([win[:, :c3], win[:, c3 + RWKV_PROJ:],
                                _rearrange_z(win[:, c3:c3 + RWKV_PROJ])], axis=1).astype(BF16),
        "mu": _rearrange_z(mu_shift[l])[None, :],
        "w0": w0[l][None, :], "w2": jnp.pad(w2[l], lora_rows).astype(BF16),
        "a0": a0[l][None, :], "a2": jnp.pad(a2[l], lora_rows).astype(BF16),
        "g2": g2[l].astype(BF16), "k_k": k_k[l][None, :], "k_a": k_a[l][None, :],
        "r_k": r_k[l].reshape(1, C_RWKV), "lnx_g": lnx_g[l][None, :], "lnx_b": lnx_b[l][None, :],
        "wua": w_up_attn[l].astype(BF16), "wur": w_up_rwkv[l].astype(BF16), "wout": w_out[l].astype(BF16),
    }
    n_c = bsz + dbs
    c_rows = -(-n_c // 8) * 8
    c_all = jnp.concatenate([c_prompt, c_sample, jnp.zeros((c_rows - n_c, d), F32)], axis=0)
    ada = _ada_call(c_all, w_ada[l], b_ada[l]).reshape(c_rows, N_ADA, d)

    bias_tiles = _bias_tiles_call(rel_bias)
    far = rel_bias[NUM_BUCKETS - 1]
    attend_p = lambda q, k, v: _attn_prompt_call(q, k, v, bias_tiles, far)
    yp, kp, vp, wp, sp = _trunk(
        x_prompt, ada[:bsz], attend_p, jnp.zeros((bsz, RWKV_PROJ), F32),
        jnp.zeros((bsz, H_RWKV, HS_RWKV, HS_RWKV), F32), w,
        ffn_blk=(1, 512), mm_tm=1024, pre_tm=256, comb_blk=(1, 256))

    cache_k2 = cache_k.reshape(DEPTH * n_phys, PAGE_SIZE, H_ATT, HD_ATT)
    cache_v2 = cache_v.reshape(DEPTH * n_phys, PAGE_SIZE, H_ATT, HD_ATT)
    page_table = page_table + l * n_phys
    n_full = past_len // MOBA_BLOCK

    def attend_s(q, k, v):
        means = _cache_means_call(page_table, cache_k2, n_full).reshape(dbs, n_full, C_ATT)
        ids = _topk_ids_call(q, means)[..., :MOBA_TOPK]
        o = _attn_sample_call(page_table, ids.reshape(-1), rel_bias, q, k, v, cache_k2, cache_v2, past_len)
        return o.astype(BF16)

    ys, kn, vn, wn, sn = _trunk(
        x_sample, ada[bsz:n_c], attend_s, state_shift[l], state_wkv[l], w,
        ffn_blk=(dbs, ds), mm_tm=dbs * ds, pre_tm=ds, comb_blk=(dbs, ds))

    return (yp, ys, kp[None], vp[None], kn[None], vn[None], wp[None], wn[None], sp[None], sn[None])
```

```python
import functools
import math

import jax
import jax.numpy as jnp
from jax import lax
from jax.experimental import pallas as pl
from jax.experimental.pallas import tpu as pltpu

F32 = jnp.float32
BF16 = jnp.bfloat16

D_MODEL = 2048
D_FF = 5632
N_ADA = 9
H_ATT = 8
HD_ATT = 128
C_ATT = H_ATT * HD_ATT
MOBA_BLOCK = 256
MOBA_TOPK = 3
NUM_BUCKETS = 32
MAX_DISTANCE = 128
PAGE_SIZE = 128
HS_RWKV = 64
C_RWKV = 1024
H_RWKV = C_RWKV // HS_RWKV
D_DECAY_LORA = 96
D_AAA_LORA = 96
D_GATE_LORA = 256
RWKV_PROJ = 3 * C_RWKV + D_DECAY_LORA + D_AAA_LORA + D_GATE_LORA
LORA_PAD = 128
Z_PAD = 3 * C_RWKV + 2 * LORA_PAD + D_GATE_LORA
GN_EPS = 64e-5
LN_EPS = 1e-5
DEPTH = 1
DEEPNORM_ALPHA = (2 * DEPTH) ** 0.25
NEG_BIG = -1e30

LANES = 128
CHUNK = 64
BATCH_PER_STEP = 2
PAIRS_PER_STEP = 8
VMEM_LIMIT = 56 * 1024 * 1024


def _t5_thresholds():
    max_exact = NUM_BUCKETS // 2
    thr = list(range(1, max_exact + 1))
    for b in range(max_exact + 1, NUM_BUCKETS):
        x = max_exact * (MAX_DISTANCE / max_exact) ** ((b - max_exact) / (NUM_BUCKETS - max_exact))
        thr.append(int(math.ceil(x)))
    return tuple(thr)


T5_THR = _t5_thresholds()


def _cparams(sem, vmem=VMEM_LIMIT):
    return pltpu.CompilerParams(dimension_semantics=sem, vmem_limit_bytes=vmem)


def _ln(x):
    mu = jnp.mean(x, axis=-1, keepdims=True)
    xc = x - mu
    var = jnp.mean(xc * xc, axis=-1, keepdims=True)
    return xc * lax.rsqrt(var + LN_EPS)


def _split(x, n):
    parts = []
    for i in range(n):
        p = x.astype(BF16)
        parts.append(p)
        if i + 1 < n:
            x = x - p.astype(F32)
    return parts


_NN = (((1,), (0,)), ((), ()))
_NT = (((1,), (1,)), ((), ()))
_TN = (((0,), (0,)), ((), ()))


def _dotp(ap, bp, dims=_NN, order=None):
    if order is None:
        order = max(len(ap), len(bp))
    acc = None
    if dims == _NN and len(ap) > 1:
        m = ap[0].shape[0]
        for j, b in enumerate(bp):
            sel = [a for i, a in enumerate(ap) if i + j < order]
            if not sel:
                continue
            lhs = sel[0] if len(sel) == 1 else jnp.concatenate(sel, axis=0)
            t = lax.dot_general(lhs, b, dims, preferred_element_type=F32)
            for r in range(len(sel)):
                part = t[r * m:(r + 1) * m]
                acc = part if acc is None else acc + part
        return acc
    for i, a in enumerate(ap):
        for j, b in enumerate(bp):
            if i + j < order:
                t = lax.dot_general(a, b, dims, preferred_element_type=F32)
                acc = t if acc is None else acc + t
    return acc


def _t5_bias(rel, tbl_ref, h):
    bias = jnp.full(rel.shape, tbl_ref[0, h], F32)
    for b in range(1, NUM_BUCKETS):
        bias = jnp.where(rel >= T5_THR[b - 1], tbl_ref[b, h], bias)
    return bias


def _seg_ones():
    r = lax.broadcasted_iota(jnp.int32, (LANES, LANES), 0) // HS_RWKV
    c = lax.broadcasted_iota(jnp.int32, (LANES, LANES), 1) // HS_RWKV
    return jnp.where(r == c, 1.0, 0.0).astype(BF16)


def _segsum64(x, p128):
    outs = []
    for j in range(x.shape[-1] // LANES):
        xs = x[:, j * LANES:(j + 1) * LANES]
        outs.append(_dotp(_split(xs, 3), [p128]))
    return jnp.concatenate(outs, axis=-1)


ADA_TN = 1024


def _ada_kernel(c_ref, w_ref, b_ref, o_ref):
    c = c_ref[...]
    sc = c * jax.nn.sigmoid(c)
    rows = sc.shape[0]
    s0, s1, s2 = _split(sc, 3)
    w_hi, w_lo = _split(w_ref[...], 2)
    a = jnp.dot(jnp.concatenate([s0, s1, s2], axis=0), w_hi, preferred_element_type=F32)
    b = jnp.dot(jnp.concatenate([s0, s1], axis=0), w_lo, preferred_element_type=F32)
    o_ref[...] = (a[:rows] + a[rows:2 * rows] + a[2 * rows:] + b[:rows] + b[rows:]) + b_ref[...]


def _ada_call(c_all, w_ada, b_ada):
    rows, d = c_all.shape
    n = w_ada.shape[1]
    return pl.pallas_call(
        _ada_kernel,
        grid=(n // ADA_TN,),
        in_specs=[pl.BlockSpec((rows, d), lambda j: (0, 0)),
                  pl.BlockSpec((d, ADA_TN), lambda j: (0, j)),
                  pl.BlockSpec((1, ADA_TN), lambda j: (0, j))],
        out_specs=pl.BlockSpec((rows, ADA_TN), lambda j: (0, j)),
        out_shape=jax.ShapeDtypeStruct((rows, n), F32),
        compiler_params=_cparams(("parallel",)),
        name="ada",
    )(c_all, w_ada, b_ada.reshape(1, n))


FFN_TF = 512
FFN_TF_SMALL_M = 1408


def _ffn_kernel(x_ref, ada_ref, lng_ref, lnb_ref, wig_ref, wiu_ref, wo_ref, *rest, sub, emit_next):
    if emit_next:
        o_ref, h_next_ref, h_scr, acc = rest
    else:
        o_ref, h_scr, acc = rest
    bb, t, d = x_ref.shape
    j = pl.program_id(1)

    @pl.when(j == 0)
    def _():
        h = _ln(x_ref[...]) * (1.0 + ada_ref[:, 3 * sub + 1:3 * sub + 2, :]) + ada_ref[:, 3 * sub:3 * sub + 1, :]
        h_scr[...] = h.reshape(bb * t, d).astype(BF16)
        acc[...] = jnp.zeros_like(acc)

    h = h_scr[...]
    g = jnp.dot(h, wig_ref[...], preferred_element_type=F32)
    u = jnp.dot(h, wiu_ref[...], preferred_element_type=F32)
    act = (g * jax.nn.sigmoid(g)) * u
    acc[...] += jnp.dot(act.astype(BF16), wo_ref[...], preferred_element_type=F32)

    @pl.when(j == pl.num_programs(1) - 1)
    def _():
        gate = ada_ref[:, 3 * sub + 2:3 * sub + 3, :]
        y = DEEPNORM_ALPHA * x_ref[...] + 0.5 * gate * acc[...].reshape(bb, t, d)
        y = _ln(y) * lng_ref[sub:sub + 1, :] + lnb_ref[sub:sub + 1, :]
        o_ref[...] = y
        if emit_next:
            nxt = sub + 1
            hn = _ln(y) * (1.0 + ada_ref[:, 3 * nxt + 1:3 * nxt + 2, :]) + ada_ref[:, 3 * nxt:3 * nxt + 1, :]
            h_next_ref[...] = hn.astype(BF16)


def _ffn_call(x, ada, ln_g, ln_b, wi, wo, *, layer, which, sub, emit_next, blk):
    bsz, t, d = x.shape
    bb, tt = blk
    tf = FFN_TF if bb * tt >= FFN_TF else FFN_TF_SMALL_M
    assert D_FF % tf == 0
    nj = D_FF // tf
    nt = t // tt
    grid = ((bsz // bb) * nt, nj)
    xmap = lambda i, j: (i // nt, i % nt, 0)
    amap = lambda i, j: (i // nt, 0, 0)
    out_shape = [jax.ShapeDtypeStruct(x.shape, F32)]
    out_specs = [pl.BlockSpec((bb, tt, d), xmap)]
    if emit_next:
        out_shape.append(jax.ShapeDtypeStruct(x.shape, BF16))
        out_specs.append(pl.BlockSpec((bb, tt, d), xmap))
    res = pl.pallas_call(
        functools.partial(_ffn_kernel, sub=sub, emit_next=emit_next),
        grid=grid,
        in_specs=[pl.BlockSpec((bb, tt, d), xmap),
                  pl.BlockSpec((bb, N_ADA, d), amap),
                  pl.BlockSpec((3, d), lambda i, j: (0, 0)),
                  pl.BlockSpec((3, d), lambda i, j: (0, 0)),
                  pl.BlockSpec((None, None, d, tf), lambda i, j: (layer, which, 0, j)),
                  pl.BlockSpec((None, None, d, tf), lambda i, j: (layer, which, 0, j + nj)),
                  pl.BlockSpec((None, None, tf, d), lambda i, j: (layer, which, j, 0))],
        out_specs=out_specs,
        out_shape=out_shape,
        scratch_shapes=[pltpu.VMEM((bb * tt, d), BF16), pltpu.VMEM((bb * tt, d), F32)],
        compiler_params=_cparams(("parallel", "arbitrary")),
        name=f"ffn{sub}",
    )(x, ada, ln_g, ln_b, wi, wi, wo)
    return res if emit_next else res[0]


def _mm_kernel(h_ref, w_ref, o_ref):
    o_ref[...] = jnp.dot(h_ref[...], w_ref[...], preferred_element_type=F32)


def _mm_call(h, w, *, tm, tn, col0, n, name="proj"):
    m, k = h.shape
    assert col0 % tn == 0 and n % tn == 0
    cb = col0 // tn
    return pl.pallas_call(
        _mm_kernel,
        grid=(m // tm, n // tn),
        in_specs=[pl.BlockSpec((tm, k), lambda i, j: (i, 0)),
                  pl.BlockSpec((k, tn), lambda i, j: (0, j + cb))],
        out_specs=pl.BlockSpec((tm, tn), lambda i, j: (i, j)),
        out_shape=jax.ShapeDtypeStruct((m, n), F32),
        compiler_params=_cparams(("parallel", "arbitrary")),
        name=name,
    )(h, w)


def _bias_tiles_kernel(tbl_ref, o_ref):
    h = pl.program_id(0)
    r = lax.broadcasted_iota(jnp.int32, (MOBA_BLOCK, MOBA_BLOCK), 0)
    c = lax.broadcasted_iota(jnp.int32, (MOBA_BLOCK, MOBA_BLOCK), 1)
    rel0 = r - c
    o_ref[0, 0] = jnp.where(rel0 >= 0, _t5_bias(rel0, tbl_ref, h), NEG_BIG)
    o_ref[0, 1] = _t5_bias(rel0 + MOBA_BLOCK, tbl_ref, h)


def _bias_tiles_call(rel_bias):
    return pl.pallas_call(
        _bias_tiles_kernel,
        grid=(H_ATT,),
        in_specs=[pl.BlockSpec(memory_space=pltpu.SMEM)],
        out_specs=pl.BlockSpec((1, 2, MOBA_BLOCK, MOBA_BLOCK), lambda h: (h, 0, 0, 0)),
        out_shape=jax.ShapeDtypeStruct((H_ATT, 2, MOBA_BLOCK, MOBA_BLOCK), F32),
        compiler_params=_cparams(("arbitrary",)),
        name="bias_tiles",
    )(rel_bias)


def _rank_select(s, n_valid):
    col = lax.broadcasted_iota(jnp.int32, s.shape, 1)
    cnt = jnp.zeros(s.shape, jnp.int32)
    for m in range(n_valid):
        sm = s[:, m:m + 1]
        beats = (sm > s) | ((sm == s) & (m < col))
        cnt = cnt + jnp.where(beats, 1, 0)
    return (col < n_valid) & (cnt < MOBA_TOPK)


def _attn_prompt_kernel(far_ref, q_ref, k_ref, v_ref, bt_ref, o_ref):
    h = pl.program_id(1)
    s_len = q_ref.shape[1]
    nb = s_len // MOBA_BLOCK
    q = q_ref[0]
    k = k_ref[0]
    kb = k.astype(BF16)
    vb = v_ref[0].astype(BF16)
    qb = (q * (HD_ATT ** -0.5)).astype(BF16)
    means = jnp.mean(k.reshape(nb, MOBA_BLOCK, HD_ATT), axis=1)
    means = jnp.concatenate([means, jnp.zeros((LANES - nb, HD_ATT), F32)], axis=0)
    mparts = _split(means, 3)
    far = far_ref[h]
    rows = [slice(i * MOBA_BLOCK, (i + 1) * MOBA_BLOCK) for i in range(nb)]
    logits = [lax.dot_general(qb[rows[i]], kb[:(i + 1) * MOBA_BLOCK], _NT, preferred_element_type=F32)
              for i in range(nb)]
    scores = [None] + [_dotp(_split(q[rows[i]], 3), mparts, _NT, order=3) for i in range(1, nb)]
    negs = [None] + [jnp.where(_rank_select(scores[i], i), 0.0, NEG_BIG) for i in range(1, nb)]
    probs, dens = [], []
    for i in range(nb):
        tiles = []
        for j in range(i + 1):
            tile = logits[i][:, j * MOBA_BLOCK:(j + 1) * MOBA_BLOCK]
            if j == i:
                tile = tile + bt_ref[0, 0]
            elif j == i - 1:
                tile = tile + bt_ref[0, 1] + negs[i][:, j:j + 1]
            else:
                tile = tile + (far + negs[i][:, j:j + 1])
            tiles.append(tile)
        lg = jnp.concatenate(tiles, axis=-1) if len(tiles) > 1 else tiles[0]
        mx = jnp.max(lg, axis=-1, keepdims=True)
        p = jnp.exp(lg - mx)
        dens.append(jnp.sum(p, axis=-1, keepdims=True))
        probs.append(p.astype(BF16))
    for i in range(nb):
        out = jnp.dot(probs[i], vb[:(i + 1) * MOBA_BLOCK], preferred_element_type=F32) / dens[i]
        o_ref[0, rows[i], :] = out.astype(o_ref.dtype)


def _attn_prompt_call(q, k, v, bias_tiles, far):
    bsz, s_len, _ = q.shape
    qspec = pl.BlockSpec((1, s_len, HD_ATT), lambda b, h: (b, 0, h))
    return pl.pallas_call(
        _attn_prompt_kernel,
        grid=(bsz, H_ATT),
        in_specs=[pl.BlockSpec(memory_space=pltpu.SMEM), qspec, qspec, qspec,
                  pl.BlockSpec((1, 2, MOBA_BLOCK, MOBA_BLOCK), lambda b, h: (h, 0, 0, 0))],
        out_specs=pl.BlockSpec((1, s_len, HD_ATT), lambda b, h: (b, 0, h)),
        out_shape=jax.ShapeDtypeStruct((bsz, s_len, C_ATT), BF16),
        compiler_params=_cparams(("parallel", "arbitrary")),
        name="attn_prompt",
    )(far, q, k, v, bias_tiles)


MEAN_BLOCKS = 16
PAGES_PER_BLOCK = MOBA_BLOCK // PAGE_SIZE


def _cache_means_kernel(pt_ref, *refs):
    page_refs, o_ref = refs[:-1], refs[-1]
    for m in range(MEAN_BLOCKS):
        tot = None
        for u in range(PAGES_PER_BLOCK):
            s = jnp.sum(page_refs[m * PAGES_PER_BLOCK + u][0], axis=0)
            tot = s if tot is None else tot + s
        o_ref[0, m] = tot * (1.0 / MOBA_BLOCK)


def _cache_means_call(page_table, cache_k4, n_blocks):
    dbs = page_table.shape[0]
    npg = MEAN_BLOCKS * PAGES_PER_BLOCK
    in_specs = [pl.BlockSpec((1, PAGE_SIZE, H_ATT, HD_ATT), functools.partial(
        lambda b, g, pt, u: (pt[b, g * npg + u], 0, 0, 0), u=u)) for u in range(npg)]
    return pl.pallas_call(
        _cache_means_kernel,
        grid_spec=pltpu.PrefetchScalarGridSpec(
            num_scalar_prefetch=1,
            grid=(dbs, n_blocks // MEAN_BLOCKS),
            in_specs=in_specs,
            out_specs=pl.BlockSpec((1, MEAN_BLOCKS, H_ATT, HD_ATT), lambda b, g, pt: (b, g, 0, 0)),
        ),
        out_shape=jax.ShapeDtypeStruct((dbs, n_blocks, H_ATT, HD_ATT), F32),
        compiler_params=_cparams(("parallel", "arbitrary")),
        name="cache_means",
    )(page_table, *([cache_k4] * npg))


def _topk_ids_kernel(q_ref, m_ref, o_ref):
    q = q_ref[0]
    means = m_ref[0]
    n_blocks = means.shape[0]
    ds = q.shape[0]
    pad = jnp.zeros((LANES - n_blocks, HD_ATT), F32)
    col = lax.broadcasted_iota(jnp.int32, (ds, LANES), 1)
    for h in range(H_ATT):
        sl = slice(h * HD_ATT, (h + 1) * HD_ATT)
        mh = jnp.concatenate([means[:, sl], pad], axis=0) if n_blocks < LANES else means[:, sl]
        s = _dotp(_split(q[:, sl], 3), _split(mh, 3), _NT, order=3)
        s = jnp.where(col < n_blocks, s, NEG_BIG)
        ids = jnp.zeros((ds, LANES), jnp.int32)
        for t in range(MOBA_TOPK):
            mx = jnp.max(s, axis=-1, keepdims=True)
            idx = jnp.min(jnp.where(s == mx, col, LANES), axis=-1, keepdims=True)
            ids = jnp.where(col == t, idx, ids)
            s = jnp.where(col == idx, NEG_BIG * 2, s)
        o_ref[0, h] = ids


def _topk_ids_call(q, means):
    dbs, ds, _ = q.shape
    n_blocks = means.shape[1]
    return pl.pallas_call(
        _topk_ids_kernel,
        grid=(dbs,),
        in_specs=[pl.BlockSpec((1, ds, C_ATT), lambda b: (b, 0, 0)),
                  pl.BlockSpec((1, n_blocks, C_ATT), lambda b: (b, 0, 0))],
        out_specs=pl.BlockSpec((1, H_ATT, ds, LANES), lambda b: (b, 0, 0, 0)),
        out_shape=jax.ShapeDtypeStruct((dbs, H_ATT, ds, LANES), jnp.int32),
        compiler_params=_cparams(("parallel",)),
        name="topk_ids",
    )(q, means)


N_SEL_PAGES = MOBA_TOPK * PAGES_PER_BLOCK


def _attn_sample_kernel(pt_ref, ids_ref, tbl_ref, q_ref, kn_ref, vn_ref, ck_hbm, cv_hbm, o_ref,
                        kbuf, vbuf, sems, *, past_len):
    ds = q_ref.shape[1]
    n_pg = ds * N_SEL_PAGES
    b, h = pl.program_id(0), pl.program_id(1)
    nh = pl.num_programs(1)
    step = b * nh + h
    n_steps = pl.num_programs(0) * nh
    slot = step % 2

    def page_copies(bb, hh, sl):
        base = (bb * H_ATT + hh) * ds * MOBA_TOPK
        cps = []
        for u in range(n_pg):
            pg = pt_ref[bb, ids_ref[base + u // PAGES_PER_BLOCK] * PAGES_PER_BLOCK + u % PAGES_PER_BLOCK]
            cps.append(pltpu.make_async_copy(ck_hbm.at[pg, :, hh, :], kbuf.at[sl, u], sems.at[sl, u]))
            cps.append(pltpu.make_async_copy(cv_hbm.at[pg, :, hh, :], vbuf.at[sl, u], sems.at[sl, n_pg + u]))
        return cps

    @pl.when(step == 0)
    def _():
        for cp in page_copies(b, h, slot):
            cp.start()

    @pl.when(step + 1 < n_steps)
    def _():
        nxt = step + 1
        for cp in page_copies(nxt // nh, nxt % nh, 1 - slot):
            cp.start()

    for cp in page_copies(b, h, slot):
        cp.wait()
    kp_refs = [kbuf.at[slot, u] for u in range(n_pg)]
    vp_refs = [vbuf.at[slot, u] for u in range(n_pg)]
    qb = (q_ref[0] * (HD_ATT ** -0.5)).astype(BF16)
    qrow = lax.broadcasted_iota(jnp.int32, (ds, 1), 0)
    lane = lax.broadcasted_iota(jnp.int32, (ds, PAGE_SIZE), 1)
    lo = lax.dot_general(qb, kn_ref[0].astype(BF16), _NT, preferred_element_type=F32)
    rel_o = lax.broadcasted_iota(jnp.int32, (ds, ds), 0) - lax.broadcasted_iota(jnp.int32, (ds, ds), 1)
    lo = jnp.where(rel_o >= 0, lo + _t5_bias(rel_o, tbl_ref, h), NEG_BIG)
    mx_o = jnp.max(lo, axis=-1, keepdims=True)
    vn = vn_ref[0].astype(BF16)
    tiles = []
    for qi in range(ds):
        flat = (b * H_ATT + h) * ds + qi
        tq = []
        for s in range(MOBA_TOPK):
            blk = ids_ref[flat * MOBA_TOPK + s]
            for u in range(PAGES_PER_BLOCK):
                kpg = kp_refs[(qi * MOBA_TOPK + s) * PAGES_PER_BLOCK + u][...].astype(BF16)
                lg = lax.dot_general(qb, kpg, _NT, preferred_element_type=F32)
                rel = (past_len + qi) - (blk * MOBA_BLOCK + u * PAGE_SIZE + lane)
                tq.append(lg + _t5_bias(rel, tbl_ref, h))
        tiles.append(tq)
    mxs = []
    for tq in tiles:
        mx = mx_o
        for t in tq:
            mx = jnp.maximum(mx, jnp.max(t, axis=-1, keepdims=True))
        mxs.append(mx)
    pos = [jnp.exp(lo - mx) for mx in mxs]
    ps = [[jnp.exp(t - mx) for t in tq] for tq, mx in zip(tiles, mxs)]
    accs = [jnp.dot(po.astype(BF16), vn, preferred_element_type=F32) for po in pos]
    for qi in range(ds):
        for i in range(N_SEL_PAGES):
            accs[qi] = accs[qi] + jnp.dot(ps[qi][i].astype(BF16), vp_refs[qi * N_SEL_PAGES + i][...].astype(BF16),
                                          preferred_element_type=F32)
    result = jnp.zeros((ds, HD_ATT), F32)
    for qi in range(ds):
        den = jnp.sum(pos[qi], axis=-1, keepdims=True)
        for p in ps[qi]:
            den = den + jnp.sum(p, axis=-1, keepdims=True)
        result = jnp.where(qrow == qi, accs[qi] / den, result)
    o_ref[0] = result


def _attn_sample_call(page_table, ids, rel_bias, q, k_new, v_new, cache_k2, cache_v2, past_len):
    dbs, ds, _ = q.shape
    n_pg = ds * N_SEL_PAGES

    nspec = pl.BlockSpec((1, ds, HD_ATT), lambda b, h, p, i: (b, 0, h))
    pool = pl.BlockSpec(memory_space=pl.ANY)
    return pl.pallas_call(
        functools.partial(_attn_sample_kernel, past_len=past_len),
        grid_spec=pltpu.PrefetchScalarGridSpec(
            num_scalar_prefetch=2,
            grid=(dbs, H_ATT),
            in_specs=[pl.BlockSpec(memory_space=pltpu.SMEM), nspec, nspec, nspec, pool, pool],
            out_specs=pl.BlockSpec((1, ds, HD_ATT), lambda b, h, p, i: (b, 0, h)),
            scratch_shapes=[pltpu.VMEM((2, n_pg, PAGE_SIZE, HD_ATT), F32),
                            pltpu.VMEM((2, n_pg, PAGE_SIZE, HD_ATT), F32),
                            pltpu.SemaphoreType.DMA((2, 2 * n_pg))],
        ),
        out_shape=jax.ShapeDtypeStruct((dbs, ds, C_ATT), F32),
        compiler_params=_cparams(("arbitrary", "arbitrary")),
        name="attn_sample",
    )(page_table, ids, rel_bias, q, k_new, v_new, cache_k2, cache_v2)


def _rwkv_pre_kernel(h_ref, wz_ref, sh0_ref, mu_ref, w0_ref, w2_ref, a0_ref, a2_ref, g2_ref, kk_ref, ka_ref,
                     r_o, k_o, v_o, kk_o, b_o, ld_o, g_o, zl_o, carry):
    i = pl.program_id(1)
    h = h_ref[0]
    tm = h.shape[0]
    c = C_RWKV
    row = lax.broadcasted_iota(jnp.int32, (tm, 1), 0)

    @pl.when(i == 0)
    def _():
        carry[...] = sh0_ref[0]

    def project(lo, hi):
        z = jnp.dot(h, wz_ref[:, lo:hi], preferred_element_type=F32)
        first = carry[:, lo:hi]
        carry[:, lo:hi] = z[tm - 1:tm, :]
        zl_o[0, 0, :, lo:hi] = z[tm - 8:tm, :]
        zprev = jnp.where(row == 0, first, pltpu.roll(z, 1, 0))
        return z + (zprev - z) * mu_ref[:, lo:hi]

    zl = project(3 * c, Z_PAD)
    zk = project(c, 2 * c)
    zw, za, zg = zl[:, :LORA_PAD], zl[:, LORA_PAD:2 * LORA_PAD], zl[:, 2 * LORA_PAD:]
    wl = w0_ref[...] + jnp.dot(jnp.tanh(zw).astype(BF16), w2_ref[...], preferred_element_type=F32)
    w = -(jnp.maximum(-wl, 0.0) + jnp.log(1.0 + jnp.exp(-jnp.abs(wl)))) - 0.5
    ld_o[0] = -jnp.exp(w)
    a = jax.nn.sigmoid(a0_ref[...] + jnp.dot(za.astype(BF16), a2_ref[...], preferred_element_type=F32))
    g_o[0] = jnp.dot(jax.nn.sigmoid(zg).astype(BF16), g2_ref[...], preferred_element_type=F32)
    zr = project(0, c)
    kkr = zk * kk_ref[...]
    n2 = _segsum64(kkr * kkr, _seg_ones())
    kkn = kkr / jnp.maximum(jnp.sqrt(n2), 1e-12)
    k_o[0] = zk * (1.0 + (a - 1.0) * ka_ref[...])
    kk_o[0] = kkn
    b_o[0] = kkn * a
    zv = project(2 * c, 3 * c)
    r_o[0] = zr
    v_o[0] = zv


def _rwkv_pre_call(h, win, z_col0, shift0, p, *, tm):
    bsz, t, d = h.shape
    zp = Z_PAD
    assert z_col0 % zp == 0
    nt = t // tm
    c = C_RWKV
    row = lambda b, i: (0, 0)
    vec = lambda n: pl.BlockSpec((1, n), row)
    tile = pl.BlockSpec((1, tm, c), lambda b, i: (b, i, 0))
    return pl.pallas_call(
        _rwkv_pre_kernel,
        grid=(bsz, nt),
        in_specs=[pl.BlockSpec((1, tm, d), lambda b, i: (b, i, 0)),
                  pl.BlockSpec((d, zp), lambda b, i: (0, z_col0 // zp)),
                  pl.BlockSpec((1, 1, zp), lambda b, i: (b, 0, 0)),
                  vec(zp), vec(c),
                  pl.BlockSpec((LORA_PAD, c), row), vec(c),
                  pl.BlockSpec((LORA_PAD, c), row),
                  pl.BlockSpec((D_GATE_LORA, c), row), vec(c), vec(c)],
        out_specs=[tile] * 7 + [pl.BlockSpec((1, 1, 8, zp), lambda b, i: (b, i, 0, 0))],
        out_shape=[jax.ShapeDtypeStruct((bsz, t, c), F32)] * 7 + [jax.ShapeDtypeStruct((bsz, nt, 8, zp), F32)],
        scratch_shapes=[pltpu.VMEM((1, zp), F32)],
        compiler_params=_cparams(("parallel", "arbitrary")),
        name="rwkv_pre",
    )(h, win, shift0.reshape(bsz, 1, zp), p["mu"], p["w0"], p["w2"], p["a0"], p["a2"], p["g2"], p["k_k"], p["k_a"])


SOLVE_BASE = 8
CHUNK_PREC = {"gram": (1, 1), "state_read": (1, 1), "mkv": (1, 1), "solve1": (1, 1), "solve_sq": (1, 1),
              "solve_ap": (1, 1), "out": (1, 1), "state_upd": (1, 1)}


def _chunk_pairs(rs, ks, vs, kks, bs, lds, ss):
    c = CHUNK
    lane = lax.broadcasted_iota(jnp.int32, (1, LANES), 1)
    m_a = jnp.where(lane < HS_RWKV, 1.0, 0.0)
    m_b = 1.0 - m_a
    row = lax.broadcasted_iota(jnp.int32, (c, 2 * c), 0)
    coli = lax.broadcasted_iota(jnp.int32, (c, 2 * c), 1) % c
    strict = coli < row
    incl = coli <= row
    lr = lax.broadcasted_iota(jnp.int32, (c, c), 0)
    lc = lax.broadcasted_iota(jnp.int32, (c, c), 1)
    ltri = jnp.where(lc <= lr, 1.0, 0.0).astype(BF16)
    ones = jnp.ones((c, LANES), BF16)
    rr = lax.broadcasted_iota(jnp.int32, (LANES, LANES), 0) // HS_RWKV
    cc = lax.broadcasted_iota(jnp.int32, (LANES, LANES), 1) // HS_RWKV
    same_head = rr == cc

    def each(f, *lists):
        return [f(*args) for args in zip(*lists)]

    def stack2(x):
        return jnp.concatenate([x * m_a, x * m_b], axis=0)

    def prod(site, a, b, dims=_NN):
        na, nb = CHUNK_PREC[site]
        return _dotp(_split(a, na), _split(b, nb), dims, order=max(na, nb))

    def pm(site, mcats, xs):
        return each(lambda m, x: prod(site, m, stack2(x)), mcats, xs)

    ldp = each(lambda x: _split(x, 2), lds)
    cums = each(lambda p: _dotp([ltri], p, _NN), ldp)
    gcols = each(lambda p: jnp.exp(_dotp(p, [ones], _TN)), ldp)
    g_inv = each(lambda cu: jnp.exp(-cu), cums)
    g_end = each(lambda cu: jnp.exp(cu[c - 1:c, :] - cu), cums)
    p_all = each(lambda kk, r, cu, ld: jnp.concatenate([kk * jnp.exp(cu - ld), r * jnp.exp(cu)], axis=0),
                 kks, rs, cums, lds)
    z2 = each(lambda k, b, gi: jnp.concatenate([stack2(k * gi), stack2(b * gi)], axis=0), ks, bs, g_inv)
    g4 = each(lambda p, z: prod("gram", p, z, _NT), p_all, z2)
    mk = each(lambda g: jnp.where(strict, g[:c, :2 * c], 0.0), g4)
    pj = each(lambda g: jnp.where(strict, -g[:c, 2 * c:], 0.0), g4)
    akb = each(lambda g: jnp.concatenate([jnp.where(incl, g[c:, :2 * c], 0.0),
                                          jnp.where(incl, -g[c:, 2 * c:], 0.0)], axis=1), g4)
    ps = each(lambda p, s: prod("state_read", p, s), p_all, ss)
    mkv = pm("mkv", mk, vs)
    rhs = each(lambda p, m: p[:c] + m, ps, mkv)
    same_blk = lambda s: (row // s) == (coli // s)
    eye = jnp.where(coli == row, 1.0, 0.0)
    pk = each(lambda n_: jnp.where(same_blk(SOLVE_BASE), n_, 0.0), pj)
    ts = each(lambda p: eye + p, pk)
    n = 2
    while n < SOLVE_BASE:
        pk = pm("solve_sq", pk, pk)
        ts = each(lambda t, d: t + d, ts, pm("solve_ap", ts, pk))
        n *= 2
    s = SOLVE_BASE
    while s < c:
        low = same_blk(2 * s) & jnp.logical_not(same_blk(s))
        cs = each(lambda n_: jnp.where(low, -n_, 0.0), pj)
        ts = each(lambda t, d: t - d, ts, pm("solve_ap", ts, pm("solve_sq", cs, ts)))
        s *= 2
    us = pm("solve1", ts, rhs)
    ys = each(lambda p, m, v, u: p[c:] + prod("out", m, jnp.concatenate([stack2(v), stack2(u)], axis=0)),
              ps, akb, vs, us)
    upd = each(lambda k, b, ge, v, u: prod("state_upd", jnp.concatenate([k * ge, -(b * ge)], axis=0),
                                           jnp.concatenate([v, u], axis=0), _TN), ks, bs, g_end, vs, us)
    s_new = each(lambda gc, s, up: gc * s + jnp.where(same_head, up, 0.0), gcols, ss, upd)
    return ys, s_new


def _rwkv_out_pairs(ys, rs, ks, vs, gs, rks, lgs, lbs):
    p128 = _seg_ones()
    c = CHUNK
    inv = 1.0 / HS_RWKV

    def seg3(x):
        return _split(x, 3)

    def sum3(t, o):
        return t[o * c:(o + 1) * c] + t[(o + 1) * c:(o + 2) * c] + t[(o + 2) * c:(o + 3) * c]

    t1 = [jnp.dot(jnp.concatenate(seg3(y) + seg3(r * k * rk), axis=0), p128, preferred_element_type=F32)
          for y, r, k, rk in zip(ys, rs, ks, rks)]
    ycs = [y - sum3(t, 0) * inv for y, t in zip(ys, t1)]
    t2 = [jnp.dot(jnp.concatenate(seg3(yc * yc), axis=0), p128, preferred_element_type=F32) for yc in ycs]
    outs = []
    for yc, ta, tb, v, g, lg, lb in zip(ycs, t1, t2, vs, gs, lgs, lbs):
        yn = yc * lax.rsqrt(sum3(tb, 0) * inv + GN_EPS) * lg + lb
        outs.append(((yn + sum3(ta, 3) * v) * g).astype(BF16))
    return outs


def _rwkv_chunk_kernel(r_ref, k_ref, v_ref, kk_ref, b_ref, ld_ref, g_ref, rk_ref, lg_ref, lb_ref, s0_ref,
                       o_ref, so_ref, s_scr):
    ci = pl.program_id(2)

    @pl.when(ci == 0)
    def _():
        s_scr[...] = s0_ref[...]

    nb = r_ref.shape[0]
    units = [(bi, p, slice(p * LANES, (p + 1) * LANES)) for bi in range(nb) for p in range(PAIRS_PER_STEP)]
    pairs = lambda ref: [ref[bi, :, sl] for bi, _, sl in units]
    vecs = lambda ref: [ref[:, sl] for _, _, sl in units]
    rs, ks, vs = pairs(r_ref), pairs(k_ref), pairs(v_ref)
    ys, s_new = _chunk_pairs(rs, ks, vs, pairs(kk_ref), pairs(b_ref), pairs(ld_ref),
                             [s_scr[bi, p] for bi, p, _ in units])
    outs = _rwkv_out_pairs(ys, rs, ks, vs, pairs(g_ref), vecs(rk_ref), vecs(lg_ref), vecs(lb_ref))
    for u, (bi, p, sl) in enumerate(units):
        o_ref[bi, :, sl] = outs[u]
        s_scr[bi, p] = s_new[u]
        so_ref[bi, p] = s_new[u]


def _rwkv_chunk_call(r, k, v, kk, b, logd, g, s0_blk, p):
    bsz, t, c = r.shape
    pp = PAIRS_PER_STEP
    nb = BATCH_PER_STEP
    assert bsz % nb == 0
    w = pp * LANES
    tile = pl.BlockSpec((nb, CHUNK, w), lambda bi, pi, ci: (bi, ci, pi))
    vec = pl.BlockSpec((1, w), lambda bi, pi, ci: (0, pi))
    sspec = pl.BlockSpec((nb, pp, LANES, LANES), lambda bi, pi, ci: (bi, pi, 0, 0))
    return pl.pallas_call(
        _rwkv_chunk_kernel,
        grid=(bsz // nb, c // w, t // CHUNK),
        in_specs=[tile] * 7 + [vec] * 3 + [sspec],
        out_specs=[tile, sspec],
        out_shape=[jax.ShapeDtypeStruct((bsz, t, c), BF16),
                   jax.ShapeDtypeStruct(s0_blk.shape, F32)],
        scratch_shapes=[pltpu.VMEM((nb, pp, LANES, LANES), F32)],
        compiler_params=_cparams(("parallel", "parallel", "arbitrary")),
        name="rwkv_chunk",
    )(r, k, v, kk, b, logd, g, p["r_k"], p["lnx_g"], p["lnx_b"], s0_blk)


def _combine_kernel(x_ref, ada_ref, lng_ref, lnb_ref, oa_ref, or_ref, ga_ref, gr_ref, wa_ref, wr_ref, wo_ref, o_ref):
    bb, t, d = x_ref.shape
    ua = jnp.dot(oa_ref[...], wa_ref[...], preferred_element_type=F32)
    ur = jnp.dot(or_ref[...], wr_ref[...], preferred_element_type=F32)
    m = jax.nn.sigmoid(ga_ref[...]) * ua + jax.nn.sigmoid(gr_ref[...]) * ur
    mo = jnp.dot(m.astype(BF16), wo_ref[...], preferred_element_type=F32)
    y = DEEPNORM_ALPHA * x_ref[...] + ada_ref[:, 5:6, :] * mo.reshape(bb, t, d)
    o_ref[...] = _ln(y) * lng_ref[1:2, :] + lnb_ref[1:2, :]


def _combine_call(x, ada, ln_g, ln_b, oa, orw, ga, gr, wa, wr, wo, *, blk):
    bsz, t, d = x.shape
    bb, tt = blk
    nt = t // tt
    rows = bb * tt
    xmap = lambda i: (i // nt, i % nt, 0)
    const = lambda i: (0, 0)
    rowt = lambda n: pl.BlockSpec((rows, n), lambda i: (i, 0))
    return pl.pallas_call(
        _combine_kernel,
        grid=((bsz // bb) * nt,),
        in_specs=[pl.BlockSpec((bb, tt, d), xmap),
                  pl.BlockSpec((bb, N_ADA, d), lambda i: (i // nt, 0, 0)),
                  pl.BlockSpec((3, d), const), pl.BlockSpec((3, d), const),
                  rowt(C_ATT), rowt(C_RWKV), rowt(d), rowt(d),
                  pl.BlockSpec((C_ATT, d), const), pl.BlockSpec((C_RWKV, d), const), pl.BlockSpec((d, d), const)],
        out_specs=pl.BlockSpec((bb, tt, d), xmap),
        out_shape=jax.ShapeDtypeStruct(x.shape, F32),
        compiler_params=_cparams(("parallel",)),
        name="combine",
    )(x, ada, ln_g, ln_b, oa, orw, ga, gr, wa, wr, wo)


def _rearrange_z(a):
    c3 = 3 * C_RWKV
    pad = [(0, 0)] * (a.ndim - 1) + [(0, LORA_PAD - D_DECAY_LORA)]
    return jnp.concatenate([a[..., :c3],
                            jnp.pad(a[..., c3:c3 + D_DECAY_LORA], pad),
                            jnp.pad(a[..., c3 + D_DECAY_LORA:c3 + D_DECAY_LORA + D_AAA_LORA], pad),
                            a[..., c3 + D_DECAY_LORA + D_AAA_LORA:]], axis=-1)


def _unarrange_z(a):
    c3 = 3 * C_RWKV
    return jnp.concatenate([a[..., :c3], a[..., c3:c3 + D_DECAY_LORA],
                            a[..., c3 + LORA_PAD:c3 + LORA_PAD + D_AAA_LORA], a[..., c3 + 2 * LORA_PAD:]], axis=-1)


def _state_to_blocks(state):
    bsz = state.shape[0]
    st = jnp.swapaxes(state, -1, -2).reshape(bsz, H_RWKV // 2, 2, HS_RWKV, HS_RWKV)
    blk = jnp.einsum('bphkv,hg->bphkgv', st, jnp.eye(2, dtype=state.dtype))
    return blk.reshape(bsz, H_RWKV // 2, LANES, LANES)


def _blocks_to_state(blk):
    bsz = blk.shape[0]
    b6 = blk.reshape(bsz, H_RWKV // 2, 2, HS_RWKV, 2, HS_RWKV)
    st = jnp.stack([b6[:, :, 0, :, 0, :], b6[:, :, 1, :, 1, :]], axis=2)
    return jnp.swapaxes(st, -1, -2).reshape(bsz, H_RWKV, HS_RWKV, HS_RWKV)


def _trunk(x, ada, attend, shift0, wkv0, w, *, ffn_blk, mm_tm, pre_tm, comb_blk):
    bsz, t, d = x.shape
    m = bsz * t
    ffn = functools.partial(_ffn_call, ada=ada, ln_g=w["ln_g"], ln_b=w["ln_b"], wi=w["ffn_wi"], wo=w["ffn_wo"],
                            layer=w["layer"], blk=ffn_blk)
    x1, h2 = ffn(x, which=0, sub=0, emit_next=True)
    h2f = h2.reshape(m, d)
    win = w["win"]
    proj = functools.partial(_mm_call, h2f, win, tm=mm_tm)
    q = proj(tn=C_ATT, col0=0, n=C_ATT, name="proj_q").reshape(bsz, t, C_ATT)
    k = proj(tn=C_ATT, col0=C_ATT, n=C_ATT, name="proj_k").reshape(bsz, t, C_ATT)
    v = proj(tn=C_ATT, col0=2 * C_ATT, n=C_ATT, name="proj_v").reshape(bsz, t, C_ATT)
    ga = proj(tn=d // 2, col0=3 * C_ATT, n=d, name="proj_ga")
    gr = proj(tn=d // 2, col0=3 * C_ATT + d, n=d, name="proj_gr")
    o_att = attend(q, k, v)
    r, k2, vv, kk, b, logd, g, z_last = _rwkv_pre_call(h2, win, 3 * C_ATT + 2 * d, _rearrange_z(shift0), w, tm=pre_tm)
    tp = -(-t // CHUNK) * CHUNK
    seq = [r, k2, vv, kk, b, logd, g]
    if tp != t:
        seq = [jnp.pad(a, ((0, 0), (0, tp - t), (0, 0))) for a in seq]
    o_rwkv, s_blk = _rwkv_chunk_call(*seq, _state_to_blocks(wkv0), w)
    o_rwkv = o_rwkv[:, :t].reshape(m, C_RWKV)
    x2 = _combine_call(x1, ada, w["ln_g"], w["ln_b"], o_att.reshape(m, C_ATT), o_rwkv, ga, gr,
                       w["wua"], w["wur"], w["wout"], blk=comb_blk)
    x3 = ffn(x2, which=1, sub=2, emit_next=False)
    shift_new = _unarrange_z(z_last[:, -1, 7])
    return (x3, k.reshape(bsz, t, H_ATT, HD_ATT), v.reshape(bsz, t, H_ATT, HD_ATT), _blocks_to_state(s_blk), shift_new)


def kernel(x_prompt, x_sample, cache_k, cache_v, state_wkv, state_shift, page_table, c_prompt, c_sample, rel_bias, w_ada, b_ada, ln_g, ln_b, ffn_wi, ffn_wo, w_in, mu_shift, w0, w2, a0, a2, g2, k_k, k_a, r_k, lnx_g, lnx_b, w_up_attn, w_up_rwkv, w_out):
    assert w_ada.shape[0] == DEPTH == 1
    bsz, s_len, d = x_prompt.shape
    dbs, ds, _ = x_sample.shape
    past_len = page_table.shape[1] * PAGE_SIZE
    n_phys = cache_k.shape[1]
    l = 0
    win = w_in[l]
    c3 = 3 * C_ATT
    lora_rows = ((0, LORA_PAD - D_DECAY_LORA), (0, 0))
    w = {
        "ln_g": ln_g[l], "ln_b": ln_b[l],
        "layer": l, "ffn_wi": ffn_wi.astype(BF16), "ffn_wo": ffn_wo.astype(BF16),
        "win": jnp.concatenate([win[:, :c3], win[:, c3 + RWKV_PROJ:],
                                _rearrange_z(win[:, c3:c3 + RWKV_PROJ])], axis=1).astype(BF16),
        "mu": _rearrange_z(mu_shift[l])[None, :],
        "w0": w0[l][None, :], "w2": jnp.pad(w2[l], lora_rows).astype(BF16),
        "a0": a0[l][None, :], "a2": jnp.pad(a2[l], lora_rows).astype(BF16),
        "g2": g2[l].astype(BF16), "k_k": k_k[l][None, :], "k_a": k_a[l][None, :],
        "r_k": r_k[l].reshape(1, C_RWKV), "lnx_g": lnx_g[l][None, :], "lnx_b": lnx_b[l][None, :],
        "wua": w_up_attn[l].astype(BF16), "wur": w_up_rwkv[l].astype(BF16), "wout": w_out[l].astype(BF16),
    }
    n_c = bsz + dbs
    c_rows = -(-n_c // 8) * 8
    c_all = jnp.concatenate([c_prompt, c_sample, jnp.zeros((c_rows - n_c, d), F32)], axis=0)
    ada = _ada_call(c_all, w_ada[l], b_ada[l]).reshape(c_rows, N_ADA, d)

    bias_tiles = _bias_tiles_call(rel_bias)
    far = rel_bias[NUM_BUCKETS - 1]
    attend_p = lambda q, k, v: _attn_prompt_call(q, k, v, bias_tiles, far)
    yp, kp, vp, wp, sp = _trunk(
        x_prompt, ada[:bsz], attend_p, jnp.zeros((bsz, RWKV_PROJ), F32),
        jnp.zeros((bsz, H_RWKV, HS_RWKV, HS_RWKV), F32), w,
        ffn_blk=(1, 512), mm_tm=1024, pre_tm=256, comb_blk=(1, 256))

    cache_k2 = cache_k.reshape(DEPTH * n_phys, PAGE_SIZE, H_ATT, HD_ATT)
    cache_v2 = cache_v.reshape(DEPTH * n_phys, PAGE_SIZE, H_ATT, HD_ATT)
    page_table = page_table + l * n_phys
    n_full = past_len // MOBA_BLOCK

    def attend_s(q, k, v):
        means = _cache_means_call(page_table, cache_k2, n_full).reshape(dbs, n_full, C_ATT)
        ids = _topk_ids_call(q, means)[..., :MOBA_TOPK]
        o = _attn_sample_call(page_table, ids.reshape(-1), rel_bias, q, k, v, cache_k2, cache_v2, past_len)
        return o.astype(BF16)

    ys, kn, vn, wn, sn = _trunk(
        x_sample, ada[bsz:n_c], attend_s, state_shift[l], state_wkv[l], w,
        ffn_blk=(dbs, ds), mm_tm=dbs * ds, pre_tm=ds, comb_blk=(dbs, ds))

    return (yp, ys, kp[None], vp[None], kn[None], vn[None], wp[None], wn[None], sp[None], sn[None])
```

```python
import functools
import math

import jax
import jax.numpy as jnp
from jax import lax
from jax.experimental import pallas as pl
from jax.experimental.pallas import tpu as pltpu

F32 = jnp.float32
BF16 = jnp.bfloat16

D_MODEL = 2048
D_FF = 5632
N_ADA = 9
H_ATT = 8
HD_ATT = 128
C_ATT = H_ATT * HD_ATT
MOBA_BLOCK = 256
MOBA_TOPK = 3
NUM_BUCKETS = 32
MAX_DISTANCE = 128
PAGE_SIZE = 128
HS_RWKV = 64
C_RWKV = 1024
H_RWKV = C_RWKV // HS_RWKV
D_DECAY_LORA = 96
D_AAA_LORA = 96
D_GATE_LORA = 256
RWKV_PROJ = 3 * C_RWKV + D_DECAY_LORA + D_AAA_LORA + D_GATE_LORA
LORA_PAD = 128
Z_PAD = 3 * C_RWKV + 2 * LORA_PAD + D_GATE_LORA
GN_EPS = 64e-5
LN_EPS = 1e-5
DEPTH = 1
DEEPNORM_ALPHA = (2 * DEPTH) ** 0.25
NEG_BIG = -1e30

LANES = 128
CHUNK = 64
BATCH_PER_STEP = 2
PAIRS_PER_STEP = 8
VMEM_LIMIT = 56 * 1024 * 1024


def _t5_thresholds():
    max_exact = NUM_BUCKETS // 2
    thr = list(range(1, max_exact + 1))
    for b in range(max_exact + 1, NUM_BUCKETS):
        x = max_exact * (MAX_DISTANCE / max_exact) ** ((b - max_exact) / (NUM_BUCKETS - max_exact))
        thr.append(int(math.ceil(x)))
    return tuple(thr)


T5_THR = _t5_thresholds()


def _cparams(sem, vmem=VMEM_LIMIT):
    return pltpu.CompilerParams(dimension_semantics=sem, vmem_limit_bytes=vmem)


def _ln(x):
    mu = jnp.mean(x, axis=-1, keepdims=True)
    xc = x - mu
    var = jnp.mean(xc * xc, axis=-1, keepdims=True)
    return xc * lax.rsqrt(var + LN_EPS)


def _split(x, n):
    parts = []
    for i in range(n):
        p = x.astype(BF16)
        parts.append(p)
        if i + 1 < n:
            x = x - p.astype(F32)
    return parts


_NN = (((1,), (0,)), ((), ()))
_NT = (((1,), (1,)), ((), ()))
_TN = (((0,), (0,)), ((), ()))


def _dotp(ap, bp, dims=_NN, order=None):
    if order is None:
        order = max(len(ap), len(bp))
    acc = None
    if dims == _NN and len(ap) > 1:
        m = ap[0].shape[0]
        for j, b in enumerate(bp):
            sel = [a for i, a in enumerate(ap) if i + j < order]
            if not sel:
                continue
            lhs = sel[0] if len(sel) == 1 else jnp.concatenate(sel, axis=0)
            t = lax.dot_general(lhs, b, dims, preferred_element_type=F32)
            for r in range(len(sel)):
                part = t[r * m:(r + 1) * m]
                acc = part if acc is None else acc + part
        return acc
    for i, a in enumerate(ap):
        for j, b in enumerate(bp):
            if i + j < order:
                t = lax.dot_general(a, b, dims, preferred_element_type=F32)
                acc = t if acc is None else acc + t
    return acc


def _t5_bias(rel, tbl_ref, h):
    bias = jnp.full(rel.shape, tbl_ref[0, h], F32)
    for b in range(1, NUM_BUCKETS):
        bias = jnp.where(rel >= T5_THR[b - 1], tbl_ref[b, h], bias)
    return bias


def _seg_ones():
    r = lax.broadcasted_iota(jnp.int32, (LANES, LANES), 0) // HS_RWKV
    c = lax.broadcasted_iota(jnp.int32, (LANES, LANES), 1) // HS_RWKV
    return jnp.where(r == c, 1.0, 0.0).astype(BF16)


def _segsum64(x, p128):
    outs = []
    for j in range(x.shape[-1] // LANES):
        xs = x[:, j * LANES:(j + 1) * LANES]
        outs.append(_dotp(_split(xs, 3), [p128]))
    return jnp.concatenate(outs, axis=-1)


ADA_TN = 1024


def _ada_kernel(c_ref, w_ref, b_ref, o_ref):
    c = c_ref[...]
    sc = c * jax.nn.sigmoid(c)
    rows = sc.shape[0]
    s0, s1, s2 = _split(sc, 3)
    w_hi, w_lo = _split(w_ref[...], 2)
    a = jnp.dot(jnp.concatenate([s0, s1, s2], axis=0), w_hi, preferred_element_type=F32)
    b = jnp.dot(jnp.concatenate([s0, s1], axis=0), w_lo, preferred_element_type=F32)
    o_ref[...] = (a[:rows] + a[rows:2 * rows] + a[2 * rows:] + b[:rows] + b[rows:]) + b_ref[...]


def _ada_call(c_all, w_ada, b_ada):
    rows, d = c_all.shape
    n = w_ada.shape[1]
    return pl.pallas_call(
        _ada_kernel,
        grid=(n // ADA_TN,),
        in_specs=[pl.BlockSpec((rows, d), lambda j: (0, 0)),
                  pl.BlockSpec((d, ADA_TN), lambda j: (0, j)),
                  pl.BlockSpec((1, ADA_TN), lambda j: (0, j))],
        out_specs=pl.BlockSpec((rows, ADA_TN), lambda j: (0, j)),
        out_shape=jax.ShapeDtypeStruct((rows, n), F32),
        compiler_params=_cparams(("parallel",)),
        name="ada",
    )(c_all, w_ada, b_ada.reshape(1, n))


FFN_TF = 512
FFN_TF_SMALL_M = 1408


def _ffn_kernel(x_ref, ada_ref, lng_ref, lnb_ref, wig_ref, wiu_ref, wo_ref, *rest, sub, emit_next):
    if emit_next:
        o_ref, h_next_ref, h_scr, acc = rest
    else:
        o_ref, h_scr, acc = rest
    bb, t, d = x_ref.shape
    j = pl.program_id(1)

    @pl.when(j == 0)
    def _():
        h = _ln(x_ref[...]) * (1.0 + ada_ref[:, 3 * sub + 1:3 * sub + 2, :]) + ada_ref[:, 3 * sub:3 * sub + 1, :]
        h_scr[...] = h.reshape(bb * t, d).astype(BF16)
        acc[...] = jnp.zeros_like(acc)

    h = h_scr[...]
    g = jnp.dot(h, wig_ref[...], preferred_element_type=F32)
    u = jnp.dot(h, wiu_ref[...], preferred_element_type=F32)
    act = (g * jax.nn.sigmoid(g)) * u
    acc[...] += jnp.dot(act.astype(BF16), wo_ref[...], preferred_element_type=F32)

    @pl.when(j == pl.num_programs(1) - 1)
    def _():
        gate = ada_ref[:, 3 * sub + 2:3 * sub + 3, :]
        y = DEEPNORM_ALPHA * x_ref[...] + 0.5 * gate * acc[...].reshape(bb, t, d)
        y = _ln(y) * lng_ref[sub:sub + 1, :] + lnb_ref[sub:sub + 1, :]
        o_ref[...] = y
        if emit_next:
            nxt = sub + 1
            hn = _ln(y) * (1.0 + ada_ref[:, 3 * nxt + 1:3 * nxt + 2, :]) + ada_ref[:, 3 * nxt:3 * nxt + 1, :]
            h_next_ref[...] = hn.astype(BF16)


def _ffn_call(x, ada, ln_g, ln_b, wi, wo, *, layer, which, sub, emit_next, blk):
    bsz, t, d = x.shape
    bb, tt = blk
    tf = FFN_TF if bb * tt >= FFN_TF else FFN_TF_SMALL_M
    assert D_FF % tf == 0
    nj = D_FF // tf
    nt = t // tt
    grid = ((bsz // bb) * nt, nj)
    xmap = lambda i, j: (i // nt, i % nt, 0)
    amap = lambda i, j: (i // nt, 0, 0)
    out_shape = [jax.ShapeDtypeStruct(x.shape, F32)]
    out_specs = [pl.BlockSpec((bb, tt, d), xmap)]
    if emit_next:
        out_shape.append(jax.ShapeDtypeStruct(x.shape, BF16))
        out_specs.append(pl.BlockSpec((bb, tt, d), xmap))
    res = pl.pallas_call(
        functools.partial(_ffn_kernel, sub=sub, emit_next=emit_next),
        grid=grid,
        in_specs=[pl.BlockSpec((bb, tt, d), xmap),
                  pl.BlockSpec((bb, N_ADA, d), amap),
                  pl.BlockSpec((3, d), lambda i, j: (0, 0)),
                  pl.BlockSpec((3, d), lambda i, j: (0, 0)),
                  pl.BlockSpec((None, None, d, tf), lambda i, j: (layer, which, 0, j)),
                  pl.BlockSpec((None, None, d, tf), lambda i, j: (layer, which, 0, j + nj)),
                  pl.BlockSpec((None, None, tf, d), lambda i, j: (layer, which, j, 0))],
        out_specs=out_specs,
        out_shape=out_shape,
        scratch_shapes=[pltpu.VMEM((bb * tt, d), BF16), pltpu.VMEM((bb * tt, d), F32)],
        compiler_params=_cparams(("parallel", "arbitrary")),
        name=f"ffn{sub}",
    )(x, ada, ln_g, ln_b, wi, wi, wo)
    return res if emit_next else res[0]


def _mm_kernel(h_ref, w_ref, o_ref):
    o_ref[...] = jnp.dot(h_ref[...], w_ref[...], preferred_element_type=F32)


def _mm_call(h, w, *, tm, tn, col0, n, name="proj"):
    m, k = h.shape
    assert col0 % tn == 0 and n % tn == 0
    cb = col0 // tn
    return pl.pallas_call(
        _mm_kernel,
        grid=(m // tm, n // tn),
        in_specs=[pl.BlockSpec((tm, k), lambda i, j: (i, 0)),
                  pl.BlockSpec((k, tn), lambda i, j: (0, j + cb))],
        out_specs=pl.BlockSpec((tm, tn), lambda i, j: (i, j)),
        out_shape=jax.ShapeDtypeStruct((m, n), F32),
        compiler_params=_cparams(("parallel", "arbitrary")),
        name=name,
    )(h, w)


def _bias_tiles_kernel(tbl_ref, o_ref):
    h = pl.program_id(0)
    r = lax.broadcasted_iota(jnp.int32, (MOBA_BLOCK, MOBA_BLOCK), 0)
    c = lax.broadcasted_iota(jnp.int32, (MOBA_BLOCK, MOBA_BLOCK), 1)
    rel0 = r - c
    o_ref[0, 0] = jnp.where(rel0 >= 0, _t5_bias(rel0, tbl_ref, h), NEG_BIG)
    o_ref[0, 1] = _t5_bias(rel0 + MOBA_BLOCK, tbl_ref, h)


def _bias_tiles_call(rel_bias):
    return pl.pallas_call(
        _bias_tiles_kernel,
        grid=(H_ATT,),
        in_specs=[pl.BlockSpec(memory_space=pltpu.SMEM)],
        out_specs=pl.BlockSpec((1, 2, MOBA_BLOCK, MOBA_BLOCK), lambda h: (h, 0, 0, 0)),
        out_shape=jax.ShapeDtypeStruct((H_ATT, 2, MOBA_BLOCK, MOBA_BLOCK), F32),
        compiler_params=_cparams(("arbitrary",)),
        name="bias_tiles",
    )(rel_bias)


CAND_ROWS = 16


def _rank_select(st, n_valid):
    cand = lax.broadcasted_iota(jnp.int32, st.shape, 0)
    cnt = jnp.zeros(st.shape, jnp.int32)
    for m in range(n_valid):
        sm = st[m:m + 1, :]
        beats = (sm > st) | ((sm == st) & (m < cand))
        cnt = cnt + jnp.where(beats, 1, 0)
    sel = jnp.where((cand < n_valid) & (cnt < MOBA_TOPK), 1.0, 0.0).astype(BF16)
    eye = jnp.where(lax.broadcasted_iota(jnp.int32, (CAND_ROWS, LANES), 0)
                    == lax.broadcasted_iota(jnp.int32, (CAND_ROWS, LANES), 1), 1.0, 0.0).astype(BF16)
    sel_cols = lax.dot_general(sel, eye, _TN, preferred_element_type=F32)
    return jnp.where(sel_cols > 0.5, 0.0, NEG_BIG)


def _attn_prompt_kernel(far_ref, q_ref, k_ref, v_ref, bt_ref, o_ref):
    h = pl.program_id(1)
    s_len = q_ref.shape[1]
    nb = s_len // MOBA_BLOCK
    q = q_ref[0]
    k = k_ref[0]
    kb = k.astype(BF16)
    vb = v_ref[0].astype(BF16)
    qb = (q * (HD_ATT ** -0.5)).astype(BF16)
    means = jnp.mean(k.reshape(nb, MOBA_BLOCK, HD_ATT), axis=1)
    means = jnp.concatenate([means, jnp.zeros((CAND_ROWS - nb, HD_ATT), F32)], axis=0)
    mparts = _split(means, 3)
    far = far_ref[h]
    rows = [slice(i * MOBA_BLOCK, (i + 1) * MOBA_BLOCK) for i in range(nb)]
    logits = [lax.dot_general(qb[rows[i]], kb[:(i + 1) * MOBA_BLOCK], _NT, preferred_element_type=F32)
              for i in range(nb)]
    scores = [None] + [_dotp(mparts, _split(q[rows[i]], 3), _NT, order=3) for i in range(1, nb)]
    negs = [None] + [_rank_select(scores[i], i) for i in range(1, nb)]
    probs, dens = [], []
    for i in range(nb):
        tiles = []
        for j in range(i + 1):
            tile = logits[i][:, j * MOBA_BLOCK:(j + 1) * MOBA_BLOCK]
            if j == i:
                tile = tile + bt_ref[0, 0]
            elif j == i - 1:
                tile = tile + bt_ref[0, 1] + negs[i][:, j:j + 1]
            else:
                tile = tile + (far + negs[i][:, j:j + 1])
            tiles.append(tile)
        lg = jnp.concatenate(tiles, axis=-1) if len(tiles) > 1 else tiles[0]
        mx = jnp.max(lg, axis=-1, keepdims=True)
        p = jnp.exp(lg - mx)
        dens.append(jnp.sum(p, axis=-1, keepdims=True))
        probs.append(p.astype(BF16))
    for i in range(nb):
        out = jnp.dot(probs[i], vb[:(i + 1) * MOBA_BLOCK], preferred_element_type=F32) / dens[i]
        o_ref[0, rows[i], :] = out.astype(o_ref.dtype)


def _attn_prompt_call(q, k, v, bias_tiles, far):
    bsz, s_len, _ = q.shape
    qspec = pl.BlockSpec((1, s_len, HD_ATT), lambda b, h: (b, 0, h))
    return pl.pallas_call(
        _attn_prompt_kernel,
        grid=(bsz, H_ATT),
        in_specs=[pl.BlockSpec(memory_space=pltpu.SMEM), qspec, qspec, qspec,
                  pl.BlockSpec((1, 2, MOBA_BLOCK, MOBA_BLOCK), lambda b, h: (h, 0, 0, 0))],
        out_specs=pl.BlockSpec((1, s_len, HD_ATT), lambda b, h: (b, 0, h)),
        out_shape=jax.ShapeDtypeStruct((bsz, s_len, C_ATT), BF16),
        compiler_params=_cparams(("parallel", "arbitrary")),
        name="attn_prompt",
    )(far, q, k, v, bias_tiles)


MEAN_BLOCKS = 16
PAGES_PER_BLOCK = MOBA_BLOCK // PAGE_SIZE


def _cache_means_kernel(pt_ref, *refs):
    page_refs, o_ref = refs[:-1], refs[-1]
    for m in range(MEAN_BLOCKS):
        tot = None
        for u in range(PAGES_PER_BLOCK):
            s = jnp.sum(page_refs[m * PAGES_PER_BLOCK + u][0], axis=0)
            tot = s if tot is None else tot + s
        o_ref[0, m] = tot * (1.0 / MOBA_BLOCK)


def _cache_means_call(page_table, cache_k4, n_blocks):
    dbs = page_table.shape[0]
    npg = MEAN_BLOCKS * PAGES_PER_BLOCK
    in_specs = [pl.BlockSpec((1, PAGE_SIZE, H_ATT, HD_ATT), functools.partial(
        lambda b, g, pt, u: (pt[b, g * npg + u], 0, 0, 0), u=u)) for u in range(npg)]
    return pl.pallas_call(
        _cache_means_kernel,
        grid_spec=pltpu.PrefetchScalarGridSpec(
            num_scalar_prefetch=1,
            grid=(dbs, n_blocks // MEAN_BLOCKS),
            in_specs=in_specs,
            out_specs=pl.BlockSpec((1, MEAN_BLOCKS, H_ATT, HD_ATT), lambda b, g, pt: (b, g, 0, 0)),
        ),
        out_shape=jax.ShapeDtypeStruct((dbs, n_blocks, H_ATT, HD_ATT), F32),
        compiler_params=_cparams(("parallel", "arbitrary")),
        name="cache_means",
    )(page_table, *([cache_k4] * npg))


def _topk_ids_kernel(q_ref, m_ref, o_ref):
    q = q_ref[0]
    means = m_ref[0]
    n_blocks = means.shape[0]
    ds = q.shape[0]
    pad = jnp.zeros((LANES - n_blocks, HD_ATT), F32)
    col = lax.broadcasted_iota(jnp.int32, (ds, LANES), 1)
    for h in range(H_ATT):
        sl = slice(h * HD_ATT, (h + 1) * HD_ATT)
        mh = jnp.concatenate([means[:, sl], pad], axis=0) if n_blocks < LANES else means[:, sl]
        s = _dotp(_split(q[:, sl], 3), _split(mh, 3), _NT, order=3)
        s = jnp.where(col < n_blocks, s, NEG_BIG)
        ids = jnp.zeros((ds, LANES), jnp.int32)
        for t in range(MOBA_TOPK):
            mx = jnp.max(s, axis=-1, keepdims=True)
            idx = jnp.min(jnp.where(s == mx, col, LANES), axis=-1, keepdims=True)
            ids = jnp.where(col == t, idx, ids)
            s = jnp.where(col == idx, NEG_BIG * 2, s)
        o_ref[0, h] = ids


def _topk_ids_call(q, means):
    dbs, ds, _ = q.shape
    n_blocks = means.shape[1]
    return pl.pallas_call(
        _topk_ids_kernel,
        grid=(dbs,),
        in_specs=[pl.BlockSpec((1, ds, C_ATT), lambda b: (b, 0, 0)),
                  pl.BlockSpec((1, n_blocks, C_ATT), lambda b: (b, 0, 0))],
        out_specs=pl.BlockSpec((1, H_ATT, ds, LANES), lambda b: (b, 0, 0, 0)),
        out_shape=jax.ShapeDtypeStruct((dbs, H_ATT, ds, LANES), jnp.int32),
        compiler_params=_cparams(("parallel",)),
        name="topk_ids",
    )(q, means)


N_SEL_PAGES = MOBA_TOPK * PAGES_PER_BLOCK


def _attn_sample_kernel(pt_ref, ids_ref, tbl_ref, q_ref, kn_ref, vn_ref, ck_hbm, cv_hbm, o_ref,
                        kbuf, vbuf, sems, *, past_len):
    ds = q_ref.shape[1]
    n_pg = ds * N_SEL_PAGES
    b, h = pl.program_id(0), pl.program_id(1)
    nh = pl.num_programs(1)
    step = b * nh + h
    n_steps = pl.num_programs(0) * nh
    slot = step % 2

    def page_copies(bb, hh, sl):
        base = (bb * H_ATT + hh) * ds * MOBA_TOPK
        cps = []
        for u in range(n_pg):
            pg = pt_ref[bb, ids_ref[base + u // PAGES_PER_BLOCK] * PAGES_PER_BLOCK + u % PAGES_PER_BLOCK]
            cps.append(pltpu.make_async_copy(ck_hbm.at[pg, :, hh, :], kbuf.at[sl, u], sems.at[sl, u]))
            cps.append(pltpu.make_async_copy(cv_hbm.at[pg, :, hh, :], vbuf.at[sl, u], sems.at[sl, n_pg + u]))
        return cps

    @pl.when(step == 0)
    def _():
        for cp in page_copies(b, h, slot):
            cp.start()

    @pl.when(step + 1 < n_steps)
    def _():
        nxt = step + 1
        for cp in page_copies(nxt // nh, nxt % nh, 1 - slot):
            cp.start()

    for cp in page_copies(b, h, slot):
        cp.wait()
    kp_refs = [kbuf.at[slot, u] for u in range(n_pg)]
    vp_refs = [vbuf.at[slot, u] for u in range(n_pg)]
    qb = (q_ref[0] * (HD_ATT ** -0.5)).astype(BF16)
    qrow = lax.broadcasted_iota(jnp.int32, (ds, 1), 0)
    lane = lax.broadcasted_iota(jnp.int32, (ds, PAGE_SIZE), 1)
    lo = lax.dot_general(qb, kn_ref[0].astype(BF16), _NT, preferred_element_type=F32)
    rel_o = lax.broadcasted_iota(jnp.int32, (ds, ds), 0) - lax.broadcasted_iota(jnp.int32, (ds, ds), 1)
    lo = jnp.where(rel_o >= 0, lo + _t5_bias(rel_o, tbl_ref, h), NEG_BIG)
    mx_o = jnp.max(lo, axis=-1, keepdims=True)
    vn = vn_ref[0].astype(BF16)
    tiles = []
    for qi in range(ds):
        flat = (b * H_ATT + h) * ds + qi
        tq = []
        for s in range(MOBA_TOPK):
            blk = ids_ref[flat * MOBA_TOPK + s]
            for u in range(PAGES_PER_BLOCK):
                kpg = kp_refs[(qi * MOBA_TOPK + s) * PAGES_PER_BLOCK + u][...].astype(BF16)
                lg = lax.dot_general(qb, kpg, _NT, preferred_element_type=F32)
                rel = (past_len + qi) - (blk * MOBA_BLOCK + u * PAGE_SIZE + lane)
                tq.append(lg + _t5_bias(rel, tbl_ref, h))
        tiles.append(tq)
    mxs = []
    for tq in tiles:
        mx = mx_o
        for t in tq:
            mx = jnp.maximum(mx, jnp.max(t, axis=-1, keepdims=True))
        mxs.append(mx)
    pos = [jnp.exp(lo - mx) for mx in mxs]
    ps = [[jnp.exp(t - mx) for t in tq] for tq, mx in zip(tiles, mxs)]
    accs = [jnp.dot(po.astype(BF16), vn, preferred_element_type=F32) for po in pos]
    for qi in range(ds):
        for i in range(N_SEL_PAGES):
            accs[qi] = accs[qi] + jnp.dot(ps[qi][i].astype(BF16), vp_refs[qi * N_SEL_PAGES + i][...].astype(BF16),
                                          preferred_element_type=F32)
    result = jnp.zeros((ds, HD_ATT), F32)
    for qi in range(ds):
        den = jnp.sum(pos[qi], axis=-1, keepdims=True)
        for p in ps[qi]:
            den = den + jnp.sum(p, axis=-1, keepdims=True)
        result = jnp.where(qrow == qi, accs[qi] / den, result)
    o_ref[0] = result


def _attn_sample_call(page_table, ids, rel_bias, q, k_new, v_new, cache_k2, cache_v2, past_len):
    dbs, ds, _ = q.shape
    n_pg = ds * N_SEL_PAGES

    nspec = pl.BlockSpec((1, ds, HD_ATT), lambda b, h, p, i: (b, 0, h))
    pool = pl.BlockSpec(memory_space=pl.ANY)
    return pl.pallas_call(
        functools.partial(_attn_sample_kernel, past_len=past_len),
        grid_spec=pltpu.PrefetchScalarGridSpec(
            num_scalar_prefetch=2,
            grid=(dbs, H_ATT),
            in_specs=[pl.BlockSpec(memory_space=pltpu.SMEM), nspec, nspec, nspec, pool, pool],
            out_specs=pl.BlockSpec((1, ds, HD_ATT), lambda b, h, p, i: (b, 0, h)),
            scratch_shapes=[pltpu.VMEM((2, n_pg, PAGE_SIZE, HD_ATT), F32),
                            pltpu.VMEM((2, n_pg, PAGE_SIZE, HD_ATT), F32),
                            pltpu.SemaphoreType.DMA((2, 2 * n_pg))],
        ),
        out_shape=jax.ShapeDtypeStruct((dbs, ds, C_ATT), F32),
        compiler_params=_cparams(("arbitrary", "arbitrary")),
        name="attn_sample",
    )(page_table, ids, rel_bias, q, k_new, v_new, cache_k2, cache_v2)


def _rwkv_pre_kernel(h_ref, wz_ref, sh0_ref, mu_ref, w0_ref, w2_ref, a0_ref, a2_ref, g2_ref, kk_ref, ka_ref,
                     r_o, k_o, v_o, kk_o, b_o, ld_o, g_o, zl_o, carry):
    i = pl.program_id(1)
    h = h_ref[0]
    tm = h.shape[0]
    c = C_RWKV
    row = lax.broadcasted_iota(jnp.int32, (tm, 1), 0)

    @pl.when(i == 0)
    def _():
        carry[...] = sh0_ref[0]

    def project(lo, hi):
        z = jnp.dot(h, wz_ref[:, lo:hi], preferred_element_type=F32)
        first = carry[:, lo:hi]
        carry[:, lo:hi] = z[tm - 1:tm, :]
        zl_o[0, 0, :, lo:hi] = z[tm - 8:tm, :]
        zprev = jnp.where(row == 0, first, pltpu.roll(z, 1, 0))
        return z + (zprev - z) * mu_ref[:, lo:hi]

    zl = project(3 * c, Z_PAD)
    zk = project(c, 2 * c)
    zw, za, zg = zl[:, :LORA_PAD], zl[:, LORA_PAD:2 * LORA_PAD], zl[:, 2 * LORA_PAD:]
    wl = w0_ref[...] + jnp.dot(jnp.tanh(zw).astype(BF16), w2_ref[...], preferred_element_type=F32)
    w = -(jnp.maximum(-wl, 0.0) + jnp.log(1.0 + jnp.exp(-jnp.abs(wl)))) - 0.5
    ld_o[0] = -jnp.exp(w)
    a = jax.nn.sigmoid(a0_ref[...] + jnp.dot(za.astype(BF16), a2_ref[...], preferred_element_type=F32))
    g_o[0] = jnp.dot(jax.nn.sigmoid(zg).astype(BF16), g2_ref[...], preferred_element_type=F32)
    zr = project(0, c)
    kkr = zk * kk_ref[...]
    n2 = _segsum64(kkr * kkr, _seg_ones())
    kkn = kkr / jnp.maximum(jnp.sqrt(n2), 1e-12)
    k_o[0] = zk * (1.0 + (a - 1.0) * ka_ref[...])
    kk_o[0] = kkn
    b_o[0] = kkn * a
    zv = project(2 * c, 3 * c)
    r_o[0] = zr
    v_o[0] = zv


def _rwkv_pre_call(h, win, z_col0, shift0, p, *, tm):
    bsz, t, d = h.shape
    zp = Z_PAD
    assert z_col0 % zp == 0
    nt = t // tm
    c = C_RWKV
    row = lambda b, i: (0, 0)
    vec = lambda n: pl.BlockSpec((1, n), row)
    tile = pl.BlockSpec((1, tm, c), lambda b, i: (b, i, 0))
    return pl.pallas_call(
        _rwkv_pre_kernel,
        grid=(bsz, nt),
        in_specs=[pl.BlockSpec((1, tm, d), lambda b, i: (b, i, 0)),
                  pl.BlockSpec((d, zp), lambda b, i: (0, z_col0 // zp)),
                  pl.BlockSpec((1, 1, zp), lambda b, i: (b, 0, 0)),
                  vec(zp), vec(c),
                  pl.BlockSpec((LORA_PAD, c), row), vec(c),
                  pl.BlockSpec((LORA_PAD, c), row),
                  pl.BlockSpec((D_GATE_LORA, c), row), vec(c), vec(c)],
        out_specs=[tile] * 7 + [pl.BlockSpec((1, 1, 8, zp), lambda b, i: (b, i, 0, 0))],
        out_shape=[jax.ShapeDtypeStruct((bsz, t, c), F32)] * 7 + [jax.ShapeDtypeStruct((bsz, nt, 8, zp), F32)],
        scratch_shapes=[pltpu.VMEM((1, zp), F32)],
        compiler_params=_cparams(("parallel", "arbitrary")),
        name="rwkv_pre",
    )(h, win, shift0.reshape(bsz, 1, zp), p["mu"], p["w0"], p["w2"], p["a0"], p["a2"], p["g2"], p["k_k"], p["k_a"])


SOLVE_BASE = 8
CHUNK_PREC = {"gram": (1, 1), "state_read": (1, 1), "mkv": (1, 1), "solve1": (1, 1), "solve_sq": (1, 1),
              "solve_ap": (1, 1), "out": (1, 1), "state_upd": (1, 1)}


def _chunk_pairs(rs, ks, vs, kks, bs, lds, ss):
    c = CHUNK
    lane = lax.broadcasted_iota(jnp.int32, (1, LANES), 1)
    m_a = jnp.where(lane < HS_RWKV, 1.0, 0.0)
    m_b = 1.0 - m_a
    row = lax.broadcasted_iota(jnp.int32, (c, 2 * c), 0)
    coli = lax.broadcasted_iota(jnp.int32, (c, 2 * c), 1) % c
    strict = coli < row
    incl = coli <= row
    lr = lax.broadcasted_iota(jnp.int32, (c, c), 0)
    lc = lax.broadcasted_iota(jnp.int32, (c, c), 1)
    ltri = jnp.where(lc <= lr, 1.0, 0.0).astype(BF16)
    ones = jnp.ones((c, LANES), BF16)
    rr = lax.broadcasted_iota(jnp.int32, (LANES, LANES), 0) // HS_RWKV
    cc = lax.broadcasted_iota(jnp.int32, (LANES, LANES), 1) // HS_RWKV
    same_head = rr == cc

    def each(f, *lists):
        return [f(*args) for args in zip(*lists)]

    def stack2(x):
        return jnp.concatenate([x * m_a, x * m_b], axis=0)

    def prod(site, a, b, dims=_NN):
        na, nb = CHUNK_PREC[site]
        return _dotp(_split(a, na), _split(b, nb), dims, order=max(na, nb))

    def pm(site, mcats, xs):
        return each(lambda m, x: prod(site, m, stack2(x)), mcats, xs)

    ldp = each(lambda x: _split(x, 2), lds)
    cums = each(lambda p: _dotp([ltri], p, _NN), ldp)
    gcols = each(lambda p: jnp.exp(_dotp(p, [ones], _TN)), ldp)
    g_inv = each(lambda cu: jnp.exp(-cu), cums)
    g_end = each(lambda cu: jnp.exp(cu[c - 1:c, :] - cu), cums)
    p_all = each(lambda kk, r, cu, ld: jnp.concatenate([kk * jnp.exp(cu - ld), r * jnp.exp(cu)], axis=0),
                 kks, rs, cums, lds)
    z2 = each(lambda k, b, gi: jnp.concatenate([stack2(k * gi), stack2(b * gi)], axis=0), ks, bs, g_inv)
    g4 = each(lambda p, z: prod("gram", p, z, _NT), p_all, z2)
    mk = each(lambda g: jnp.where(strict, g[:c, :2 * c], 0.0), g4)
    pj = each(lambda g: jnp.where(strict, -g[:c, 2 * c:], 0.0), g4)
    akb = each(lambda g: jnp.concatenate([jnp.where(incl, g[c:, :2 * c], 0.0),
                                          jnp.where(incl, -g[c:, 2 * c:], 0.0)], axis=1), g4)
    ps = each(lambda p, s: prod("state_read", p, s), p_all, ss)
    mkv = pm("mkv", mk, vs)
    rhs = each(lambda p, m: p[:c] + m, ps, mkv)
    same_blk = lambda s: (row // s) == (coli // s)
    eye = jnp.where(coli == row, 1.0, 0.0)
    pk = each(lambda n_: jnp.where(same_blk(SOLVE_BASE), n_, 0.0), pj)
    ts = each(lambda p: eye + p, pk)
    n = 2
    while n < SOLVE_BASE:
        pk = pm("solve_sq", pk, pk)
        ts = each(lambda t, d: t + d, ts, pm("solve_ap", ts, pk))
        n *= 2
    s = SOLVE_BASE
    while s < c:
        low = same_blk(2 * s) & jnp.logical_not(same_blk(s))
        cs = each(lambda n_: jnp.where(low, -n_, 0.0), pj)
        ts = each(lambda t, d: t - d, ts, pm("solve_ap", ts, pm("solve_sq", cs, ts)))
        s *= 2
    us = pm("solve1", ts, rhs)
    ys = each(lambda p, m, v, u: p[c:] + prod("out", m, jnp.concatenate([stack2(v), stack2(u)], axis=0)),
              ps, akb, vs, us)
    upd = each(lambda k, b, ge, v, u: prod("state_upd", jnp.concatenate([k * ge, -(b * ge)], axis=0),
                                           jnp.concatenate([v, u], axis=0), _TN), ks, bs, g_end, vs, us)
    s_new = each(lambda gc, s, up: gc * s + jnp.where(same_head, up, 0.0), gcols, ss, upd)
    return ys, s_new


def _rwkv_out_pairs(ys, rs, ks, vs, gs, rks, lgs, lbs):
    p128 = _seg_ones()
    c = CHUNK
    inv = 1.0 / HS_RWKV

    def seg3(x):
        return _split(x, 3)

    def sum3(t, o):
        return t[o * c:(o + 1) * c] + t[(o + 1) * c:(o + 2) * c] + t[(o + 2) * c:(o + 3) * c]

    t1 = [jnp.dot(jnp.concatenate(seg3(y) + seg3(r * k * rk), axis=0), p128, preferred_element_type=F32)
          for y, r, k, rk in zip(ys, rs, ks, rks)]
    ycs = [y - sum3(t, 0) * inv for y, t in zip(ys, t1)]
    t2 = [jnp.dot(jnp.concatenate(seg3(yc * yc), axis=0), p128, preferred_element_type=F32) for yc in ycs]
    outs = []
    for yc, ta, tb, v, g, lg, lb in zip(ycs, t1, t2, vs, gs, lgs, lbs):
        yn = yc * lax.rsqrt(sum3(tb, 0) * inv + GN_EPS) * lg + lb
        outs.append(((yn + sum3(ta, 3) * v) * g).astype(BF16))
    return outs


def _rwkv_chunk_kernel(r_ref, k_ref, v_ref, kk_ref, b_ref, ld_ref, g_ref, rk_ref, lg_ref, lb_ref, s0_ref,
                       o_ref, so_ref, s_scr):
    ci = pl.program_id(2)

    @pl.when(ci == 0)
    def _():
        s_scr[...] = s0_ref[...]

    nb = r_ref.shape[0]
    units = [(bi, p, slice(p * LANES, (p + 1) * LANES)) for bi in range(nb) for p in range(PAIRS_PER_STEP)]
    pairs = lambda ref: [ref[bi, :, sl] for bi, _, sl in units]
    vecs = lambda ref: [ref[:, sl] for _, _, sl in units]
    rs, ks, vs = pairs(r_ref), pairs(k_ref), pairs(v_ref)
    ys, s_new = _chunk_pairs(rs, ks, vs, pairs(kk_ref), pairs(b_ref), pairs(ld_ref),
                             [s_scr[bi, p] for bi, p, _ in units])
    outs = _rwkv_out_pairs(ys, rs, ks, vs, pairs(g_ref), vecs(rk_ref), vecs(lg_ref), vecs(lb_ref))
    for u, (bi, p, sl) in enumerate(units):
        o_ref[bi, :, sl] = outs[u]
        s_scr[bi, p] = s_new[u]
        so_ref[bi, p] = s_new[u]


def _rwkv_chunk_call(r, k, v, kk, b, logd, g, s0_blk, p):
    bsz, t, c = r.shape
    pp = PAIRS_PER_STEP
    nb = BATCH_PER_STEP
    assert bsz % nb == 0
    w = pp * LANES
    tile = pl.BlockSpec((nb, CHUNK, w), lambda bi, pi, ci: (bi, ci, pi))
    vec = pl.BlockSpec((1, w), lambda bi, pi, ci: (0, pi))
    sspec = pl.BlockSpec((nb, pp, LANES, LANES), lambda bi, pi, ci: (bi, pi, 0, 0))
    return pl.pallas_call(
        _rwkv_chunk_kernel,
        grid=(bsz // nb, c // w, t // CHUNK),
        in_specs=[tile] * 7 + [vec] * 3 + [sspec],
        out_specs=[tile, sspec],
        out_shape=[jax.ShapeDtypeStruct((bsz, t, c), BF16),
                   jax.ShapeDtypeStruct(s0_blk.shape, F32)],
        scratch_shapes=[pltpu.VMEM((nb, pp, LANES, LANES), F32)],
        compiler_params=_cparams(("parallel", "parallel", "arbitrary")),
        name="rwkv_chunk",
    )(r, k, v, kk, b, logd, g, p["r_k"], p["lnx_g"], p["lnx_b"], s0_blk)


def _combine_kernel(x_ref, ada_ref, lng_ref, lnb_ref, oa_ref, or_ref, ga_ref, gr_ref, wa_ref, wr_ref, wo_ref, o_ref):
    bb, t, d = x_ref.shape
    ua = jnp.dot(oa_ref[...], wa_ref[...], preferred_element_type=F32)
    ur = jnp.dot(or_ref[...], wr_ref[...], preferred_element_type=F32)
    m = jax.nn.sigmoid(ga_ref[...]) * ua + jax.nn.sigmoid(gr_ref[...]) * ur
    mo = jnp.dot(m.astype(BF16), wo_ref[...], preferred_element_type=F32)
    y = DEEPNORM_ALPHA * x_ref[...] + ada_ref[:, 5:6, :] * mo.reshape(bb, t, d)
    o_ref[...] = _ln(y) * lng_ref[1:2, :] + lnb_ref[1:2, :]


def _combine_call(x, ada, ln_g, ln_b, oa, orw, ga, gr, wa, wr, wo, *, blk):
    bsz, t, d = x.shape
    bb, tt = blk
    nt = t // tt
    rows = bb * tt
    xmap = lambda i: (i // nt, i % nt, 0)
    const = lambda i: (0, 0)
    rowt = lambda n: pl.BlockSpec((rows, n), lambda i: (i, 0))
    return pl.pallas_call(
        _combine_kernel,
        grid=((bsz // bb) * nt,),
        in_specs=[pl.BlockSpec((bb, tt, d), xmap),
                  pl.BlockSpec((bb, N_ADA, d), lambda i: (i // nt, 0, 0)),
                  pl.BlockSpec((3, d), const), pl.BlockSpec((3, d), const),
                  rowt(C_ATT), rowt(C_RWKV), rowt(d), rowt(d),
                  pl.BlockSpec((C_ATT, d), const), pl.BlockSpec((C_RWKV, d), const), pl.BlockSpec((d, d), const)],
        out_specs=pl.BlockSpec((bb, tt, d), xmap),
        out_shape=jax.ShapeDtypeStruct(x.shape, F32),
        compiler_params=_cparams(("parallel",)),
        name="combine",
    )(x, ada, ln_g, ln_b, oa, orw, ga, gr, wa, wr, wo)


def _rearrange_z(a):
    c3 = 3 * C_RWKV
    pad = [(0, 0)] * (a.ndim - 1) + [(0, LORA_PAD - D_DECAY_LORA)]
    return jnp.concatenate([a[..., :c3],
                            jnp.pad(a[..., c3:c3 + D_DECAY_LORA], pad),
                            jnp.pad(a[..., c3 + D_DECAY_LORA:c3 + D_DECAY_LORA + D_AAA_LORA], pad),
                            a[..., c3 + D_DECAY_LORA + D_AAA_LORA:]], axis=-1)


def _unarrange_z(a):
    c3 = 3 * C_RWKV
    return jnp.concatenate([a[..., :c3], a[..., c3:c3 + D_DECAY_LORA],
                            a[..., c3 + LORA_PAD:c3 + LORA_PAD + D_AAA_LORA], a[..., c3 + 2 * LORA_PAD:]], axis=-1)


def _state_to_blocks(state):
    bsz = state.shape[0]
    st = jnp.swapaxes(state, -1, -2).reshape(bsz, H_RWKV // 2, 2, HS_RWKV, HS_RWKV)
    blk = jnp.einsum('bphkv,hg->bphkgv', st, jnp.eye(2, dtype=state.dtype))
    return blk.reshape(bsz, H_RWKV // 2, LANES, LANES)


def _blocks_to_state(blk):
    bsz = blk.shape[0]
    b6 = blk.reshape(bsz, H_RWKV // 2, 2, HS_RWKV, 2, HS_RWKV)
    st = jnp.stack([b6[:, :, 0, :, 0, :], b6[:, :, 1, :, 1, :]], axis=2)
    return jnp.swapaxes(st, -1, -2).reshape(bsz, H_RWKV, HS_RWKV, HS_RWKV)


def _trunk(x, ada, attend, shift0, wkv0, w, *, ffn_blk, mm_tm, pre_tm, comb_blk):
    bsz, t, d = x.shape
    m = bsz * t
    ffn = functools.partial(_ffn_call, ada=ada, ln_g=w["ln_g"], ln_b=w["ln_b"], wi=w["ffn_wi"], wo=w["ffn_wo"],
                            layer=w["layer"], blk=ffn_blk)
    x1, h2 = ffn(x, which=0, sub=0, emit_next=True)
    h2f = h2.reshape(m, d)
    win = w["win"]
    proj = functools.partial(_mm_call, h2f, win, tm=mm_tm)
    q = proj(tn=C_ATT, col0=0, n=C_ATT, name="proj_q").reshape(bsz, t, C_ATT)
    k = proj(tn=C_ATT, col0=C_ATT, n=C_ATT, name="proj_k").reshape(bsz, t, C_ATT)
    v = proj(tn=C_ATT, col0=2 * C_ATT, n=C_ATT, name="proj_v").reshape(bsz, t, C_ATT)
    ga = proj(tn=d // 2, col0=3 * C_ATT, n=d, name="proj_ga")
    gr = proj(tn=d // 2, col0=3 * C_ATT + d, n=d, name="proj_gr")
    o_att = attend(q, k, v)
    r, k2, vv, kk, b, logd, g, z_last = _rwkv_pre_call(h2, win, 3 * C_ATT + 2 * d, _rearrange_z(shift0), w, tm=pre_tm)
    tp = -(-t // CHUNK) * CHUNK
    seq = [r, k2, vv, kk, b, logd, g]
    if tp != t:
        seq = [jnp.pad(a, ((0, 0), (0, tp - t), (0, 0))) for a in seq]
    o_rwkv, s_blk = _rwkv_chunk_call(*seq, _state_to_blocks(wkv0), w)
    o_rwkv = o_rwkv[:, :t].reshape(m, C_RWKV)
    x2 = _combine_call(x1, ada, w["ln_g"], w["ln_b"], o_att.reshape(m, C_ATT), o_rwkv, ga, gr,
                       w["wua"], w["wur"], w["wout"], blk=comb_blk)
    x3 = ffn(x2, which=1, sub=2, emit_next=False)
    shift_new = _unarrange_z(z_last[:, -1, 7])
    return (x3, k.reshape(bsz, t, H_ATT, HD_ATT), v.reshape(bsz, t, H_ATT, HD_ATT), _blocks_to_state(s_blk), shift_new)


def kernel(x_prompt, x_sample, cache_k, cache_v, state_wkv, state_shift, page_table, c_prompt, c_sample, rel_bias, w_ada, b_ada, ln_g, ln_b, ffn_wi, ffn_wo, w_in, mu_shift, w0, w2, a0, a2, g2, k_k, k_a, r_k, lnx_g, lnx_b, w_up_attn, w_up_rwkv, w_out):
    assert w_ada.shape[0] == DEPTH == 1
    bsz, s_len, d = x_prompt.shape
    dbs, ds, _ = x_sample.shape
    past_len = page_table.shape[1] * PAGE_SIZE
    n_phys = cache_k.shape[1]
    l = 0
    win = w_in[l]
    c3 = 3 * C_ATT
    lora_rows = ((0, LORA_PAD - D_DECAY_LORA), (0, 0))
    w = {
        "ln_g": ln_g[l], "ln_b": ln_b[l],
        "layer": l, "ffn_wi": ffn_wi.astype(BF16), "ffn_wo": ffn_wo.astype(BF16),
        "win": jnp.concatenate([win[:, :c3], win[:, c3 + RWKV_PROJ:],
                                _rearrange_z(win[:, c3:c3 + RWKV_PROJ])], axis=1).astype(BF16),
        "mu": _rearrange_z(mu_shift[l])[None, :],
        "w0": w0[l][None, :], "w2": jnp.pad(w2[l], lora_rows).astype(BF16),
        "a0": a0[l][None, :], "a2": jnp.pad(a2[l], lora_rows).astype(BF16),
        "g2": g2[l].astype(BF16), "k_k": k_k[l][None, :], "k_a": k_a[l][None, :],
        "r_k": r_k[l].reshape(1, C_RWKV), "lnx_g": lnx_g[l][None, :], "lnx_b": lnx_b[l][None, :],
        "wua": w_up_attn[l].astype(BF16), "wur": w_up_rwkv[l].astype(BF16), "wout": w_out[l].astype(BF16),
    }
    n_c = bsz + dbs
    c_rows = -(-n_c // 8) * 8
    c_all = jnp.concatenate([c_prompt, c_sample, jnp.zeros((c_rows - n_c, d), F32)], axis=0)
    ada = _ada_call(c_all, w_ada[l], b_ada[l]).reshape(c_rows, N_ADA, d)

    bias_tiles = _bias_tiles_call(rel_bias)
    far = rel_bias[NUM_BUCKETS - 1]
    attend_p = lambda q, k, v: _attn_prompt_call(q, k, v, bias_tiles, far)
    yp, kp, vp, wp, sp = _trunk(
        x_prompt, ada[:bsz], attend_p, jnp.zeros((bsz, RWKV_PROJ), F32),
        jnp.zeros((bsz, H_RWKV, HS_RWKV, HS_RWKV), F32), w,
        ffn_blk=(1, 512), mm_tm=1024, pre_tm=256, comb_blk=(1, 256))

    cache_k2 = cache_k.reshape(DEPTH * n_phys, PAGE_SIZE, H_ATT, HD_ATT)
    cache_v2 = cache_v.reshape(DEPTH * n_phys, PAGE_SIZE, H_ATT, HD_ATT)
    page_table = page_table + l * n_phys
    n_full = past_len // MOBA_BLOCK

    def attend_s(q, k, v):
        means = _cache_means_call(page_table, cache_k2, n_full).reshape(dbs, n_full, C_ATT)
        ids = _topk_ids_call(q, means)[..., :MOBA_TOPK]
        o = _attn_sample_call(page_table, ids.reshape(-1), rel_bias, q, k, v, cache_k2, cache_v2, past_len)
        return o.astype(BF16)

    ys, kn, vn, wn, sn = _trunk(
        x_sample, ada[bsz:n_c], attend_s, state_shift[l], state_wkv[l], w,
        ffn_blk=(dbs, ds), mm_tm=dbs * ds, pre_tm=ds, comb_blk=(dbs, ds))

    return (yp, ys, kp[None], vp[None], kn[None], vn[None], wp[None], wn[None], sp[None], sn[None])
```

```python
import functools
import math

import jax
import jax.numpy as jnp
from jax import lax
from jax.experimental import pallas as pl
from jax.experimental.pallas import tpu as pltpu

F32 = jnp.float32
BF16 = jnp.bfloat16

D_MODEL = 2048
D_FF = 5632
N_ADA = 9
H_ATT = 8
HD_ATT = 128
C_ATT = H_ATT * HD_ATT
MOBA_BLOCK = 256
MOBA_TOPK = 3
NUM_BUCKETS = 32
MAX_DISTANCE = 128
PAGE_SIZE = 128
HS_RWKV = 64
C_RWKV = 1024
H_RWKV = C_RWKV // HS_RWKV
D_DECAY_LORA = 96
D_AAA_LORA = 96
D_GATE_LORA = 256
RWKV_PROJ = 3 * C_RWKV + D_DECAY_LORA + D_AAA_LORA + D_GATE_LORA
LORA_PAD = 128
Z_PAD = 3 * C_RWKV + 2 * LORA_PAD + D_GATE_LORA
GN_EPS = 64e-5
LN_EPS = 1e-5
DEPTH = 1
DEEPNORM_ALPHA = (2 * DEPTH) ** 0.25
NEG_BIG = -1e30

LANES = 128
CHUNK = 64
BATCH_PER_STEP = 2
PAIRS_PER_STEP = 8
VMEM_LIMIT = 56 * 1024 * 1024


def _t5_thresholds():
    max_exact = NUM_BUCKETS // 2
    thr = list(range(1, max_exact + 1))
    for b in range(max_exact + 1, NUM_BUCKETS):
        x = max_exact * (MAX_DISTANCE / max_exact) ** ((b - max_exact) / (NUM_BUCKETS - max_exact))
        thr.append(int(math.ceil(x)))
    return tuple(thr)


T5_THR = _t5_thresholds()


def _cparams(sem, vmem=VMEM_LIMIT):
    return pltpu.CompilerParams(dimension_semantics=sem, vmem_limit_bytes=vmem)


def _ln(x):
    mu = jnp.mean(x, axis=-1, keepdims=True)
    xc = x - mu
    var = jnp.mean(xc * xc, axis=-1, keepdims=True)
    return xc * lax.rsqrt(var + LN_EPS)


def _split(x, n):
    parts = []
    for i in range(n):
        p = x.astype(BF16)
        parts.append(p)
        if i + 1 < n:
            x = x - p.astype(F32)
    return parts


_NN = (((1,), (0,)), ((), ()))
_NT = (((1,), (1,)), ((), ()))
_TN = (((0,), (0,)), ((), ()))


def _dotp(ap, bp, dims=_NN, order=None):
    if order is None:
        order = max(len(ap), len(bp))
    acc = None
    if dims == _NN and len(ap) > 1:
        m = ap[0].shape[0]
        for j, b in enumerate(bp):
            sel = [a for i, a in enumerate(ap) if i + j < order]
            if not sel:
                continue
            lhs = sel[0] if len(sel) == 1 else jnp.concatenate(sel, axis=0)
            t = lax.dot_general(lhs, b, dims, preferred_element_type=F32)
            for r in range(len(sel)):
                part = t[r * m:(r + 1) * m]
                acc = part if acc is None else acc + part
        return acc
    for i, a in enumerate(ap):
        for j, b in enumerate(bp):
            if i + j < order:
                t = lax.dot_general(a, b, dims, preferred_element_type=F32)
                acc = t if acc is None else acc + t
    return acc


def _t5_bias(rel, tbl_ref, h):
    bias = jnp.full(rel.shape, tbl_ref[0, h], F32)
    for b in range(1, NUM_BUCKETS):
        bias = jnp.where(rel >= T5_THR[b - 1], tbl_ref[b, h], bias)
    return bias


def _seg_ones():
    r = lax.broadcasted_iota(jnp.int32, (LANES, LANES), 0) // HS_RWKV
    c = lax.broadcasted_iota(jnp.int32, (LANES, LANES), 1) // HS_RWKV
    return jnp.where(r == c, 1.0, 0.0).astype(BF16)


def _segsum64(x, p128):
    outs = []
    for j in range(x.shape[-1] // LANES):
        xs = x[:, j * LANES:(j + 1) * LANES]
        outs.append(_dotp(_split(xs, 3), [p128]))
    return jnp.concatenate(outs, axis=-1)


ADA_TN = 1024


def _ada_kernel(c_ref, w_ref, b_ref, o_ref):
    c = c_ref[...]
    sc = c * jax.nn.sigmoid(c)
    rows = sc.shape[0]
    s0, s1, s2 = _split(sc, 3)
    w_hi, w_lo = _split(w_ref[...], 2)
    a = jnp.dot(jnp.concatenate([s0, s1, s2], axis=0), w_hi, preferred_element_type=F32)
    b = jnp.dot(jnp.concatenate([s0, s1], axis=0), w_lo, preferred_element_type=F32)
    o_ref[...] = (a[:rows] + a[rows:2 * rows] + a[2 * rows:] + b[:rows] + b[rows:]) + b_ref[...]


def _ada_call(c_all, w_ada, b_ada):
    rows, d = c_all.shape
    n = w_ada.shape[1]
    return pl.pallas_call(
        _ada_kernel,
        grid=(n // ADA_TN,),
        in_specs=[pl.BlockSpec((rows, d), lambda j: (0, 0)),
                  pl.BlockSpec((d, ADA_TN), lambda j: (0, j)),
                  pl.BlockSpec((1, ADA_TN), lambda j: (0, j))],
        out_specs=pl.BlockSpec((rows, ADA_TN), lambda j: (0, j)),
        out_shape=jax.ShapeDtypeStruct((rows, n), F32),
        compiler_params=_cparams(("parallel",)),
        name="ada",
    )(c_all, w_ada, b_ada.reshape(1, n))


FFN_TF = 512
FFN_TF_SMALL_M = 1408


def _ffn_kernel(x_ref, ada_ref, lng_ref, lnb_ref, wig_ref, wiu_ref, wo_ref, *rest, sub, emit_next):
    if emit_next:
        o_ref, h_next_ref, h_scr, acc = rest
    else:
        o_ref, h_scr, acc = rest
    bb, t, d = x_ref.shape
    j = pl.program_id(1)

    @pl.when(j == 0)
    def _():
        h = _ln(x_ref[...]) * (1.0 + ada_ref[:, 3 * sub + 1:3 * sub + 2, :]) + ada_ref[:, 3 * sub:3 * sub + 1, :]
        h_scr[...] = h.reshape(bb * t, d).astype(BF16)
        acc[...] = jnp.zeros_like(acc)

    h = h_scr[...]
    g = jnp.dot(h, wig_ref[...], preferred_element_type=F32)
    u = jnp.dot(h, wiu_ref[...], preferred_element_type=F32)
    act = (g * jax.nn.sigmoid(g)) * u
    acc[...] += jnp.dot(act.astype(BF16), wo_ref[...], preferred_element_type=F32)

    @pl.when(j == pl.num_programs(1) - 1)
    def _():
        gate = ada_ref[:, 3 * sub + 2:3 * sub + 3, :]
        y = DEEPNORM_ALPHA * x_ref[...] + 0.5 * gate * acc[...].reshape(bb, t, d)
        y = _ln(y) * lng_ref[sub:sub + 1, :] + lnb_ref[sub:sub + 1, :]
        o_ref[...] = y
        if emit_next:
            nxt = sub + 1
            hn = _ln(y) * (1.0 + ada_ref[:, 3 * nxt + 1:3 * nxt + 2, :]) + ada_ref[:, 3 * nxt:3 * nxt + 1, :]
            h_next_ref[...] = hn.astype(BF16)


def _ffn_call(x, ada, ln_g, ln_b, wi, wo, *, layer, which, sub, emit_next, blk):
    bsz, t, d = x.shape
    bb, tt = blk
    tf = FFN_TF if bb * tt >= FFN_TF else FFN_TF_SMALL_M
    assert D_FF % tf == 0
    nj = D_FF // tf
    nt = t // tt
    grid = ((bsz // bb) * nt, nj)
    xmap = lambda i, j: (i // nt, i % nt, 0)
    amap = lambda i, j: (i // nt, 0, 0)
    out_shape = [jax.ShapeDtypeStruct(x.shape, F32)]
    out_specs = [pl.BlockSpec((bb, tt, d), xmap)]
    if emit_next:
        out_shape.append(jax.ShapeDtypeStruct(x.shape, BF16))
        out_specs.append(pl.BlockSpec((bb, tt, d), xmap))
    res = pl.pallas_call(
        functools.partial(_ffn_kernel, sub=sub, emit_next=emit_next),
        grid=grid,
        in_specs=[pl.BlockSpec((bb, tt, d), xmap),
                  pl.BlockSpec((bb, N_ADA, d), amap),
                  pl.BlockSpec((3, d), lambda i, j: (0, 0)),
                  pl.BlockSpec((3, d), lambda i, j: (0, 0)),
                  pl.BlockSpec((None, None, d, tf), lambda i, j: (layer, which, 0, j)),
                  pl.BlockSpec((None, None, d, tf), lambda i, j: (layer, which, 0, j + nj)),
                  pl.BlockSpec((None, None, tf, d), lambda i, j: (layer, which, j, 0))],
        out_specs=out_specs,
        out_shape=out_shape,
        scratch_shapes=[pltpu.VMEM((bb * tt, d), BF16), pltpu.VMEM((bb * tt, d), F32)],
        compiler_params=_cparams(("parallel", "arbitrary")),
        name=f"ffn{sub}",
    )(x, ada, ln_g, ln_b, wi, wi, wo)
    return res if emit_next else res[0]


def _mm_kernel(h_ref, w_ref, o_ref):
    o_ref[...] = jnp.dot(h_ref[...], w_ref[...], preferred_element_type=F32)


def _mm_call(h, w, *, tm, tn, col0, n, name="proj"):
    m, k = h.shape
    assert col0 % tn == 0 and n % tn == 0
    cb = col0 // tn
    return pl.pallas_call(
        _mm_kernel,
        grid=(m // tm, n // tn),
        in_specs=[pl.BlockSpec((tm, k), lambda i, j: (i, 0)),
                  pl.BlockSpec((k, tn), lambda i, j: (0, j + cb))],
        out_specs=pl.BlockSpec((tm, tn), lambda i, j: (i, j)),
        out_shape=jax.ShapeDtypeStruct((m, n), F32),
        compiler_params=_cparams(("parallel", "arbitrary")),
        name=name,
    )(h, w)


def _bias_tiles_kernel(tbl_ref, o_ref):
    h = pl.program_id(0)
    r = lax.broadcasted_iota(jnp.int32, (MOBA_BLOCK, MOBA_BLOCK), 0)
    c = lax.broadcasted_iota(jnp.int32, (MOBA_BLOCK, MOBA_BLOCK), 1)
    rel0 = r - c
    o_ref[0, 0] = jnp.where(rel0 >= 0, _t5_bias(rel0, tbl_ref, h), NEG_BIG)
    o_ref[0, 1] = _t5_bias(rel0 + MOBA_BLOCK, tbl_ref, h)


def _bias_tiles_call(rel_bias):
    return pl.pallas_call(
        _bias_tiles_kernel,
        grid=(H_ATT,),
        in_specs=[pl.BlockSpec(memory_space=pltpu.SMEM)],
        out_specs=pl.BlockSpec((1, 2, MOBA_BLOCK, MOBA_BLOCK), lambda h: (h, 0, 0, 0)),
        out_shape=jax.ShapeDtypeStruct((H_ATT, 2, MOBA_BLOCK, MOBA_BLOCK), F32),
        compiler_params=_cparams(("arbitrary",)),
        name="bias_tiles",
    )(rel_bias)


CAND_ROWS = 16


def _rank_select(st, n_valid):
    cand = lax.broadcasted_iota(jnp.int32, st.shape, 0)
    cnt = jnp.zeros(st.shape, jnp.int32)
    for m in range(n_valid):
        sm = st[m:m + 1, :]
        beats = (sm > st) | ((sm == st) & (m < cand))
        cnt = cnt + jnp.where(beats, 1, 0)
    sel = jnp.where((cand < n_valid) & (cnt < MOBA_TOPK), 1.0, 0.0).astype(BF16)
    eye = jnp.where(lax.broadcasted_iota(jnp.int32, (CAND_ROWS, LANES), 0)
                    == lax.broadcasted_iota(jnp.int32, (CAND_ROWS, LANES), 1), 1.0, 0.0).astype(BF16)
    sel_cols = lax.dot_general(sel, eye, _TN, preferred_element_type=F32)
    return jnp.where(sel_cols > 0.5, 0.0, NEG_BIG)


def _attn_prompt_kernel(far_ref, q_ref, k_ref, v_ref, bt_ref, o_ref):
    h = pl.program_id(1)
    s_len = q_ref.shape[1]
    nb = s_len // MOBA_BLOCK
    q = q_ref[0]
    k = k_ref[0]
    kb = k.astype(BF16)
    vb = v_ref[0].astype(BF16)
    qb = (q * (HD_ATT ** -0.5)).astype(BF16)
    means = jnp.mean(k.reshape(nb, MOBA_BLOCK, HD_ATT), axis=1)
    means = jnp.concatenate([means, jnp.zeros((CAND_ROWS - nb, HD_ATT), F32)], axis=0)
    mparts = _split(means, 3)
    far = far_ref[h]
    rows = [slice(i * MOBA_BLOCK, (i + 1) * MOBA_BLOCK) for i in range(nb)]
    logits = [lax.dot_general(qb[rows[i]], kb[:(i + 1) * MOBA_BLOCK], _NT, preferred_element_type=F32)
              for i in range(nb)]
    scores = [None] + [_dotp(mparts, _split(q[rows[i]], 3), _NT, order=3) for i in range(1, nb)]
    negs = [None] + [_rank_select(scores[i], i) for i in range(1, nb)]
    probs, dens = [], []
    for i in range(nb):
        tiles = []
        for j in range(i + 1):
            tile = logits[i][:, j * MOBA_BLOCK:(j + 1) * MOBA_BLOCK]
            if j == i:
                tile = tile + bt_ref[0, 0]
            elif j == i - 1:
                tile = tile + bt_ref[0, 1] + negs[i][:, j:j + 1]
            else:
                tile = tile + (far + negs[i][:, j:j + 1])
            tiles.append(tile)
        lg = jnp.concatenate(tiles, axis=-1) if len(tiles) > 1 else tiles[0]
        mx = jnp.max(lg, axis=-1, keepdims=True)
        p = jnp.exp(lg - mx)
        dens.append(jnp.sum(p, axis=-1, keepdims=True))
        probs.append(p.astype(BF16))
    for i in range(nb):
        out = jnp.dot(probs[i], vb[:(i + 1) * MOBA_BLOCK], preferred_element_type=F32) / dens[i]
        o_ref[0, rows[i], :] = out.astype(o_ref.dtype)


def _attn_prompt_call(q, k, v, bias_tiles, far):
    bsz, s_len, _ = q.shape
    qspec = pl.BlockSpec((1, s_len, HD_ATT), lambda b, h: (b, 0, h))
    return pl.pallas_call(
        _attn_prompt_kernel,
        grid=(bsz, H_ATT),
        in_specs=[pl.BlockSpec(memory_space=pltpu.SMEM), qspec, qspec, qspec,
                  pl.BlockSpec((1, 2, MOBA_BLOCK, MOBA_BLOCK), lambda b, h: (h, 0, 0, 0))],
        out_specs=pl.BlockSpec((1, s_len, HD_ATT), lambda b, h: (b, 0, h)),
        out_shape=jax.ShapeDtypeStruct((bsz, s_len, C_ATT), BF16),
        compiler_params=_cparams(("parallel", "arbitrary")),
        name="attn_prompt",
    )(far, q, k, v, bias_tiles)


MEAN_BLOCKS = 16
PAGES_PER_BLOCK = MOBA_BLOCK // PAGE_SIZE


def _cache_means_kernel(pt_ref, *refs):
    page_refs, o_ref = refs[:-1], refs[-1]
    for m in range(MEAN_BLOCKS):
        tot = None
        for u in range(PAGES_PER_BLOCK):
            s = jnp.sum(page_refs[m * PAGES_PER_BLOCK + u][0], axis=0)
            tot = s if tot is None else tot + s
        o_ref[0, m] = tot * (1.0 / MOBA_BLOCK)


def _cache_means_call(page_table, cache_k4, n_blocks):
    dbs = page_table.shape[0]
    npg = MEAN_BLOCKS * PAGES_PER_BLOCK
    in_specs = [pl.BlockSpec((1, PAGE_SIZE, H_ATT, HD_ATT), functools.partial(
        lambda b, g, pt, u: (pt[b, g * npg + u], 0, 0, 0), u=u)) for u in range(npg)]
    return pl.pallas_call(
        _cache_means_kernel,
        grid_spec=pltpu.PrefetchScalarGridSpec(
            num_scalar_prefetch=1,
            grid=(dbs, n_blocks // MEAN_BLOCKS),
            in_specs=in_specs,
            out_specs=pl.BlockSpec((1, MEAN_BLOCKS, H_ATT, HD_ATT), lambda b, g, pt: (b, g, 0, 0)),
        ),
        out_shape=jax.ShapeDtypeStruct((dbs, n_blocks, H_ATT, HD_ATT), F32),
        compiler_params=_cparams(("parallel", "arbitrary")),
        name="cache_means",
    )(page_table, *([cache_k4] * npg))


def _topk_ids_kernel(q_ref, m_ref, o_ref):
    q = q_ref[0]
    means = m_ref[0]
    n_blocks = means.shape[0]
    ds = q.shape[0]
    pad = jnp.zeros((LANES - n_blocks, HD_ATT), F32)
    col = lax.broadcasted_iota(jnp.int32, (ds, LANES), 1)
    for h in range(H_ATT):
        sl = slice(h * HD_ATT, (h + 1) * HD_ATT)
        mh = jnp.concatenate([means[:, sl], pad], axis=0) if n_blocks < LANES else means[:, sl]
        s = _dotp(_split(q[:, sl], 3), _split(mh, 3), _NT, order=3)
        s = jnp.where(col < n_blocks, s, NEG_BIG)
        ids = jnp.zeros((ds, LANES), jnp.int32)
        for t in range(MOBA_TOPK):
            mx = jnp.max(s, axis=-1, keepdims=True)
            idx = jnp.min(jnp.where(s == mx, col, LANES), axis=-1, keepdims=True)
            ids = jnp.where(col == t, idx, ids)
            s = jnp.where(col == idx, NEG_BIG * 2, s)
        o_ref[0, h] = ids


def _topk_ids_call(q, means):
    dbs, ds, _ = q.shape
    n_blocks = means.shape[1]
    return pl.pallas_call(
        _topk_ids_kernel,
        grid=(dbs,),
        in_specs=[pl.BlockSpec((1, ds, C_ATT), lambda b: (b, 0, 0)),
                  pl.BlockSpec((1, n_blocks, C_ATT), lambda b: (b, 0, 0))],
        out_specs=pl.BlockSpec((1, H_ATT, ds, LANES), lambda b: (b, 0, 0, 0)),
        out_shape=jax.ShapeDtypeStruct((dbs, H_ATT, ds, LANES), jnp.int32),
        compiler_params=_cparams(("parallel",)),
        name="topk_ids",
    )(q, means)


N_SEL_PAGES = MOBA_TOPK * PAGES_PER_BLOCK
SAMPLE_HEADS_PER_STEP = 2


def _attn_sample_kernel(pt_ref, ids_ref, tbl_ref, q_ref, kn_ref, vn_ref, ck_hbm, cv_hbm, o_ref,
                        kbuf, vbuf, sems, *, past_len):
    ds = q_ref.shape[1]
    hps = SAMPLE_HEADS_PER_STEP
    n_pg = ds * N_SEL_PAGES
    b, hg = pl.program_id(0), pl.program_id(1)
    ng = pl.num_programs(1)
    step = b * ng + hg
    n_steps = pl.num_programs(0) * ng
    slot = step % 2

    def page_copies(bb, gg, sl):
        cps = []
        for hh in range(hps):
            head = gg * hps + hh
            base = (bb * H_ATT + head) * ds * MOBA_TOPK
            for u in range(n_pg):
                pg = pt_ref[bb, ids_ref[base + u // PAGES_PER_BLOCK] * PAGES_PER_BLOCK + u % PAGES_PER_BLOCK]
                w = hh * n_pg + u
                cps.append(pltpu.make_async_copy(ck_hbm.at[pg, :, head, :], kbuf.at[sl, w], sems.at[sl, 0]))
                cps.append(pltpu.make_async_copy(cv_hbm.at[pg, :, head, :], vbuf.at[sl, w], sems.at[sl, 1]))
        return cps

    def slot_waits(sl):
        n_buf = hps * n_pg
        return [pltpu.make_async_copy(ck_hbm.at[pl.ds(0, n_buf), :, 0, :], kbuf.at[sl], sems.at[sl, 0]),
                pltpu.make_async_copy(cv_hbm.at[pl.ds(0, n_buf), :, 0, :], vbuf.at[sl], sems.at[sl, 1])]

    @pl.when(step == 0)
    def _():
        for cp in page_copies(b, hg, slot):
            cp.start()

    @pl.when(step + 1 < n_steps)
    def _():
        nxt = step + 1
        for cp in page_copies(nxt // ng, nxt % ng, 1 - slot):
            cp.start()

    for cp in slot_waits(slot):
        cp.wait()
    qrow = lax.broadcasted_iota(jnp.int32, (ds, 1), 0)
    lane = lax.broadcasted_iota(jnp.int32, (ds, PAGE_SIZE), 1)
    rel_o = lax.broadcasted_iota(jnp.int32, (ds, ds), 0) - lax.broadcasted_iota(jnp.int32, (ds, ds), 1)
    heads = [hg * hps + hh for hh in range(hps)]
    hsl = [slice(hh * HD_ATT, (hh + 1) * HD_ATT) for hh in range(hps)]
    qbs = [(q_ref[0, :, sl] * (HD_ATT ** -0.5)).astype(BF16) for sl in hsl]
    vns = [vn_ref[0, :, sl].astype(BF16) for sl in hsl]
    los = [lax.dot_general(qb, kn_ref[0, :, sl].astype(BF16), _NT, preferred_element_type=F32)
           for qb, sl in zip(qbs, hsl)]
    los = [jnp.where(rel_o >= 0, lo + _t5_bias(rel_o, tbl_ref, head), NEG_BIG) for lo, head in zip(los, heads)]
    mx_os = [jnp.max(lo, axis=-1, keepdims=True) for lo in los]
    units = [(hh, qi) for hh in range(hps) for qi in range(ds)]
    tiles = []
    for hh, qi in units:
        flat = (b * H_ATT + heads[hh]) * ds + qi
        tq = []
        for s in range(MOBA_TOPK):
            blk = ids_ref[flat * MOBA_TOPK + s]
            for u in range(PAGES_PER_BLOCK):
                w = hh * n_pg + (qi * MOBA_TOPK + s) * PAGES_PER_BLOCK + u
                kpg = kbuf[slot, w].astype(BF16)
                lg = lax.dot_general(qbs[hh], kpg, _NT, preferred_element_type=F32)
                rel = (past_len + qi) - (blk * MOBA_BLOCK + u * PAGE_SIZE + lane)
                tq.append(lg + _t5_bias(rel, tbl_ref, heads[hh]))
        tiles.append(tq)
    mxs = []
    for (hh, qi), tq in zip(units, tiles):
        mx = mx_os[hh]
        for t in tq:
            mx = jnp.maximum(mx, jnp.max(t, axis=-1, keepdims=True))
        mxs.append(mx)
    pos = [jnp.exp(los[hh] - mx) for (hh, qi), mx in zip(units, mxs)]
    ps = [[jnp.exp(t - mx) for t in tq] for tq, mx in zip(tiles, mxs)]
    accs = [jnp.dot(po.astype(BF16), vns[hh], preferred_element_type=F32) for (hh, qi), po in zip(units, pos)]
    for n, (hh, qi) in enumerate(units):
        for i in range(N_SEL_PAGES):
            w = hh * n_pg + qi * N_SEL_PAGES + i
            accs[n] = accs[n] + jnp.dot(ps[n][i].astype(BF16), vbuf[slot, w].astype(BF16),
                                        preferred_element_type=F32)
    results = [jnp.zeros((ds, HD_ATT), F32) for _ in range(hps)]
    for n, (hh, qi) in enumerate(units):
        den = jnp.sum(pos[n], axis=-1, keepdims=True)
        for p in ps[n]:
            den = den + jnp.sum(p, axis=-1, keepdims=True)
        results[hh] = jnp.where(qrow == qi, accs[n] / den, results[hh])
    for hh in range(hps):
        o_ref[0, :, hsl[hh]] = results[hh]


def _attn_sample_call(page_table, ids, rel_bias, q, k_new, v_new, cache_k2, cache_v2, past_len):
    dbs, ds, _ = q.shape
    hps = SAMPLE_HEADS_PER_STEP
    assert H_ATT % hps == 0
    n_buf = hps * ds * N_SEL_PAGES
    nspec = pl.BlockSpec((1, ds, hps * HD_ATT), lambda b, g, p, i: (b, 0, g))
    pool = pl.BlockSpec(memory_space=pl.ANY)
    return pl.pallas_call(
        functools.partial(_attn_sample_kernel, past_len=past_len),
        grid_spec=pltpu.PrefetchScalarGridSpec(
            num_scalar_prefetch=2,
            grid=(dbs, H_ATT // hps),
            in_specs=[pl.BlockSpec(memory_space=pltpu.SMEM), nspec, nspec, nspec, pool, pool],
            out_specs=pl.BlockSpec((1, ds, hps * HD_ATT), lambda b, g, p, i: (b, 0, g)),
            scratch_shapes=[pltpu.VMEM((2, n_buf, PAGE_SIZE, HD_ATT), F32),
                            pltpu.VMEM((2, n_buf, PAGE_SIZE, HD_ATT), F32),
                            pltpu.SemaphoreType.DMA((2, 2))],
        ),
        out_shape=jax.ShapeDtypeStruct((dbs, ds, C_ATT), F32),
        compiler_params=_cparams(("arbitrary", "arbitrary")),
        name="attn_sample",
    )(page_table, ids, rel_bias, q, k_new, v_new, cache_k2, cache_v2)


def _rwkv_pre_kernel(h_ref, wz_ref, sh0_ref, mu_ref, w0_ref, w2_ref, a0_ref, a2_ref, g2_ref, kk_ref, ka_ref,
                     r_o, k_o, v_o, kk_o, b_o, ld_o, g_o, zl_o, carry):
    i = pl.program_id(1)
    h = h_ref[0]
    tm = h.shape[0]
    c = C_RWKV
    row = lax.broadcasted_iota(jnp.int32, (tm, 1), 0)

    @pl.when(i == 0)
    def _():
        carry[...] = sh0_ref[0]

    def project(lo, hi):
        z = jnp.dot(h, wz_ref[:, lo:hi], preferred_element_type=F32)
        first = carry[:, lo:hi]
        carry[:, lo:hi] = z[tm - 1:tm, :]
        zl_o[0, 0, :, lo:hi] = z[tm - 8:tm, :]
        zprev = jnp.where(row == 0, first, pltpu.roll(z, 1, 0))
        return z + (zprev - z) * mu_ref[:, lo:hi]

    zl = project(3 * c, Z_PAD)
    zk = project(c, 2 * c)
    zw, za, zg = zl[:, :LORA_PAD], zl[:, LORA_PAD:2 * LORA_PAD], zl[:, 2 * LORA_PAD:]
    wl = w0_ref[...] + jnp.dot(jnp.tanh(zw).astype(BF16), w2_ref[...], preferred_element_type=F32)
    w = -(jnp.maximum(-wl, 0.0) + jnp.log(1.0 + jnp.exp(-jnp.abs(wl)))) - 0.5
    ld_o[0] = -jnp.exp(w)
    a = jax.nn.sigmoid(a0_ref[...] + jnp.dot(za.astype(BF16), a2_ref[...], preferred_element_type=F32))
    g_o[0] = jnp.dot(jax.nn.sigmoid(zg).astype(BF16), g2_ref[...], preferred_element_type=F32)
    zr = project(0, c)
    kkr = zk * kk_ref[...]
    n2 = _segsum64(kkr * kkr, _seg_ones())
    kkn = kkr / jnp.maximum(jnp.sqrt(n2), 1e-12)
    k_o[0] = zk * (1.0 + (a - 1.0) * ka_ref[...])
    kk_o[0] = kkn
    b_o[0] = kkn * a
    zv = project(2 * c, 3 * c)
    r_o[0] = zr
    v_o[0] = zv


def _rwkv_pre_call(h, win, z_col0, shift0, p, *, tm):
    bsz, t, d = h.shape
    zp = Z_PAD
    assert z_col0 % zp == 0
    nt = t // tm
    c = C_RWKV
    row = lambda b, i: (0, 0)
    vec = lambda n: pl.BlockSpec((1, n), row)
    tile = pl.BlockSpec((1, tm, c), lambda b, i: (b, i, 0))
    return pl.pallas_call(
        _rwkv_pre_kernel,
        grid=(bsz, nt),
        in_specs=[pl.BlockSpec((1, tm, d), lambda b, i: (b, i, 0)),
                  pl.BlockSpec((d, zp), lambda b, i: (0, z_col0 // zp)),
                  pl.BlockSpec((1, 1, zp), lambda b, i: (b, 0, 0)),
                  vec(zp), vec(c),
                  pl.BlockSpec((LORA_PAD, c), row), vec(c),
                  pl.BlockSpec((LORA_PAD, c), row),
                  pl.BlockSpec((D_GATE_LORA, c), row), vec(c), vec(c)],
        out_specs=[tile] * 7 + [pl.BlockSpec((1, 1, 8, zp), lambda b, i: (b, i, 0, 0))],
        out_shape=[jax.ShapeDtypeStruct((bsz, t, c), F32)] * 7 + [jax.ShapeDtypeStruct((bsz, nt, 8, zp), F32)],
        scratch_shapes=[pltpu.VMEM((1, zp), F32)],
        compiler_params=_cparams(("parallel", "arbitrary")),
        name="rwkv_pre",
    )(h, win, shift0.reshape(bsz, 1, zp), p["mu"], p["w0"], p["w2"], p["a0"], p["a2"], p["g2"], p["k_k"], p["k_a"])


SOLVE_BASE = 8
CHUNK_PREC = {"gram": (1, 1), "state_read": (1, 1), "mkv": (1, 1), "solve1": (1, 1), "solve_sq": (1, 1),
              "solve_ap": (1, 1), "out": (1, 1), "state_upd": (1, 1)}


def _chunk_pairs(rs, ks, vs, kks, bs, lds, ss):
    c = CHUNK
    lane = lax.broadcasted_iota(jnp.int32, (1, LANES), 1)
    m_a = jnp.where(lane < HS_RWKV, 1.0, 0.0)
    m_b = 1.0 - m_a
    row = lax.broadcasted_iota(jnp.int32, (c, 2 * c), 0)
    coli = lax.broadcasted_iota(jnp.int32, (c, 2 * c), 1) % c
    strict = coli < row
    incl = coli <= row
    lr = lax.broadcasted_iota(jnp.int32, (c, c), 0)
    lc = lax.broadcasted_iota(jnp.int32, (c, c), 1)
    ltri = jnp.where(lc <= lr, 1.0, 0.0).astype(BF16)
    ones = jnp.ones((c, LANES), BF16)
    rr = lax.broadcasted_iota(jnp.int32, (LANES, LANES), 0) // HS_RWKV
    cc = lax.broadcasted_iota(jnp.int32, (LANES, LANES), 1) // HS_RWKV
    same_head = rr == cc

    def each(f, *lists):
        return [f(*args) for args in zip(*lists)]

    def stack2(x):
        return jnp.concatenate([x * m_a, x * m_b], axis=0)

    def prod(site, a, b, dims=_NN):
        na, nb = CHUNK_PREC[site]
        return _dotp(_split(a, na), _split(b, nb), dims, order=max(na, nb))

    def pm(site, mcats, xs):
        return each(lambda m, x: prod(site, m, stack2(x)), mcats, xs)

    ldp = each(lambda x: _split(x, 2), lds)
    cums = each(lambda p: _dotp([ltri], p, _NN), ldp)
    gcols = each(lambda p: jnp.exp(_dotp(p, [ones], _TN)), ldp)
    g_inv = each(lambda cu: jnp.exp(-cu), cums)
    g_end = each(lambda cu: jnp.exp(cu[c - 1:c, :] - cu), cums)
    p_all = each(lambda kk, r, cu, ld: jnp.concatenate([kk * jnp.exp(cu - ld), r * jnp.exp(cu)], axis=0),
                 kks, rs, cums, lds)
    z2 = each(lambda k, b, gi: jnp.concatenate([stack2(k * gi), stack2(b * gi)], axis=0), ks, bs, g_inv)
    g4 = each(lambda p, z: prod("gram", p, z, _NT), p_all, z2)
    mk = each(lambda g: jnp.where(strict, g[:c, :2 * c], 0.0), g4)
    pj = each(lambda g: jnp.where(strict, -g[:c, 2 * c:], 0.0), g4)
    akb = each(lambda g: jnp.concatenate([jnp.where(incl, g[c:, :2 * c], 0.0),
                                          jnp.where(incl, -g[c:, 2 * c:], 0.0)], axis=1), g4)
    ps = each(lambda p, s: prod("state_read", p, s), p_all, ss)
    mkv = pm("mkv", mk, vs)
    rhs = each(lambda p, m: p[:c] + m, ps, mkv)
    same_blk = lambda s: (row // s) == (coli // s)
    eye = jnp.where(coli == row, 1.0, 0.0)
    pk = each(lambda n_: jnp.where(same_blk(SOLVE_BASE), n_, 0.0), pj)
    ts = each(lambda p: eye + p, pk)
    n = 2
    while n < SOLVE_BASE:
        pk = pm("solve_sq", pk, pk)
        ts = each(lambda t, d: t + d, ts, pm("solve_ap", ts, pk))
        n *= 2
    s = SOLVE_BASE
    while s < c:
        low = same_blk(2 * s) & jnp.logical_not(same_blk(s))
        cs = each(lambda n_: jnp.where(low, -n_, 0.0), pj)
        ts = each(lambda t, d: t - d, ts, pm("solve_ap", ts, pm("solve_sq", cs, ts)))
        s *= 2
    us = pm("solve1", ts, rhs)
    ys = each(lambda p, m, v, u: p[c:] + prod("out", m, jnp.concatenate([stack2(v), stack2(u)], axis=0)),
              ps, akb, vs, us)
    upd = each(lambda k, b, ge, v, u: prod("state_upd", jnp.concatenate([k * ge, -(b * ge)], axis=0),
                                           jnp.concatenate([v, u], axis=0), _TN), ks, bs, g_end, vs, us)
    s_new = each(lambda gc, s, up: gc * s + jnp.where(same_head, up, 0.0), gcols, ss, upd)
    return ys, s_new


def _rwkv_out_pairs(ys, rs, ks, vs, gs, rks, lgs, lbs):
    p128 = _seg_ones()
    c = CHUNK
    inv = 1.0 / HS_RWKV

    def seg3(x):
        return _split(x, 3)

    def sum3(t, o):
        return t[o * c:(o + 1) * c] + t[(o + 1) * c:(o + 2) * c] + t[(o + 2) * c:(o + 3) * c]

    t1 = [jnp.dot(jnp.concatenate(seg3(y) + seg3(r * k * rk), axis=0), p128, preferred_element_type=F32)
          for y, r, k, rk in zip(ys, rs, ks, rks)]
    ycs = [y - sum3(t, 0) * inv for y, t in zip(ys, t1)]
    t2 = [jnp.dot(jnp.concatenate(seg3(yc * yc), axis=0), p128, preferred_element_type=F32) for yc in ycs]
    outs = []
    for yc, ta, tb, v, g, lg, lb in zip(ycs, t1, t2, vs, gs, lgs, lbs):
        yn = yc * lax.rsqrt(sum3(tb, 0) * inv + GN_EPS) * lg + lb
        outs.append(((yn + sum3(ta, 3) * v) * g).astype(BF16))
    return outs


def _rwkv_chunk_kernel(r_ref, k_ref, v_ref, kk_ref, b_ref, ld_ref, g_ref, rk_ref, lg_ref, lb_ref, s0_ref,
                       o_ref, so_ref, s_scr):
    ci = pl.program_id(2)

    @pl.when(ci == 0)
    def _():
        s_scr[...] = s0_ref[...]

    nb = r_ref.shape[0]
    units = [(bi, p, slice(p * LANES, (p + 1) * LANES)) for bi in range(nb) for p in range(PAIRS_PER_STEP)]
    pairs = lambda ref: [ref[bi, :, sl] for bi, _, sl in units]
    vecs = lambda ref: [ref[:, sl] for _, _, sl in units]
    rs, ks, vs = pairs(r_ref), pairs(k_ref), pairs(v_ref)
    ys, s_new = _chunk_pairs(rs, ks, vs, pairs(kk_ref), pairs(b_ref), pairs(ld_ref),
                             [s_scr[bi, p] for bi, p, _ in units])
    outs = _rwkv_out_pairs(ys, rs, ks, vs, pairs(g_ref), vecs(rk_ref), vecs(lg_ref), vecs(lb_ref))
    for u, (bi, p, sl) in enumerate(units):
        o_ref[bi, :, sl] = outs[u]
        s_scr[bi, p] = s_new[u]
        so_ref[bi, p] = s_new[u]


def _rwkv_chunk_call(r, k, v, kk, b, logd, g, s0_blk, p):
    bsz, t, c = r.shape
    pp = PAIRS_PER_STEP
    nb = BATCH_PER_STEP
    assert bsz % nb == 0
    w = pp * LANES
    tile = pl.BlockSpec((nb, CHUNK, w), lambda bi, pi, ci: (bi, ci, pi))
    vec = pl.BlockSpec((1, w), lambda bi, pi, ci: (0, pi))
    sspec = pl.BlockSpec((nb, pp, LANES, LANES), lambda bi, pi, ci: (bi, pi, 0, 0))
    return pl.pallas_call(
        _rwkv_chunk_kernel,
        grid=(bsz // nb, c // w, t // CHUNK),
        in_specs=[tile] * 7 + [vec] * 3 + [sspec],
        out_specs=[tile, sspec],
        out_shape=[jax.ShapeDtypeStruct((bsz, t, c), BF16),
                   jax.ShapeDtypeStruct(s0_blk.shape, F32)],
        scratch_shapes=[pltpu.VMEM((nb, pp, LANES, LANES), F32)],
        compiler_params=_cparams(("parallel", "parallel", "arbitrary")),
        name="rwkv_chunk",
    )(r, k, v, kk, b, logd, g, p["r_k"], p["lnx_g"], p["lnx_b"], s0_blk)


def _combine_kernel(x_ref, ada_ref, lng_ref, lnb_ref, oa_ref, or_ref, ga_ref, gr_ref, wa_ref, wr_ref, wo_ref, o_ref):
    bb, t, d = x_ref.shape
    ua = jnp.dot(oa_ref[...], wa_ref[...], preferred_element_type=F32)
    ur = jnp.dot(or_ref[...], wr_ref[...], preferred_element_type=F32)
    m = jax.nn.sigmoid(ga_ref[...]) * ua + jax.nn.sigmoid(gr_ref[...]) * ur
    mo = jnp.dot(m.astype(BF16), wo_ref[...], preferred_element_type=F32)
    y = DEEPNORM_ALPHA * x_ref[...] + ada_ref[:, 5:6, :] * mo.reshape(bb, t, d)
    o_ref[...] = _ln(y) * lng_ref[1:2, :] + lnb_ref[1:2, :]


def _combine_call(x, ada, ln_g, ln_b, oa, orw, ga, gr, wa, wr, wo, *, blk):
    bsz, t, d = x.shape
    bb, tt = blk
    nt = t // tt
    rows = bb * tt
    xmap = lambda i: (i // nt, i % nt, 0)
    const = lambda i: (0, 0)
    rowt = lambda n: pl.BlockSpec((rows, n), lambda i: (i, 0))
    return pl.pallas_call(
        _combine_kernel,
        grid=((bsz // bb) * nt,),
        in_specs=[pl.BlockSpec((bb, tt, d), xmap),
                  pl.BlockSpec((bb, N_ADA, d), lambda i: (i // nt, 0, 0)),
                  pl.BlockSpec((3, d), const), pl.BlockSpec((3, d), const),
                  rowt(C_ATT), rowt(C_RWKV), rowt(d), rowt(d),
                  pl.BlockSpec((C_ATT, d), const), pl.BlockSpec((C_RWKV, d), const), pl.BlockSpec((d, d), const)],
        out_specs=pl.BlockSpec((bb, tt, d), xmap),
        out_shape=jax.ShapeDtypeStruct(x.shape, F32),
        compiler_params=_cparams(("parallel",)),
        name="combine",
    )(x, ada, ln_g, ln_b, oa, orw, ga, gr, wa, wr, wo)


def _rearrange_z(a):
    c3 = 3 * C_RWKV
    pad = [(0, 0)] * (a.ndim - 1) + [(0, LORA_PAD - D_DECAY_LORA)]
    return jnp.concatenate([a[..., :c3],
                            jnp.pad(a[..., c3:c3 + D_DECAY_LORA], pad),
                            jnp.pad(a[..., c3 + D_DECAY_LORA:c3 + D_DECAY_LORA + D_AAA_LORA], pad),
                            a[..., c3 + D_DECAY_LORA + D_AAA_LORA:]], axis=-1)


def _unarrange_z(a):
    c3 = 3 * C_RWKV
    return jnp.concatenate([a[..., :c3], a[..., c3:c3 + D_DECAY_LORA],
                            a[..., c3 + LORA_PAD:c3 + LORA_PAD + D_AAA_LORA], a[..., c3 + 2 * LORA_PAD:]], axis=-1)


def _state_to_blocks(state):
    bsz = state.shape[0]
    st = jnp.swapaxes(state, -1, -2).reshape(bsz, H_RWKV // 2, 2, HS_RWKV, HS_RWKV)
    blk = jnp.einsum('bphkv,hg->bphkgv', st, jnp.eye(2, dtype=state.dtype))
    return blk.reshape(bsz, H_RWKV // 2, LANES, LANES)


def _blocks_to_state(blk):
    bsz = blk.shape[0]
    b6 = blk.reshape(bsz, H_RWKV // 2, 2, HS_RWKV, 2, HS_RWKV)
    st = jnp.stack([b6[:, :, 0, :, 0, :], b6[:, :, 1, :, 1, :]], axis=2)
    return jnp.swapaxes(st, -1, -2).reshape(bsz, H_RWKV, HS_RWKV, HS_RWKV)


def _trunk(x, ada, attend, shift0, wkv0, w, *, ffn_blk, mm_tm, pre_tm, comb_blk):
    bsz, t, d = x.shape
    m = bsz * t
    ffn = functools.partial(_ffn_call, ada=ada, ln_g=w["ln_g"], ln_b=w["ln_b"], wi=w["ffn_wi"], wo=w["ffn_wo"],
                            layer=w["layer"], blk=ffn_blk)
    x1, h2 = ffn(x, which=0, sub=0, emit_next=True)
    h2f = h2.reshape(m, d)
    win = w["win"]
    proj = functools.partial(_mm_call, h2f, win, tm=mm_tm)
    q = proj(tn=C_ATT, col0=0, n=C_ATT, name="proj_q").reshape(bsz, t, C_ATT)
    k = proj(tn=C_ATT, col0=C_ATT, n=C_ATT, name="proj_k").reshape(bsz, t, C_ATT)
    v = proj(tn=C_ATT, col0=2 * C_ATT, n=C_ATT, name="proj_v").reshape(bsz, t, C_ATT)
    ga = proj(tn=d // 2, col0=3 * C_ATT, n=d, name="proj_ga")
    gr = proj(tn=d // 2, col0=3 * C_ATT + d, n=d, name="proj_gr")
    o_att = attend(q, k, v)
    r, k2, vv, kk, b, logd, g, z_last = _rwkv_pre_call(h2, win, 3 * C_ATT + 2 * d, _rearrange_z(shift0), w, tm=pre_tm)
    tp = -(-t // CHUNK) * CHUNK
    seq = [r, k2, vv, kk, b, logd, g]
    if tp != t:
        seq = [jnp.pad(a, ((0, 0), (0, tp - t), (0, 0))) for a in seq]
    o_rwkv, s_blk = _rwkv_chunk_call(*seq, _state_to_blocks(wkv0), w)
    o_rwkv = o_rwkv[:, :t].reshape(m, C_RWKV)
    x2 = _combine_call(x1, ada, w["ln_g"], w["ln_b"], o_att.reshape(m, C_ATT), o_rwkv, ga, gr,
                       w["wua"], w["wur"], w["wout"], blk=comb_blk)
    x3 = ffn(x2, which=1, sub=2, emit_next=False)
    shift_new = _unarrange_z(z_last[:, -1, 7])
    return (x3, k.reshape(bsz, t, H_ATT, HD_ATT), v.reshape(bsz, t, H_ATT, HD_ATT), _blocks_to_state(s_blk), shift_new)


def kernel(x_prompt, x_sample, cache_k, cache_v, state_wkv, state_shift, page_table, c_prompt, c_sample, rel_bias, w_ada, b_ada, ln_g, ln_b, ffn_wi, ffn_wo, w_in, mu_shift, w0, w2, a0, a2, g2, k_k, k_a, r_k, lnx_g, lnx_b, w_up_attn, w_up_rwkv, w_out):
    assert w_ada.shape[0] == DEPTH == 1
    bsz, s_len, d = x_prompt.shape
    dbs, ds, _ = x_sample.shape
    past_len = page_table.shape[1] * PAGE_SIZE
    n_phys = cache_k.shape[1]
    l = 0
    win = w_in[l]
    c3 = 3 * C_ATT
    lora_rows = ((0, LORA_PAD - D_DECAY_LORA), (0, 0))
    w = {
        "ln_g": ln_g[l], "ln_b": ln_b[l],
        "layer": l, "ffn_wi": ffn_wi.astype(BF16), "ffn_wo": ffn_wo.astype(BF16),
        "win": jnp.concatenate([win[:, :c3], win[:, c3 + RWKV_PROJ:],
                                _rearrange_z(win[:, c3:c3 + RWKV_PROJ])], axis=1).astype(BF16),
        "mu": _rearrange_z(mu_shift[l])[None, :],
        "w0": w0[l][None, :], "w2": jnp.pad(w2[l], lora_rows).astype(BF16),
        "a0": a0[l][None, :], "a2": jnp.pad(a2[l], lora_rows).astype(BF16),
        "g2": g2[l].astype(BF16), "k_k": k_k[l][None, :], "k_a": k_a[l][None, :],
        "r_k": r_k[l].reshape(1, C_RWKV), "lnx_g": lnx_g[l][None, :], "lnx_b": lnx_b[l][None, :],
        "wua": w_up_attn[l].astype(BF16), "wur": w_up_rwkv[l].astype(BF16), "wout": w_out[l].astype(BF16),
    }
    n_c = bsz + dbs
    c_rows = -(-n_c // 8) * 8
    c_all = jnp.concatenate([c_prompt, c_sample, jnp.zeros((c_rows - n_c, d), F32)], axis=0)
    ada = _ada_call(c_all, w_ada[l], b_ada[l]).reshape(c_rows, N_ADA, d)

    bias_tiles = _bias_tiles_call(rel_bias)
    far = rel_bias[NUM_BUCKETS - 1]
    attend_p = lambda q, k, v: _attn_prompt_call(q, k, v, bias_tiles, far)
    yp, kp, vp, wp, sp = _trunk(
        x_prompt, ada[:bsz], attend_p, jnp.zeros((bsz, RWKV_PROJ), F32),
        jnp.zeros((bsz, H_RWKV, HS_RWKV, HS_RWKV), F32), w,
        ffn_blk=(1, 512), mm_tm=1024, pre_tm=256, comb_blk=(1, 256))

    cache_k2 = cache_k.reshape(DEPTH * n_phys, PAGE_SIZE, H_ATT, HD_ATT)
    cache_v2 = cache_v.reshape(DEPTH * n_phys, PAGE_SIZE, H_ATT, HD_ATT)
    page_table = page_table + l * n_phys
    n_full = past_len // MOBA_BLOCK

    def attend_s(q, k, v):
        means = _cache_means_call(page_table, cache_k2, n_full).reshape(dbs, n_full, C_ATT)
        ids = _topk_ids_call(q, means)[..., :MOBA_TOPK]
        o = _attn_sample_call(page_table, ids.reshape(-1), rel_bias, q, k, v, cache_k2, cache_v2, past_len)
        return o.astype(BF16)

    ys, kn, vn, wn, sn = _trunk(
        x_sample, ada[bsz:n_c], attend_s, state_shift[l], state_wkv[l], w,
        ffn_blk=(dbs, ds), mm_tm=dbs * ds, pre_tm=ds, comb_blk=(dbs, ds))

    return (yp, ys, kp[None], vp[None], kn[None], vn[None], wp[None], wn[None], sp[None], sn[None])
```

```python
import functools
import math

import jax
import jax.numpy as jnp
from jax import lax
from jax.experimental import pallas as pl
from jax.experimental.pallas import tpu as pltpu

F32 = jnp.float32
BF16 = jnp.bfloat16

D_MODEL = 2048
D_FF = 5632
N_ADA = 9
H_ATT = 8
HD_ATT = 128
C_ATT = H_ATT * HD_ATT
MOBA_BLOCK = 256
MOBA_TOPK = 3
NUM_BUCKETS = 32
MAX_DISTANCE = 128
PAGE_SIZE = 128
HS_RWKV = 64
C_RWKV = 1024
H_RWKV = C_RWKV // HS_RWKV
D_DECAY_LORA = 96
D_AAA_LORA = 96
D_GATE_LORA = 256
RWKV_PROJ = 3 * C_RWKV + D_DECAY_LORA + D_AAA_LORA + D_GATE_LORA
LORA_PAD = 128
Z_PAD = 3 * C_RWKV + 2 * LORA_PAD + D_GATE_LORA
GN_EPS = 64e-5
LN_EPS = 1e-5
DEPTH = 1
DEEPNORM_ALPHA = (2 * DEPTH) ** 0.25
NEG_BIG = -1e30

LANES = 128
CHUNK = 64
BATCH_PER_STEP = 2
PAIRS_PER_STEP = 8
VMEM_LIMIT = 56 * 1024 * 1024


def _t5_thresholds():
    max_exact = NUM_BUCKETS // 2
    thr = list(range(1, max_exact + 1))
    for b in range(max_exact + 1, NUM_BUCKETS):
        x = max_exact * (MAX_DISTANCE / max_exact) ** ((b - max_exact) / (NUM_BUCKETS - max_exact))
        thr.append(int(math.ceil(x)))
    return tuple(thr)


T5_THR = _t5_thresholds()


def _cparams(sem, vmem=VMEM_LIMIT):
    return pltpu.CompilerParams(dimension_semantics=sem, vmem_limit_bytes=vmem)


def _ln(x):
    mu = jnp.mean(x, axis=-1, keepdims=True)
    xc = x - mu
    var = jnp.mean(xc * xc, axis=-1, keepdims=True)
    return xc * lax.rsqrt(var + LN_EPS)


def _split(x, n):
    parts = []
    for i in range(n):
        p = x.astype(BF16)
        parts.append(p)
        if i + 1 < n:
            x = x - p.astype(F32)
    return parts


_NN = (((1,), (0,)), ((), ()))
_NT = (((1,), (1,)), ((), ()))
_TN = (((0,), (0,)), ((), ()))


def _dotp(ap, bp, dims=_NN, order=None):
    if order is None:
        order = max(len(ap), len(bp))
    acc = None
    if dims == _NN and len(ap) > 1:
        m = ap[0].shape[0]
        for j, b in enumerate(bp):
            sel = [a for i, a in enumerate(ap) if i + j < order]
            if not sel:
                continue
            lhs = sel[0] if len(sel) == 1 else jnp.concatenate(sel, axis=0)
            t = lax.dot_general(lhs, b, dims, preferred_element_type=F32)
            for r in range(len(sel)):
                part = t[r * m:(r + 1) * m]
                acc = part if acc is None else acc + part
        return acc
    for i, a in enumerate(ap):
        for j, b in enumerate(bp):
            if i + j < order:
                t = lax.dot_general(a, b, dims, preferred_element_type=F32)
                acc = t if acc is None else acc + t
    return acc


def _t5_bias(rel, tbl_ref, h):
    bias = jnp.full(rel.shape, tbl_ref[0, h], F32)
    for b in range(1, NUM_BUCKETS):
        bias = jnp.where(rel >= T5_THR[b - 1], tbl_ref[b, h], bias)
    return bias


def _seg_ones():
    r = lax.broadcasted_iota(jnp.int32, (LANES, LANES), 0) // HS_RWKV
    c = lax.broadcasted_iota(jnp.int32, (LANES, LANES), 1) // HS_RWKV
    return jnp.where(r == c, 1.0, 0.0).astype(BF16)


def _segsum64(x, p128):
    outs = []
    for j in range(x.shape[-1] // LANES):
        xs = x[:, j * LANES:(j + 1) * LANES]
        outs.append(_dotp(_split(xs, 3), [p128]))
    return jnp.concatenate(outs, axis=-1)


ADA_TN = 1024


def _ada_kernel(c_ref, w_ref, b_ref, o_ref):
    c = c_ref[...]
    sc = c * jax.nn.sigmoid(c)
    rows = sc.shape[0]
    s0, s1, s2 = _split(sc, 3)
    w_hi, w_lo = _split(w_ref[...], 2)
    a = jnp.dot(jnp.concatenate([s0, s1, s2], axis=0), w_hi, preferred_element_type=F32)
    b = jnp.dot(jnp.concatenate([s0, s1], axis=0), w_lo, preferred_element_type=F32)
    o_ref[...] = (a[:rows] + a[rows:2 * rows] + a[2 * rows:] + b[:rows] + b[rows:]) + b_ref[...]


def _ada_call(c_all, w_ada, b_ada):
    rows, d = c_all.shape
    n = w_ada.shape[1]
    return pl.pallas_call(
        _ada_kernel,
        grid=(n // ADA_TN,),
        in_specs=[pl.BlockSpec((rows, d), lambda j: (0, 0)),
                  pl.BlockSpec((d, ADA_TN), lambda j: (0, j)),
                  pl.BlockSpec((1, ADA_TN), lambda j: (0, j))],
        out_specs=pl.BlockSpec((rows, ADA_TN), lambda j: (0, j)),
        out_shape=jax.ShapeDtypeStruct((rows, n), F32),
        compiler_params=_cparams(("parallel",)),
        name="ada",
    )(c_all, w_ada, b_ada.reshape(1, n))


FFN_TF = 512
FFN_TF_SMALL_M = 1408


def _ffn_kernel(x_ref, ada_ref, lng_ref, lnb_ref, wig_ref, wiu_ref, wo_ref, *rest, sub, emit_next):
    if emit_next:
        o_ref, h_next_ref, h_scr, acc = rest
    else:
        o_ref, h_scr, acc = rest
    bb, t, d = x_ref.shape
    j = pl.program_id(1)

    @pl.when(j == 0)
    def _():
        h = _ln(x_ref[...]) * (1.0 + ada_ref[:, 3 * sub + 1:3 * sub + 2, :]) + ada_ref[:, 3 * sub:3 * sub + 1, :]
        h_scr[...] = h.reshape(bb * t, d).astype(BF16)
        acc[...] = jnp.zeros_like(acc)

    h = h_scr[...]
    g = jnp.dot(h, wig_ref[...], preferred_element_type=F32)
    u = jnp.dot(h, wiu_ref[...], preferred_element_type=F32)
    act = (g * jax.nn.sigmoid(g)) * u
    acc[...] += jnp.dot(act.astype(BF16), wo_ref[...], preferred_element_type=F32)

    @pl.when(j == pl.num_programs(1) - 1)
    def _():
        gate = ada_ref[:, 3 * sub + 2:3 * sub + 3, :]
        y = DEEPNORM_ALPHA * x_ref[...] + 0.5 * gate * acc[...].reshape(bb, t, d)
        y = _ln(y) * lng_ref[sub:sub + 1, :] + lnb_ref[sub:sub + 1, :]
        o_ref[...] = y
        if emit_next:
            nxt = sub + 1
            hn = _ln(y) * (1.0 + ada_ref[:, 3 * nxt + 1:3 * nxt + 2, :]) + ada_ref[:, 3 * nxt:3 * nxt + 1, :]
            h_next_ref[...] = hn.astype(BF16)


def _ffn_call(x, ada, ln_g, ln_b, wi, wo, *, layer, which, sub, emit_next, blk):
    bsz, t, d = x.shape
    bb, tt = blk
    tf = FFN_TF if bb * tt >= FFN_TF else FFN_TF_SMALL_M
    assert D_FF % tf == 0
    nj = D_FF // tf
    nt = t // tt
    grid = ((bsz // bb) * nt, nj)
    xmap = lambda i, j: (i // nt, i % nt, 0)
    amap = lambda i, j: (i // nt, 0, 0)
    out_shape = [jax.ShapeDtypeStruct(x.shape, F32)]
    out_specs = [pl.BlockSpec((bb, tt, d), xmap)]
    if emit_next:
        out_shape.append(jax.ShapeDtypeStruct(x.shape, BF16))
        out_specs.append(pl.BlockSpec((bb, tt, d), xmap))
    res = pl.pallas_call(
        functools.partial(_ffn_kernel, sub=sub, emit_next=emit_next),
        grid=grid,
        in_specs=[pl.BlockSpec((bb, tt, d), xmap),
                  pl.BlockSpec((bb, N_ADA, d), amap),
                  pl.BlockSpec((3, d), lambda i, j: (0, 0)),
                  pl.BlockSpec((3, d), lambda i, j: (0, 0)),
                  pl.BlockSpec((None, None, d, tf), lambda i, j: (layer, which, 0, j)),
                  pl.BlockSpec((None, None, d, tf), lambda i, j: (layer, which, 0, j + nj)),
                  pl.BlockSpec((None, None, tf, d), lambda i, j: (layer, which, j, 0))],
        out_specs=out_specs,
        out_shape=out_shape,
        scratch_shapes=[pltpu.VMEM((bb * tt, d), BF16), pltpu.VMEM((bb * tt, d), F32)],
        compiler_params=_cparams(("parallel", "arbitrary")),
        name=f"ffn{sub}",
    )(x, ada, ln_g, ln_b, wi, wi, wo)
    return res if emit_next else res[0]


def _mm_kernel(h_ref, w_ref, o_ref):
    o_ref[...] = jnp.dot(h_ref[...], w_ref[...], preferred_element_type=F32).astype(o_ref.dtype)


def _mm_call(h, w, *, tm, tn, col0, n, out_dtype=F32, name="proj"):
    m, k = h.shape
    assert col0 % tn == 0 and n % tn == 0
    cb = col0 // tn
    return pl.pallas_call(
        _mm_kernel,
        grid=(m // tm, n // tn),
        in_specs=[pl.BlockSpec((tm, k), lambda i, j: (i, 0)),
                  pl.BlockSpec((k, tn), lambda i, j: (0, j + cb))],
        out_specs=pl.BlockSpec((tm, tn), lambda i, j: (i, j)),
        out_shape=jax.ShapeDtypeStruct((m, n), out_dtype),
        compiler_params=_cparams(("parallel", "arbitrary")),
        name=name,
    )(h, w)


def _bias_tiles_kernel(tbl_ref, o_ref):
    h = pl.program_id(0)
    r = lax.broadcasted_iota(jnp.int32, (MOBA_BLOCK, MOBA_BLOCK), 0)
    c = lax.broadcasted_iota(jnp.int32, (MOBA_BLOCK, MOBA_BLOCK), 1)
    rel0 = r - c
    o_ref[0, 0] = jnp.where(rel0 >= 0, _t5_bias(rel0, tbl_ref, h), NEG_BIG)
    o_ref[0, 1] = _t5_bias(rel0 + MOBA_BLOCK, tbl_ref, h)


def _bias_tiles_call(rel_bias):
    return pl.pallas_call(
        _bias_tiles_kernel,
        grid=(H_ATT,),
        in_specs=[pl.BlockSpec(memory_space=pltpu.SMEM)],
        out_specs=pl.BlockSpec((1, 2, MOBA_BLOCK, MOBA_BLOCK), lambda h: (h, 0, 0, 0)),
        out_shape=jax.ShapeDtypeStruct((H_ATT, 2, MOBA_BLOCK, MOBA_BLOCK), F32),
        compiler_params=_cparams(("arbitrary",)),
        name="bias_tiles",
    )(rel_bias)


CAND_ROWS = 16


def _rank_select(st, n_valid):
    cand = lax.broadcasted_iota(jnp.int32, st.shape, 0)
    cnt = jnp.zeros(st.shape, jnp.int32)
    for m in range(n_valid):
        sm = st[m:m + 1, :]
        beats = (sm > st) | ((sm == st) & (m < cand))
        cnt = cnt + jnp.where(beats, 1, 0)
    sel = jnp.where((cand < n_valid) & (cnt < MOBA_TOPK), 1.0, 0.0).astype(BF16)
    eye = jnp.where(lax.broadcasted_iota(jnp.int32, (CAND_ROWS, LANES), 0)
                    == lax.broadcasted_iota(jnp.int32, (CAND_ROWS, LANES), 1), 1.0, 0.0).astype(BF16)
    sel_cols = lax.dot_general(sel, eye, _TN, preferred_element_type=F32)
    return jnp.where(sel_cols > 0.5, 0.0, NEG_BIG)


def _attn_prompt_kernel(far_ref, q_ref, k_ref, v_ref, bt_ref, o_ref):
    h = pl.program_id(1)
    s_len = q_ref.shape[1]
    nb = s_len // MOBA_BLOCK
    q = q_ref[0]
    k = k_ref[0]
    kb = k.astype(BF16)
    vb = v_ref[0].astype(BF16)
    qb = (q * (HD_ATT ** -0.5)).astype(BF16)
    means = jnp.mean(k.reshape(nb, MOBA_BLOCK, HD_ATT), axis=1)
    means = jnp.concatenate([means, jnp.zeros((CAND_ROWS - nb, HD_ATT), F32)], axis=0)
    mparts = _split(means, 3)
    far = far_ref[h]
    rows = [slice(i * MOBA_BLOCK, (i + 1) * MOBA_BLOCK) for i in range(nb)]
    logits = [lax.dot_general(qb[rows[i]], kb[:(i + 1) * MOBA_BLOCK], _NT, preferred_element_type=F32)
              for i in range(nb)]
    scores = [None] + [_dotp(mparts, _split(q[rows[i]], 3), _NT, order=3) for i in range(1, nb)]
    negs = [None] + [_rank_select(scores[i], i) for i in range(1, nb)]
    probs, dens = [], []
    for i in range(nb):
        tiles = []
        for j in range(i + 1):
            tile = logits[i][:, j * MOBA_BLOCK:(j + 1) * MOBA_BLOCK]
            if j == i:
                tile = tile + bt_ref[0, 0]
            elif j == i - 1:
                tile = tile + bt_ref[0, 1] + negs[i][:, j:j + 1]
            else:
                tile = tile + (far + negs[i][:, j:j + 1])
            tiles.append(tile)
        lg = jnp.concatenate(tiles, axis=-1) if len(tiles) > 1 else tiles[0]
        mx = jnp.max(lg, axis=-1, keepdims=True)
        p = jnp.exp(lg - mx)
        dens.append(jnp.sum(p, axis=-1, keepdims=True))
        probs.append(p.astype(BF16))
    for i in range(nb):
        out = jnp.dot(probs[i], vb[:(i + 1) * MOBA_BLOCK], preferred_element_type=F32) / dens[i]
        o_ref[0, rows[i], :] = out.astype(o_ref.dtype)


def _attn_prompt_call(q, k, v, bias_tiles, far):
    bsz, s_len, _ = q.shape
    qspec = pl.BlockSpec((1, s_len, HD_ATT), lambda b, h: (b, 0, h))
    return pl.pallas_call(
        _attn_prompt_kernel,
        grid=(bsz, H_ATT),
        in_specs=[pl.BlockSpec(memory_space=pltpu.SMEM), qspec, qspec, qspec,
                  pl.BlockSpec((1, 2, MOBA_BLOCK, MOBA_BLOCK), lambda b, h: (h, 0, 0, 0))],
        out_specs=pl.BlockSpec((1, s_len, HD_ATT), lambda b, h: (b, 0, h)),
        out_shape=jax.ShapeDtypeStruct((bsz, s_len, C_ATT), BF16),
        compiler_params=_cparams(("parallel", "arbitrary")),
        name="attn_prompt",
    )(far, q, k, v, bias_tiles)


MEAN_BLOCKS = 16
PAGES_PER_BLOCK = MOBA_BLOCK // PAGE_SIZE


def _cache_means_kernel(pt_ref, *refs):
    page_refs, o_ref = refs[:-1], refs[-1]
    for m in range(MEAN_BLOCKS):
        tot = None
        for u in range(PAGES_PER_BLOCK):
            s = jnp.sum(page_refs[m * PAGES_PER_BLOCK + u][0], axis=0)
            tot = s if tot is None else tot + s
        o_ref[0, m] = tot * (1.0 / MOBA_BLOCK)


def _cache_means_call(page_table, cache_k4, n_blocks):
    dbs = page_table.shape[0]
    npg = MEAN_BLOCKS * PAGES_PER_BLOCK
    in_specs = [pl.BlockSpec((1, PAGE_SIZE, H_ATT, HD_ATT), functools.partial(
        lambda b, g, pt, u: (pt[b, g * npg + u], 0, 0, 0), u=u)) for u in range(npg)]
    return pl.pallas_call(
        _cache_means_kernel,
        grid_spec=pltpu.PrefetchScalarGridSpec(
            num_scalar_prefetch=1,
            grid=(dbs, n_blocks // MEAN_BLOCKS),
            in_specs=in_specs,
            out_specs=pl.BlockSpec((1, MEAN_BLOCKS, H_ATT, HD_ATT), lambda b, g, pt: (b, g, 0, 0)),
        ),
        out_shape=jax.ShapeDtypeStruct((dbs, n_blocks, H_ATT, HD_ATT), F32),
        compiler_params=_cparams(("parallel", "arbitrary")),
        name="cache_means",
    )(page_table, *([cache_k4] * npg))


def _topk_ids_kernel(q_ref, m_ref, o_ref):
    q = q_ref[0]
    means = m_ref[0]
    n_blocks = means.shape[0]
    ds = q.shape[0]
    pad = jnp.zeros((LANES - n_blocks, HD_ATT), F32)
    col = lax.broadcasted_iota(jnp.int32, (ds, LANES), 1)
    for h in range(H_ATT):
        sl = slice(h * HD_ATT, (h + 1) * HD_ATT)
        mh = jnp.concatenate([means[:, sl], pad], axis=0) if n_blocks < LANES else means[:, sl]
        s = _dotp(_split(q[:, sl], 3), _split(mh, 3), _NT, order=3)
        s = jnp.where(col < n_blocks, s, NEG_BIG)
        ids = jnp.zeros((ds, LANES), jnp.int32)
        for t in range(MOBA_TOPK):
            mx = jnp.max(s, axis=-1, keepdims=True)
            idx = jnp.min(jnp.where(s == mx, col, LANES), axis=-1, keepdims=True)
            ids = jnp.where(col == t, idx, ids)
            s = jnp.where(col == idx, NEG_BIG * 2, s)
        o_ref[0, h] = ids


def _topk_ids_call(q, means):
    dbs, ds, _ = q.shape
    n_blocks = means.shape[1]
    return pl.pallas_call(
        _topk_ids_kernel,
        grid=(dbs,),
        in_specs=[pl.BlockSpec((1, ds, C_ATT), lambda b: (b, 0, 0)),
                  pl.BlockSpec((1, n_blocks, C_ATT), lambda b: (b, 0, 0))],
        out_specs=pl.BlockSpec((1, H_ATT, ds, LANES), lambda b: (b, 0, 0, 0)),
        out_shape=jax.ShapeDtypeStruct((dbs, H_ATT, ds, LANES), jnp.int32),
        compiler_params=_cparams(("parallel",)),
        name="topk_ids",
    )(q, means)


N_SEL_PAGES = MOBA_TOPK * PAGES_PER_BLOCK
SAMPLE_HEADS_PER_STEP = 2


def _attn_sample_kernel(pt_ref, ids_ref, tbl_ref, q_ref, kn_ref, vn_ref, ck_hbm, cv_hbm, o_ref,
                        kbuf, vbuf, sems, *, past_len):
    ds = q_ref.shape[1]
    hps = SAMPLE_HEADS_PER_STEP
    n_pg = ds * N_SEL_PAGES
    b, hg = pl.program_id(0), pl.program_id(1)
    ng = pl.num_programs(1)
    step = b * ng + hg
    n_steps = pl.num_programs(0) * ng
    slot = step % 2

    def page_copies(bb, gg, sl):
        cps = []
        for hh in range(hps):
            head = gg * hps + hh
            base = (bb * H_ATT + head) * ds * MOBA_TOPK
            for u in range(n_pg):
                pg = pt_ref[bb, ids_ref[base + u // PAGES_PER_BLOCK] * PAGES_PER_BLOCK + u % PAGES_PER_BLOCK]
                w = hh * n_pg + u
                cps.append(pltpu.make_async_copy(ck_hbm.at[pg, :, head, :], kbuf.at[sl, w], sems.at[sl, 0]))
                cps.append(pltpu.make_async_copy(cv_hbm.at[pg, :, head, :], vbuf.at[sl, w], sems.at[sl, 1]))
        return cps

    def slot_waits(sl):
        n_buf = hps * n_pg
        return [pltpu.make_async_copy(ck_hbm.at[pl.ds(0, n_buf), :, 0, :], kbuf.at[sl], sems.at[sl, 0]),
                pltpu.make_async_copy(cv_hbm.at[pl.ds(0, n_buf), :, 0, :], vbuf.at[sl], sems.at[sl, 1])]

    @pl.when(step == 0)
    def _():
        for cp in page_copies(b, hg, slot):
            cp.start()

    @pl.when(step + 1 < n_steps)
    def _():
        nxt = step + 1
        for cp in page_copies(nxt // ng, nxt % ng, 1 - slot):
            cp.start()

    for cp in slot_waits(slot):
        cp.wait()
    qrow = lax.broadcasted_iota(jnp.int32, (ds, 1), 0)
    lane = lax.broadcasted_iota(jnp.int32, (ds, PAGE_SIZE), 1)
    rel_o = lax.broadcasted_iota(jnp.int32, (ds, ds), 0) - lax.broadcasted_iota(jnp.int32, (ds, ds), 1)
    heads = [hg * hps + hh for hh in range(hps)]
    hsl = [slice(hh * HD_ATT, (hh + 1) * HD_ATT) for hh in range(hps)]
    qbs = [(q_ref[0, :, sl] * (HD_ATT ** -0.5)).astype(BF16) for sl in hsl]
    vns = [vn_ref[0, :, sl].astype(BF16) for sl in hsl]
    los = [lax.dot_general(qb, kn_ref[0, :, sl].astype(BF16), _NT, preferred_element_type=F32)
           for qb, sl in zip(qbs, hsl)]
    los = [jnp.where(rel_o >= 0, lo + _t5_bias(rel_o, tbl_ref, head), NEG_BIG) for lo, head in zip(los, heads)]
    mx_os = [jnp.max(lo, axis=-1, keepdims=True) for lo in los]
    units = [(hh, qi) for hh in range(hps) for qi in range(ds)]
    tiles = []
    for hh, qi in units:
        flat = (b * H_ATT + heads[hh]) * ds + qi
        tq = []
        for s in range(MOBA_TOPK):
            blk = ids_ref[flat * MOBA_TOPK + s]
            for u in range(PAGES_PER_BLOCK):
                w = hh * n_pg + (qi * MOBA_TOPK + s) * PAGES_PER_BLOCK + u
                kpg = kbuf[slot, w].astype(BF16)
                lg = lax.dot_general(qbs[hh], kpg, _NT, preferred_element_type=F32)
                rel = (past_len + qi) - (blk * MOBA_BLOCK + u * PAGE_SIZE + lane)
                tq.append(lg + _t5_bias(rel, tbl_ref, heads[hh]))
        tiles.append(tq)
    mxs = []
    for (hh, qi), tq in zip(units, tiles):
        mx = mx_os[hh]
        for t in tq:
            mx = jnp.maximum(mx, jnp.max(t, axis=-1, keepdims=True))
        mxs.append(mx)
    pos = [jnp.exp(los[hh] - mx) for (hh, qi), mx in zip(units, mxs)]
    ps = [[jnp.exp(t - mx) for t in tq] for tq, mx in zip(tiles, mxs)]
    accs = [jnp.dot(po.astype(BF16), vns[hh], preferred_element_type=F32) for (hh, qi), po in zip(units, pos)]
    for n, (hh, qi) in enumerate(units):
        for i in range(N_SEL_PAGES):
            w = hh * n_pg + qi * N_SEL_PAGES + i
            accs[n] = accs[n] + jnp.dot(ps[n][i].astype(BF16), vbuf[slot, w].astype(BF16),
                                        preferred_element_type=F32)
    results = [jnp.zeros((ds, HD_ATT), F32) for _ in range(hps)]
    for n, (hh, qi) in enumerate(units):
        den = jnp.sum(pos[n], axis=-1, keepdims=True)
        for p in ps[n]:
            den = den + jnp.sum(p, axis=-1, keepdims=True)
        results[hh] = jnp.where(qrow == qi, accs[n] / den, results[hh])
    for hh in range(hps):
        o_ref[0, :, hsl[hh]] = results[hh]


def _attn_sample_call(page_table, ids, rel_bias, q, k_new, v_new, cache_k2, cache_v2, past_len):
    dbs, ds, _ = q.shape
    hps = SAMPLE_HEADS_PER_STEP
    assert H_ATT % hps == 0
    n_buf = hps * ds * N_SEL_PAGES
    nspec = pl.BlockSpec((1, ds, hps * HD_ATT), lambda b, g, p, i: (b, 0, g))
    pool = pl.BlockSpec(memory_space=pl.ANY)
    return pl.pallas_call(
        functools.partial(_attn_sample_kernel, past_len=past_len),
        grid_spec=pltpu.PrefetchScalarGridSpec(
            num_scalar_prefetch=2,
            grid=(dbs, H_ATT // hps),
            in_specs=[pl.BlockSpec(memory_space=pltpu.SMEM), nspec, nspec, nspec, pool, pool],
            out_specs=pl.BlockSpec((1, ds, hps * HD_ATT), lambda b, g, p, i: (b, 0, g)),
            scratch_shapes=[pltpu.VMEM((2, n_buf, PAGE_SIZE, HD_ATT), F32),
                            pltpu.VMEM((2, n_buf, PAGE_SIZE, HD_ATT), F32),
                            pltpu.SemaphoreType.DMA((2, 2))],
        ),
        out_shape=jax.ShapeDtypeStruct((dbs, ds, C_ATT), F32),
        compiler_params=_cparams(("arbitrary", "arbitrary")),
        name="attn_sample",
    )(page_table, ids, rel_bias, q, k_new, v_new, cache_k2, cache_v2)


def _rwkv_pre_kernel(h_ref, wz_ref, sh0_ref, mu_ref, w0_ref, w2_ref, a0_ref, a2_ref, g2_ref, kk_ref, ka_ref,
                     r_o, k_o, v_o, kk_o, b_o, ld_o, g_o, zl_o, carry):
    i = pl.program_id(1)
    h = h_ref[0]
    tm = h.shape[0]
    c = C_RWKV
    row = lax.broadcasted_iota(jnp.int32, (tm, 1), 0)

    @pl.when(i == 0)
    def _():
        carry[...] = sh0_ref[0]

    def project(lo, hi):
        z = jnp.dot(h, wz_ref[:, lo:hi], preferred_element_type=F32)
        first = carry[:, lo:hi]
        carry[:, lo:hi] = z[tm - 1:tm, :]
        zl_o[0, 0, :, lo:hi] = z[tm - 8:tm, :]
        zprev = jnp.where(row == 0, first, pltpu.roll(z, 1, 0))
        return z + (zprev - z) * mu_ref[:, lo:hi]

    zl = project(3 * c, Z_PAD)
    zk = project(c, 2 * c)
    zw, za, zg = zl[:, :LORA_PAD], zl[:, LORA_PAD:2 * LORA_PAD], zl[:, 2 * LORA_PAD:]
    wl = w0_ref[...] + jnp.dot(jnp.tanh(zw).astype(BF16), w2_ref[...], preferred_element_type=F32)
    w = -(jnp.maximum(-wl, 0.0) + jnp.log(1.0 + jnp.exp(-jnp.abs(wl)))) - 0.5
    ld_o[0] = -jnp.exp(w)
    a = jax.nn.sigmoid(a0_ref[...] + jnp.dot(za.astype(BF16), a2_ref[...], preferred_element_type=F32))
    g_o[0] = jnp.dot(jax.nn.sigmoid(zg).astype(BF16), g2_ref[...], preferred_element_type=F32)
    zr = project(0, c)
    kkr = zk * kk_ref[...]
    n2 = _segsum64(kkr * kkr, _seg_ones())
    kkn = kkr / jnp.maximum(jnp.sqrt(n2), 1e-12)
    k_o[0] = zk * (1.0 + (a - 1.0) * ka_ref[...])
    kk_o[0] = kkn
    b_o[0] = kkn * a
    zv = project(2 * c, 3 * c)
    r_o[0] = zr
    v_o[0] = zv


def _rwkv_pre_call(h, win, z_col0, shift0, p, *, tm):
    bsz, t, d = h.shape
    zp = Z_PAD
    assert z_col0 % zp == 0
    nt = t // tm
    c = C_RWKV
    row = lambda b, i: (0, 0)
    vec = lambda n: pl.BlockSpec((1, n), row)
    tile = pl.BlockSpec((1, tm, c), lambda b, i: (b, i, 0))
    return pl.pallas_call(
        _rwkv_pre_kernel,
        grid=(bsz, nt),
        in_specs=[pl.BlockSpec((1, tm, d), lambda b, i: (b, i, 0)),
                  pl.BlockSpec((d, zp), lambda b, i: (0, z_col0 // zp)),
                  pl.BlockSpec((1, 1, zp), lambda b, i: (b, 0, 0)),
                  vec(zp), vec(c),
                  pl.BlockSpec((LORA_PAD, c), row), vec(c),
                  pl.BlockSpec((LORA_PAD, c), row),
                  pl.BlockSpec((D_GATE_LORA, c), row), vec(c), vec(c)],
        out_specs=[tile] * 7 + [pl.BlockSpec((1, 1, 8, zp), lambda b, i: (b, i, 0, 0))],
        out_shape=[jax.ShapeDtypeStruct((bsz, t, c), F32)] * 7 + [jax.ShapeDtypeStruct((bsz, nt, 8, zp), F32)],
        scratch_shapes=[pltpu.VMEM((1, zp), F32)],
        compiler_params=_cparams(("parallel", "arbitrary")),
        name="rwkv_pre",
    )(h, win, shift0.reshape(bsz, 1, zp), p["mu"], p["w0"], p["w2"], p["a0"], p["a2"], p["g2"], p["k_k"], p["k_a"])


SOLVE_BASE = 8
CHUNK_PREC = {"gram": (1, 1), "state_read": (1, 1), "mkv": (1, 1), "solve1": (1, 1), "solve_sq": (1, 1),
              "solve_ap": (1, 1), "out": (1, 1), "state_upd": (1, 1)}


def _chunk_pairs(rs, ks, vs, kks, bs, lds, ss):
    c = CHUNK
    lane = lax.broadcasted_iota(jnp.int32, (1, LANES), 1)
    m_a = jnp.where(lane < HS_RWKV, 1.0, 0.0)
    m_b = 1.0 - m_a
    row = lax.broadcasted_iota(jnp.int32, (c, 2 * c), 0)
    coli = lax.broadcasted_iota(jnp.int32, (c, 2 * c), 1) % c
    strict = coli < row
    incl = coli <= row
    lr = lax.broadcasted_iota(jnp.int32, (c, c), 0)
    lc = lax.broadcasted_iota(jnp.int32, (c, c), 1)
    ltri = jnp.where(lc <= lr, 1.0, 0.0).astype(BF16)
    ones = jnp.ones((c, LANES), BF16)
    rr = lax.broadcasted_iota(jnp.int32, (LANES, LANES), 0) // HS_RWKV
    cc = lax.broadcasted_iota(jnp.int32, (LANES, LANES), 1) // HS_RWKV
    same_head = rr == cc

    def each(f, *lists):
        return [f(*args) for args in zip(*lists)]

    def stack2(x):
        return jnp.concatenate([x * m_a, x * m_b], axis=0)

    def prod(site, a, b, dims=_NN):
        na, nb = CHUNK_PREC[site]
        return _dotp(_split(a, na), _split(b, nb), dims, order=max(na, nb))

    def pm(site, mcats, xs):
        return each(lambda m, x: prod(site, m, stack2(x)), mcats, xs)

    ldp = each(lambda x: _split(x, 2), lds)
    cums = each(lambda p: _dotp([ltri], p, _NN), ldp)
    gcols = each(lambda p: jnp.exp(_dotp(p, [ones], _TN)), ldp)
    g_inv = each(lambda cu: jnp.exp(-cu), cums)
    g_end = each(lambda cu: jnp.exp(cu[c - 1:c, :] - cu), cums)
    p_all = each(lambda kk, r, cu, ld: jnp.concatenate([kk * jnp.exp(cu - ld), r * jnp.exp(cu)], axis=0),
                 kks, rs, cums, lds)
    z2 = each(lambda k, b, gi: jnp.concatenate([stack2(k * gi), stack2(b * gi)], axis=0), ks, bs, g_inv)
    g4 = each(lambda p, z: prod("gram", p, z, _NT), p_all, z2)
    mk = each(lambda g: jnp.where(strict, g[:c, :2 * c], 0.0), g4)
    pj = each(lambda g: jnp.where(strict, -g[:c, 2 * c:], 0.0), g4)
    akb = each(lambda g: jnp.concatenate([jnp.where(incl, g[c:, :2 * c], 0.0),
                                          jnp.where(incl, -g[c:, 2 * c:], 0.0)], axis=1), g4)
    ps = each(lambda p, s: prod("state_read", p, s), p_all, ss)
    mkv = pm("mkv", mk, vs)
    rhs = each(lambda p, m: p[:c] + m, ps, mkv)
    same_blk = lambda s: (row // s) == (coli // s)
    eye = jnp.where(coli == row, 1.0, 0.0)
    pk = each(lambda n_: jnp.where(same_blk(SOLVE_BASE), n_, 0.0), pj)
    ts = each(lambda p: eye + p, pk)
    n = 2
    while n < SOLVE_BASE:
        pk = pm("solve_sq", pk, pk)
        ts = each(lambda t, d: t + d, ts, pm("solve_ap", ts, pk))
        n *= 2
    s = SOLVE_BASE
    while s < c:
        low = same_blk(2 * s) & jnp.logical_not(same_blk(s))
        cs = each(lambda n_: jnp.where(low, -n_, 0.0), pj)
        ts = each(lambda t, d: t - d, ts, pm("solve_ap", ts, pm("solve_sq", cs, ts)))
        s *= 2
    us = pm("solve1", ts, rhs)
    ys = each(lambda p, m, v, u: p[c:] + prod("out", m, jnp.concatenate([stack2(v), stack2(u)], axis=0)),
              ps, akb, vs, us)
    upd = each(lambda k, b, ge, v, u: prod("state_upd", jnp.concatenate([k * ge, -(b * ge)], axis=0),
                                           jnp.concatenate([v, u], axis=0), _TN), ks, bs, g_end, vs, us)
    s_new = each(lambda gc, s, up: gc * s + jnp.where(same_head, up, 0.0), gcols, ss, upd)
    return ys, s_new


def _rwkv_out_pairs(ys, rs, ks, vs, gs, rks, lgs, lbs):
    p128 = _seg_ones()
    c = CHUNK
    inv = 1.0 / HS_RWKV

    def seg3(x):
        return _split(x, 3)

    def sum3(t, o):
        return t[o * c:(o + 1) * c] + t[(o + 1) * c:(o + 2) * c] + t[(o + 2) * c:(o + 3) * c]

    t1 = [jnp.dot(jnp.concatenate(seg3(y) + seg3(r * k * rk), axis=0), p128, preferred_element_type=F32)
          for y, r, k, rk in zip(ys, rs, ks, rks)]
    ycs = [y - sum3(t, 0) * inv for y, t in zip(ys, t1)]
    t2 = [jnp.dot(jnp.concatenate(seg3(yc * yc), axis=0), p128, preferred_element_type=F32) for yc in ycs]
    outs = []
    for yc, ta, tb, v, g, lg, lb in zip(ycs, t1, t2, vs, gs, lgs, lbs):
        yn = yc * lax.rsqrt(sum3(tb, 0) * inv + GN_EPS) * lg + lb
        outs.append(((yn + sum3(ta, 3) * v) * g).astype(BF16))
    return outs


def _rwkv_chunk_kernel(r_ref, k_ref, v_ref, kk_ref, b_ref, ld_ref, g_ref, rk_ref, lg_ref, lb_ref, s0_ref,
                       o_ref, so_ref, s_scr):
    ci = pl.program_id(2)

    @pl.when(ci == 0)
    def _():
        s_scr[...] = s0_ref[...]

    nb = r_ref.shape[0]
    units = [(bi, p, slice(p * LANES, (p + 1) * LANES)) for bi in range(nb) for p in range(PAIRS_PER_STEP)]
    pairs = lambda ref: [ref[bi, :, sl] for bi, _, sl in units]
    vecs = lambda ref: [ref[:, sl] for _, _, sl in units]
    rs, ks, vs = pairs(r_ref), pairs(k_ref), pairs(v_ref)
    ys, s_new = _chunk_pairs(rs, ks, vs, pairs(kk_ref), pairs(b_ref), pairs(ld_ref),
                             [s_scr[bi, p] for bi, p, _ in units])
    outs = _rwkv_out_pairs(ys, rs, ks, vs, pairs(g_ref), vecs(rk_ref), vecs(lg_ref), vecs(lb_ref))
    for u, (bi, p, sl) in enumerate(units):
        o_ref[bi, :, sl] = outs[u]
        s_scr[bi, p] = s_new[u]
        so_ref[bi, p] = s_new[u]


def _rwkv_chunk_call(r, k, v, kk, b, logd, g, s0_blk, p):
    bsz, t, c = r.shape
    pp = PAIRS_PER_STEP
    nb = BATCH_PER_STEP
    assert bsz % nb == 0
    w = pp * LANES
    tile = pl.BlockSpec((nb, CHUNK, w), lambda bi, pi, ci: (bi, ci, pi))
    vec = pl.BlockSpec((1, w), lambda bi, pi, ci: (0, pi))
    sspec = pl.BlockSpec((nb, pp, LANES, LANES), lambda bi, pi, ci: (bi, pi, 0, 0))
    return pl.pallas_call(
        _rwkv_chunk_kernel,
        grid=(bsz // nb, c // w, t // CHUNK),
        in_specs=[tile] * 7 + [vec] * 3 + [sspec],
        out_specs=[tile, sspec],
        out_shape=[jax.ShapeDtypeStruct((bsz, t, c), BF16),
                   jax.ShapeDtypeStruct(s0_blk.shape, F32)],
        scratch_shapes=[pltpu.VMEM((nb, pp, LANES, LANES), F32)],
        compiler_params=_cparams(("parallel", "parallel", "arbitrary")),
        name="rwkv_chunk",
    )(r, k, v, kk, b, logd, g, p["r_k"], p["lnx_g"], p["lnx_b"], s0_blk)


def _combine_kernel(x_ref, ada_ref, lng_ref, lnb_ref, oa_ref, or_ref, ga_ref, gr_ref, wa_ref, wr_ref, wo_ref, o_ref):
    bb, t, d = x_ref.shape
    ua = jnp.dot(oa_ref[...], wa_ref[...], preferred_element_type=F32)
    ur = jnp.dot(or_ref[...], wr_ref[...], preferred_element_type=F32)
    m = jax.nn.sigmoid(ga_ref[...].astype(F32)) * ua + jax.nn.sigmoid(gr_ref[...].astype(F32)) * ur
    mo = jnp.dot(m.astype(BF16), wo_ref[...], preferred_element_type=F32)
    y = DEEPNORM_ALPHA * x_ref[...] + ada_ref[:, 5:6, :] * mo.reshape(bb, t, d)
    o_ref[...] = _ln(y) * lng_ref[1:2, :] + lnb_ref[1:2, :]


def _combine_call(x, ada, ln_g, ln_b, oa, orw, ga, gr, wa, wr, wo, *, blk):
    bsz, t, d = x.shape
    bb, tt = blk
    nt = t // tt
    rows = bb * tt
    xmap = lambda i: (i // nt, i % nt, 0)
    const = lambda i: (0, 0)
    rowt = lambda n: pl.BlockSpec((rows, n), lambda i: (i, 0))
    return pl.pallas_call(
        _combine_kernel,
        grid=((bsz // bb) * nt,),
        in_specs=[pl.BlockSpec((bb, tt, d), xmap),
                  pl.BlockSpec((bb, N_ADA, d), lambda i: (i // nt, 0, 0)),
                  pl.BlockSpec((3, d), const), pl.BlockSpec((3, d), const),
                  rowt(C_ATT), rowt(C_RWKV), rowt(d), rowt(d),
                  pl.BlockSpec((C_ATT, d), const), pl.BlockSpec((C_RWKV, d), const), pl.BlockSpec((d, d), const)],
        out_specs=pl.BlockSpec((bb, tt, d), xmap),
        out_shape=jax.ShapeDtypeStruct(x.shape, F32),
        compiler_params=_cparams(("parallel",)),
        name="combine",
    )(x, ada, ln_g, ln_b, oa, orw, ga, gr, wa, wr, wo)


def _rearrange_z(a):
    c3 = 3 * C_RWKV
    pad = [(0, 0)] * (a.ndim - 1) + [(0, LORA_PAD - D_DECAY_LORA)]
    return jnp.concatenate([a[..., :c3],
                            jnp.pad(a[..., c3:c3 + D_DECAY_LORA], pad),
                            jnp.pad(a[..., c3 + D_DECAY_LORA:c3 + D_DECAY_LORA + D_AAA_LORA], pad),
                            a[..., c3 + D_DECAY_LORA + D_AAA_LORA:]], axis=-1)


def _unarrange_z(a):
    c3 = 3 * C_RWKV
    return jnp.concatenate([a[..., :c3], a[..., c3:c3 + D_DECAY_LORA],
                            a[..., c3 + LORA_PAD:c3 + LORA_PAD + D_AAA_LORA], a[..., c3 + 2 * LORA_PAD:]], axis=-1)


def _state_to_blocks(state):
    bsz = state.shape[0]
    st = jnp.swapaxes(state, -1, -2).reshape(bsz, H_RWKV // 2, 2, HS_RWKV, HS_RWKV)
    blk = jnp.einsum('bphkv,hg->bphkgv', st, jnp.eye(2, dtype=state.dtype))
    return blk.reshape(bsz, H_RWKV // 2, LANES, LANES)


def _blocks_to_state(blk):
    bsz = blk.shape[0]
    b6 = blk.reshape(bsz, H_RWKV // 2, 2, HS_RWKV, 2, HS_RWKV)
    st = jnp.stack([b6[:, :, 0, :, 0, :], b6[:, :, 1, :, 1, :]], axis=2)
    return jnp.swapaxes(st, -1, -2).reshape(bsz, H_RWKV, HS_RWKV, HS_RWKV)


def _trunk(x, ada, attend, shift0, wkv0, w, *, ffn_blk, mm_tm, pre_tm, comb_blk):
    bsz, t, d = x.shape
    m = bsz * t
    ffn = functools.partial(_ffn_call, ada=ada, ln_g=w["ln_g"], ln_b=w["ln_b"], wi=w["ffn_wi"], wo=w["ffn_wo"],
                            layer=w["layer"], blk=ffn_blk)
    x1, h2 = ffn(x, which=0, sub=0, emit_next=True)
    h2f = h2.reshape(m, d)
    win = w["win"]
    proj = functools.partial(_mm_call, h2f, win, tm=mm_tm)
    q = proj(tn=C_ATT, col0=0, n=C_ATT, name="proj_q").reshape(bsz, t, C_ATT)
    k = proj(tn=C_ATT, col0=C_ATT, n=C_ATT, name="proj_k").reshape(bsz, t, C_ATT)
    v = proj(tn=C_ATT, col0=2 * C_ATT, n=C_ATT, name="proj_v").reshape(bsz, t, C_ATT)
    ga = proj(tn=d // 2, col0=3 * C_ATT, n=d, out_dtype=BF16, name="proj_ga")
    gr = proj(tn=d // 2, col0=3 * C_ATT + d, n=d, out_dtype=BF16, name="proj_gr")
    o_att = attend(q, k, v)
    r, k2, vv, kk, b, logd, g, z_last = _rwkv_pre_call(h2, win, 3 * C_ATT + 2 * d, _rearrange_z(shift0), w, tm=pre_tm)
    tp = -(-t // CHUNK) * CHUNK
    seq = [r, k2, vv, kk, b, logd, g]
    if tp != t:
        seq = [jnp.pad(a, ((0, 0), (0, tp - t), (0, 0))) for a in seq]
    o_rwkv, s_blk = _rwkv_chunk_call(*seq, _state_to_blocks(wkv0), w)
    o_rwkv = o_rwkv[:, :t].reshape(m, C_RWKV)
    x2 = _combine_call(x1, ada, w["ln_g"], w["ln_b"], o_att.reshape(m, C_ATT), o_rwkv, ga, gr,
                       w["wua"], w["wur"], w["wout"], blk=comb_blk)
    x3 = ffn(x2, which=1, sub=2, emit_next=False)
    shift_new = _unarrange_z(z_last[:, -1, 7])
    return (x3, k.reshape(bsz, t, H_ATT, HD_ATT), v.reshape(bsz, t, H_ATT, HD_ATT), _blocks_to_state(s_blk), shift_new)


def kernel(x_prompt, x_sample, cache_k, cache_v, state_wkv, state_shift, page_table, c_prompt, c_sample, rel_bias, w_ada, b_ada, ln_g, ln_b, ffn_wi, ffn_wo, w_in, mu_shift, w0, w2, a0, a2, g2, k_k, k_a, r_k, lnx_g, lnx_b, w_up_attn, w_up_rwkv, w_out):
    assert w_ada.shape[0] == DEPTH == 1
    bsz, s_len, d = x_prompt.shape
    dbs, ds, _ = x_sample.shape
    past_len = page_table.shape[1] * PAGE_SIZE
    n_phys = cache_k.shape[1]
    l = 0
    win = w_in[l]
    c3 = 3 * C_ATT
    lora_rows = ((0, LORA_PAD - D_DECAY_LORA), (0, 0))
    w = {
        "ln_g": ln_g[l], "ln_b": ln_b[l],
        "layer": l, "ffn_wi": ffn_wi.astype(BF16), "ffn_wo": ffn_wo.astype(BF16),
        "win": jnp.concatenate([win[:, :c3], win[:, c3 + RWKV_PROJ:],
                                _rearrange_z(win[:, c3:c3 + RWKV_PROJ])], axis=1).astype(BF16),
        "mu": _rearrange_z(mu_shift[l])[None, :],
        "w0": w0[l][None, :], "w2": jnp.pad(w2[l], lora_rows).astype(BF16),
        "a0": a0[l][None, :], "a2": jnp.pad(a2[l], lora_rows).astype(BF16),
        "g2": g2[l].astype(BF16), "k_k": k_k[l][None, :], "k_a": k_a[l][None, :],
        "r_k": r_k[l].reshape(1, C_RWKV), "lnx_g": lnx_g[l][None, :], "lnx_b": lnx_b[l][None, :],
        "wua": w_up_attn[l].astype(BF16), "wur": w_up_rwkv[l].astype(BF16), "wout": w_out[l].astype(BF16),
    }
    n_c = bsz + dbs
    c_rows = -(-n_c // 8) * 8
    c_all = jnp.concatenate([c_prompt, c_sample, jnp.zeros((c_rows - n_c, d), F32)], axis=0)
    ada = _ada_call(c_all, w_ada[l], b_ada[l]).reshape(c_rows, N_ADA, d)

    bias_tiles = _bias_tiles_call(rel_bias)
    far = rel_bias[NUM_BUCKETS - 1]
    attend_p = lambda q, k, v: _attn_prompt_call(q, k, v, bias_tiles, far)
    yp, kp, vp, wp, sp = _trunk(
        x_prompt, ada[:bsz], attend_p, jnp.zeros((bsz, RWKV_PROJ), F32),
        jnp.zeros((bsz, H_RWKV, HS_RWKV, HS_RWKV), F32), w,
        ffn_blk=(1, 512), mm_tm=1024, pre_tm=256, comb_blk=(1, 256))

    cache_k2 = cache_k.reshape(DEPTH * n_phys, PAGE_SIZE, H_ATT, HD_ATT)
    cache_v2 = cache_v.reshape(DEPTH * n_phys, PAGE_SIZE, H_ATT, HD_ATT)
    page_table = page_table + l * n_phys
    n_full = past_len // MOBA_BLOCK

    def attend_s(q, k, v):
        means = _cache_means_call(page_table, cache_k2, n_full).reshape(dbs, n_full, C_ATT)
        ids = _topk_ids_call(q, means)[..., :MOBA_TOPK]
        o = _attn_sample_call(page_table, ids.reshape(-1), rel_bias, q, k, v, cache_k2, cache_v2, past_len)
        return o.astype(BF16)

    ys, kn, vn, wn, sn = _trunk(
        x_sample, ada[bsz:n_c], attend_s, state_shift[l], state_wkv[l], w,
        ffn_blk=(dbs, ds), mm_tm=dbs * ds, pre_tm=ds, comb_blk=(dbs, ds))

    return (yp, ys, kp[None], vp[None], kn[None], vn[None], wp[None], wn[None], sp[None], sn[None])
```

```python
import functools
import math

import jax
import jax.numpy as jnp
from jax import lax
from jax.experimental import pallas as pl
from jax.experimental.pallas import tpu as pltpu

F32 = jnp.float32
BF16 = jnp.bfloat16

D_MODEL = 2048
D_FF = 5632
N_ADA = 9
H_ATT = 8
HD_ATT = 128
C_ATT = H_ATT * HD_ATT
MOBA_BLOCK = 256
MOBA_TOPK = 3
NUM_BUCKETS = 32
MAX_DISTANCE = 128
PAGE_SIZE = 128
HS_RWKV = 64
C_RWKV = 1024
H_RWKV = C_RWKV // HS_RWKV
D_DECAY_LORA = 96
D_AAA_LORA = 96
D_GATE_LORA = 256
RWKV_PROJ = 3 * C_RWKV + D_DECAY_LORA + D_AAA_LORA + D_GATE_LORA
LORA_PAD = 128
Z_PAD = 3 * C_RWKV + 2 * LORA_PAD + D_GATE_LORA
GN_EPS = 64e-5
LN_EPS = 1e-5
DEPTH = 1
DEEPNORM_ALPHA = (2 * DEPTH) ** 0.25
NEG_BIG = -1e30

LANES = 128
CHUNK = 64
BATCH_PER_STEP = 2
PAIRS_PER_STEP = 8
VMEM_LIMIT = 56 * 1024 * 1024


def _t5_thresholds():
    max_exact = NUM_BUCKETS // 2
    thr = list(range(1, max_exact + 1))
    for b in range(max_exact + 1, NUM_BUCKETS):
        x = max_exact * (MAX_DISTANCE / max_exact) ** ((b - max_exact) / (NUM_BUCKETS - max_exact))
        thr.append(int(math.ceil(x)))
    return tuple(thr)


T5_THR = _t5_thresholds()


def _cparams(sem, vmem=VMEM_LIMIT):
    return pltpu.CompilerParams(dimension_semantics=sem, vmem_limit_bytes=vmem)


def _ln(x):
    mu = jnp.mean(x, axis=-1, keepdims=True)
    xc = x - mu
    var = jnp.mean(xc * xc, axis=-1, keepdims=True)
    return xc * lax.rsqrt(var + LN_EPS)


def _split(x, n):
    parts = []
    for i in range(n):
        p = x.astype(BF16)
        parts.append(p)
        if i + 1 < n:
            x = x - p.astype(F32)
    return parts


_NN = (((1,), (0,)), ((), ()))
_NT = (((1,), (1,)), ((), ()))
_TN = (((0,), (0,)), ((), ()))


def _dotp(ap, bp, dims=_NN, order=None):
    if order is None:
        order = max(len(ap), len(bp))
    acc = None
    if dims == _NN and len(ap) > 1:
        m = ap[0].shape[0]
        for j, b in enumerate(bp):
            sel = [a for i, a in enumerate(ap) if i + j < order]
            if not sel:
                continue
            lhs = sel[0] if len(sel) == 1 else jnp.concatenate(sel, axis=0)
            t = lax.dot_general(lhs, b, dims, preferred_element_type=F32)
            for r in range(len(sel)):
                part = t[r * m:(r + 1) * m]
                acc = part if acc is None else acc + part
        return acc
    for i, a in enumerate(ap):
        for j, b in enumerate(bp):
            if i + j < order:
                t = lax.dot_general(a, b, dims, preferred_element_type=F32)
                acc = t if acc is None else acc + t
    return acc


def _t5_bias(rel, tbl_ref, h):
    bias = jnp.full(rel.shape, tbl_ref[0, h], F32)
    for b in range(1, NUM_BUCKETS):
        bias = jnp.where(rel >= T5_THR[b - 1], tbl_ref[b, h], bias)
    return bias


def _seg_ones():
    r = lax.broadcasted_iota(jnp.int32, (LANES, LANES), 0) // HS_RWKV
    c = lax.broadcasted_iota(jnp.int32, (LANES, LANES), 1) // HS_RWKV
    return jnp.where(r == c, 1.0, 0.0).astype(BF16)


def _segsum64(x, p128):
    outs = []
    for j in range(x.shape[-1] // LANES):
        xs = x[:, j * LANES:(j + 1) * LANES]
        outs.append(_dotp(_split(xs, 3), [p128]))
    return jnp.concatenate(outs, axis=-1)


ADA_TN = 1024


def _ada_kernel(c_ref, w_ref, b_ref, o_ref):
    c = c_ref[...]
    sc = c * jax.nn.sigmoid(c)
    rows = sc.shape[0]
    s0, s1, s2 = _split(sc, 3)
    w_hi, w_lo = _split(w_ref[...], 2)
    a = jnp.dot(jnp.concatenate([s0, s1, s2], axis=0), w_hi, preferred_element_type=F32)
    b = jnp.dot(jnp.concatenate([s0, s1], axis=0), w_lo, preferred_element_type=F32)
    o_ref[...] = (a[:rows] + a[rows:2 * rows] + a[2 * rows:] + b[:rows] + b[rows:]) + b_ref[...]


def _ada_call(c_all, w_ada, b_ada):
    rows, d = c_all.shape
    n = w_ada.shape[1]
    return pl.pallas_call(
        _ada_kernel,
        grid=(n // ADA_TN,),
        in_specs=[pl.BlockSpec((rows, d), lambda j: (0, 0)),
                  pl.BlockSpec((d, ADA_TN), lambda j: (0, j)),
                  pl.BlockSpec((1, ADA_TN), lambda j: (0, j))],
        out_specs=pl.BlockSpec((rows, ADA_TN), lambda j: (0, j)),
        out_shape=jax.ShapeDtypeStruct((rows, n), F32),
        compiler_params=_cparams(("parallel",)),
        name="ada",
    )(c_all, w_ada, b_ada.reshape(1, n))


FFN_TF = 512
FFN_TF_SMALL_M = 1408


def _ffn_kernel(x_ref, ada_ref, lng_ref, lnb_ref, wig_ref, wiu_ref, wo_ref, *rest, sub, emit_next):
    if emit_next:
        o_ref, h_next_ref, h_scr, acc = rest
    else:
        o_ref, h_scr, acc = rest
    bb, t, d = x_ref.shape
    j = pl.program_id(1)

    @pl.when(j == 0)
    def _():
        h = _ln(x_ref[...]) * (1.0 + ada_ref[:, 3 * sub + 1:3 * sub + 2, :]) + ada_ref[:, 3 * sub:3 * sub + 1, :]
        h_scr[...] = h.reshape(bb * t, d).astype(BF16)
        acc[...] = jnp.zeros_like(acc)

    h = h_scr[...]
    g = jnp.dot(h, wig_ref[...], preferred_element_type=F32)
    u = jnp.dot(h, wiu_ref[...], preferred_element_type=F32)
    act = (g * jax.nn.sigmoid(g)) * u
    acc[...] += jnp.dot(act.astype(BF16), wo_ref[...], preferred_element_type=F32)

    @pl.when(j == pl.num_programs(1) - 1)
    def _():
        gate = ada_ref[:, 3 * sub + 2:3 * sub + 3, :]
        y = DEEPNORM_ALPHA * x_ref[...] + 0.5 * gate * acc[...].reshape(bb, t, d)
        y = _ln(y) * lng_ref[sub:sub + 1, :] + lnb_ref[sub:sub + 1, :]
        o_ref[...] = y
        if emit_next:
            nxt = sub + 1
            hn = _ln(y) * (1.0 + ada_ref[:, 3 * nxt + 1:3 * nxt + 2, :]) + ada_ref[:, 3 * nxt:3 * nxt + 1, :]
            h_next_ref[...] = hn.astype(BF16)


def _ffn_call(x, ada, ln_g, ln_b, wi, wo, *, layer, which, sub, emit_next, blk):
    bsz, t, d = x.shape
    bb, tt = blk
    tf = FFN_TF if bb * tt >= FFN_TF else FFN_TF_SMALL_M
    assert D_FF % tf == 0
    nj = D_FF // tf
    nt = t // tt
    grid = ((bsz // bb) * nt, nj)
    xmap = lambda i, j: (i // nt, i % nt, 0)
    amap = lambda i, j: (i // nt, 0, 0)
    out_shape = [jax.ShapeDtypeStruct(x.shape, F32)]
    out_specs = [pl.BlockSpec((bb, tt, d), xmap)]
    if emit_next:
        out_shape.append(jax.ShapeDtypeStruct(x.shape, BF16))
        out_specs.append(pl.BlockSpec((bb, tt, d), xmap))
    res = pl.pallas_call(
        functools.partial(_ffn_kernel, sub=sub, emit_next=emit_next),
        grid=grid,
        in_specs=[pl.BlockSpec((bb, tt, d), xmap),
                  pl.BlockSpec((bb, N_ADA, d), amap),
                  pl.BlockSpec((3, d), lambda i, j: (0, 0)),
                  pl.BlockSpec((3, d), lambda i, j: (0, 0)),
                  pl.BlockSpec((None, None, d, tf), lambda i, j: (layer, which, 0, j)),
                  pl.BlockSpec((None, None, d, tf), lambda i, j: (layer, which, 0, j + nj)),
                  pl.BlockSpec((None, None, tf, d), lambda i, j: (layer, which, j, 0))],
        out_specs=out_specs,
        out_shape=out_shape,
        scratch_shapes=[pltpu.VMEM((bb * tt, d), BF16), pltpu.VMEM((bb * tt, d), F32)],
        compiler_params=_cparams(("parallel", "arbitrary")),
        name=f"ffn{sub}",
    )(x, ada, ln_g, ln_b, wi, wi, wo)
    return res if emit_next else res[0]


def _mm_kernel(h_ref, w_ref, o_ref):
    o_ref[...] = jnp.dot(h_ref[...], w_ref[...], preferred_element_type=F32)


def _mm_call(h, w, *, tm, tn, col0, n, name="proj"):
    m, k = h.shape
    assert col0 % tn == 0 and n % tn == 0
    cb = col0 // tn
    return pl.pallas_call(
        _mm_kernel,
        grid=(m // tm, n // tn),
        in_specs=[pl.BlockSpec((tm, k), lambda i, j: (i, 0)),
                  pl.BlockSpec((k, tn), lambda i, j: (0, j + cb))],
        out_specs=pl.BlockSpec((tm, tn), lambda i, j: (i, j)),
        out_shape=jax.ShapeDtypeStruct((m, n), F32),
        compiler_params=_cparams(("parallel", "arbitrary")),
        name=name,
    )(h, w)


def _bias_tiles_kernel(tbl_ref, o_ref):
    h = pl.program_id(0)
    r = lax.broadcasted_iota(jnp.int32, (MOBA_BLOCK, MOBA_BLOCK), 0)
    c = lax.broadcasted_iota(jnp.int32, (MOBA_BLOCK, MOBA_BLOCK), 1)
    rel0 = r - c
    o_ref[0, 0] = jnp.where(rel0 >= 0, _t5_bias(rel0, tbl_ref, h), NEG_BIG)
    o_ref[0, 1] = _t5_bias(rel0 + MOBA_BLOCK, tbl_ref, h)


def _bias_tiles_call(rel_bias):
    return pl.pallas_call(
        _bias_tiles_kernel,
        grid=(H_ATT,),
        in_specs=[pl.BlockSpec(memory_space=pltpu.SMEM)],
        out_specs=pl.BlockSpec((1, 2, MOBA_BLOCK, MOBA_BLOCK), lambda h: (h, 0, 0, 0)),
        out_shape=jax.ShapeDtypeStruct((H_ATT, 2, MOBA_BLOCK, MOBA_BLOCK), F32),
        compiler_params=_cparams(("arbitrary",)),
        name="bias_tiles",
    )(rel_bias)


CAND_ROWS = 16


def _rank_select(st, n_valid):
    cand = lax.broadcasted_iota(jnp.int32, st.shape, 0)
    cnt = jnp.zeros(st.shape, jnp.int32)
    for m in range(n_valid):
        sm = st[m:m + 1, :]
        beats = (sm > st) | ((sm == st) & (m < cand))
        cnt = cnt + jnp.where(beats, 1, 0)
    sel = jnp.where((cand < n_valid) & (cnt < MOBA_TOPK), 1.0, 0.0).astype(BF16)
    eye = jnp.where(lax.broadcasted_iota(jnp.int32, (CAND_ROWS, LANES), 0)
                    == lax.broadcasted_iota(jnp.int32, (CAND_ROWS, LANES), 1), 1.0, 0.0).astype(BF16)
    sel_cols = lax.dot_general(sel, eye, _TN, preferred_element_type=F32)
    return jnp.where(sel_cols > 0.5, 0.0, NEG_BIG)


def _attn_prompt_kernel(far_ref, q_ref, k_ref, v_ref, bt_ref, o_ref):
    h = pl.program_id(1)
    s_len = q_ref.shape[1]
    nb = s_len // MOBA_BLOCK
    q = q_ref[0]
    k = k_ref[0]
    kb = k.astype(BF16)
    vb = v_ref[0].astype(BF16)
    qb = (q * (HD_ATT ** -0.5)).astype(BF16)
    means = jnp.mean(k.reshape(nb, MOBA_BLOCK, HD_ATT), axis=1)
    means = jnp.concatenate([means, jnp.zeros((CAND_ROWS - nb, HD_ATT), F32)], axis=0)
    mparts = _split(means, 3)
    far = far_ref[h]
    rows = [slice(i * MOBA_BLOCK, (i + 1) * MOBA_BLOCK) for i in range(nb)]
    logits = [lax.dot_general(qb[rows[i]], kb[:(i + 1) * MOBA_BLOCK], _NT, preferred_element_type=F32)
              for i in range(nb)]
    scores = [None] + [_dotp(mparts, _split(q[rows[i]], 3), _NT, order=3) for i in range(1, nb)]
    negs = [None] + [_rank_select(scores[i], i) for i in range(1, nb)]
    probs, dens = [], []
    for i in range(nb):
        tiles = []
        for j in range(i + 1):
            tile = logits[i][:, j * MOBA_BLOCK:(j + 1) * MOBA_BLOCK]
            if j == i:
                tile = tile + bt_ref[0, 0]
            elif j == i - 1:
                tile = tile + bt_ref[0, 1] + negs[i][:, j:j + 1]
            else:
                tile = tile + (far + negs[i][:, j:j + 1])
            tiles.append(tile)
        lg = jnp.concatenate(tiles, axis=-1) if len(tiles) > 1 else tiles[0]
        mx = jnp.max(lg, axis=-1, keepdims=True)
        p = jnp.exp(lg - mx)
        dens.append(jnp.sum(p, axis=-1, keepdims=True))
        probs.append(p.astype(BF16))
    for i in range(nb):
        out = jnp.dot(probs[i], vb[:(i + 1) * MOBA_BLOCK], preferred_element_type=F32) / dens[i]
        o_ref[0, rows[i], :] = out.astype(o_ref.dtype)


def _attn_prompt_call(q, k, v, bias_tiles, far):
    bsz, s_len, _ = q.shape
    qspec = pl.BlockSpec((1, s_len, HD_ATT), lambda b, h: (b, 0, h))
    return pl.pallas_call(
        _attn_prompt_kernel,
        grid=(bsz, H_ATT),
        in_specs=[pl.BlockSpec(memory_space=pltpu.SMEM), qspec, qspec, qspec,
                  pl.BlockSpec((1, 2, MOBA_BLOCK, MOBA_BLOCK), lambda b, h: (h, 0, 0, 0))],
        out_specs=pl.BlockSpec((1, s_len, HD_ATT), lambda b, h: (b, 0, h)),
        out_shape=jax.ShapeDtypeStruct((bsz, s_len, C_ATT), BF16),
        compiler_params=_cparams(("parallel", "arbitrary")),
        name="attn_prompt",
    )(far, q, k, v, bias_tiles)


MEAN_BLOCKS = 16
PAGES_PER_BLOCK = MOBA_BLOCK // PAGE_SIZE


def _cache_means_kernel(pt_ref, *refs):
    page_refs, o_ref = refs[:-1], refs[-1]
    for m in range(MEAN_BLOCKS):
        tot = None
        for u in range(PAGES_PER_BLOCK):
            s = jnp.sum(page_refs[m * PAGES_PER_BLOCK + u][0], axis=0)
            tot = s if tot is None else tot + s
        o_ref[0, m] = tot * (1.0 / MOBA_BLOCK)


def _cache_means_call(page_table, cache_k4, n_blocks):
    dbs = page_table.shape[0]
    npg = MEAN_BLOCKS * PAGES_PER_BLOCK
    in_specs = [pl.BlockSpec((1, PAGE_SIZE, H_ATT, HD_ATT), functools.partial(
        lambda b, g, pt, u: (pt[b, g * npg + u], 0, 0, 0), u=u)) for u in range(npg)]
    return pl.pallas_call(
        _cache_means_kernel,
        grid_spec=pltpu.PrefetchScalarGridSpec(
            num_scalar_prefetch=1,
            grid=(dbs, n_blocks // MEAN_BLOCKS),
            in_specs=in_specs,
            out_specs=pl.BlockSpec((1, MEAN_BLOCKS, H_ATT, HD_ATT), lambda b, g, pt: (b, g, 0, 0)),
        ),
        out_shape=jax.ShapeDtypeStruct((dbs, n_blocks, H_ATT, HD_ATT), F32),
        compiler_params=_cparams(("parallel", "arbitrary")),
        name="cache_means",
    )(page_table, *([cache_k4] * npg))


def _topk_ids_kernel(q_ref, m_ref, o_ref):
    q = q_ref[0]
    means = m_ref[0]
    n_blocks = means.shape[0]
    ds = q.shape[0]
    pad = jnp.zeros((LANES - n_blocks, HD_ATT), F32)
    col = lax.broadcasted_iota(jnp.int32, (ds, LANES), 1)
    for h in range(H_ATT):
        sl = slice(h * HD_ATT, (h + 1) * HD_ATT)
        mh = jnp.concatenate([means[:, sl], pad], axis=0) if n_blocks < LANES else means[:, sl]
        s = _dotp(_split(q[:, sl], 3), _split(mh, 3), _NT, order=3)
        s = jnp.where(col < n_blocks, s, NEG_BIG)
        ids = jnp.zeros((ds, LANES), jnp.int32)
        for t in range(MOBA_TOPK):
            mx = jnp.max(s, axis=-1, keepdims=True)
            idx = jnp.min(jnp.where(s == mx, col, LANES), axis=-1, keepdims=True)
            ids = jnp.where(col == t, idx, ids)
            s = jnp.where(col == idx, NEG_BIG * 2, s)
        o_ref[0, h] = ids


def _topk_ids_call(q, means):
    dbs, ds, _ = q.shape
    n_blocks = means.shape[1]
    return pl.pallas_call(
        _topk_ids_kernel,
        grid=(dbs,),
        in_specs=[pl.BlockSpec((1, ds, C_ATT), lambda b: (b, 0, 0)),
                  pl.BlockSpec((1, n_blocks, C_ATT), lambda b: (b, 0, 0))],
        out_specs=pl.BlockSpec((1, H_ATT, ds, LANES), lambda b: (b, 0, 0, 0)),
        out_shape=jax.ShapeDtypeStruct((dbs, H_ATT, ds, LANES), jnp.int32),
        compiler_params=_cparams(("parallel",)),
        name="topk_ids",
    )(q, means)


N_SEL_PAGES = MOBA_TOPK * PAGES_PER_BLOCK
SAMPLE_HEADS_PER_STEP = 2


def _attn_sample_kernel(pt_ref, ids_ref, tbl_ref, q_ref, kn_ref, vn_ref, ck_hbm, cv_hbm, o_ref,
                        kbuf, vbuf, sems, *, past_len):
    ds = q_ref.shape[1]
    hps = SAMPLE_HEADS_PER_STEP
    n_pg = ds * N_SEL_PAGES
    b, hg = pl.program_id(0), pl.program_id(1)
    ng = pl.num_programs(1)
    step = b * ng + hg
    n_steps = pl.num_programs(0) * ng
    slot = step % 2

    def page_copies(bb, gg, sl):
        cps = []
        for hh in range(hps):
            head = gg * hps + hh
            base = (bb * H_ATT + head) * ds * MOBA_TOPK
            for u in range(n_pg):
                pg = pt_ref[bb, ids_ref[base + u // PAGES_PER_BLOCK] * PAGES_PER_BLOCK + u % PAGES_PER_BLOCK]
                w = hh * n_pg + u
                cps.append(pltpu.make_async_copy(ck_hbm.at[pg, :, head, :], kbuf.at[sl, w], sems.at[sl, 0]))
                cps.append(pltpu.make_async_copy(cv_hbm.at[pg, :, head, :], vbuf.at[sl, w], sems.at[sl, 1]))
        return cps

    def slot_waits(sl):
        n_buf = hps * n_pg
        return [pltpu.make_async_copy(ck_hbm.at[pl.ds(0, n_buf), :, 0, :], kbuf.at[sl], sems.at[sl, 0]),
                pltpu.make_async_copy(cv_hbm.at[pl.ds(0, n_buf), :, 0, :], vbuf.at[sl], sems.at[sl, 1])]

    @pl.when(step == 0)
    def _():
        for n, cp in enumerate(page_copies(b, hg, slot)):
            cp.start(priority=n % 2)

    @pl.when(step + 1 < n_steps)
    def _():
        nxt = step + 1
        for n, cp in enumerate(page_copies(nxt // ng, nxt % ng, 1 - slot)):
            cp.start(priority=n % 2)

    for cp in slot_waits(slot):
        cp.wait()
    qrow = lax.broadcasted_iota(jnp.int32, (ds, 1), 0)
    lane = lax.broadcasted_iota(jnp.int32, (ds, PAGE_SIZE), 1)
    rel_o = lax.broadcasted_iota(jnp.int32, (ds, ds), 0) - lax.broadcasted_iota(jnp.int32, (ds, ds), 1)
    heads = [hg * hps + hh for hh in range(hps)]
    hsl = [slice(hh * HD_ATT, (hh + 1) * HD_ATT) for hh in range(hps)]
    qbs = [(q_ref[0, :, sl] * (HD_ATT ** -0.5)).astype(BF16) for sl in hsl]
    vns = [vn_ref[0, :, sl].astype(BF16) for sl in hsl]
    los = [lax.dot_general(qb, kn_ref[0, :, sl].astype(BF16), _NT, preferred_element_type=F32)
           for qb, sl in zip(qbs, hsl)]
    los = [jnp.where(rel_o >= 0, lo + _t5_bias(rel_o, tbl_ref, head), NEG_BIG) for lo, head in zip(los, heads)]
    mx_os = [jnp.max(lo, axis=-1, keepdims=True) for lo in los]
    units = [(hh, qi) for hh in range(hps) for qi in range(ds)]
    tiles = []
    for hh, qi in units:
        flat = (b * H_ATT + heads[hh]) * ds + qi
        tq = []
        for s in range(MOBA_TOPK):
            blk = ids_ref[flat * MOBA_TOPK + s]
            for u in range(PAGES_PER_BLOCK):
                w = hh * n_pg + (qi * MOBA_TOPK + s) * PAGES_PER_BLOCK + u
                kpg = kbuf[slot, w].astype(BF16)
                lg = lax.dot_general(qbs[hh], kpg, _NT, preferred_element_type=F32)
                rel = (past_len + qi) - (blk * MOBA_BLOCK + u * PAGE_SIZE + lane)
                tq.append(lg + _t5_bias(rel, tbl_ref, heads[hh]))
        tiles.append(tq)
    mxs = []
    for (hh, qi), tq in zip(units, tiles):
        mx = mx_os[hh]
        for t in tq:
            mx = jnp.maximum(mx, jnp.max(t, axis=-1, keepdims=True))
        mxs.append(mx)
    pos = [jnp.exp(los[hh] - mx) for (hh, qi), mx in zip(units, mxs)]
    ps = [[jnp.exp(t - mx) for t in tq] for tq, mx in zip(tiles, mxs)]
    accs = [jnp.dot(po.astype(BF16), vns[hh], preferred_element_type=F32) for (hh, qi), po in zip(units, pos)]
    for n, (hh, qi) in enumerate(units):
        for i in range(N_SEL_PAGES):
            w = hh * n_pg + qi * N_SEL_PAGES + i
            accs[n] = accs[n] + jnp.dot(ps[n][i].astype(BF16), vbuf[slot, w].astype(BF16),
                                        preferred_element_type=F32)
    results = [jnp.zeros((ds, HD_ATT), F32) for _ in range(hps)]
    for n, (hh, qi) in enumerate(units):
        den = jnp.sum(pos[n], axis=-1, keepdims=True)
        for p in ps[n]:
            den = den + jnp.sum(p, axis=-1, keepdims=True)
        results[hh] = jnp.where(qrow == qi, accs[n] / den, results[hh])
    for hh in range(hps):
        o_ref[0, :, hsl[hh]] = results[hh]


def _attn_sample_call(page_table, ids, rel_bias, q, k_new, v_new, cache_k2, cache_v2, past_len):
    dbs, ds, _ = q.shape
    hps = SAMPLE_HEADS_PER_STEP
    assert H_ATT % hps == 0
    n_buf = hps * ds * N_SEL_PAGES
    nspec = pl.BlockSpec((1, ds, hps * HD_ATT), lambda b, g, p, i: (b, 0, g))
    pool = pl.BlockSpec(memory_space=pl.ANY)
    return pl.pallas_call(
        functools.partial(_attn_sample_kernel, past_len=past_len),
        grid_spec=pltpu.PrefetchScalarGridSpec(
            num_scalar_prefetch=2,
            grid=(dbs, H_ATT // hps),
            in_specs=[pl.BlockSpec(memory_space=pltpu.SMEM), nspec, nspec, nspec, pool, pool],
            out_specs=pl.BlockSpec((1, ds, hps * HD_ATT), lambda b, g, p, i: (b, 0, g)),
            scratch_shapes=[pltpu.VMEM((2, n_buf, PAGE_SIZE, HD_ATT), F32),
                            pltpu.VMEM((2, n_buf, PAGE_SIZE, HD_ATT), F32),
                            pltpu.SemaphoreType.DMA((2, 2))],
        ),
        out_shape=jax.ShapeDtypeStruct((dbs, ds, C_ATT), F32),
        compiler_params=_cparams(("arbitrary", "arbitrary")),
        name="attn_sample",
    )(page_table, ids, rel_bias, q, k_new, v_new, cache_k2, cache_v2)


def _rwkv_pre_kernel(h_ref, wz_ref, sh0_ref, mu_ref, w0_ref, w2_ref, a0_ref, a2_ref, g2_ref, kk_ref, ka_ref,
                     r_o, k_o, v_o, kk_o, b_o, ld_o, g_o, zl_o, carry):
    i = pl.program_id(1)
    h = h_ref[0]
    tm = h.shape[0]
    c = C_RWKV
    row = lax.broadcasted_iota(jnp.int32, (tm, 1), 0)

    @pl.when(i == 0)
    def _():
        carry[...] = sh0_ref[0]

    def project(lo, hi):
        z = jnp.dot(h, wz_ref[:, lo:hi], preferred_element_type=F32)
        first = carry[:, lo:hi]
        carry[:, lo:hi] = z[tm - 1:tm, :]
        zl_o[0, 0, :, lo:hi] = z[tm - 8:tm, :]
        zprev = jnp.where(row == 0, first, pltpu.roll(z, 1, 0))
        return z + (zprev - z) * mu_ref[:, lo:hi]

    zl = project(3 * c, Z_PAD)
    zk = project(c, 2 * c)
    zw, za, zg = zl[:, :LORA_PAD], zl[:, LORA_PAD:2 * LORA_PAD], zl[:, 2 * LORA_PAD:]
    wl = w0_ref[...] + jnp.dot(jnp.tanh(zw).astype(BF16), w2_ref[...], preferred_element_type=F32)
    w = -(jnp.maximum(-wl, 0.0) + jnp.log(1.0 + jnp.exp(-jnp.abs(wl)))) - 0.5
    ld_o[0] = -jnp.exp(w)
    a = jax.nn.sigmoid(a0_ref[...] + jnp.dot(za.astype(BF16), a2_ref[...], preferred_element_type=F32))
    g_o[0] = jnp.dot(jax.nn.sigmoid(zg).astype(BF16), g2_ref[...], preferred_element_type=F32)
    zr = project(0, c)
    kkr = zk * kk_ref[...]
    n2 = _segsum64(kkr * kkr, _seg_ones())
    kkn = kkr / jnp.maximum(jnp.sqrt(n2), 1e-12)
    k_o[0] = zk * (1.0 + (a - 1.0) * ka_ref[...])
    kk_o[0] = kkn
    b_o[0] = kkn * a
    zv = project(2 * c, 3 * c)
    r_o[0] = zr
    v_o[0] = zv


def _rwkv_pre_call(h, win, z_col0, shift0, p, *, tm):
    bsz, t, d = h.shape
    zp = Z_PAD
    assert z_col0 % zp == 0
    nt = t // tm
    c = C_RWKV
    row = lambda b, i: (0, 0)
    vec = lambda n: pl.BlockSpec((1, n), row)
    tile = pl.BlockSpec((1, tm, c), lambda b, i: (b, i, 0))
    return pl.pallas_call(
        _rwkv_pre_kernel,
        grid=(bsz, nt),
        in_specs=[pl.BlockSpec((1, tm, d), lambda b, i: (b, i, 0)),
                  pl.BlockSpec((d, zp), lambda b, i: (0, z_col0 // zp)),
                  pl.BlockSpec((1, 1, zp), lambda b, i: (b, 0, 0)),
                  vec(zp), vec(c),
                  pl.BlockSpec((LORA_PAD, c), row), vec(c),
                  pl.BlockSpec((LORA_PAD, c), row),
                  pl.BlockSpec((D_GATE_LORA, c), row), vec(c), vec(c)],
        out_specs=[tile] * 7 + [pl.BlockSpec((1, 1, 8, zp), lambda b, i: (b, i, 0, 0))],
        out_shape=[jax.ShapeDtypeStruct((bsz, t, c), F32)] * 7 + [jax.ShapeDtypeStruct((bsz, nt, 8, zp), F32)],
        scratch_shapes=[pltpu.VMEM((1, zp), F32)],
        compiler_params=_cparams(("parallel", "arbitrary")),
        name="rwkv_pre",
    )(h, win, shift0.reshape(bsz, 1, zp), p["mu"], p["w0"], p["w2"], p["a0"], p["a2"], p["g2"], p["k_k"], p["k_a"])


SOLVE_BASE = 8
CHUNK_PREC = {"gram": (1, 1), "state_read": (1, 1), "mkv": (1, 1), "solve1": (1, 1), "solve_sq": (1, 1),
              "solve_ap": (1, 1), "out": (1, 1), "state_upd": (1, 1)}


def _chunk_pairs(rs, ks, vs, kks, bs, lds, ss):
    c = CHUNK
    lane = lax.broadcasted_iota(jnp.int32, (1, LANES), 1)
    m_a = jnp.where(lane < HS_RWKV, 1.0, 0.0)
    m_b = 1.0 - m_a
    row = lax.broadcasted_iota(jnp.int32, (c, 2 * c), 0)
    coli = lax.broadcasted_iota(jnp.int32, (c, 2 * c), 1) % c
    strict = coli < row
    incl = coli <= row
    lr = lax.broadcasted_iota(jnp.int32, (c, c), 0)
    lc = lax.broadcasted_iota(jnp.int32, (c, c), 1)
    ltri = jnp.where(lc <= lr, 1.0, 0.0).astype(BF16)
    ones = jnp.ones((c, LANES), BF16)
    rr = lax.broadcasted_iota(jnp.int32, (LANES, LANES), 0) // HS_RWKV
    cc = lax.broadcasted_iota(jnp.int32, (LANES, LANES), 1) // HS_RWKV
    same_head = rr == cc

    def each(f, *lists):
        return [f(*args) for args in zip(*lists)]

    def stack2(x):
        return jnp.concatenate([x * m_a, x * m_b], axis=0)

    def prod(site, a, b, dims=_NN):
        na, nb = CHUNK_PREC[site]
        return _dotp(_split(a, na), _split(b, nb), dims, order=max(na, nb))

    def pm(site, mcats, xs):
        return each(lambda m, x: prod(site, m, stack2(x)), mcats, xs)

    ldp = each(lambda x: _split(x, 2), lds)
    cums = each(lambda p: _dotp([ltri], p, _NN), ldp)
    gcols = each(lambda p: jnp.exp(_dotp(p, [ones], _TN)), ldp)
    g_inv = each(lambda cu: jnp.exp(-cu), cums)
    g_end = each(lambda cu: jnp.exp(cu[c - 1:c, :] - cu), cums)
    p_all = each(lambda kk, r, cu, ld: jnp.concatenate([kk * jnp.exp(cu - ld), r * jnp.exp(cu)], axis=0),
                 kks, rs, cums, lds)
    z2 = each(lambda k, b, gi: jnp.concatenate([stack2(k * gi), stack2(b * gi)], axis=0), ks, bs, g_inv)
    g4 = each(lambda p, z: prod("gram", p, z, _NT), p_all, z2)
    mk = each(lambda g: jnp.where(strict, g[:c, :2 * c], 0.0), g4)
    pj = each(lambda g: jnp.where(strict, -g[:c, 2 * c:], 0.0), g4)
    akb = each(lambda g: jnp.concatenate([jnp.where(incl, g[c:, :2 * c], 0.0),
                                          jnp.where(incl, -g[c:, 2 * c:], 0.0)], axis=1), g4)
    ps = each(lambda p, s: prod("state_read", p, s), p_all, ss)
    mkv = pm("mkv", mk, vs)
    rhs = each(lambda p, m: p[:c] + m, ps, mkv)
    same_blk = lambda s: (row // s) == (coli // s)
    eye = jnp.where(coli == row, 1.0, 0.0)
    pk = each(lambda n_: jnp.where(same_blk(SOLVE_BASE), n_, 0.0), pj)
    ts = each(lambda p: eye + p, pk)
    n = 2
    while n < SOLVE_BASE:
        pk = pm("solve_sq", pk, pk)
        ts = each(lambda t, d: t + d, ts, pm("solve_ap", ts, pk))
        n *= 2
    s = SOLVE_BASE
    while s < c:
        low = same_blk(2 * s) & jnp.logical_not(same_blk(s))
        cs = each(lambda n_: jnp.where(low, -n_, 0.0), pj)
        ts = each(lambda t, d: t - d, ts, pm("solve_ap", ts, pm("solve_sq", cs, ts)))
        s *= 2
    us = pm("solve1", ts, rhs)
    ys = each(lambda p, m, v, u: p[c:] + prod("out", m, jnp.concatenate([stack2(v), stack2(u)], axis=0)),
              ps, akb, vs, us)
    upd = each(lambda k, b, ge, v, u: prod("state_upd", jnp.concatenate([k * ge, -(b * ge)], axis=0),
                                           jnp.concatenate([v, u], axis=0), _TN), ks, bs, g_end, vs, us)
    s_new = each(lambda gc, s, up: gc * s + jnp.where(same_head, up, 0.0), gcols, ss, upd)
    return ys, s_new


def _rwkv_out_pairs(ys, rs, ks, vs, gs, rks, lgs, lbs):
    p128 = _seg_ones()
    c = CHUNK
    inv = 1.0 / HS_RWKV

    def seg3(x):
        return _split(x, 3)

    def sum3(t, o):
        return t[o * c:(o + 1) * c] + t[(o + 1) * c:(o + 2) * c] + t[(o + 2) * c:(o + 3) * c]

    t1 = [jnp.dot(jnp.concatenate(seg3(y) + seg3(r * k * rk), axis=0), p128, preferred_element_type=F32)
          for y, r, k, rk in zip(ys, rs, ks, rks)]
    ycs = [y - sum3(t, 0) * inv for y, t in zip(ys, t1)]
    t2 = [jnp.dot(jnp.concatenate(seg3(yc * yc), axis=0), p128, preferred_element_type=F32) for yc in ycs]
    outs = []
    for yc, ta, tb, v, g, lg, lb in zip(ycs, t1, t2, vs, gs, lgs, lbs):
        yn = yc * lax.rsqrt(sum3(tb, 0) * inv + GN_EPS) * lg + lb
        outs.append(((yn + sum3(ta, 3) * v) * g).astype(BF16))
    return outs


def _rwkv_chunk_kernel(r_ref, k_ref, v_ref, kk_ref, b_ref, ld_ref, g_ref, rk_ref, lg_ref, lb_ref, s0_ref,
                       o_ref, so_ref, s_scr):
    ci = pl.program_id(2)

    @pl.when(ci == 0)
    def _():
        s_scr[...] = s0_ref[...]

    nb = r_ref.shape[0]
    units = [(bi, p, slice(p * LANES, (p + 1) * LANES)) for bi in range(nb) for p in range(PAIRS_PER_STEP)]
    pairs = lambda ref: [ref[bi, :, sl] for bi, _, sl in units]
    vecs = lambda ref: [ref[:, sl] for _, _, sl in units]
    rs, ks, vs = pairs(r_ref), pairs(k_ref), pairs(v_ref)
    ys, s_new = _chunk_pairs(rs, ks, vs, pairs(kk_ref), pairs(b_ref), pairs(ld_ref),
                             [s_scr[bi, p] for bi, p, _ in units])
    outs = _rwkv_out_pairs(ys, rs, ks, vs, pairs(g_ref), vecs(rk_ref), vecs(lg_ref), vecs(lb_ref))
    for u, (bi, p, sl) in enumerate(units):
        o_ref[bi, :, sl] = outs[u]
        s_scr[bi, p] = s_new[u]
        so_ref[bi, p] = s_new[u]


def _rwkv_chunk_call(r, k, v, kk, b, logd, g, s0_blk, p):
    bsz, t, c = r.shape
    pp = PAIRS_PER_STEP
    nb = BATCH_PER_STEP
    assert bsz % nb == 0
    w = pp * LANES
    tile = pl.BlockSpec((nb, CHUNK, w), lambda bi, pi, ci: (bi, ci, pi))
    vec = pl.BlockSpec((1, w), lambda bi, pi, ci: (0, pi))
    sspec = pl.BlockSpec((nb, pp, LANES, LANES), lambda bi, pi, ci: (bi, pi, 0, 0))
    return pl.pallas_call(
        _rwkv_chunk_kernel,
        grid=(bsz // nb, c // w, t // CHUNK),
        in_specs=[tile] * 7 + [vec] * 3 + [sspec],
        out_specs=[tile, sspec],
        out_shape=[jax.ShapeDtypeStruct((bsz, t, c), BF16),
                   jax.ShapeDtypeStruct(s0_blk.shape, F32)],
        scratch_shapes=[pltpu.VMEM((nb, pp, LANES, LANES), F32)],
        compiler_params=_cparams(("parallel", "parallel", "arbitrary")),
        name="rwkv_chunk",
    )(r, k, v, kk, b, logd, g, p["r_k"], p["lnx_g"], p["lnx_b"], s0_blk)


def _combine_kernel(x_ref, ada_ref, lng_ref, lnb_ref, oa_ref, or_ref, ga_ref, gr_ref, wa_ref, wr_ref, wo_ref, o_ref):
    bb, t, d = x_ref.shape
    ua = jnp.dot(oa_ref[...], wa_ref[...], preferred_element_type=F32)
    ur = jnp.dot(or_ref[...], wr_ref[...], preferred_element_type=F32)
    m = jax.nn.sigmoid(ga_ref[...]) * ua + jax.nn.sigmoid(gr_ref[...]) * ur
    mo = jnp.dot(m.astype(BF16), wo_ref[...], preferred_element_type=F32)
    y = DEEPNORM_ALPHA * x_ref[...] + ada_ref[:, 5:6, :] * mo.reshape(bb, t, d)
    o_ref[...] = _ln(y) * lng_ref[1:2, :] + lnb_ref[1:2, :]


def _combine_call(x, ada, ln_g, ln_b, oa, orw, ga, gr, wa, wr, wo, *, blk):
    bsz, t, d = x.shape
    bb, tt = blk
    nt = t // tt
    rows = bb * tt
    xmap = lambda i: (i // nt, i % nt, 0)
    const = lambda i: (0, 0)
    rowt = lambda n: pl.BlockSpec((rows, n), lambda i: (i, 0))
    return pl.pallas_call(
        _combine_kernel,
        grid=((bsz // bb) * nt,),
        in_specs=[pl.BlockSpec((bb, tt, d), xmap),
                  pl.BlockSpec((bb, N_ADA, d), lambda i: (i // nt, 0, 0)),
                  pl.BlockSpec((3, d), const), pl.BlockSpec((3, d), const),
                  rowt(C_ATT), rowt(C_RWKV), rowt(d), rowt(d),
                  pl.BlockSpec((C_ATT, d), const), pl.BlockSpec((C_RWKV, d), const), pl.BlockSpec((d, d), const)],
        out_specs=pl.BlockSpec((bb, tt, d), xmap),
        out_shape=jax.ShapeDtypeStruct(x.shape, F32),
        compiler_params=_cparams(("parallel",)),
        name="combine",
    )(x, ada, ln_g, ln_b, oa, orw, ga, gr, wa, wr, wo)


def _rearrange_z(a):
    c3 = 3 * C_RWKV
    pad = [(0, 0)] * (a.ndim - 1) + [(0, LORA_PAD - D_DECAY_LORA)]
    return jnp.concatenate([a[..., :c3],
                            jnp.pad(a[..., c3:c3 + D_DECAY_LORA], pad),
                            jnp.pad(a[..., c3 + D_DECAY_LORA:c3 + D_DECAY_LORA + D_AAA_LORA], pad),
                            a[..., c3 + D_DECAY_LORA + D_AAA_LORA:]], axis=-1)


def _unarrange_z(a):
    c3 = 3 * C_RWKV
    return jnp.concatenate([a[..., :c3], a[..., c3:c3 + D_DECAY_LORA],
                            a[..., c3 + LORA_PAD:c3 + LORA_PAD + D_AAA_LORA], a[..., c3 + 2 * LORA_PAD:]], axis=-1)


def _state_to_blocks(state):
    bsz = state.shape[0]
    st = jnp.swapaxes(state, -1, -2).reshape(bsz, H_RWKV // 2, 2, HS_RWKV, HS_RWKV)
    blk = jnp.einsum('bphkv,hg->bphkgv', st, jnp.eye(2, dtype=state.dtype))
    return blk.reshape(bsz, H_RWKV // 2, LANES, LANES)


def _blocks_to_state(blk):
    bsz = blk.shape[0]
    b6 = blk.reshape(bsz, H_RWKV // 2, 2, HS_RWKV, 2, HS_RWKV)
    st = jnp.stack([b6[:, :, 0, :, 0, :], b6[:, :, 1, :, 1, :]], axis=2)
    return jnp.swapaxes(st, -1, -2).reshape(bsz, H_RWKV, HS_RWKV, HS_RWKV)


def _trunk(x, ada, attend, shift0, wkv0, w, *, ffn_blk, mm_tm, pre_tm, comb_blk):
    bsz, t, d = x.shape
    m = bsz * t
    ffn = functools.partial(_ffn_call, ada=ada, ln_g=w["ln_g"], ln_b=w["ln_b"], wi=w["ffn_wi"], wo=w["ffn_wo"],
                            layer=w["layer"], blk=ffn_blk)
    x1, h2 = ffn(x, which=0, sub=0, emit_next=True)
    h2f = h2.reshape(m, d)
    win = w["win"]
    proj = functools.partial(_mm_call, h2f, win, tm=mm_tm)
    q = proj(tn=C_ATT, col0=0, n=C_ATT, name="proj_q").reshape(bsz, t, C_ATT)
    k = proj(tn=C_ATT, col0=C_ATT, n=C_ATT, name="proj_k").reshape(bsz, t, C_ATT)
    v = proj(tn=C_ATT, col0=2 * C_ATT, n=C_ATT, name="proj_v").reshape(bsz, t, C_ATT)
    ga = proj(tn=d // 2, col0=3 * C_ATT, n=d, name="proj_ga")
    gr = proj(tn=d // 2, col0=3 * C_ATT + d, n=d, name="proj_gr")
    o_att = attend(q, k, v)
    r, k2, vv, kk, b, logd, g, z_last = _rwkv_pre_call(h2, win, 3 * C_ATT + 2 * d, _rearrange_z(shift0), w, tm=pre_tm)
    tp = -(-t // CHUNK) * CHUNK
    seq = [r, k2, vv, kk, b, logd, g]
    if tp != t:
        seq = [jnp.pad(a, ((0, 0), (0, tp - t), (0, 0))) for a in seq]
    o_rwkv, s_blk = _rwkv_chunk_call(*seq, _state_to_blocks(wkv0), w)
    o_rwkv = o_rwkv[:, :t].reshape(m, C_RWKV)
    x2 = _combine_call(x1, ada, w["ln_g"], w["ln_b"], o_att.reshape(m, C_ATT), o_rwkv, ga, gr,
                       w["wua"], w["wur"], w["wout"], blk=comb_blk)
    x3 = ffn(x2, which=1, sub=2, emit_next=False)
    shift_new = _unarrange_z(z_last[:, -1, 7])
    return (x3, k.reshape(bsz, t, H_ATT, HD_ATT), v.reshape(bsz, t, H_ATT, HD_ATT), _blocks_to_state(s_blk), shift_new)


def kernel(x_prompt, x_sample, cache_k, cache_v, state_wkv, state_shift, page_table, c_prompt, c_sample, rel_bias, w_ada, b_ada, ln_g, ln_b, ffn_wi, ffn_wo, w_in, mu_shift, w0, w2, a0, a2, g2, k_k, k_a, r_k, lnx_g, lnx_b, w_up_attn, w_up_rwkv, w_out):
    assert w_ada.shape[0] == DEPTH == 1
    bsz, s_len, d = x_prompt.shape
    dbs, ds, _ = x_sample.shape
    past_len = page_table.shape[1] * PAGE_SIZE
    n_phys = cache_k.shape[1]
    l = 0
    win = w_in[l]
    c3 = 3 * C_ATT
    lora_rows = ((0, LORA_PAD - D_DECAY_LORA), (0, 0))
    w = {
        "ln_g": ln_g[l], "ln_b": ln_b[l],
        "layer": l, "ffn_wi": ffn_wi.astype(BF16), "ffn_wo": ffn_wo.astype(BF16),
        "win": jnp.concatenate([win[:, :c3], win[:, c3 + RWKV_PROJ:],
                                _rearrange_z(win[:, c3:c3 + RWKV_PROJ])], axis=1).astype(BF16),
        "mu": _rearrange_z(mu_shift[l])[None, :],
        "w0": w0[l][None, :], "w2": jnp.pad(w2[l], lora_rows).astype(BF16),
        "a0": a0[l][None, :], "a2": jnp.pad(a2[l], lora_rows).astype(BF16),
        "g2": g2[l].astype(BF16), "k_k": k_k[l][None, :], "k_a": k_a[l][None, :],
        "r_k": r_k[l].reshape(1, C_RWKV), "lnx_g": lnx_g[l][None, :], "lnx_b": lnx_b[l][None, :],
        "wua": w_up_attn[l].astype(BF16), "wur": w_up_rwkv[l].astype(BF16), "wout": w_out[l].astype(BF16),
    }
    n_c = bsz + dbs
    c_rows = -(-n_c // 8) * 8
    c_all = jnp.concatenate([c_prompt, c_sample, jnp.zeros((c_rows - n_c, d), F32)], axis=0)
    ada = _ada_call(c_all, w_ada[l], b_ada[l]).reshape(c_rows, N_ADA, d)

    bias_tiles = _bias_tiles_call(rel_bias)
    far = rel_bias[NUM_BUCKETS - 1]
    attend_p = lambda q, k, v: _attn_prompt_call(q, k, v, bias_tiles, far)
    yp, kp, vp, wp, sp = _trunk(
        x_prompt, ada[:bsz], attend_p, jnp.zeros((bsz, RWKV_PROJ), F32),
        jnp.zeros((bsz, H_RWKV, HS_RWKV, HS_RWKV), F32), w,
        ffn_blk=(1, 512), mm_tm=1024, pre_tm=256, comb_blk=(1, 256))

    cache_k2 = cache_k.reshape(DEPTH * n_phys, PAGE_SIZE, H_ATT, HD_ATT)
    cache_v2 = cache_v.reshape(DEPTH * n_phys, PAGE_SIZE, H_ATT, HD_ATT)
    page_table = page_table + l * n_phys
    n_full = past_len // MOBA_BLOCK

    def attend_s(q, k, v):
        means = _cache_means_call(page_table, cache_k2, n_full).reshape(dbs, n_full, C_ATT)
        ids = _topk_ids_call(q, means)[..., :MOBA_TOPK]
        o = _attn_sample_call(page_table, ids.reshape(-1), rel_bias, q, k, v, cache_k2, cache_v2, past_len)
        return o.astype(BF16)

    ys, kn, vn, wn, sn = _trunk(
        x_sample, ada[bsz:n_c], attend_s, state_shift[l], state_wkv[l], w,
        ffn_blk=(dbs, ds), mm_tm=dbs * ds, pre_tm=ds, comb_blk=(dbs, ds))

    return (yp, ys, kp[None], vp[None], kn[None], vn[None], wp[None], wn[None], sp[None], sn[None])
```
